```python
import math
import jax
import jax.numpy as jnp
from jax import lax
import numpy as np

D_MODEL = 2048
BATCH = 32
SEQ = 256
DEPTH = 1
DEC_BATCH = 2
DEC_SEQ = 1024
PAST_LEN = 256

GRID_W = 64
D_HY = 1024
HY_ORDER = 2
HY_BANDS = 16
HY_EMB = 1 + 2 * HY_BANDS
HY_FILTER_HIDDEN = 64
HY_DECAY_FAST = 0.3
HY_DECAY_SLOW = 1.5
HY_DECAY_TARGET = 1e-2
N_RET_HEADS = 8
RET_HEAD_DIM = 128
D_RET = N_RET_HEADS * RET_HEAD_DIM
RET_CHUNK = 128
ROPE_BASE = 10000.0
ROPE_FREQS = RET_HEAD_DIM // 4
D_FF = 5632
N_IN = 3 * D_HY + 4 * D_RET + 2 * D_MODEL
RMS_EPS = 1e-6
GN_EPS = 1e-5
FILTER_EPS = 1e-6
F32 = jnp.float32

kernel_name = 'hybrid_hyena_retention_diffusion_step'


def rmsnorm(x, g):
    xf = x.astype(F32)
    y = xf * lax.rsqrt(jnp.mean(xf * xf, axis=-1, keepdims=True) + RMS_EPS)
    return (y * g.astype(F32)).astype(x.dtype)


def dwconv3(x, w):
    xp = jnp.pad(x, ((0, 0), (1, 1), (0, 0)))
    return xp[:, :-2] * w[0] + xp[:, 1:-1] * w[1] + xp[:, 2:] * w[2]


def rope_2d(L):
    ROWS = L // GRID_W
    row = jnp.repeat(jnp.arange(ROWS), GRID_W).astype(F32)
    col = jnp.tile(jnp.arange(GRID_W), ROWS).astype(F32)
    inv = ROPE_BASE ** (-jnp.arange(ROPE_FREQS, dtype=F32) / ROPE_FREQS)
    ang = jnp.concatenate([row[:, None] * inv, col[:, None] * inv], axis=-1)
    return jnp.cos(ang), jnp.sin(ang)


def apply_rope(x, cos, sin):
    half = x.shape[-1] // 2
    x1, x2 = x[..., :half], x[..., half:]
    c = cos[None, :, None, :]
    s = sin[None, :, None, :]
    return jnp.concatenate([x1 * c - x2 * s, x2 * c + x1 * s], axis=-1)


def hyena_filter_fft(L, w1, b1, w2, b2, w3, b3, freq, decay):
    n = jnp.arange(L, dtype=F32)
    t = n / L
    f = jnp.linspace(1e-4, HY_BANDS - 1, HY_BANDS, dtype=F32)
    w = 2.0 * math.pi * n / L
    z = jnp.concatenate([t[:, None], jnp.cos(w[:, None] * f), jnp.sin(w[:, None] * f)], axis=-1)
    fr = freq.astype(F32)
    h = jnp.sin(fr[0] * (z @ w1.astype(F32) + b1.astype(F32)))
    h = jnp.sin(fr[1] * (h @ w2.astype(F32) + b2.astype(F32)))
    h = (h @ w3.astype(F32) + b3.astype(F32)).reshape(L, 2, HY_ORDER, D_HY)
    h = h * jnp.exp(-t[:, None, None, None] * jnp.abs(decay.astype(F32))[None])
    h_fwd, h_bwd = h[:, 0], h[:, 1]
    k = jnp.concatenate([h_fwd, jnp.zeros((1, HY_ORDER, D_HY), F32), h_bwd[1:][::-1]], axis=0)
    k = k / (jnp.sum(jnp.abs(k), axis=0, keepdims=True) + FILTER_EPS)
    return jnp.fft.rfft(k, axis=0)


def fftconv(u, kf, bias):
    L = u.shape[1]
    uf = jnp.fft.rfft(u, n=2 * L, axis=1)
    y = jnp.fft.irfft(uf * kf[None], n=2 * L, axis=1)[:, :L]
    return y + u * bias


def ret_chunk_scan(q, k, v, gamma, state0):
    B, L, H, _ = q.shape
    DV = v.shape[-1]
    C = RET_CHUNK
    N = L // C
    log_g = jnp.log(gamma)
    idx = jnp.arange(C, dtype=F32)
    diff = idx[:, None] - idx[None, :]
    intra = jnp.where(diff[None] >= 0, jnp.exp(log_g[:, None, None] * jnp.maximum(diff, 0.0)[None]), 0.0)
    xi = jnp.exp(log_g[None, :] * (idx[:, None] + 1.0))
    zeta = jnp.exp(log_g[None, :] * (C - 1.0 - idx)[:, None])
    chunk_decay = jnp.exp(log_g * C)

    def chunks(a):
        return a.reshape(B, N, C, H, a.shape[-1]).transpose(1, 0, 2, 3, 4)

    def step(R, qkv):
        qc, kc, vc = qkv
        s = jnp.einsum('bihd,bjhd->bhij', qc, kc) * intra[None]
        o = jnp.einsum('bhij,bjhe->bihe', s, vc) + jnp.einsum('bihd,bhde->bihe', qc, R) * xi[None, :, :, None]
        R = R * chunk_decay[None, :, None, None] + jnp.einsum('bjhd,bjhe->bhde', kc * zeta[None, :, :, None], vc)
        return R, o

    R, o = lax.scan(step, state0, (chunks(q), chunks(k), chunks(v)))
    return o.transpose(1, 0, 2, 3, 4).reshape(B, L, H, DV), R


def token_mixer(h, rope, state0, w_in, hy_short_w, hy_w1, hy_b1, hy_w2, hy_b2, hy_w3, hy_b3,
                hy_freq, hy_decay, hy_bias, ret_decay_logit, ret_gn, w_br_hy, w_br_ret, w_out):
    dt = h.dtype
    B, L, _ = h.shape
    proj = h @ w_in
    o_ret = 3 * D_HY
    o_gate = o_ret + 4 * D_RET
    hy = dwconv3(proj[..., :o_ret], hy_short_w).astype(F32)
    x1, x2, v_h = jnp.split(hy, 3, axis=-1)
    kf = hyena_filter_fft(L, hy_w1, hy_b1, hy_w2, hy_b2, hy_w3, hy_b3, hy_freq, hy_decay)
    hb = hy_bias.astype(F32)
    z = x1 * fftconv(v_h, kf[:, 0], hb[0])
    y_hy = x2 * fftconv(z, kf[:, 1], hb[1])
    q, k, v, g = jnp.split(proj[..., o_ret:o_gate].astype(F32), 4, axis=-1)
    q = q.reshape(B, L, N_RET_HEADS, RET_HEAD_DIM)
    k = k.reshape(B, L, N_RET_HEADS, RET_HEAD_DIM)
    v = v.reshape(B, L, N_RET_HEADS, RET_HEAD_DIM)
    if rope is not None:
        q = apply_rope(q, rope[0], rope[1])
        k = apply_rope(k, rope[0], rope[1])
    k = k * RET_HEAD_DIM ** -0.5
    gamma = jax.nn.sigmoid(ret_decay_logit.astype(F32))
    s0 = state0.astype(F32)
    o_f, r_f = ret_chunk_scan(q, k, v, gamma[0], s0[:, 0])
    o_b, r_b = ret_chunk_scan(q[:, ::-1], k[:, ::-1], v[:, ::-1], gamma[1], s0[:, 1])
    o = o_f + o_b[:, ::-1]
    mu = jnp.mean(o, axis=-1, keepdims=True)
    var = jnp.mean(jnp.square(o - mu), axis=-1, keepdims=True)
    o = ((o - mu) * lax.rsqrt(var + GN_EPS)).reshape(B, L, D_RET) * ret_gn.astype(F32)
    y_ret = o * jax.nn.silu(g)
    g_hy, g_ret = jnp.split(jax.nn.sigmoid(proj[..., o_gate:].astype(F32)), 2, axis=-1)
    merged = (g_hy * (y_hy.astype(dt) @ w_br_hy).astype(F32)
              + g_ret * (y_ret.astype(dt) @ w_br_ret).astype(F32))
    out = merged.astype(dt) @ w_out
    return out.astype(dt), jnp.stack([r_f, r_b], axis=1)


def block(x, mod, rope, state0, norms, mixer_p, ffn_p):
    g_pre_m, g_post_m, g_pre_f, g_post_f = norms
    ffn_w_up, ffn_conv, ffn_w_down = ffn_p
    mod = mod.astype(x.dtype)
    shift_m, scale_m, gate_m, shift_f, scale_f, gate_f = jnp.split(mod, 6, axis=-1)
    h = rmsnorm(x, g_pre_m) * (1.0 + scale_m) + shift_m
    m, st = token_mixer(h, rope, state0, *mixer_p)
    x = x + (gate_m * rmsnorm(m, g_post_m)).astype(x.dtype)
    h = rmsnorm(x, g_pre_f) * (1.0 + scale_f) + shift_f
    u = dwconv3(h @ ffn_w_up, ffn_conv)
    a, b = jnp.split(u, 2, axis=-1)
    f = (jax.nn.gelu(a, approximate=True) * b) @ ffn_w_down
    x = x + (gate_f * rmsnorm(f, g_post_f)).astype(x.dtype)
    return x, st


def setup_inputs(seed: int = 0) -> dict:
    key = jax.random.key(seed)
    ks = jax.random.split(key, 40)

    def nrm(k, shape, scale):
        return jax.random.normal(k, shape, jnp.float32) * scale

    hy_rates = jnp.abs(jnp.linspace(math.log(HY_DECAY_TARGET) / HY_DECAY_FAST,
                                    math.log(HY_DECAY_TARGET) / HY_DECAY_SLOW, D_HY, dtype=jnp.float32))
    gam = 1.0 - 2.0 ** (-5.0 - jnp.arange(N_RET_HEADS, dtype=jnp.float32))
    ret_logit = jnp.log(gam) - jnp.log1p(-gam)
    FH = HY_FILTER_HIDDEN
    return {
        'x_prompt': nrm(ks[0], (BATCH, SEQ, D_MODEL), 1.0),
        'x_sample': nrm(ks[1], (DEC_BATCH, DEC_SEQ, D_MODEL), 1.0),
        'state_ret': nrm(ks[2], (DEC_BATCH, DEPTH, 2, N_RET_HEADS, RET_HEAD_DIM, RET_HEAD_DIM), 0.5),
        'c': nrm(ks[3], (DEC_BATCH, D_MODEL), 1.0),
        'c_ctx': nrm(ks[4], (D_MODEL,), 1.0),
        'w_ada': nrm(ks[5], (DEPTH, D_MODEL, 6 * D_MODEL), 0.5 * D_MODEL ** -0.5),
        'b_ada': nrm(ks[6], (DEPTH, 6 * D_MODEL), 0.02),
        'norm_pre_mix': 1.0 + nrm(ks[7], (DEPTH, D_MODEL), 0.05),
        'norm_post_mix': 1.0 + nrm(ks[8], (DEPTH, D_MODEL), 0.05),
        'norm_pre_ffn': 1.0 + nrm(ks[9], (DEPTH, D_MODEL), 0.05),
        'norm_post_ffn': 1.0 + nrm(ks[10], (DEPTH, D_MODEL), 0.05),
        'w_in': nrm(ks[11], (DEPTH, D_MODEL, N_IN), D_MODEL ** -0.5),
        'hy_short_w': nrm(ks[12], (DEPTH, 3, 3 * D_HY), 3 ** -0.5),
        'hy_w1': nrm(ks[13], (DEPTH, HY_EMB, FH), HY_EMB ** -0.5),
        'hy_b1': nrm(ks[14], (DEPTH, FH), 0.1),
        'hy_w2': nrm(ks[15], (DEPTH, FH, FH), FH ** -0.5),
        'hy_b2': nrm(ks[16], (DEPTH, FH), 0.1),
        'hy_w3': nrm(ks[17], (DEPTH, FH, 2 * HY_ORDER * D_HY), FH ** -0.5),
        'hy_b3': nrm(ks[18], (DEPTH, 2 * HY_ORDER * D_HY), 0.02),
        'hy_freq': 1.0 + nrm(ks[19], (DEPTH, 2, FH), 0.1),
        'hy_decay': hy_rates * (1.0 + nrm(ks[20], (DEPTH, 2, HY_ORDER, D_HY), 0.05)),
        'hy_bias': nrm(ks[21], (DEPTH, HY_ORDER, D_HY), 0.1),
        'ret_decay_logit': ret_logit + nrm(ks[22], (DEPTH, 2, N_RET_HEADS), 0.01),
        'ret_gn': 1.0 + nrm(ks[23], (DEPTH, D_RET), 0.05),
        'w_br_hy': nrm(ks[24], (DEPTH, D_HY, D_MODEL), D_HY ** -0.5),
        'w_br_ret': nrm(ks[25], (DEPTH, D_RET, D_MODEL), D_RET ** -0.5),
        'w_out': nrm(ks[26], (DEPTH, D_MODEL, D_MODEL), D_MODEL ** -0.5),
        'ffn_w_up': nrm(ks[27], (DEPTH, D_MODEL, 2 * D_FF), D_MODEL ** -0.5),
        'ffn_conv': nrm(ks[28], (DEPTH, 3, 2 * D_FF), 3 ** -0.5),
        'ffn_w_down': nrm(ks[29], (DEPTH, D_FF, D_MODEL), D_FF ** -0.5),
    }


def reference(x_prompt, x_sample, state_ret, c, c_ctx, w_ada, b_ada, norm_pre_mix, norm_post_mix,
              norm_pre_ffn, norm_post_ffn, w_in, hy_short_w, hy_w1, hy_b1, hy_w2, hy_b2, hy_w3, hy_b3,
              hy_freq, hy_decay, hy_bias, ret_decay_logit, ret_gn, w_br_hy, w_br_ret, w_out,
              ffn_w_up, ffn_conv, ffn_w_down):
    zero_state = jnp.zeros((x_prompt.shape[0], 2, N_RET_HEADS, RET_HEAD_DIM, RET_HEAD_DIM), F32)
    rope = rope_2d(x_sample.shape[1])
    x_p = x_prompt
    x_s = x_sample
    states = []
    for l in range(DEPTH):
        mixer_p = (w_in[l], hy_short_w[l], hy_w1[l], hy_b1[l], hy_w2[l], hy_b2[l], hy_w3[l], hy_b3[l],
                   hy_freq[l], hy_decay[l], hy_bias[l], ret_decay_logit[l], ret_gn[l],
                   w_br_hy[l], w_br_ret[l], w_out[l])
        norms = (norm_pre_mix[l], norm_post_mix[l], norm_pre_ffn[l], norm_post_ffn[l])
        ffn_p = (ffn_w_up[l], ffn_conv[l], ffn_w_down[l])
        mod_ctx = (jax.nn.silu(c_ctx) @ w_ada[l] + b_ada[l])[None, None, :]
        mod_lat = (jax.nn.silu(c) @ w_ada[l] + b_ada[l])[:, None, :]
        x_p, st = block(x_p, mod_ctx, None, zero_state, norms, mixer_p, ffn_p)
        x_s, _ = block(x_s, mod_lat, rope, state_ret[:, l], norms, mixer_p, ffn_p)
        states.append(st)
    new_state_ret = jnp.stack(states, axis=1)
    return (x_p, x_s, new_state_ret)
```

```python
import functools
import math

import jax
import jax.numpy as jnp
import numpy as np
from jax import lax
from jax.experimental import pallas as pl
from jax.experimental.pallas import tpu as pltpu

F32 = jnp.float32
BF16 = jnp.bfloat16

RMS_EPS = 1e-6
GN_EPS = 1e-5
FILTER_EPS = 1e-6
HY_BANDS = 16
GRID_W = 64
ROPE_BASE = 10000.0

V7X_VMEM_LIMIT_BYTES = 56 * 1024 * 1024
V7X_LANES = 128
V7X_SUBLANES = 8


def _cparams(*sem):
    return pltpu.CompilerParams(dimension_semantics=sem, vmem_limit_bytes=V7X_VMEM_LIMIT_BYTES)


def _tile(n, target, unit=V7X_LANES):
    if n <= target:
        return n
    best = unit
    for t in range(unit, target + 1, unit):
        if n % t == 0:
            best = t
    assert n % best == 0, (n, target, unit)
    return best


def _row_tile(m, seq, per_seq_mod, target):
    if per_seq_mod or seq >= target:
        return seq
    return seq * _tile(m // seq, target // seq, unit=1)


def _resident(shape):
    return pl.BlockSpec(shape, lambda *_: (0,) * len(shape), pipeline_mode=pl.Buffered(1))


def _sigmoid(x):
    return 1.0 / (1.0 + jnp.exp(-x))


def _rms_scale(x, g):
    ms = jnp.mean(x * x, axis=-1, keepdims=True)
    return (x * lax.rsqrt(ms + RMS_EPS)) * g


def _dot(a, b):
    return jnp.dot(a, b, preferred_element_type=F32)


def _dot_exact(a, b):
    return jnp.dot(a, b, preferred_element_type=F32, precision=lax.Precision.HIGHEST)


def _dwconv3_rows(x, w, first, last):
    rows = x.shape[0]
    prev = jnp.where(first, 0.0, pltpu.roll(x, 1, 0))
    nxt = jnp.where(last, 0.0, pltpu.roll(x, rows - 1, 0))
    return prev * w[0:1] + x * w[1:2] + nxt * w[2:3]


def _ada_kernel(cc_ref, w_ref, b_ref, o_ref):
    cc = cc_ref[...]
    s = cc * _sigmoid(cc)
    o_ref[...] = _dot(s.astype(BF16), w_ref[...].astype(BF16)) + b_ref[...]


def _ada_mod(cc, w, b):
    d, n = w.shape
    tn = _tile(n, 1024)
    return pl.pallas_call(
        _ada_kernel,
        grid=(n // tn,),
        in_specs=[pl.BlockSpec((V7X_SUBLANES, d), lambda j: (0, 0)),
                  pl.BlockSpec((d, tn), lambda j: (0, j)),
                  pl.BlockSpec((1, tn), lambda j: (0, j))],
        out_specs=pl.BlockSpec((V7X_SUBLANES, tn), lambda j: (0, j)),
        out_shape=jax.ShapeDtypeStruct((V7X_SUBLANES, n), F32),
        compiler_params=_cparams("parallel"),
        name="ada_mod",
    )(cc, w, b)


def _in_proj_kernel(x_ref, g_ref, sc_ref, sh_ref, w_ref, o_ref, h_ref):
    @pl.when(pl.program_id(1) == 0)
    def _():
        h = _rms_scale(x_ref[...], g_ref[...]) * (1.0 + sc_ref[...]) + sh_ref[...]
        h_ref[...] = h.astype(BF16)

    o_ref[...] = _dot(h_ref[...], w_ref[...])


def _in_proj(x, g, scale, shift, w, rows_per_mod, tm):
    m, d = x.shape
    n = w.shape[1]
    tn = _tile(n, 1024)
    mod_spec = pl.BlockSpec((None, 1, d), lambda i, j: ((i * tm) // rows_per_mod, 0, 0))
    return pl.pallas_call(
        _in_proj_kernel,
        grid=(m // tm, n // tn),
        in_specs=[pl.BlockSpec((tm, d), lambda i, j: (i, 0)),
                  pl.BlockSpec((1, d), lambda i, j: (0, 0)),
                  mod_spec, mod_spec,
                  pl.BlockSpec((d, tn), lambda i, j: (0, j))],
        out_specs=pl.BlockSpec((tm, tn), lambda i, j: (i, j)),
        out_shape=jax.ShapeDtypeStruct((m, n), F32),
        scratch_shapes=[pltpu.VMEM((tm, d), BF16)],
        compiler_params=_cparams("parallel", "arbitrary"),
        name="in_proj",
    )(x, g, scale, shift, w)


def _dft_tables(seq):
    n_fft = 2 * seq
    idx = np.arange(seq)
    ang = 2.0 * np.pi * ((idx[:, None] * idx[None, :]) % n_fft) / n_fft
    cos = np.cos(ang)
    msin = -np.sin(ang)
    sign = np.where(idx % 2 == 0, 1.0, -1.0)
    msin[0, :] = sign
    fwd = np.concatenate([cos, msin], axis=0)
    wgt = np.full((seq,), 2.0 / n_fft)
    wgt[0] = 1.0 / n_fft
    inv_re = cos.T * wgt[None, :]
    inv_im = msin.T * wgt[None, :]
    inv_im[:, 0] = sign / n_fft
    inv = np.concatenate([inv_re, inv_im], axis=1)
    return fwd, inv


def _filter_feats(seq):
    n = np.arange(seq, dtype=np.float64)
    t = n / seq
    f = np.linspace(1e-4, HY_BANDS - 1, HY_BANDS)
    w = 2.0 * math.pi * n / seq
    z = np.concatenate([t[:, None], np.cos(w[:, None] * f), np.sin(w[:, None] * f)], axis=-1)
    out = np.zeros((seq, V7X_LANES), np.float32)
    out[:, :z.shape[1]] = z
    return out


def _filter_kernel(z_ref, w1_ref, b1_ref, w2_ref, b2_ref, fr_ref, w3f_ref, w3b_ref, b3f_ref, b3b_ref,
                   decf_ref, decb_ref, cmat_ref, smat_ref, kr_ref, kiz_ref, krn_ref, h2_ref):
    @pl.when(pl.program_id(0) == 0)
    def _():
        h1 = jnp.sin(fr_ref[0:1, :] * (_dot_exact(z_ref[...], w1_ref[...]) + b1_ref[...]))
        h2_ref[...] = jnp.sin(fr_ref[1:2, :] * (_dot_exact(h1, w2_ref[...]) + b2_ref[...]))

    h2 = h2_ref[...]
    seq, cw = kr_ref.shape
    t = z_ref[:, 0:1]
    row = lax.broadcasted_iota(jnp.int32, (seq, cw), 0)
    hf = (_dot_exact(h2, w3f_ref[...]) + b3f_ref[...]) * jnp.exp(-t * jnp.abs(decf_ref[...]))
    hb = (_dot_exact(h2, w3b_ref[...]) + b3b_ref[...]) * jnp.exp(-t * jnp.abs(decb_ref[...]))
    hb = jnp.where(row == 0, 0.0, hb)
    norm = (jnp.sum(jnp.abs(hf), axis=0, keepdims=True)
            + jnp.sum(jnp.abs(hb), axis=0, keepdims=True) + FILTER_EPS)
    inv = 1.0 / norm
    even = (hf + hb) * inv
    odd = (hf - hb) * inv
    kr = _dot_exact(cmat_ref[...], even)
    ki = _dot_exact(smat_ref[...], odd)
    nyq = jnp.sum(jnp.where((row & 1) == 0, even, -even), axis=0, keepdims=True)
    kr_ref[...] = kr
    kiz_ref[...] = jnp.where(row == 0, 0.0, ki)
    krn_ref[...] = jnp.where(row == 0, nyq, kr)


def _hyena_filters(seq, w1, b1, w2, b2, w3, b3, freq, decay, d_hy):
    fh = w1.shape[1]
    pad = V7X_LANES
    w1p = jnp.zeros((pad, pad), F32).at[:w1.shape[0], :fh].set(w1)
    b1p = jnp.zeros((1, pad), F32).at[0, :fh].set(b1)
    w2p = jnp.zeros((pad, pad), F32).at[:fh, :fh].set(w2)
    b2p = jnp.zeros((1, pad), F32).at[0, :fh].set(b2)
    frp = jnp.zeros((2, pad), F32).at[:, :fh].set(freq)
    ncol = w3.shape[1] // 2
    w3p = jnp.zeros((pad, 2 * ncol), F32).at[:fh].set(w3)
    b3r = b3.reshape(1, 2 * ncol)
    dec = decay.reshape(1, 2 * ncol)
    fwd, _ = _dft_tables(seq)
    cmat = jnp.asarray(fwd[:seq], F32)
    smat = jnp.asarray(fwd[seq:], F32)
    z = jnp.asarray(_filter_feats(seq))
    cw = _tile(ncol, 256)
    nb = ncol // cw
    full = lambda shape: pl.BlockSpec(shape, lambda j: (0, 0))
    colf = lambda rows: pl.BlockSpec((rows, cw), lambda j: (0, j))
    colb = lambda rows: pl.BlockSpec((rows, cw), lambda j: (0, nb + j))
    out = jax.ShapeDtypeStruct((seq, ncol), F32)
    return pl.pallas_call(
        _filter_kernel,
        grid=(nb,),
        in_specs=[full((seq, pad)), full((pad, pad)), full((1, pad)), full((pad, pad)), full((1, pad)),
                  full((2, pad)), colf(pad), colb(pad), colf(1), colb(1), colf(1), colb(1),
                  full((seq, seq)), full((seq, seq))],
        out_specs=[colf(seq), colf(seq), colf(seq)],
        out_shape=[out, out, out],
        scratch_shapes=[pltpu.VMEM((seq, pad), F32)],
        compiler_params=_cparams("arbitrary"),
        name=f"hy_filter_{seq}",
    )(z, w1p, b1p, w2p, b2p, frp, w3p, w3p, b3r, b3r, dec, dec, cmat, smat)


def _hyena_kernel(x1_ref, x2_ref, v_ref, w1_ref, w2_ref, wv_ref, hb_ref,
                  kr0_ref, kiz0_ref, krn0_ref, kr1_ref, kiz1_ref, krn1_ref, f_ref, g_ref, o_ref):
    seq, cw = o_ref.shape
    row = lax.broadcasted_iota(jnp.int32, (seq, cw), 0)
    first = row == 0
    last = row == seq - 1
    x1 = _dwconv3_rows(x1_ref[...], w1_ref[...], first, last)
    x2 = _dwconv3_rows(x2_ref[...], w2_ref[...], first, last)
    v = _dwconv3_rows(v_ref[...], wv_ref[...], first, last)

    def long_conv(u, kr_ref, kiz_ref, krn_ref, bias):
        spec = _dot(f_ref[...], u.astype(BF16))
        ur = spec[:seq]
        ui = spec[seq:]
        kiz = kiz_ref[...]
        yr = ur * kr_ref[...] - ui * kiz
        yi = ur * kiz + ui * krn_ref[...]
        y = _dot(g_ref[:, :seq], yr.astype(BF16)) + _dot(g_ref[:, seq:], yi.astype(BF16))
        return y + u * bias

    z = x1 * long_conv(v, kr0_ref, kiz0_ref, krn0_ref, hb_ref[0:1])
    o_ref[...] = (x2 * long_conv(z, kr1_ref, kiz1_ref, krn1_ref, hb_ref[1:2])).astype(BF16)


def _hyena(proj, batch, seq, short_w, hy_bias, tables, d_hy):
    kr, kiz, krn = tables
    cw = _tile(d_hy, 512 if seq <= 256 else 256)
    nb = d_hy // cw
    fwd, inv = _dft_tables(seq)
    fmat = jnp.asarray(fwd, BF16)
    gmat = jnp.asarray(inv, BF16)
    col = lambda rows, off: pl.BlockSpec((rows, cw), lambda j, b: (0, off * nb + j))
    act = lambda off: pl.BlockSpec((seq, cw), lambda j, b: (b, off * nb + j))
    return pl.pallas_call(
        _hyena_kernel,
        grid=(nb, batch),
        in_specs=[act(0), act(1), act(2), col(3, 0), col(3, 1), col(3, 2), col(2, 0),
                  col(seq, 0), col(seq, 0), col(seq, 0), col(seq, 1), col(seq, 1), col(seq, 1),
                  pl.BlockSpec((2 * seq, seq), lambda j, b: (0, 0)),
                  pl.BlockSpec((seq, 2 * seq), lambda j, b: (0, 0))],
        out_specs=pl.BlockSpec((seq, cw), lambda j, b: (b, j)),
        out_shape=jax.ShapeDtypeStruct((batch * seq, d_hy), BF16),
        compiler_params=_cparams("parallel", "parallel"),
        name=f"hyena_{seq}",
    )(proj, proj, proj, short_w, short_w, short_w, hy_bias, kr, kiz, krn, kr, kiz, krn, fmat, gmat)


def _rope_tables(seq, dk):
    rows = seq // GRID_W
    row = jnp.repeat(jnp.arange(rows), GRID_W).astype(F32)
    col = jnp.tile(jnp.arange(GRID_W), rows).astype(F32)
    nfreq = dk // 4
    inv = ROPE_BASE ** (-jnp.arange(nfreq, dtype=F32) / nfreq)
    ang = jnp.concatenate([row[:, None] * inv, col[:, None] * inv], axis=-1)
    cos, sin = jnp.cos(ang), jnp.sin(ang)
    return jnp.concatenate([cos, cos], axis=-1), jnp.concatenate([-sin, sin], axis=-1)


def _retention_kernel(*refs, seq, hb, dk, use_rope, use_state, want_state):
    refs = list(refs)
    q_ref, k_ref, v_ref, g_ref, lg_ref, gn_ref = refs[:6]
    pos = 6
    if use_rope:
        cos_ref, sin_ref = refs[pos:pos + 2]
        pos += 2
    if use_state:
        s0_ref = refs[pos]
        pos += 1
    o_ref = refs[pos]
    pos += 1
    if want_state:
        st_ref = refs[pos]
        pos += 1
    d_ref = refs[pos]

    def log_gamma(hh, direction):
        return jnp.log(_sigmoid(lg_ref[hh, direction]))[:, 0:1]

    @pl.when(pl.program_id(1) == 0)
    def _():
        i = lax.broadcasted_iota(jnp.int32, (seq, seq), 0)
        j = lax.broadcasted_iota(jnp.int32, (seq, seq), 1)
        diff = (i - j).astype(F32)
        for hh in range(hb):
            fwd = jnp.where(diff >= 0, jnp.exp(log_gamma(hh, 0) * jnp.maximum(diff, 0.0)), 0.0)
            bwd = jnp.where(diff <= 0, jnp.exp(log_gamma(hh, 1) * jnp.maximum(-diff, 0.0)), 0.0)
            d_ref[hh] = fwd + bwd

    pos_f = lax.broadcasted_iota(jnp.int32, (seq, dk), 0).astype(F32)
    for hh in range(hb):
        sl = slice(hh * dk, (hh + 1) * dk)
        q = q_ref[:, sl]
        k = k_ref[:, sl]
        v = v_ref[:, sl]
        if use_rope:
            cos = cos_ref[...]
            sin = sin_ref[...]
            q = q * cos + pltpu.roll(q, dk // 2, 1) * sin
            k = k * cos + pltpu.roll(k, dk // 2, 1) * sin
        k = k * dk ** -0.5
        qb = q.astype(BF16)
        vb = v.astype(BF16)
        s = lax.dot_general(qb, k.astype(BF16), (((1,), (1,)), ((), ())), preferred_element_type=F32)
        o = _dot((s * d_ref[hh]).astype(BF16), vb)
        lf = log_gamma(hh, 0)
        lb = log_gamma(hh, 1)
        if use_state:
            o = o + _dot(qb, s0_ref[0, hh].astype(BF16)) * jnp.exp(lf * (pos_f + 1.0))
            o = o + _dot(qb, s0_ref[1, hh].astype(BF16)) * jnp.exp(lb * (seq - pos_f))
        if want_state:
            kf = (k * jnp.exp(lf * (seq - 1.0 - pos_f))).astype(BF16)
            kb = (k * jnp.exp(lb * pos_f)).astype(BF16)
            tn = (((0,), (0,)), ((), ()))
            st_ref[0, hh] = lax.dot_general(kf, vb, tn, preferred_element_type=F32)
            st_ref[1, hh] = lax.dot_general(kb, vb, tn, preferred_element_type=F32)
        mu = jnp.mean(o, axis=-1, keepdims=True)
        oc = o - mu
        var = jnp.mean(oc * oc, axis=-1, keepdims=True)
        gate = g_ref[:, sl]
        y = (oc * lax.rsqrt(var + GN_EPS)) * gn_ref[:, sl] * (gate * _sigmoid(gate))
        o_ref[:, sl] = y.astype(BF16)


def _retention(proj, batch, seq, col0, n_heads, dk, decay_logit, ret_gn, rope, state0, want_state):
    d_ret = n_heads * dk
    hb = n_heads if seq <= 256 else min(n_heads, 2)
    bw = hb * dk
    nhb = n_heads // hb
    lg = jnp.broadcast_to(decay_logit.T[:, :, None, None], (n_heads, 2, 1, V7X_LANES))
    act = lambda part: pl.BlockSpec((seq, bw), lambda h, b: (b, (col0 + part * d_ret) // bw + h))
    in_specs = [act(0), act(1), act(2), act(3),
                pl.BlockSpec((hb, 2, 1, V7X_LANES), lambda h, b: (h, 0, 0, 0)),
                pl.BlockSpec((1, bw), lambda h, b: (0, h))]
    args = [proj, proj, proj, proj, lg, ret_gn.reshape(1, d_ret)]
    if rope is not None:
        in_specs += [pl.BlockSpec((seq, dk), lambda h, b: (0, 0))] * 2
        args += list(rope)
    if state0 is not None:
        in_specs.append(pl.BlockSpec((None, 2, hb, dk, dk), lambda h, b: (b, 0, h, 0, 0)))
        args.append(state0)
    out_specs = [pl.BlockSpec((seq, bw), lambda h, b: (b, h))]
    out_shape = [jax.ShapeDtypeStruct((batch * seq, d_ret), BF16)]
    if want_state:
        out_specs.append(pl.BlockSpec((None, 2, hb, dk, dk), lambda h, b: (b, 0, h, 0, 0)))
        out_shape.append(jax.ShapeDtypeStruct((batch, 2, n_heads, dk, dk), F32))
    body = functools.partial(_retention_kernel, seq=seq, hb=hb, dk=dk, use_rope=rope is not None,
                             use_state=state0 is not None, want_state=want_state)
    return pl.pallas_call(
        body,
        grid=(nhb, batch),
        in_specs=in_specs,
        out_specs=out_specs,
        out_shape=out_shape,
        scratch_shapes=[pltpu.VMEM((hb, seq, seq), F32)],
        compiler_params=_cparams("parallel", "arbitrary"),
        name=f"retention_{seq}",
    )(*args)


def _merge_kernel(*refs, nblk):
    gate_refs = refs[:2 * nblk]
    (yhy_ref, yret_ref, x_ref, wbh_ref, wbr_ref, wo_ref, gpost_ref, gm_ref, o_ref) = refs[2 * nblk:]
    a = _dot(yhy_ref[...], wbh_ref[...])
    b = _dot(yret_ref[...], wbr_ref[...])
    wblk = gate_refs[0].shape[1]
    parts = []
    for kk in range(nblk):
        sl = slice(kk * wblk, (kk + 1) * wblk)
        g_hy = _sigmoid(gate_refs[kk][...])
        g_ret = _sigmoid(gate_refs[nblk + kk][...])
        parts.append((g_hy * a[:, sl] + g_ret * b[:, sl]).astype(BF16))
    merged = parts[0] if nblk == 1 else jnp.concatenate(parts, axis=1)
    out = _dot(merged, wo_ref[...])
    o_ref[...] = x_ref[...] + gm_ref[...] * _rms_scale(out, gpost_ref[...])


def _merge(proj, col0, y_hy, y_ret, x, w_br_hy, w_br_ret, w_out, g_post, gate_m, rows_per_mod, tm):
    m, d = x.shape
    wblk = math.gcd(col0, d)
    nblk = d // wblk
    gate_spec = lambda kk: pl.BlockSpec((tm, wblk), lambda i: (i, col0 // wblk + kk))
    const = lambda arr: _resident(arr.shape)
    row = lambda width: pl.BlockSpec((tm, width), lambda i: (i, 0))
    return pl.pallas_call(
        functools.partial(_merge_kernel, nblk=nblk),
        grid=(m // tm,),
        in_specs=[gate_spec(kk) for kk in range(2 * nblk)]
        + [row(y_hy.shape[1]), row(y_ret.shape[1]), row(d), const(w_br_hy), const(w_br_ret), const(w_out),
           pl.BlockSpec((1, d), lambda i: (0, 0)),
           pl.BlockSpec((None, 1, d), lambda i: ((i * tm) // rows_per_mod, 0, 0))],
        out_specs=row(d),
        out_shape=jax.ShapeDtypeStruct((m, d), F32),
        compiler_params=_cparams("parallel"),
        name="merge",
    )(*([proj] * (2 * nblk)), y_hy, y_ret, x, w_br_hy, w_br_ret, w_out, g_post, gate_m)


def _ffn_kernel(x_ref, gpre_ref, sc_ref, sh_ref, gate_ref, gpost_ref, wa_ref, wb_ref, ca_ref, cb_ref, wd_ref,
                o_ref, h_ref, *, seq):
    j = pl.program_id(1)

    @pl.when(j == 0)
    def _():
        h = _rms_scale(x_ref[...], gpre_ref[...]) * (1.0 + sc_ref[...]) + sh_ref[...]
        h_ref[...] = h.astype(BF16)

    tm = x_ref.shape[0]
    tf = wa_ref.shape[1]
    pos = lax.broadcasted_iota(jnp.int32, (tm, tf), 0) % seq
    first = pos == 0
    last = pos == seq - 1
    h = h_ref[...]
    a = _dwconv3_rows(_dot(h, wa_ref[...]), ca_ref[...], first, last)
    b = _dwconv3_rows(_dot(h, wb_ref[...]), cb_ref[...], first, last)
    gelu = 0.5 * a * (1.0 + jnp.tanh(math.sqrt(2.0 / math.pi) * (a + 0.044715 * (a * a * a))))
    part = _dot((gelu * b).astype(BF16), wd_ref[...])

    @pl.when(j == 0)
    def _():
        o_ref[...] = part

    @pl.when(j > 0)
    def _():
        o_ref[...] += part

    @pl.when(j == pl.num_programs(1) - 1)
    def _():
        o_ref[...] = x_ref[...] + gate_ref[...] * _rms_scale(o_ref[...], gpost_ref[...])


def _ffn(x, seq, g_pre, scale, shift, gate, g_post, w_up, conv_w, w_down, rows_per_mod, tm):
    m, d = x.shape
    d_ff = w_down.shape[0]
    tf = _tile(d_ff, 256)
    nf = d_ff // tf
    mod_spec = pl.BlockSpec((None, 1, d), lambda i, j: ((i * tm) // rows_per_mod, 0, 0))
    vec = pl.BlockSpec((1, d), lambda i, j: (0, 0))
    return pl.pallas_call(
        functools.partial(_ffn_kernel, seq=seq),
        grid=(m // tm, nf),
        in_specs=[pl.BlockSpec((tm, d), lambda i, j: (i, 0), pipeline_mode=pl.Buffered(1)),
                  vec, mod_spec, mod_spec, mod_spec, vec,
                  pl.BlockSpec((d, tf), lambda i, j: (0, j)),
                  pl.BlockSpec((d, tf), lambda i, j: (0, nf + j)),
                  pl.BlockSpec((3, tf), lambda i, j: (0, j)),
                  pl.BlockSpec((3, tf), lambda i, j: (0, nf + j)),
                  pl.BlockSpec((tf, d), lambda i, j: (j, 0))],
        out_specs=pl.BlockSpec((tm, d), lambda i, j: (i, 0)),
        out_shape=jax.ShapeDtypeStruct((m, d), F32),
        scratch_shapes=[pltpu.VMEM((tm, d), BF16)],
        compiler_params=_cparams("parallel", "arbitrary"),
        name="ffn",
    )(x, g_pre, scale, shift, gate, g_post, w_up, w_up, conv_w, conv_w, w_down)


def _block(x3, mod, rope, state0, want_state, p):
    batch, seq, d = x3.shape
    m = batch * seq
    x = x3.reshape(m, d)
    per_seq_mod = mod.shape[0] != 1
    rows_per_mod = seq if per_seq_mod else m
    tm = _row_tile(m, seq, per_seq_mod, 1024)
    tm_merge = _tile(tm, 256, unit=V7X_SUBLANES)
    shift_m, scale_m, gate_m, shift_f, scale_f, gate_f = (mod[:, i][:, None, :] for i in range(6))
    d_hy, d_ret, n_heads, dk = p["d_hy"], p["d_ret"], p["n_heads"], p["dk"]

    proj = _in_proj(x, p["g_pre_m"], scale_m, shift_m, p["w_in"], rows_per_mod, tm)
    tables = _hyena_filters(seq, p["hy_w1"], p["hy_b1"], p["hy_w2"], p["hy_b2"], p["hy_w3"], p["hy_b3"],
                            p["hy_freq"], p["hy_decay"], d_hy)
    y_hy = _hyena(proj, batch, seq, p["hy_short_w"], p["hy_bias"], tables, d_hy)
    ret = _retention(proj, batch, seq, 3 * d_hy, n_heads, dk, p["ret_decay_logit"], p["ret_gn"],
                     rope, state0, want_state)
    y_ret = ret[0]
    x = _merge(proj, 3 * d_hy + 4 * d_ret, y_hy, y_ret, x, p["w_br_hy"], p["w_br_ret"], p["w_out"],
               p["g_post_m"], gate_m, rows_per_mod, tm_merge)
    x = _ffn(x, seq, p["g_pre_f"], scale_f, shift_f, gate_f, p["g_post_f"], p["ffn_w_up"], p["ffn_conv"],
             p["ffn_w_down"], rows_per_mod, tm)
    return x.reshape(batch, seq, d), (ret[1] if want_state else None)


def kernel(x_prompt, x_sample, state_ret, c, c_ctx, w_ada, b_ada, norm_pre_mix, norm_post_mix, norm_pre_ffn,
           norm_post_ffn, w_in, hy_short_w, hy_w1, hy_b1, hy_w2, hy_b2, hy_w3, hy_b3, hy_freq, hy_decay, hy_bias,
           ret_decay_logit, ret_gn, w_br_hy, w_br_ret, w_out, ffn_w_up, ffn_conv, ffn_w_down):
    depth = w_in.shape[0]
    d = x_prompt.shape[-1]
    n_dec = x_sample.shape[0]
    n_heads, dk = state_ret.shape[3], state_ret.shape[4]
    d_hy = hy_bias.shape[-1]
    cc = jnp.zeros((V7X_SUBLANES, d), F32).at[0].set(c_ctx).at[1:1 + n_dec].set(c)
    rope = _rope_tables(x_sample.shape[1], dk)
    x_p, x_s = x_prompt, x_sample
    states = []
    for l in range(depth):
        p = dict(
            d_hy=d_hy, d_ret=n_heads * dk, n_heads=n_heads, dk=dk,
            g_pre_m=norm_pre_mix[l][None], g_post_m=norm_post_mix[l][None],
            g_pre_f=norm_pre_ffn[l][None], g_post_f=norm_post_ffn[l][None],
            w_in=w_in[l].astype(BF16), hy_short_w=hy_short_w[l],
            hy_w1=hy_w1[l], hy_b1=hy_b1[l], hy_w2=hy_w2[l], hy_b2=hy_b2[l], hy_w3=hy_w3[l], hy_b3=hy_b3[l],
            hy_freq=hy_freq[l], hy_decay=hy_decay[l], hy_bias=hy_bias[l],
            ret_decay_logit=ret_decay_logit[l], ret_gn=ret_gn[l],
            w_br_hy=w_br_hy[l].astype(BF16), w_br_ret=w_br_ret[l].astype(BF16), w_out=w_out[l].astype(BF16),
            ffn_w_up=ffn_w_up[l].astype(BF16), ffn_conv=ffn_conv[l], ffn_w_down=ffn_w_down[l].astype(BF16),
        )
        mod = _ada_mod(cc, w_ada[l], b_ada[l][None]).reshape(V7X_SUBLANES, 6, d)
        x_p, st = _block(x_p, mod[0:1], None, None, True, p)
        x_s, _ = _block(x_s, mod[1:1 + n_dec], rope, state_ret[:, l], False, p)
        states.append(st)
    return x_p, x_s, jnp.stack(states, axis=1)
```

```python
import functools
import math

import jax
import jax.numpy as jnp
import numpy as np
from jax import lax
from jax.experimental import pallas as pl
from jax.experimental.pallas import tpu as pltpu

F32 = jnp.float32
BF16 = jnp.bfloat16

RMS_EPS = 1e-6
GN_EPS = 1e-5
FILTER_EPS = 1e-6
HY_BANDS = 16
GRID_W = 64
ROPE_BASE = 10000.0

V7X_VMEM_LIMIT_BYTES = 56 * 1024 * 1024
V7X_LANES = 128
V7X_SUBLANES = 8


def _cparams(*sem):
    return pltpu.CompilerParams(dimension_semantics=sem, vmem_limit_bytes=V7X_VMEM_LIMIT_BYTES)


def _tile(n, target, unit=V7X_LANES):
    if n <= target:
        return n
    best = unit
    for t in range(unit, target + 1, unit):
        if n % t == 0:
            best = t
    assert n % best == 0, (n, target, unit)
    return best


def _row_tile(m, seq, per_seq_mod, target):
    if per_seq_mod or seq >= target:
        return seq
    return seq * _tile(m // seq, target // seq, unit=1)


def _resident(shape):
    return pl.BlockSpec(shape, lambda *_: (0,) * len(shape), pipeline_mode=pl.Buffered(1))


def _sigmoid(x):
    return 1.0 / (1.0 + jnp.exp(-x))


def _rms_scale(x, g):
    ms = jnp.mean(x * x, axis=-1, keepdims=True)
    return (x * lax.rsqrt(ms + RMS_EPS)) * g


def _dot(a, b):
    return jnp.dot(a, b, preferred_element_type=F32)


def _dot_exact(a, b):
    return jnp.dot(a, b, preferred_element_type=F32, precision=lax.Precision.HIGHEST)


def _dwconv3_rows(x, w, first, last):
    rows = x.shape[0]
    prev = jnp.where(first, 0.0, pltpu.roll(x, 1, 0))
    nxt = jnp.where(last, 0.0, pltpu.roll(x, rows - 1, 0))
    return prev * w[0:1] + x * w[1:2] + nxt * w[2:3]


def _ada_kernel(cc_ref, w_ref, b_ref, o_ref):
    cc = cc_ref[...]
    s = cc * _sigmoid(cc)
    o_ref[...] = _dot(s.astype(BF16), w_ref[...].astype(BF16)) + b_ref[...]


def _ada_mod(cc, w, b):
    d, n = w.shape
    tn = _tile(n, 1024)
    return pl.pallas_call(
        _ada_kernel,
        grid=(n // tn,),
        in_specs=[pl.BlockSpec((V7X_SUBLANES, d), lambda j: (0, 0)),
                  pl.BlockSpec((d, tn), lambda j: (0, j)),
                  pl.BlockSpec((1, tn), lambda j: (0, j))],
        out_specs=pl.BlockSpec((V7X_SUBLANES, tn), lambda j: (0, j)),
        out_shape=jax.ShapeDtypeStruct((V7X_SUBLANES, n), F32),
        compiler_params=_cparams("parallel"),
        name="ada_mod",
    )(cc, w, b)


def _in_proj_kernel(x_ref, g_ref, sc_ref, sh_ref, w_ref, o_ref, h_ref):
    @pl.when(pl.program_id(1) == 0)
    def _():
        h = _rms_scale(x_ref[...], g_ref[...]) * (1.0 + sc_ref[...]) + sh_ref[...]
        h_ref[...] = h.astype(BF16)

    o_ref[...] = _dot(h_ref[...], w_ref[...])


def _in_proj(x, g, scale, shift, w, rows_per_mod, tm):
    m, d = x.shape
    n = w.shape[1]
    tn = _tile(n, 1024)
    mod_spec = pl.BlockSpec((None, 1, d), lambda i, j: ((i * tm) // rows_per_mod, 0, 0))
    return pl.pallas_call(
        _in_proj_kernel,
        grid=(m // tm, n // tn),
        in_specs=[pl.BlockSpec((tm, d), lambda i, j: (i, 0)),
                  pl.BlockSpec((1, d), lambda i, j: (0, 0)),
                  mod_spec, mod_spec,
                  pl.BlockSpec((d, tn), lambda i, j: (0, j))],
        out_specs=pl.BlockSpec((tm, tn), lambda i, j: (i, j)),
        out_shape=jax.ShapeDtypeStruct((m, n), F32),
        scratch_shapes=[pltpu.VMEM((tm, d), BF16)],
        compiler_params=_cparams("parallel", "arbitrary"),
        name="in_proj",
    )(x, g, scale, shift, w)


def _dft_tables(seq):
    n_fft = 2 * seq
    idx = np.arange(seq)
    ang = 2.0 * np.pi * ((idx[:, None] * idx[None, :]) % n_fft) / n_fft
    cos = np.cos(ang)
    msin = -np.sin(ang)
    sign = np.where(idx % 2 == 0, 1.0, -1.0)
    msin[0, :] = sign
    fwd = np.concatenate([cos, msin], axis=0)
    wgt = np.full((seq,), 2.0 / n_fft)
    wgt[0] = 1.0 / n_fft
    inv_re = cos.T * wgt[None, :]
    inv_im = msin.T * wgt[None, :]
    inv_im[:, 0] = sign / n_fft
    inv = np.concatenate([inv_re, inv_im], axis=1)
    return fwd, inv


def _filter_feats(seq):
    n = np.arange(seq, dtype=np.float64)
    t = n / seq
    f = np.linspace(1e-4, HY_BANDS - 1, HY_BANDS)
    w = 2.0 * math.pi * n / seq
    z = np.concatenate([t[:, None], np.cos(w[:, None] * f), np.sin(w[:, None] * f)], axis=-1)
    out = np.zeros((seq, V7X_LANES), np.float32)
    out[:, :z.shape[1]] = z
    return out


def _filter_kernel(z_ref, w1_ref, b1_ref, w2_ref, b2_ref, fr_ref, w3f_ref, w3b_ref, b3f_ref, b3b_ref,
                   decf_ref, decb_ref, cmat_ref, smat_ref, kr_ref, kiz_ref, krn_ref, h2_ref):
    @pl.when(pl.program_id(0) == 0)
    def _():
        h1 = jnp.sin(fr_ref[0:1, :] * (_dot_exact(z_ref[...], w1_ref[...]) + b1_ref[...]))
        h2_ref[...] = jnp.sin(fr_ref[1:2, :] * (_dot_exact(h1, w2_ref[...]) + b2_ref[...]))

    h2 = h2_ref[...]
    seq, cw = kr_ref.shape
    t = z_ref[:, 0:1]
    row = lax.broadcasted_iota(jnp.int32, (seq, cw), 0)
    hf = (_dot_exact(h2, w3f_ref[...]) + b3f_ref[...]) * jnp.exp(-t * jnp.abs(decf_ref[...]))
    hb = (_dot_exact(h2, w3b_ref[...]) + b3b_ref[...]) * jnp.exp(-t * jnp.abs(decb_ref[...]))
    hb = jnp.where(row == 0, 0.0, hb)
    norm = (jnp.sum(jnp.abs(hf), axis=0, keepdims=True)
            + jnp.sum(jnp.abs(hb), axis=0, keepdims=True) + FILTER_EPS)
    inv = 1.0 / norm
    even = (hf + hb) * inv
    odd = (hf - hb) * inv
    kr = _dot_exact(cmat_ref[...], even)
    ki = _dot_exact(smat_ref[...], odd)
    nyq = jnp.sum(jnp.where((row & 1) == 0, even, -even), axis=0, keepdims=True)
    kr_ref[...] = kr
    kiz_ref[...] = jnp.where(row == 0, 0.0, ki)
    krn_ref[...] = jnp.where(row == 0, nyq, kr)


def _hyena_filters(seq, w1, b1, w2, b2, w3, b3, freq, decay, d_hy):
    fh = w1.shape[1]
    pad = V7X_LANES
    w1p = jnp.zeros((pad, pad), F32).at[:w1.shape[0], :fh].set(w1)
    b1p = jnp.zeros((1, pad), F32).at[0, :fh].set(b1)
    w2p = jnp.zeros((pad, pad), F32).at[:fh, :fh].set(w2)
    b2p = jnp.zeros((1, pad), F32).at[0, :fh].set(b2)
    frp = jnp.zeros((2, pad), F32).at[:, :fh].set(freq)
    ncol = w3.shape[1] // 2
    w3p = jnp.zeros((pad, 2 * ncol), F32).at[:fh].set(w3)
    b3r = b3.reshape(1, 2 * ncol)
    dec = decay.reshape(1, 2 * ncol)
    fwd, _ = _dft_tables(seq)
    cmat = jnp.asarray(fwd[:seq], F32)
    smat = jnp.asarray(fwd[seq:], F32)
    z = jnp.asarray(_filter_feats(seq))
    cw = _tile(ncol, 256)
    nb = ncol // cw
    full = lambda shape: pl.BlockSpec(shape, lambda j: (0, 0))
    colf = lambda rows: pl.BlockSpec((rows, cw), lambda j: (0, j))
    colb = lambda rows: pl.BlockSpec((rows, cw), lambda j: (0, nb + j))
    out = jax.ShapeDtypeStruct((seq, ncol), F32)
    return pl.pallas_call(
        _filter_kernel,
        grid=(nb,),
        in_specs=[full((seq, pad)), full((pad, pad)), full((1, pad)), full((pad, pad)), full((1, pad)),
                  full((2, pad)), colf(pad), colb(pad), colf(1), colb(1), colf(1), colb(1),
                  full((seq, seq)), full((seq, seq))],
        out_specs=[colf(seq), colf(seq), colf(seq)],
        out_shape=[out, out, out],
        scratch_shapes=[pltpu.VMEM((seq, pad), F32)],
        compiler_params=_cparams("arbitrary"),
        name=f"hy_filter_{seq}",
    )(z, w1p, b1p, w2p, b2p, frp, w3p, w3p, b3r, b3r, dec, dec, cmat, smat)


def _hyena_kernel(x1_ref, x2_ref, v_ref, w1_ref, w2_ref, wv_ref, hb_ref,
                  kr0_ref, kiz0_ref, krn0_ref, kr1_ref, kiz1_ref, krn1_ref, f_ref, g_ref, o_ref):
    seq, cw = o_ref.shape
    row = lax.broadcasted_iota(jnp.int32, (seq, cw), 0)
    first = row == 0
    last = row == seq - 1
    x1 = _dwconv3_rows(x1_ref[...], w1_ref[...], first, last)
    x2 = _dwconv3_rows(x2_ref[...], w2_ref[...], first, last)
    v = _dwconv3_rows(v_ref[...], wv_ref[...], first, last)

    def long_conv(u, kr_ref, kiz_ref, krn_ref, bias):
        spec = _dot(f_ref[...], u.astype(BF16))
        ur = spec[:seq]
        ui = spec[seq:]
        kiz = kiz_ref[...]
        yr = ur * kr_ref[...] - ui * kiz
        yi = ur * kiz + ui * krn_ref[...]
        y = _dot(g_ref[:, :seq], yr.astype(BF16)) + _dot(g_ref[:, seq:], yi.astype(BF16))
        return y + u * bias

    z = x1 * long_conv(v, kr0_ref, kiz0_ref, krn0_ref, hb_ref[0:1])
    o_ref[...] = (x2 * long_conv(z, kr1_ref, kiz1_ref, krn1_ref, hb_ref[1:2])).astype(BF16)


def _hyena(proj, batch, seq, short_w, hy_bias, tables, d_hy):
    kr, kiz, krn = tables
    cw = _tile(d_hy, 512 if seq <= 256 else 256)
    nb = d_hy // cw
    fwd, inv = _dft_tables(seq)
    fmat = jnp.asarray(fwd, F32).astype(BF16)
    gmat = jnp.asarray(inv, F32).astype(BF16)
    col = lambda rows, off: pl.BlockSpec((rows, cw), lambda j, b: (0, off * nb + j))
    act = lambda off: pl.BlockSpec((seq, cw), lambda j, b: (b, off * nb + j))
    return pl.pallas_call(
        _hyena_kernel,
        grid=(nb, batch),
        in_specs=[act(0), act(1), act(2), col(3, 0), col(3, 1), col(3, 2), col(2, 0),
                  col(seq, 0), col(seq, 0), col(seq, 0), col(seq, 1), col(seq, 1), col(seq, 1),
                  pl.BlockSpec((2 * seq, seq), lambda j, b: (0, 0)),
                  pl.BlockSpec((seq, 2 * seq), lambda j, b: (0, 0))],
        out_specs=pl.BlockSpec((seq, cw), lambda j, b: (b, j)),
        out_shape=jax.ShapeDtypeStruct((batch * seq, d_hy), BF16),
        compiler_params=_cparams("parallel", "parallel"),
        name=f"hyena_{seq}",
    )(proj, proj, proj, short_w, short_w, short_w, hy_bias, kr, kiz, krn, kr, kiz, krn, fmat, gmat)


def _rope_tables(seq, dk):
    rows = seq // GRID_W
    row = jnp.repeat(jnp.arange(rows), GRID_W).astype(F32)
    col = jnp.tile(jnp.arange(GRID_W), rows).astype(F32)
    nfreq = dk // 4
    inv = ROPE_BASE ** (-jnp.arange(nfreq, dtype=F32) / nfreq)
    ang = jnp.concatenate([row[:, None] * inv, col[:, None] * inv], axis=-1)
    cos, sin = jnp.cos(ang), jnp.sin(ang)
    return jnp.concatenate([cos, cos], axis=-1), jnp.concatenate([-sin, sin], axis=-1)


def _retention_kernel(*refs, seq, hb, dk, use_rope, use_state, want_state):
    refs = list(refs)
    q_ref, k_ref, v_ref, g_ref, lg_ref, gn_ref = refs[:6]
    pos = 6
    if use_rope:
        cos_ref, sin_ref = refs[pos:pos + 2]
        pos += 2
    if use_state:
        s0_ref = refs[pos]
        pos += 1
    o_ref = refs[pos]
    pos += 1
    if want_state:
        st_ref = refs[pos]
        pos += 1
    d_ref = refs[pos]

    def log_gamma(hh, direction):
        return jnp.log(_sigmoid(lg_ref[hh, direction]))[:, 0:1]

    @pl.when(pl.program_id(1) == 0)
    def _():
        i = lax.broadcasted_iota(jnp.int32, (seq, seq), 0)
        j = lax.broadcasted_iota(jnp.int32, (seq, seq), 1)
        diff = (i - j).astype(F32)
        for hh in range(hb):
            fwd = jnp.where(diff >= 0, jnp.exp(log_gamma(hh, 0) * jnp.maximum(diff, 0.0)), 0.0)
            bwd = jnp.where(diff <= 0, jnp.exp(log_gamma(hh, 1) * jnp.maximum(-diff, 0.0)), 0.0)
            d_ref[hh] = fwd + bwd

    pos_f = lax.broadcasted_iota(jnp.int32, (seq, dk), 0).astype(F32)
    for hh in range(hb):
        sl = slice(hh * dk, (hh + 1) * dk)
        q = q_ref[:, sl]
        k = k_ref[:, sl]
        v = v_ref[:, sl]
        if use_rope:
            cos = cos_ref[...]
            sin = sin_ref[...]
            q = q * cos + pltpu.roll(q, dk // 2, 1) * sin
            k = k * cos + pltpu.roll(k, dk // 2, 1) * sin
        k = k * dk ** -0.5
        qb = q.astype(BF16)
        vb = v.astype(BF16)
        s = lax.dot_general(qb, k.astype(BF16), (((1,), (1,)), ((), ())), preferred_element_type=F32)
        o = _dot((s * d_ref[hh]).astype(BF16), vb)
        lf = log_gamma(hh, 0)
        lb = log_gamma(hh, 1)
        if use_state:
            o = o + _dot(qb, s0_ref[0, hh].astype(BF16)) * jnp.exp(lf * (pos_f + 1.0))
            o = o + _dot(qb, s0_ref[1, hh].astype(BF16)) * jnp.exp(lb * (seq - pos_f))
        if want_state:
            kf = (k * jnp.exp(lf * (seq - 1.0 - pos_f))).astype(BF16)
            kb = (k * jnp.exp(lb * pos_f)).astype(BF16)
            tn = (((0,), (0,)), ((), ()))
            st_ref[0, hh] = lax.dot_general(kf, vb, tn, preferred_element_type=F32)
            st_ref[1, hh] = lax.dot_general(kb, vb, tn, preferred_element_type=F32)
        mu = jnp.mean(o, axis=-1, keepdims=True)
        oc = o - mu
        var = jnp.mean(oc * oc, axis=-1, keepdims=True)
        gate = g_ref[:, sl]
        y = (oc * lax.rsqrt(var + GN_EPS)) * gn_ref[:, sl] * (gate * _sigmoid(gate))
        o_ref[:, sl] = y.astype(BF16)


def _retention(proj, batch, seq, col0, n_heads, dk, decay_logit, ret_gn, rope, state0, want_state):
    d_ret = n_heads * dk
    hb = n_heads if seq <= 256 else min(n_heads, 2)
    bw = hb * dk
    nhb = n_heads // hb
    lg = jnp.broadcast_to(decay_logit.T[:, :, None, None], (n_heads, 2, 1, V7X_LANES))
    act = lambda part: pl.BlockSpec((seq, bw), lambda h, b: (b, (col0 + part * d_ret) // bw + h))
    in_specs = [act(0), act(1), act(2), act(3),
                pl.BlockSpec((hb, 2, 1, V7X_LANES), lambda h, b: (h, 0, 0, 0)),
                pl.BlockSpec((1, bw), lambda h, b: (0, h))]
    args = [proj, proj, proj, proj, lg, ret_gn.reshape(1, d_ret)]
    if rope is not None:
        in_specs += [pl.BlockSpec((seq, dk), lambda h, b: (0, 0))] * 2
        args += list(rope)
    if state0 is not None:
        in_specs.append(pl.BlockSpec((None, 2, hb, dk, dk), lambda h, b: (b, 0, h, 0, 0)))
        args.append(state0)
    out_specs = [pl.BlockSpec((seq, bw), lambda h, b: (b, h))]
    out_shape = [jax.ShapeDtypeStruct((batch * seq, d_ret), BF16)]
    if want_state:
        out_specs.append(pl.BlockSpec((None, 2, hb, dk, dk), lambda h, b: (b, 0, h, 0, 0)))
        out_shape.append(jax.ShapeDtypeStruct((batch, 2, n_heads, dk, dk), F32))
    body = functools.partial(_retention_kernel, seq=seq, hb=hb, dk=dk, use_rope=rope is not None,
                             use_state=state0 is not None, want_state=want_state)
    return pl.pallas_call(
        body,
        grid=(nhb, batch),
        in_specs=in_specs,
        out_specs=out_specs,
        out_shape=out_shape,
        scratch_shapes=[pltpu.VMEM((hb, seq, seq), F32)],
        compiler_params=_cparams("parallel", "arbitrary"),
        name=f"retention_{seq}",
    )(*args)


def _merge_kernel(*refs, nblk):
    gate_refs = refs[:2 * nblk]
    (yhy_ref, yret_ref, x_ref, wbh_ref, wbr_ref, wo_ref, gpost_ref, gm_ref, o_ref) = refs[2 * nblk:]
    a = _dot(yhy_ref[...], wbh_ref[...])
    b = _dot(yret_ref[...], wbr_ref[...])
    wblk = gate_refs[0].shape[1]
    parts = []
    for kk in range(nblk):
        sl = slice(kk * wblk, (kk + 1) * wblk)
        g_hy = _sigmoid(gate_refs[kk][...])
        g_ret = _sigmoid(gate_refs[nblk + kk][...])
        parts.append((g_hy * a[:, sl] + g_ret * b[:, sl]).astype(BF16))
    merged = parts[0] if nblk == 1 else jnp.concatenate(parts, axis=1)
    out = _dot(merged, wo_ref[...])
    o_ref[...] = x_ref[...] + gm_ref[...] * _rms_scale(out, gpost_ref[...])


def _merge(proj, col0, y_hy, y_ret, x, w_br_hy, w_br_ret, w_out, g_post, gate_m, rows_per_mod, tm):
    m, d = x.shape
    wblk = math.gcd(col0, d)
    nblk = d // wblk
    gate_spec = lambda kk: pl.BlockSpec((tm, wblk), lambda i: (i, col0 // wblk + kk))
    const = lambda arr: _resident(arr.shape)
    row = lambda width: pl.BlockSpec((tm, width), lambda i: (i, 0))
    return pl.pallas_call(
        functools.partial(_merge_kernel, nblk=nblk),
        grid=(m // tm,),
        in_specs=[gate_spec(kk) for kk in range(2 * nblk)]
        + [row(y_hy.shape[1]), row(y_ret.shape[1]), row(d), const(w_br_hy), const(w_br_ret), const(w_out),
           pl.BlockSpec((1, d), lambda i: (0, 0)),
           pl.BlockSpec((None, 1, d), lambda i: ((i * tm) // rows_per_mod, 0, 0))],
        out_specs=row(d),
        out_shape=jax.ShapeDtypeStruct((m, d), F32),
        compiler_params=_cparams("parallel"),
        name="merge",
    )(*([proj] * (2 * nblk)), y_hy, y_ret, x, w_br_hy, w_br_ret, w_out, g_post, gate_m)


def _ffn_kernel(x_ref, gpre_ref, sc_ref, sh_ref, gate_ref, gpost_ref, wa_ref, wb_ref, ca_ref, cb_ref, wd_ref,
                o_ref, h_ref, *, seq, rc, halo):
    j = pl.program_id(1)

    tm = x_ref.shape[0]
    tf = wa_ref.shape[1]

    @pl.when(j == 0)
    def _():
        h = _rms_scale(x_ref[...], gpre_ref[...]) * (1.0 + sc_ref[...]) + sh_ref[...]
        h_ref[halo:halo + tm] = h.astype(BF16)
        if halo:
            zeros = jnp.zeros((halo, h_ref.shape[1]), BF16)
            h_ref[0:halo] = zeros
            h_ref[halo + tm:] = zeros
        o_ref[...] = jnp.zeros_like(o_ref)

    rows = rc + 2 * halo
    for c in range(tm // rc):
        r0 = c * rc
        pos = (lax.broadcasted_iota(jnp.int32, (rows, tf), 0) + (r0 - halo)) % seq
        first = pos == 0
        last = pos == seq - 1
        h = h_ref[r0:r0 + rows]
        a = _dwconv3_rows(_dot(h, wa_ref[...]), ca_ref[...], first, last)
        b = _dwconv3_rows(_dot(h, wb_ref[...]), cb_ref[...], first, last)
        gelu = 0.5 * a * (1.0 + jnp.tanh(math.sqrt(2.0 / math.pi) * (a + 0.044715 * (a * a * a))))
        act = (gelu * b)[halo:halo + rc].astype(BF16)
        o_ref[r0:r0 + rc] += _dot(act, wd_ref[...])

    @pl.when(j == pl.num_programs(1) - 1)
    def _():
        o_ref[...] = x_ref[...] + gate_ref[...] * _rms_scale(o_ref[...], gpost_ref[...])


V7X_BF16_ROW_TILE = 2 * V7X_SUBLANES


def _ffn(x, seq, g_pre, scale, shift, gate, g_post, w_up, conv_w, w_down, rows_per_mod, tm):
    m, d = x.shape
    d_ff = w_down.shape[0]
    tf = _tile(d_ff, 512)
    nf = d_ff // tf
    rc = _tile(tm, 256, unit=V7X_BF16_ROW_TILE)
    halo = 0 if rc % seq == 0 else V7X_BF16_ROW_TILE
    mod_spec = pl.BlockSpec((None, 1, d), lambda i, j: ((i * tm) // rows_per_mod, 0, 0))
    vec = pl.BlockSpec((1, d), lambda i, j: (0, 0))
    return pl.pallas_call(
        functools.partial(_ffn_kernel, seq=seq, rc=rc, halo=halo),
        grid=(m // tm, nf),
        in_specs=[pl.BlockSpec((tm, d), lambda i, j: (i, 0), pipeline_mode=pl.Buffered(1)),
                  vec, mod_spec, mod_spec, mod_spec, vec,
                  pl.BlockSpec((d, tf), lambda i, j: (0, j)),
                  pl.BlockSpec((d, tf), lambda i, j: (0, nf + j)),
                  pl.BlockSpec((3, tf), lambda i, j: (0, j)),
                  pl.BlockSpec((3, tf), lambda i, j: (0, nf + j)),
                  pl.BlockSpec((tf, d), lambda i, j: (j, 0))],
        out_specs=pl.BlockSpec((tm, d), lambda i, j: (i, 0)),
        out_shape=jax.ShapeDtypeStruct((m, d), F32),
        scratch_shapes=[pltpu.VMEM((tm + 2 * halo, d), BF16)],
        compiler_params=_cparams("parallel", "arbitrary"),
        name="ffn",
    )(x, g_pre, scale, shift, gate, g_post, w_up, w_up, conv_w, conv_w, w_down)


def _block(x3, mod, rope, state0, want_state, p):
    batch, seq, d = x3.shape
    m = batch * seq
    x = x3.reshape(m, d)
    per_seq_mod = mod.shape[0] != 1
    rows_per_mod = seq if per_seq_mod else m
    tm = _row_tile(m, seq, per_seq_mod, 1024)
    tm_merge = _tile(tm, 256, unit=V7X_SUBLANES)
    shift_m, scale_m, gate_m, shift_f, scale_f, gate_f = (mod[:, i][:, None, :] for i in range(6))
    d_hy, d_ret, n_heads, dk = p["d_hy"], p["d_ret"], p["n_heads"], p["dk"]

    proj = _in_proj(x, p["g_pre_m"], scale_m, shift_m, p["w_in"], rows_per_mod, tm)
    tables = _hyena_filters(seq, p["hy_w1"], p["hy_b1"], p["hy_w2"], p["hy_b2"], p["hy_w3"], p["hy_b3"],
                            p["hy_freq"], p["hy_decay"], d_hy)
    y_hy = _hyena(proj, batch, seq, p["hy_short_w"], p["hy_bias"], tables, d_hy)
    ret = _retention(proj, batch, seq, 3 * d_hy, n_heads, dk, p["ret_decay_logit"], p["ret_gn"],
                     rope, state0, want_state)
    y_ret = ret[0]
    x = _merge(proj, 3 * d_hy + 4 * d_ret, y_hy, y_ret, x, p["w_br_hy"], p["w_br_ret"], p["w_out"],
               p["g_post_m"], gate_m, rows_per_mod, tm_merge)
    x = _ffn(x, seq, p["g_pre_f"], scale_f, shift_f, gate_f, p["g_post_f"], p["ffn_w_up"], p["ffn_conv"],
             p["ffn_w_down"], rows_per_mod, tm)
    return x.reshape(batch, seq, d), (ret[1] if want_state else None)


def kernel(x_prompt, x_sample, state_ret, c, c_ctx, w_ada, b_ada, norm_pre_mix, norm_post_mix, norm_pre_ffn,
           norm_post_ffn, w_in, hy_short_w, hy_w1, hy_b1, hy_w2, hy_b2, hy_w3, hy_b3, hy_freq, hy_decay, hy_bias,
           ret_decay_logit, ret_gn, w_br_hy, w_br_ret, w_out, ffn_w_up, ffn_conv, ffn_w_down):
    depth = w_in.shape[0]
    d = x_prompt.shape[-1]
    n_dec = x_sample.shape[0]
    n_heads, dk = state_ret.shape[3], state_ret.shape[4]
    d_hy = hy_bias.shape[-1]
    cc = jnp.zeros((V7X_SUBLANES, d), F32).at[0].set(c_ctx).at[1:1 + n_dec].set(c)
    rope = _rope_tables(x_sample.shape[1], dk)
    x_p, x_s = x_prompt, x_sample
    states = []
    for l in range(depth):
        p = dict(
            d_hy=d_hy, d_ret=n_heads * dk, n_heads=n_heads, dk=dk,
            g_pre_m=norm_pre_mix[l][None], g_post_m=norm_post_mix[l][None],
            g_pre_f=norm_pre_ffn[l][None], g_post_f=norm_post_ffn[l][None],
            w_in=w_in[l].astype(BF16), hy_short_w=hy_short_w[l],
            hy_w1=hy_w1[l], hy_b1=hy_b1[l], hy_w2=hy_w2[l], hy_b2=hy_b2[l], hy_w3=hy_w3[l], hy_b3=hy_b3[l],
            hy_freq=hy_freq[l], hy_decay=hy_decay[l], hy_bias=hy_bias[l],
            ret_decay_logit=ret_decay_logit[l], ret_gn=ret_gn[l],
            w_br_hy=w_br_hy[l].astype(BF16), w_br_ret=w_br_ret[l].astype(BF16), w_out=w_out[l].astype(BF16),
            ffn_w_up=ffn_w_up[l].astype(BF16), ffn_conv=ffn_conv[l], ffn_w_down=ffn_w_down[l].astype(BF16),
        )
        mod = _ada_mod(cc, w_ada[l], b_ada[l][None]).reshape(V7X_SUBLANES, 6, d)
        x_p, st = _block(x_p, mod[0:1], None, None, True, p)
        x_s, _ = _block(x_s, mod[1:1 + n_dec], rope, state_ret[:, l], False, p)
        states.append(st)
    return x_p, x_s, jnp.stack(states, axis=1)
```

```python
import functools
import math

import jax
import jax.numpy as jnp
import numpy as np
from jax import lax
from jax.experimental import pallas as pl
from jax.experimental.pallas import tpu as pltpu

F32 = jnp.float32
BF16 = jnp.bfloat16

RMS_EPS = 1e-6
GN_EPS = 1e-5
FILTER_EPS = 1e-6
HY_BANDS = 16
GRID_W = 64
ROPE_BASE = 10000.0

V7X_VMEM_LIMIT_BYTES = 56 * 1024 * 1024
V7X_LANES = 128
V7X_SUBLANES = 8


def _cparams(*sem):
    return pltpu.CompilerParams(dimension_semantics=sem, vmem_limit_bytes=V7X_VMEM_LIMIT_BYTES)


def _tile(n, target, unit=V7X_LANES):
    if n <= target:
        return n
    best = unit
    for t in range(unit, target + 1, unit):
        if n % t == 0:
            best = t
    assert n % best == 0, (n, target, unit)
    return best


def _row_tile(m, seq, per_seq_mod, target):
    if per_seq_mod or seq >= target:
        return seq
    return seq * _tile(m // seq, target // seq, unit=1)


def _resident(shape):
    return pl.BlockSpec(shape, lambda *_: (0,) * len(shape), pipeline_mode=pl.Buffered(1))


def _sigmoid(x):
    return 1.0 / (1.0 + jnp.exp(-x))


def _rms_scale(x, g):
    ms = jnp.mean(x * x, axis=-1, keepdims=True)
    return (x * lax.rsqrt(ms + RMS_EPS)) * g


def _dot(a, b):
    return jnp.dot(a, b, preferred_element_type=F32)


def _dot_exact(a, b):
    return jnp.dot(a, b, preferred_element_type=F32, precision=lax.Precision.HIGHEST)


def _dwconv3_rows(x, w, first, last):
    rows = x.shape[0]
    prev = jnp.where(first, 0.0, pltpu.roll(x, 1, 0))
    nxt = jnp.where(last, 0.0, pltpu.roll(x, rows - 1, 0))
    return prev * w[0:1] + x * w[1:2] + nxt * w[2:3]


def _ada_kernel(cc_ref, w_ref, b_ref, o_ref):
    cc = cc_ref[...]
    s = cc * _sigmoid(cc)
    o_ref[...] = _dot(s.astype(BF16), w_ref[...].astype(BF16)) + b_ref[...]


def _ada_mod(cc, w, b):
    d, n = w.shape
    tn = _tile(n, 1024)
    return pl.pallas_call(
        _ada_kernel,
        grid=(n // tn,),
        in_specs=[pl.BlockSpec((V7X_SUBLANES, d), lambda j: (0, 0)),
                  pl.BlockSpec((d, tn), lambda j: (0, j)),
                  pl.BlockSpec((1, tn), lambda j: (0, j))],
        out_specs=pl.BlockSpec((V7X_SUBLANES, tn), lambda j: (0, j)),
        out_shape=jax.ShapeDtypeStruct((V7X_SUBLANES, n), F32),
        compiler_params=_cparams("parallel"),
        name="ada_mod",
    )(cc, w, b)


def _in_proj_kernel(x_ref, g_ref, sc_ref, sh_ref, w_ref, o_ref, h_ref):
    @pl.when(pl.program_id(1) == 0)
    def _():
        h = _rms_scale(x_ref[...], g_ref[...]) * (1.0 + sc_ref[...]) + sh_ref[...]
        h_ref[...] = h.astype(BF16)

    o_ref[...] = _dot(h_ref[...], w_ref[...])


def _in_proj(x, g, scale, shift, w, rows_per_mod, tm):
    m, d = x.shape
    n = w.shape[1]
    tn = _tile(n, 1024)
    mod_spec = pl.BlockSpec((None, 1, d), lambda i, j: ((i * tm) // rows_per_mod, 0, 0))
    return pl.pallas_call(
        _in_proj_kernel,
        grid=(m // tm, n // tn),
        in_specs=[pl.BlockSpec((tm, d), lambda i, j: (i, 0)),
                  pl.BlockSpec((1, d), lambda i, j: (0, 0)),
                  mod_spec, mod_spec,
                  pl.BlockSpec((d, tn), lambda i, j: (0, j))],
        out_specs=pl.BlockSpec((tm, tn), lambda i, j: (i, j)),
        out_shape=jax.ShapeDtypeStruct((m, n), F32),
        scratch_shapes=[pltpu.VMEM((tm, d), BF16)],
        compiler_params=_cparams("parallel", "arbitrary"),
        name="in_proj",
    )(x, g, scale, shift, w)


def _dft_tables(seq):
    n_fft = 2 * seq
    idx = np.arange(seq)
    ang = 2.0 * np.pi * ((idx[:, None] * idx[None, :]) % n_fft) / n_fft
    cos = np.cos(ang)
    msin = -np.sin(ang)
    sign = np.where(idx % 2 == 0, 1.0, -1.0)
    msin[0, :] = sign
    fwd = np.concatenate([cos, msin], axis=0)
    wgt = np.full((seq,), 2.0 / n_fft)
    wgt[0] = 1.0 / n_fft
    inv_re = cos.T * wgt[None, :]
    inv_im = msin.T * wgt[None, :]
    inv_im[:, 0] = sign / n_fft
    inv = np.concatenate([inv_re, inv_im], axis=1)
    return fwd, inv


def _filter_feats(seq):
    n = np.arange(seq, dtype=np.float64)
    t = n / seq
    f = np.linspace(1e-4, HY_BANDS - 1, HY_BANDS)
    w = 2.0 * math.pi * n / seq
    z = np.concatenate([t[:, None], np.cos(w[:, None] * f), np.sin(w[:, None] * f)], axis=-1)
    out = np.zeros((seq, V7X_LANES), np.float32)
    out[:, :z.shape[1]] = z
    return out


def _filter_kernel(z_ref, w1_ref, b1_ref, w2_ref, b2_ref, fr_ref, w3f_ref, w3b_ref, b3f_ref, b3b_ref,
                   decf_ref, decb_ref, cmat_ref, smat_ref, kr_ref, kiz_ref, krn_ref, h2_ref):
    @pl.when(pl.program_id(0) == 0)
    def _():
        h1 = jnp.sin(fr_ref[0:1, :] * (_dot_exact(z_ref[...], w1_ref[...]) + b1_ref[...]))
        h2_ref[...] = jnp.sin(fr_ref[1:2, :] * (_dot_exact(h1, w2_ref[...]) + b2_ref[...]))

    h2 = h2_ref[...]
    seq, cw = kr_ref.shape
    t = z_ref[:, 0:1]
    row = lax.broadcasted_iota(jnp.int32, (seq, cw), 0)
    hf = (_dot_exact(h2, w3f_ref[...]) + b3f_ref[...]) * jnp.exp(-t * jnp.abs(decf_ref[...]))
    hb = (_dot_exact(h2, w3b_ref[...]) + b3b_ref[...]) * jnp.exp(-t * jnp.abs(decb_ref[...]))
    hb = jnp.where(row == 0, 0.0, hb)
    norm = (jnp.sum(jnp.abs(hf), axis=0, keepdims=True)
            + jnp.sum(jnp.abs(hb), axis=0, keepdims=True) + FILTER_EPS)
    inv = 1.0 / norm
    even = (hf + hb) * inv
    odd = (hf - hb) * inv
    kr = _dot_exact(cmat_ref[...], even)
    ki = _dot_exact(smat_ref[...], odd)
    nyq = jnp.sum(jnp.where((row & 1) == 0, even, -even), axis=0, keepdims=True)
    kr_ref[...] = kr
    kiz_ref[...] = jnp.where(row == 0, 0.0, ki)
    krn_ref[...] = jnp.where(row == 0, nyq, kr)


def _hyena_filters(seq, w1, b1, w2, b2, w3, b3, freq, decay, d_hy):
    fh = w1.shape[1]
    pad = V7X_LANES
    w1p = jnp.zeros((pad, pad), F32).at[:w1.shape[0], :fh].set(w1)
    b1p = jnp.zeros((1, pad), F32).at[0, :fh].set(b1)
    w2p = jnp.zeros((pad, pad), F32).at[:fh, :fh].set(w2)
    b2p = jnp.zeros((1, pad), F32).at[0, :fh].set(b2)
    frp = jnp.zeros((2, pad), F32).at[:, :fh].set(freq)
    ncol = w3.shape[1] // 2
    w3p = jnp.zeros((pad, 2 * ncol), F32).at[:fh].set(w3)
    b3r = b3.reshape(1, 2 * ncol)
    dec = decay.reshape(1, 2 * ncol)
    fwd, _ = _dft_tables(seq)
    cmat = jnp.asarray(fwd[:seq], F32)
    smat = jnp.asarray(fwd[seq:], F32)
    z = jnp.asarray(_filter_feats(seq))
    cw = _tile(ncol, 256)
    nb = ncol // cw
    full = lambda shape: pl.BlockSpec(shape, lambda j: (0, 0))
    colf = lambda rows: pl.BlockSpec((rows, cw), lambda j: (0, j))
    colb = lambda rows: pl.BlockSpec((rows, cw), lambda j: (0, nb + j))
    out = jax.ShapeDtypeStruct((seq, ncol), F32)
    return pl.pallas_call(
        _filter_kernel,
        grid=(nb,),
        in_specs=[full((seq, pad)), full((pad, pad)), full((1, pad)), full((pad, pad)), full((1, pad)),
                  full((2, pad)), colf(pad), colb(pad), colf(1), colb(1), colf(1), colb(1),
                  full((seq, seq)), full((seq, seq))],
        out_specs=[colf(seq), colf(seq), colf(seq)],
        out_shape=[out, out, out],
        scratch_shapes=[pltpu.VMEM((seq, pad), F32)],
        compiler_params=_cparams("arbitrary"),
        name=f"hy_filter_{seq}",
    )(z, w1p, b1p, w2p, b2p, frp, w3p, w3p, b3r, b3r, dec, dec, cmat, smat)


def _hyena_kernel(x1_ref, x2_ref, v_ref, w1_ref, w2_ref, wv_ref, hb_ref,
                  kr0_ref, kiz0_ref, krn0_ref, kr1_ref, kiz1_ref, krn1_ref, f_ref, g_ref, o_ref):
    seq, cw = o_ref.shape
    row = lax.broadcasted_iota(jnp.int32, (seq, cw), 0)
    first = row == 0
    last = row == seq - 1
    x1 = _dwconv3_rows(x1_ref[...], w1_ref[...], first, last)
    x2 = _dwconv3_rows(x2_ref[...], w2_ref[...], first, last)
    v = _dwconv3_rows(v_ref[...], wv_ref[...], first, last)

    def long_conv(u, kr_ref, kiz_ref, krn_ref, bias):
        spec = _dot(f_ref[...], u.astype(BF16))
        ur = spec[:seq]
        ui = spec[seq:]
        kiz = kiz_ref[...]
        yr = ur * kr_ref[...] - ui * kiz
        yi = ur * kiz + ui * krn_ref[...]
        y = _dot(g_ref[:, :seq], yr.astype(BF16)) + _dot(g_ref[:, seq:], yi.astype(BF16))
        return y + u * bias

    z = x1 * long_conv(v, kr0_ref, kiz0_ref, krn0_ref, hb_ref[0:1])
    o_ref[...] = (x2 * long_conv(z, kr1_ref, kiz1_ref, krn1_ref, hb_ref[1:2])).astype(BF16)


def _hyena(proj, batch, seq, short_w, hy_bias, tables, d_hy):
    kr, kiz, krn = tables
    cw = _tile(d_hy, 1024 if seq <= 256 else 256)
    nb = d_hy // cw
    fwd, inv = _dft_tables(seq)
    fmat = jnp.asarray(fwd, F32).astype(BF16)
    gmat = jnp.asarray(inv, F32).astype(BF16)
    col = lambda rows, off: pl.BlockSpec((rows, cw), lambda j, b: (0, off * nb + j))
    act = lambda off: pl.BlockSpec((seq, cw), lambda j, b: (b, off * nb + j))
    return pl.pallas_call(
        _hyena_kernel,
        grid=(nb, batch),
        in_specs=[act(0), act(1), act(2), col(3, 0), col(3, 1), col(3, 2), col(2, 0),
                  col(seq, 0), col(seq, 0), col(seq, 0), col(seq, 1), col(seq, 1), col(seq, 1),
                  pl.BlockSpec((2 * seq, seq), lambda j, b: (0, 0)),
                  pl.BlockSpec((seq, 2 * seq), lambda j, b: (0, 0))],
        out_specs=pl.BlockSpec((seq, cw), lambda j, b: (b, j)),
        out_shape=jax.ShapeDtypeStruct((batch * seq, d_hy), BF16),
        compiler_params=_cparams("parallel", "parallel"),
        name=f"hyena_{seq}",
    )(proj, proj, proj, short_w, short_w, short_w, hy_bias, kr, kiz, krn, kr, kiz, krn, fmat, gmat)


def _rope_tables(seq, dk):
    rows = seq // GRID_W
    row = jnp.repeat(jnp.arange(rows), GRID_W).astype(F32)
    col = jnp.tile(jnp.arange(GRID_W), rows).astype(F32)
    nfreq = dk // 4
    inv = ROPE_BASE ** (-jnp.arange(nfreq, dtype=F32) / nfreq)
    ang = jnp.concatenate([row[:, None] * inv, col[:, None] * inv], axis=-1)
    cos, sin = jnp.cos(ang), jnp.sin(ang)
    return jnp.concatenate([cos, cos], axis=-1), jnp.concatenate([-sin, sin], axis=-1)


def _retention_kernel(*refs, seq, hb, dk, use_rope, use_state, want_state):
    refs = list(refs)
    q_ref, k_ref, v_ref, g_ref, lg_ref, gn_ref = refs[:6]
    pos = 6
    if use_rope:
        cos_ref, sin_ref = refs[pos:pos + 2]
        pos += 2
    if use_state:
        s0_ref = refs[pos]
        pos += 1
    o_ref = refs[pos]
    pos += 1
    if want_state:
        st_ref = refs[pos]
        pos += 1
    d_ref = refs[pos]

    def log_gamma(hh, direction):
        return jnp.log(_sigmoid(lg_ref[hh, direction]))[:, 0:1]

    @pl.when(pl.program_id(1) == 0)
    def _():
        i = lax.broadcasted_iota(jnp.int32, (seq, seq), 0)
        j = lax.broadcasted_iota(jnp.int32, (seq, seq), 1)
        diff = (i - j).astype(F32)
        for hh in range(hb):
            fwd = jnp.where(diff >= 0, jnp.exp(log_gamma(hh, 0) * jnp.maximum(diff, 0.0)), 0.0)
            bwd = jnp.where(diff <= 0, jnp.exp(log_gamma(hh, 1) * jnp.maximum(-diff, 0.0)), 0.0)
            d_ref[hh] = fwd + bwd

    pos_f = lax.broadcasted_iota(jnp.int32, (seq, dk), 0).astype(F32)
    for hh in range(hb):
        sl = slice(hh * dk, (hh + 1) * dk)
        q = q_ref[:, sl]
        k = k_ref[:, sl]
        v = v_ref[:, sl]
        if use_rope:
            cos = cos_ref[...]
            sin = sin_ref[...]
            q = q * cos + pltpu.roll(q, dk // 2, 1) * sin
            k = k * cos + pltpu.roll(k, dk // 2, 1) * sin
        k = k * dk ** -0.5
        qb = q.astype(BF16)
        vb = v.astype(BF16)
        s = lax.dot_general(qb, k.astype(BF16), (((1,), (1,)), ((), ())), preferred_element_type=F32)
        o = _dot((s * d_ref[hh]).astype(BF16), vb)
        lf = log_gamma(hh, 0)
        lb = log_gamma(hh, 1)
        if use_state:
            o = o + _dot(qb, s0_ref[0, hh].astype(BF16)) * jnp.exp(lf * (pos_f + 1.0))
            o = o + _dot(qb, s0_ref[1, hh].astype(BF16)) * jnp.exp(lb * (seq - pos_f))
        if want_state:
            kf = (k * jnp.exp(lf * (seq - 1.0 - pos_f))).astype(BF16)
            kb = (k * jnp.exp(lb * pos_f)).astype(BF16)
            tn = (((0,), (0,)), ((), ()))
            st_ref[0, hh] = lax.dot_general(kf, vb, tn, preferred_element_type=F32)
            st_ref[1, hh] = lax.dot_general(kb, vb, tn, preferred_element_type=F32)
        mu = jnp.mean(o, axis=-1, keepdims=True)
        oc = o - mu
        var = jnp.mean(oc * oc, axis=-1, keepdims=True)
        gate = g_ref[:, sl]
        y = (oc * lax.rsqrt(var + GN_EPS)) * gn_ref[:, sl] * (gate * _sigmoid(gate))
        o_ref[:, sl] = y.astype(BF16)


def _retention(proj, batch, seq, col0, n_heads, dk, decay_logit, ret_gn, rope, state0, want_state):
    d_ret = n_heads * dk
    hb = n_heads if seq <= 256 else min(n_heads, 2)
    bw = hb * dk
    nhb = n_heads // hb
    lg = jnp.broadcast_to(decay_logit.T[:, :, None, None], (n_heads, 2, 1, V7X_LANES))
    act = lambda part: pl.BlockSpec((seq, bw), lambda h, b: (b, (col0 + part * d_ret) // bw + h))
    in_specs = [act(0), act(1), act(2), act(3),
                pl.BlockSpec((hb, 2, 1, V7X_LANES), lambda h, b: (h, 0, 0, 0)),
                pl.BlockSpec((1, bw), lambda h, b: (0, h))]
    args = [proj, proj, proj, proj, lg, ret_gn.reshape(1, d_ret)]
    if rope is not None:
        in_specs += [pl.BlockSpec((seq, dk), lambda h, b: (0, 0))] * 2
        args += list(rope)
    if state0 is not None:
        in_specs.append(pl.BlockSpec((None, 2, hb, dk, dk), lambda h, b: (b, 0, h, 0, 0)))
        args.append(state0)
    out_specs = [pl.BlockSpec((seq, bw), lambda h, b: (b, h))]
    out_shape = [jax.ShapeDtypeStruct((batch * seq, d_ret), BF16)]
    if want_state:
        out_specs.append(pl.BlockSpec((None, 2, hb, dk, dk), lambda h, b: (b, 0, h, 0, 0)))
        out_shape.append(jax.ShapeDtypeStruct((batch, 2, n_heads, dk, dk), F32))
    body = functools.partial(_retention_kernel, seq=seq, hb=hb, dk=dk, use_rope=rope is not None,
                             use_state=state0 is not None, want_state=want_state)
    return pl.pallas_call(
        body,
        grid=(nhb, batch),
        in_specs=in_specs,
        out_specs=out_specs,
        out_shape=out_shape,
        scratch_shapes=[pltpu.VMEM((hb, seq, seq), F32)],
        compiler_params=_cparams("parallel", "arbitrary"),
        name=f"retention_{seq}",
    )(*args)


def _merge_kernel(*refs, nblk):
    gate_refs = refs[:2 * nblk]
    (yhy_ref, yret_ref, x_ref, wbh_ref, wbr_ref, wo_ref, gpost_ref, gm_ref, gpre_ref, sc_ref, sh_ref,
     o_ref, h_ref) = refs[2 * nblk:]
    a = _dot(yhy_ref[...], wbh_ref[...])
    b = _dot(yret_ref[...], wbr_ref[...])
    wblk = gate_refs[0].shape[1]
    parts = []
    for kk in range(nblk):
        sl = slice(kk * wblk, (kk + 1) * wblk)
        g_hy = _sigmoid(gate_refs[kk][...])
        g_ret = _sigmoid(gate_refs[nblk + kk][...])
        parts.append((g_hy * a[:, sl] + g_ret * b[:, sl]).astype(BF16))
    merged = parts[0] if nblk == 1 else jnp.concatenate(parts, axis=1)
    out = _dot(merged, wo_ref[...])
    x1 = x_ref[...] + gm_ref[...] * _rms_scale(out, gpost_ref[...])
    o_ref[...] = x1
    h_ref[...] = (_rms_scale(x1, gpre_ref[...]) * (1.0 + sc_ref[...]) + sh_ref[...]).astype(BF16)


def _merge(proj, col0, y_hy, y_ret, x, w_br_hy, w_br_ret, w_out, g_post, gate_m, g_pre_f, scale_f, shift_f,
           rows_per_mod, tm):
    m, d = x.shape
    vec = pl.BlockSpec((1, d), lambda i: (0, 0))
    mod_spec = pl.BlockSpec((None, 1, d), lambda i: ((i * tm) // rows_per_mod, 0, 0))
    wblk = math.gcd(col0, d)
    nblk = d // wblk
    gate_spec = lambda kk: pl.BlockSpec((tm, wblk), lambda i: (i, col0 // wblk + kk))
    const = lambda arr: _resident(arr.shape)
    row = lambda width: pl.BlockSpec((tm, width), lambda i: (i, 0))
    return pl.pallas_call(
        functools.partial(_merge_kernel, nblk=nblk),
        grid=(m // tm,),
        in_specs=[gate_spec(kk) for kk in range(2 * nblk)]
        + [row(y_hy.shape[1]), row(y_ret.shape[1]), row(d), const(w_br_hy), const(w_br_ret), const(w_out),
           vec, mod_spec, vec, mod_spec, mod_spec],
        out_specs=[row(d), row(d)],
        out_shape=[jax.ShapeDtypeStruct((m, d), F32), jax.ShapeDtypeStruct((m, d), BF16)],
        compiler_params=_cparams("parallel"),
        name="merge",
    )(*([proj] * (2 * nblk)), y_hy, y_ret, x, w_br_hy, w_br_ret, w_out, g_post, gate_m, g_pre_f, scale_f, shift_f)


def _ffn_kernel(x_ref, h_ref, gate_ref, gpost_ref, wa_ref, wb_ref, ca_ref, cb_ref, wd_ref, o_ref, *, seq):
    j = pl.program_id(1)
    tm = x_ref.shape[0]
    tf = wa_ref.shape[1]

    @pl.when(j == 0)
    def _():
        o_ref[...] = jnp.zeros_like(o_ref)

    pos = lax.broadcasted_iota(jnp.int32, (tm, tf), 0) % seq
    first = pos == 0
    last = pos == seq - 1
    h = h_ref[...]
    a = _dwconv3_rows(_dot(h, wa_ref[...]), ca_ref[...], first, last)
    b = _dwconv3_rows(_dot(h, wb_ref[...]), cb_ref[...], first, last)
    gelu = 0.5 * a * (1.0 + jnp.tanh(math.sqrt(2.0 / math.pi) * (a + 0.044715 * (a * a * a))))
    o_ref[...] += _dot((gelu * b).astype(BF16), wd_ref[...])

    @pl.when(j == pl.num_programs(1) - 1)
    def _():
        o_ref[...] = x_ref[...] + gate_ref[...] * _rms_scale(o_ref[...], gpost_ref[...])


def _ffn(x, h, seq, gate, g_post, w_up, conv_w, w_down, rows_per_mod, tm):
    m, d = x.shape
    d_ff = w_down.shape[0]
    tf = _tile(d_ff, 512)
    nf = d_ff // tf
    once = pl.BlockSpec((tm, d), lambda i, j: (i, 0), pipeline_mode=pl.Buffered(1))
    return pl.pallas_call(
        functools.partial(_ffn_kernel, seq=seq),
        grid=(m // tm, nf),
        in_specs=[once, once,
                  pl.BlockSpec((None, 1, d), lambda i, j: ((i * tm) // rows_per_mod, 0, 0)),
                  pl.BlockSpec((1, d), lambda i, j: (0, 0)),
                  pl.BlockSpec((d, tf), lambda i, j: (0, j)),
                  pl.BlockSpec((d, tf), lambda i, j: (0, nf + j)),
                  pl.BlockSpec((3, tf), lambda i, j: (0, j)),
                  pl.BlockSpec((3, tf), lambda i, j: (0, nf + j)),
                  pl.BlockSpec((tf, d), lambda i, j: (j, 0))],
        out_specs=pl.BlockSpec((tm, d), lambda i, j: (i, 0)),
        out_shape=jax.ShapeDtypeStruct((m, d), F32),
        compiler_params=_cparams("parallel", "arbitrary"),
        name="ffn",
    )(x, h, gate, g_post, w_up, w_up, conv_w, conv_w, w_down)


def _block(x3, mod, rope, state0, want_state, p):
    batch, seq, d = x3.shape
    m = batch * seq
    x = x3.reshape(m, d)
    per_seq_mod = mod.shape[0] != 1
    rows_per_mod = seq if per_seq_mod else m
    tm = _row_tile(m, seq, per_seq_mod, 1024)
    tm_merge = _tile(tm, 256, unit=V7X_SUBLANES)
    shift_m, scale_m, gate_m, shift_f, scale_f, gate_f = (mod[:, i][:, None, :] for i in range(6))
    d_hy, d_ret, n_heads, dk = p["d_hy"], p["d_ret"], p["n_heads"], p["dk"]

    proj = _in_proj(x, p["g_pre_m"], scale_m, shift_m, p["w_in"], rows_per_mod, tm)
    tables = _hyena_filters(seq, p["hy_w1"], p["hy_b1"], p["hy_w2"], p["hy_b2"], p["hy_w3"], p["hy_b3"],
                            p["hy_freq"], p["hy_decay"], d_hy)
    y_hy = _hyena(proj, batch, seq, p["hy_short_w"], p["hy_bias"], tables, d_hy)
    ret = _retention(proj, batch, seq, 3 * d_hy, n_heads, dk, p["ret_decay_logit"], p["ret_gn"],
                     rope, state0, want_state)
    y_ret = ret[0]
    x, h_ffn = _merge(proj, 3 * d_hy + 4 * d_ret, y_hy, y_ret, x, p["w_br_hy"], p["w_br_ret"], p["w_out"],
                      p["g_post_m"], gate_m, p["g_pre_f"], scale_f, shift_f, rows_per_mod, tm_merge)
    x = _ffn(x, h_ffn, seq, gate_f, p["g_post_f"], p["ffn_w_up"], p["ffn_conv"], p["ffn_w_down"], rows_per_mod, tm)
    return x.reshape(batch, seq, d), (ret[1] if want_state else None)


def kernel(x_prompt, x_sample, state_ret, c, c_ctx, w_ada, b_ada, norm_pre_mix, norm_post_mix, norm_pre_ffn,
           norm_post_ffn, w_in, hy_short_w, hy_w1, hy_b1, hy_w2, hy_b2, hy_w3, hy_b3, hy_freq, hy_decay, hy_bias,
           ret_decay_logit, ret_gn, w_br_hy, w_br_ret, w_out, ffn_w_up, ffn_conv, ffn_w_down):
    depth = w_in.shape[0]
    d = x_prompt.shape[-1]
    n_dec = x_sample.shape[0]
    n_heads, dk = state_ret.shape[3], state_ret.shape[4]
    d_hy = hy_bias.shape[-1]
    cc = jnp.zeros((V7X_SUBLANES, d), F32).at[0].set(c_ctx).at[1:1 + n_dec].set(c)
    rope = _rope_tables(x_sample.shape[1], dk)
    x_p, x_s = x_prompt, x_sample
    states = []
    for l in range(depth):
        p = dict(
            d_hy=d_hy, d_ret=n_heads * dk, n_heads=n_heads, dk=dk,
            g_pre_m=norm_pre_mix[l][None], g_post_m=norm_post_mix[l][None],
            g_pre_f=norm_pre_ffn[l][None], g_post_f=norm_post_ffn[l][None],
            w_in=w_in[l].astype(BF16), hy_short_w=hy_short_w[l],
            hy_w1=hy_w1[l], hy_b1=hy_b1[l], hy_w2=hy_w2[l], hy_b2=hy_b2[l], hy_w3=hy_w3[l], hy_b3=hy_b3[l],
            hy_freq=hy_freq[l], hy_decay=hy_decay[l], hy_bias=hy_bias[l],
            ret_decay_logit=ret_decay_logit[l], ret_gn=ret_gn[l],
            w_br_hy=w_br_hy[l].astype(BF16), w_br_ret=w_br_ret[l].astype(BF16), w_out=w_out[l].astype(BF16),
            ffn_w_up=ffn_w_up[l].astype(BF16), ffn_conv=ffn_conv[l], ffn_w_down=ffn_w_down[l].astype(BF16),
        )
        mod = _ada_mod(cc, w_ada[l], b_ada[l][None]).reshape(V7X_SUBLANES, 6, d)
        x_p, st = _block(x_p, mod[0:1], None, None, True, p)
        x_s, _ = _block(x_s, mod[1:1 + n_dec], rope, state_ret[:, l], False, p)
        states.append(st)
    return x_p, x_s, jnp.stack(states, axis=1)
```

```python
import functools
import math

import jax
import jax.numpy as jnp
import numpy as np
from jax import lax
from jax.experimental import pallas as pl
from jax.experimental.pallas import tpu as pltpu

F32 = jnp.float32
BF16 = jnp.bfloat16

RMS_EPS = 1e-6
GN_EPS = 1e-5
FILTER_EPS = 1e-6
HY_BANDS = 16
GRID_W = 64
ROPE_BASE = 10000.0

V7X_VMEM_LIMIT_BYTES = 56 * 1024 * 1024
V7X_LANES = 128
V7X_SUBLANES = 8


def _cparams(*sem):
    return pltpu.CompilerParams(dimension_semantics=sem, vmem_limit_bytes=V7X_VMEM_LIMIT_BYTES)


def _tile(n, target, unit=V7X_LANES):
    if n <= target:
        return n
    best = unit
    for t in range(unit, target + 1, unit):
        if n % t == 0:
            best = t
    assert n % best == 0, (n, target, unit)
    return best


def _row_tile(m, seq, per_seq_mod, target):
    if per_seq_mod or seq >= target:
        return seq
    return seq * _tile(m // seq, target // seq, unit=1)


def _resident(shape):
    return pl.BlockSpec(shape, lambda *_: (0,) * len(shape), pipeline_mode=pl.Buffered(1))


def _sigmoid(x):
    return 1.0 / (1.0 + jnp.exp(-x))


def _rms_scale(x, g):
    ms = jnp.mean(x * x, axis=-1, keepdims=True)
    return (x * lax.rsqrt(ms + RMS_EPS)) * g


def _dot(a, b):
    return jnp.dot(a, b, preferred_element_type=F32)


def _dot_exact(a, b):
    return jnp.dot(a, b, preferred_element_type=F32, precision=lax.Precision.HIGHEST)


def _split_bf16(table):
    hi = table.astype(BF16)
    lo = (table - hi.astype(np.float64)).astype(BF16)
    return jnp.asarray(hi), jnp.asarray(lo)


def _dot_split(a_hi_ref, a_lo_ref, b):
    b_hi = b.astype(BF16)
    b_lo = (b - b_hi.astype(F32)).astype(BF16)
    a_hi = a_hi_ref[...]
    return _dot(a_hi, b_hi) + (_dot(a_lo_ref[...], b_hi) + _dot(a_hi, b_lo))


def _dwconv3_rows(x, w, first, last):
    rows = x.shape[0]
    prev = jnp.where(first, 0.0, pltpu.roll(x, 1, 0))
    nxt = jnp.where(last, 0.0, pltpu.roll(x, rows - 1, 0))
    return prev * w[0:1] + x * w[1:2] + nxt * w[2:3]


def _ada_kernel(cc_ref, w_ref, b_ref, o_ref):
    cc = cc_ref[...]
    s = cc * _sigmoid(cc)
    o_ref[...] = _dot(s.astype(BF16), w_ref[...].astype(BF16)) + b_ref[...]


def _ada_mod(cc, w, b):
    d, n = w.shape
    tn = _tile(n, 1024)
    return pl.pallas_call(
        _ada_kernel,
        grid=(n // tn,),
        in_specs=[pl.BlockSpec((V7X_SUBLANES, d), lambda j: (0, 0)),
                  pl.BlockSpec((d, tn), lambda j: (0, j)),
                  pl.BlockSpec((1, tn), lambda j: (0, j))],
        out_specs=pl.BlockSpec((V7X_SUBLANES, tn), lambda j: (0, j)),
        out_shape=jax.ShapeDtypeStruct((V7X_SUBLANES, n), F32),
        compiler_params=_cparams("parallel"),
        name="ada_mod",
    )(cc, w, b)


def _in_proj_kernel(x_ref, g_ref, sc_ref, sh_ref, w_ref, o_ref, h_ref):
    @pl.when(pl.program_id(1) == 0)
    def _():
        h = _rms_scale(x_ref[...], g_ref[...]) * (1.0 + sc_ref[...]) + sh_ref[...]
        h_ref[...] = h.astype(BF16)

    o_ref[...] = _dot(h_ref[...], w_ref[...])


def _in_proj(x, g, scale, shift, w, rows_per_mod, tm):
    m, d = x.shape
    n = w.shape[1]
    tn = _tile(n, 1024)
    mod_spec = pl.BlockSpec((None, 1, d), lambda i, j: ((i * tm) // rows_per_mod, 0, 0))
    return pl.pallas_call(
        _in_proj_kernel,
        grid=(m // tm, n // tn),
        in_specs=[pl.BlockSpec((tm, d), lambda i, j: (i, 0)),
                  pl.BlockSpec((1, d), lambda i, j: (0, 0)),
                  mod_spec, mod_spec,
                  pl.BlockSpec((d, tn), lambda i, j: (0, j))],
        out_specs=pl.BlockSpec((tm, tn), lambda i, j: (i, j)),
        out_shape=jax.ShapeDtypeStruct((m, n), F32),
        scratch_shapes=[pltpu.VMEM((tm, d), BF16)],
        compiler_params=_cparams("parallel", "arbitrary"),
        name="in_proj",
    )(x, g, scale, shift, w)


def _dft_tables(seq):
    n_fft = 2 * seq
    idx = np.arange(seq)
    ang = 2.0 * np.pi * ((idx[:, None] * idx[None, :]) % n_fft) / n_fft
    cos = np.cos(ang)
    msin = -np.sin(ang)
    sign = np.where(idx % 2 == 0, 1.0, -1.0)
    msin[0, :] = sign
    fwd = np.concatenate([cos, msin], axis=0)
    wgt = np.full((seq,), 2.0 / n_fft)
    wgt[0] = 1.0 / n_fft
    inv_re = cos.T * wgt[None, :]
    inv_im = msin.T * wgt[None, :]
    inv_im[:, 0] = sign / n_fft
    inv = np.concatenate([inv_re, inv_im], axis=1)
    return fwd, inv


def _filter_feats(seq):
    n = np.arange(seq, dtype=np.float64)
    t = n / seq
    f = np.linspace(1e-4, HY_BANDS - 1, HY_BANDS)
    w = 2.0 * math.pi * n / seq
    z = np.concatenate([t[:, None], np.cos(w[:, None] * f), np.sin(w[:, None] * f)], axis=-1)
    out = np.zeros((seq, V7X_LANES), np.float32)
    out[:, :z.shape[1]] = z
    return out


def _filter_kernel(z_ref, w1_ref, b1_ref, w2_ref, b2_ref, fr_ref, w3f_ref, w3b_ref, b3f_ref, b3b_ref,
                   decf_ref, decb_ref, chi_ref, clo_ref, shi_ref, slo_ref, kr_ref, kiz_ref, krn_ref, h2_ref):
    @pl.when(pl.program_id(0) == 0)
    def _():
        h1 = jnp.sin(fr_ref[0:1, :] * (_dot_exact(z_ref[...], w1_ref[...]) + b1_ref[...]))
        h2_ref[...] = jnp.sin(fr_ref[1:2, :] * (_dot_exact(h1, w2_ref[...]) + b2_ref[...]))

    h2 = h2_ref[...]
    seq, cw = kr_ref.shape
    t = z_ref[:, 0:1]
    row = lax.broadcasted_iota(jnp.int32, (seq, cw), 0)
    hf = (_dot_exact(h2, w3f_ref[...]) + b3f_ref[...]) * jnp.exp(-t * jnp.abs(decf_ref[...]))
    hb = (_dot_exact(h2, w3b_ref[...]) + b3b_ref[...]) * jnp.exp(-t * jnp.abs(decb_ref[...]))
    hb = jnp.where(row == 0, 0.0, hb)
    norm = (jnp.sum(jnp.abs(hf), axis=0, keepdims=True)
            + jnp.sum(jnp.abs(hb), axis=0, keepdims=True) + FILTER_EPS)
    inv = 1.0 / norm
    even = (hf + hb) * inv
    odd = (hf - hb) * inv
    kr = _dot_split(chi_ref, clo_ref, even)
    ki = _dot_split(shi_ref, slo_ref, odd)
    nyq = jnp.sum(jnp.where((row & 1) == 0, even, -even), axis=0, keepdims=True)
    kr_ref[...] = kr
    kiz_ref[...] = jnp.where(row == 0, 0.0, ki)
    krn_ref[...] = jnp.where(row == 0, nyq, kr)


def _hyena_filters(seq, w1, b1, w2, b2, w3, b3, freq, decay, d_hy):
    fh = w1.shape[1]
    pad = V7X_LANES
    w1p = jnp.zeros((pad, pad), F32).at[:w1.shape[0], :fh].set(w1)
    b1p = jnp.zeros((1, pad), F32).at[0, :fh].set(b1)
    w2p = jnp.zeros((pad, pad), F32).at[:fh, :fh].set(w2)
    b2p = jnp.zeros((1, pad), F32).at[0, :fh].set(b2)
    frp = jnp.zeros((2, pad), F32).at[:, :fh].set(freq)
    ncol = w3.shape[1] // 2
    w3p = jnp.zeros((pad, 2 * ncol), F32).at[:fh].set(w3)
    b3r = b3.reshape(1, 2 * ncol)
    dec = decay.reshape(1, 2 * ncol)
    fwd, _ = _dft_tables(seq)
    chi, clo = _split_bf16(fwd[:seq])
    shi, slo = _split_bf16(fwd[seq:])
    z = jnp.asarray(_filter_feats(seq))
    cw = _tile(ncol, 256)
    nb = ncol // cw
    full = lambda shape: pl.BlockSpec(shape, lambda j: (0, 0))
    colf = lambda rows: pl.BlockSpec((rows, cw), lambda j: (0, j))
    colb = lambda rows: pl.BlockSpec((rows, cw), lambda j: (0, nb + j))
    out = jax.ShapeDtypeStruct((seq, ncol), F32)
    return pl.pallas_call(
        _filter_kernel,
        grid=(nb,),
        in_specs=[full((seq, pad)), full((pad, pad)), full((1, pad)), full((pad, pad)), full((1, pad)),
                  full((2, pad)), colf(pad), colb(pad), colf(1), colb(1), colf(1), colb(1),
                  _resident((seq, seq)), _resident((seq, seq)), _resident((seq, seq)), _resident((seq, seq))],
        out_specs=[colf(seq), colf(seq), colf(seq)],
        out_shape=[out, out, out],
        scratch_shapes=[pltpu.VMEM((seq, pad), F32)],
        compiler_params=_cparams("arbitrary"),
        name=f"hy_filter_{seq}",
    )(z, w1p, b1p, w2p, b2p, frp, w3p, w3p, b3r, b3r, dec, dec, chi, clo, shi, slo)


def _hyena_kernel(x1_ref, x2_ref, v_ref, w1_ref, w2_ref, wv_ref, hb_ref,
                  kr0_ref, kiz0_ref, krn0_ref, kr1_ref, kiz1_ref, krn1_ref, f_ref, g_ref, o_ref, *, seq):
    cw = o_ref.shape[1]
    row = lax.broadcasted_iota(jnp.int32, (seq, cw), 0)
    first = row == 0
    last = row == seq - 1

    def long_conv(u, kr_ref, kiz_ref, krn_ref, bias):
        spec = _dot(f_ref[...], u.astype(BF16))
        ur = spec[:seq]
        ui = spec[seq:]
        kiz = kiz_ref[...]
        yr = ur * kr_ref[...] - ui * kiz
        yi = ur * kiz + ui * krn_ref[...]
        y = _dot(g_ref[:, :seq], yr.astype(BF16)) + _dot(g_ref[:, seq:], yi.astype(BF16))
        return y + u * bias

    for s in range(o_ref.shape[0] // seq):
        rs = slice(s * seq, (s + 1) * seq)
        x1 = _dwconv3_rows(x1_ref[rs], w1_ref[...], first, last)
        x2 = _dwconv3_rows(x2_ref[rs], w2_ref[...], first, last)
        v = _dwconv3_rows(v_ref[rs], wv_ref[...], first, last)
        z = x1 * long_conv(v, kr0_ref, kiz0_ref, krn0_ref, hb_ref[0:1])
        o_ref[rs] = (x2 * long_conv(z, kr1_ref, kiz1_ref, krn1_ref, hb_ref[1:2])).astype(BF16)


def _hyena(proj, batch, seq, short_w, hy_bias, tables, d_hy):
    kr, kiz, krn = tables
    cw = _tile(d_hy, 1024 if seq <= 256 else 256)
    nb = d_hy // cw
    fwd, inv = _dft_tables(seq)
    fmat = jnp.asarray(fwd, F32).astype(BF16)
    gmat = jnp.asarray(inv, F32).astype(BF16)
    bs = 2 if batch % 2 == 0 else 1
    col = lambda rows, off: pl.BlockSpec((rows, cw), lambda j, b: (0, off * nb + j))
    act = lambda off: pl.BlockSpec((bs * seq, cw), lambda j, b: (b, off * nb + j))
    return pl.pallas_call(
        functools.partial(_hyena_kernel, seq=seq),
        grid=(nb, batch // bs),
        in_specs=[act(0), act(1), act(2), col(3, 0), col(3, 1), col(3, 2), col(2, 0),
                  col(seq, 0), col(seq, 0), col(seq, 0), col(seq, 1), col(seq, 1), col(seq, 1),
                  _resident((2 * seq, seq)), _resident((seq, 2 * seq))],
        out_specs=pl.BlockSpec((bs * seq, cw), lambda j, b: (b, j)),
        out_shape=jax.ShapeDtypeStruct((batch * seq, d_hy), BF16),
        compiler_params=_cparams("parallel", "parallel"),
        name=f"hyena_{seq}",
    )(proj, proj, proj, short_w, short_w, short_w, hy_bias, kr, kiz, krn, kr, kiz, krn, fmat, gmat)


def _rope_tables(seq, dk):
    rows = seq // GRID_W
    row = jnp.repeat(jnp.arange(rows), GRID_W).astype(F32)
    col = jnp.tile(jnp.arange(GRID_W), rows).astype(F32)
    nfreq = dk // 4
    inv = ROPE_BASE ** (-jnp.arange(nfreq, dtype=F32) / nfreq)
    ang = jnp.concatenate([row[:, None] * inv, col[:, None] * inv], axis=-1)
    cos, sin = jnp.cos(ang), jnp.sin(ang)
    return jnp.concatenate([cos, cos], axis=-1), jnp.concatenate([-sin, sin], axis=-1)


def _retention_kernel(*refs, seq, hb, dk, use_rope, use_state, want_state):
    refs = list(refs)
    q_ref, k_ref, v_ref, g_ref, lg_ref, gn_ref = refs[:6]
    pos = 6
    if use_rope:
        cos_ref, sin_ref = refs[pos:pos + 2]
        pos += 2
    if use_state:
        s0_ref = refs[pos]
        pos += 1
    o_ref = refs[pos]
    pos += 1
    if want_state:
        st_ref = refs[pos]
        pos += 1
    d_ref = refs[pos]

    def log_gamma(hh, direction):
        return jnp.log(_sigmoid(lg_ref[hh, direction]))[:, 0:1]

    @pl.when(pl.program_id(1) == 0)
    def _():
        i = lax.broadcasted_iota(jnp.int32, (seq, seq), 0)
        j = lax.broadcasted_iota(jnp.int32, (seq, seq), 1)
        diff = (i - j).astype(F32)
        for hh in range(hb):
            fwd = jnp.where(diff >= 0, jnp.exp(log_gamma(hh, 0) * jnp.maximum(diff, 0.0)), 0.0)
            bwd = jnp.where(diff <= 0, jnp.exp(log_gamma(hh, 1) * jnp.maximum(-diff, 0.0)), 0.0)
            d_ref[hh] = fwd + bwd

    pos_f = lax.broadcasted_iota(jnp.int32, (seq, dk), 0).astype(F32)
    for hh in range(hb):
        sl = slice(hh * dk, (hh + 1) * dk)
        q = q_ref[:, sl]
        k = k_ref[:, sl]
        v = v_ref[:, sl]
        if use_rope:
            cos = cos_ref[...]
            sin = sin_ref[...]
            q = q * cos + pltpu.roll(q, dk // 2, 1) * sin
            k = k * cos + pltpu.roll(k, dk // 2, 1) * sin
        k = k * dk ** -0.5
        qb = q.astype(BF16)
        vb = v.astype(BF16)
        s = lax.dot_general(qb, k.astype(BF16), (((1,), (1,)), ((), ())), preferred_element_type=F32)
        o = _dot((s * d_ref[hh]).astype(BF16), vb)
        lf = log_gamma(hh, 0)
        lb = log_gamma(hh, 1)
        if use_state:
            o = o + _dot(qb, s0_ref[0, hh].astype(BF16)) * jnp.exp(lf * (pos_f + 1.0))
            o = o + _dot(qb, s0_ref[1, hh].astype(BF16)) * jnp.exp(lb * (seq - pos_f))
        if want_state:
            kf = (k * jnp.exp(lf * (seq - 1.0 - pos_f))).astype(BF16)
            kb = (k * jnp.exp(lb * pos_f)).astype(BF16)
            tn = (((0,), (0,)), ((), ()))
            st_ref[0, hh] = lax.dot_general(kf, vb, tn, preferred_element_type=F32)
            st_ref[1, hh] = lax.dot_general(kb, vb, tn, preferred_element_type=F32)
        mu = jnp.mean(o, axis=-1, keepdims=True)
        oc = o - mu
        var = jnp.mean(oc * oc, axis=-1, keepdims=True)
        gate = g_ref[:, sl]
        y = (oc * lax.rsqrt(var + GN_EPS)) * gn_ref[:, sl] * (gate * _sigmoid(gate))
        o_ref[:, sl] = y.astype(BF16)


def _retention(proj, batch, seq, col0, n_heads, dk, decay_logit, ret_gn, rope, state0, want_state):
    d_ret = n_heads * dk
    hb = n_heads if seq <= 256 else min(n_heads, 2)
    bw = hb * dk
    nhb = n_heads // hb
    lg = jnp.broadcast_to(decay_logit.T[:, :, None, None], (n_heads, 2, 1, V7X_LANES))
    act = lambda part: pl.BlockSpec((seq, bw), lambda h, b: (b, (col0 + part * d_ret) // bw + h))
    in_specs = [act(0), act(1), act(2), act(3),
                pl.BlockSpec((hb, 2, 1, V7X_LANES), lambda h, b: (h, 0, 0, 0)),
                pl.BlockSpec((1, bw), lambda h, b: (0, h))]
    args = [proj, proj, proj, proj, lg, ret_gn.reshape(1, d_ret)]
    if rope is not None:
        in_specs += [pl.BlockSpec((seq, dk), lambda h, b: (0, 0))] * 2
        args += list(rope)
    if state0 is not None:
        in_specs.append(pl.BlockSpec((None, 2, hb, dk, dk), lambda h, b: (b, 0, h, 0, 0)))
        args.append(state0)
    out_specs = [pl.BlockSpec((seq, bw), lambda h, b: (b, h))]
    out_shape = [jax.ShapeDtypeStruct((batch * seq, d_ret), BF16)]
    if want_state:
        out_specs.append(pl.BlockSpec((None, 2, hb, dk, dk), lambda h, b: (b, 0, h, 0, 0)))
        out_shape.append(jax.ShapeDtypeStruct((batch, 2, n_heads, dk, dk), F32))
    body = functools.partial(_retention_kernel, seq=seq, hb=hb, dk=dk, use_rope=rope is not None,
                             use_state=state0 is not None, want_state=want_state)
    return pl.pallas_call(
        body,
        grid=(nhb, batch),
        in_specs=in_specs,
        out_specs=out_specs,
        out_shape=out_shape,
        scratch_shapes=[pltpu.VMEM((hb, seq, seq), F32)],
        compiler_params=_cparams("parallel", "arbitrary"),
        name=f"retention_{seq}",
    )(*args)


def _merge_kernel(*refs, nblk):
    gate_refs = refs[:2 * nblk]
    (yhy_ref, yret_ref, x_ref, wbh_ref, wbr_ref, wo_ref, gpost_ref, gm_ref, gpre_ref, sc_ref, sh_ref,
     o_ref, h_ref) = refs[2 * nblk:]
    a = _dot(yhy_ref[...], wbh_ref[...])
    b = _dot(yret_ref[...], wbr_ref[...])
    wblk = gate_refs[0].shape[1]
    parts = []
    for kk in range(nblk):
        sl = slice(kk * wblk, (kk + 1) * wblk)
        g_hy = _sigmoid(gate_refs[kk][...])
        g_ret = _sigmoid(gate_refs[nblk + kk][...])
        parts.append((g_hy * a[:, sl] + g_ret * b[:, sl]).astype(BF16))
    merged = parts[0] if nblk == 1 else jnp.concatenate(parts, axis=1)
    out = _dot(merged, wo_ref[...])
    x1 = x_ref[...] + gm_ref[...] * _rms_scale(out, gpost_ref[...])
    o_ref[...] = x1
    h_ref[...] = (_rms_scale(x1, gpre_ref[...]) * (1.0 + sc_ref[...]) + sh_ref[...]).astype(BF16)


def _merge(proj, col0, y_hy, y_ret, x, w_br_hy, w_br_ret, w_out, g_post, gate_m, g_pre_f, scale_f, shift_f,
           rows_per_mod, tm):
    m, d = x.shape
    vec = pl.BlockSpec((1, d), lambda i: (0, 0))
    mod_spec = pl.BlockSpec((None, 1, d), lambda i: ((i * tm) // rows_per_mod, 0, 0))
    wblk = math.gcd(col0, d)
    nblk = d // wblk
    gate_spec = lambda kk: pl.BlockSpec((tm, wblk), lambda i: (i, col0 // wblk + kk))
    const = lambda arr: _resident(arr.shape)
    row = lambda width: pl.BlockSpec((tm, width), lambda i: (i, 0))
    return pl.pallas_call(
        functools.partial(_merge_kernel, nblk=nblk),
        grid=(m // tm,),
        in_specs=[gate_spec(kk) for kk in range(2 * nblk)]
        + [row(y_hy.shape[1]), row(y_ret.shape[1]), row(d), const(w_br_hy), const(w_br_ret), const(w_out),
           vec, mod_spec, vec, mod_spec, mod_spec],
        out_specs=[row(d), row(d)],
        out_shape=[jax.ShapeDtypeStruct((m, d), F32), jax.ShapeDtypeStruct((m, d), BF16)],
        compiler_params=_cparams("parallel"),
        name="merge",
    )(*([proj] * (2 * nblk)), y_hy, y_ret, x, w_br_hy, w_br_ret, w_out, g_post, gate_m, g_pre_f, scale_f, shift_f)


def _ffn_kernel(x_ref, h_ref, gate_ref, gpost_ref, wa_ref, wb_ref, ca_ref, cb_ref, wd_ref, o_ref, act_ref, *, seq):
    j = pl.program_id(1)
    nf = pl.num_programs(1) - 1
    tm = x_ref.shape[0]
    tf = wa_ref.shape[1]

    def up(slot):
        pos = lax.broadcasted_iota(jnp.int32, (tm, tf), 0) % seq
        first = pos == 0
        last = pos == seq - 1
        h = h_ref[...]
        a = _dwconv3_rows(_dot(h, wa_ref[...]), ca_ref[...], first, last)
        b = _dwconv3_rows(_dot(h, wb_ref[...]), cb_ref[...], first, last)
        gelu = 0.5 * a * (1.0 + jnp.tanh(math.sqrt(2.0 / math.pi) * (a + 0.044715 * (a * a * a))))
        act_ref[slot] = (gelu * b).astype(BF16)

    def down(slot):
        return _dot(act_ref[slot], wd_ref[...])

    @pl.when(j == 0)
    def _():
        up(0)

    @pl.when(j == 1)
    def _():
        o_ref[...] = down(0)
        up(1)

    @pl.when((j > 1) & (j < nf))
    def _():
        slot = j % 2
        o_ref[...] += down(1 - slot)
        up(slot)

    @pl.when(j == nf)
    def _():
        f = o_ref[...] + down((nf - 1) % 2)
        o_ref[...] = x_ref[...] + gate_ref[...] * _rms_scale(f, gpost_ref[...])


def _ffn(x, h, seq, gate, g_post, w_up, conv_w, w_down, rows_per_mod, tm):
    m, d = x.shape
    d_ff = w_down.shape[0]
    tf = _tile(d_ff, 512)
    nf = d_ff // tf
    assert nf >= 2, (d_ff, tf)
    once = pl.BlockSpec((tm, d), lambda i, j: (i, 0), pipeline_mode=pl.Buffered(1))
    up_blk = lambda j: jnp.minimum(j, nf - 1)
    down_blk = lambda j: jnp.maximum(j - 1, 0)
    return pl.pallas_call(
        functools.partial(_ffn_kernel, seq=seq),
        grid=(m // tm, nf + 1),
        in_specs=[once, once,
                  pl.BlockSpec((None, 1, d), lambda i, j: ((i * tm) // rows_per_mod, 0, 0)),
                  pl.BlockSpec((1, d), lambda i, j: (0, 0)),
                  pl.BlockSpec((d, tf), lambda i, j: (0, up_blk(j))),
                  pl.BlockSpec((d, tf), lambda i, j: (0, nf + up_blk(j))),
                  pl.BlockSpec((3, tf), lambda i, j: (0, up_blk(j))),
                  pl.BlockSpec((3, tf), lambda i, j: (0, nf + up_blk(j))),
                  pl.BlockSpec((tf, d), lambda i, j: (down_blk(j), 0))],
        out_specs=pl.BlockSpec((tm, d), lambda i, j: (i, 0)),
        out_shape=jax.ShapeDtypeStruct((m, d), F32),
        scratch_shapes=[pltpu.VMEM((2, tm, tf), BF16)],
        compiler_params=_cparams("parallel", "arbitrary"),
        name="ffn",
    )(x, h, gate, g_post, w_up, w_up, conv_w, conv_w, w_down)


def _block(x3, mod, rope, state0, want_state, p):
    batch, seq, d = x3.shape
    m = batch * seq
    x = x3.reshape(m, d)
    per_seq_mod = mod.shape[0] != 1
    rows_per_mod = seq if per_seq_mod else m
    tm = _row_tile(m, seq, per_seq_mod, 1024)
    tm_merge = _tile(tm, 256, unit=V7X_SUBLANES)
    shift_m, scale_m, gate_m, shift_f, scale_f, gate_f = (mod[:, i][:, None, :] for i in range(6))
    d_hy, d_ret, n_heads, dk = p["d_hy"], p["d_ret"], p["n_heads"], p["dk"]

    proj = _in_proj(x, p["g_pre_m"], scale_m, shift_m, p["w_in"], rows_per_mod, tm)
    tables = _hyena_filters(seq, p["hy_w1"], p["hy_b1"], p["hy_w2"], p["hy_b2"], p["hy_w3"], p["hy_b3"],
                            p["hy_freq"], p["hy_decay"], d_hy)
    y_hy = _hyena(proj, batch, seq, p["hy_short_w"], p["hy_bias"], tables, d_hy)
    ret = _retention(proj, batch, seq, 3 * d_hy, n_heads, dk, p["ret_decay_logit"], p["ret_gn"],
                     rope, state0, want_state)
    y_ret = ret[0]
    x, h_ffn = _merge(proj, 3 * d_hy + 4 * d_ret, y_hy, y_ret, x, p["w_br_hy"], p["w_br_ret"], p["w_out"],
                      p["g_post_m"], gate_m, p["g_pre_f"], scale_f, shift_f, rows_per_mod, tm_merge)
    x = _ffn(x, h_ffn, seq, gate_f, p["g_post_f"], p["ffn_w_up"], p["ffn_conv"], p["ffn_w_down"], rows_per_mod, tm)
    return x.reshape(batch, seq, d), (ret[1] if want_state else None)


def kernel(x_prompt, x_sample, state_ret, c, c_ctx, w_ada, b_ada, norm_pre_mix, norm_post_mix, norm_pre_ffn,
           norm_post_ffn, w_in, hy_short_w, hy_w1, hy_b1, hy_w2, hy_b2, hy_w3, hy_b3, hy_freq, hy_decay, hy_bias,
           ret_decay_logit, ret_gn, w_br_hy, w_br_ret, w_out, ffn_w_up, ffn_conv, ffn_w_down):
    depth = w_in.shape[0]
    d = x_prompt.shape[-1]
    n_dec = x_sample.shape[0]
    n_heads, dk = state_ret.shape[3], state_ret.shape[4]
    d_hy = hy_bias.shape[-1]
    cc = jnp.zeros((V7X_SUBLANES, d), F32).at[0].set(c_ctx).at[1:1 + n_dec].set(c)
    rope = _rope_tables(x_sample.shape[1], dk)
    x_p, x_s = x_prompt, x_sample
    states = []
    for l in range(depth):
        p = dict(
            d_hy=d_hy, d_ret=n_heads * dk, n_heads=n_heads, dk=dk,
            g_pre_m=norm_pre_mix[l][None], g_post_m=norm_post_mix[l][None],
            g_pre_f=norm_pre_ffn[l][None], g_post_f=norm_post_ffn[l][None],
            w_in=w_in[l].astype(BF16), hy_short_w=hy_short_w[l],
            hy_w1=hy_w1[l], hy_b1=hy_b1[l], hy_w2=hy_w2[l], hy_b2=hy_b2[l], hy_w3=hy_w3[l], hy_b3=hy_b3[l],
            hy_freq=hy_freq[l], hy_decay=hy_decay[l], hy_bias=hy_bias[l],
            ret_decay_logit=ret_decay_logit[l], ret_gn=ret_gn[l],
            w_br_hy=w_br_hy[l].astype(BF16), w_br_ret=w_br_ret[l].astype(BF16), w_out=w_out[l].astype(BF16),
            ffn_w_up=ffn_w_up[l].astype(BF16), ffn_conv=ffn_conv[l], ffn_w_down=ffn_w_down[l].astype(BF16),
        )
        mod = _ada_mod(cc, w_ada[l], b_ada[l][None]).reshape(V7X_SUBLANES, 6, d)
        x_p, st = _block(x_p, mod[0:1], None, None, True, p)
        x_s, _ = _block(x_s, mod[1:1 + n_dec], rope, state_ret[:, l], False, p)
        states.append(st)
    return x_p, x_s, jnp.stack(states, axis=1)
```

```python
import functools
import math

import jax
import jax.numpy as jnp
import numpy as np
from jax import lax
from jax.experimental import pallas as pl
from jax.experimental.pallas import tpu as pltpu

F32 = jnp.float32
BF16 = jnp.bfloat16

RMS_EPS = 1e-6
GN_EPS = 1e-5
FILTER_EPS = 1e-6
HY_BANDS = 16
GRID_W = 64
ROPE_BASE = 10000.0

V7X_VMEM_LIMIT_BYTES = 56 * 1024 * 1024
V7X_LANES = 128
V7X_SUBLANES = 8


def _cparams(*sem):
    return pltpu.CompilerParams(dimension_semantics=sem, vmem_limit_bytes=V7X_VMEM_LIMIT_BYTES)


def _tile(n, target, unit=V7X_LANES):
    if n <= target:
        return n
    best = unit
    for t in range(unit, target + 1, unit):
        if n % t == 0:
            best = t
    assert n % best == 0, (n, target, unit)
    return best


def _row_tile(m, seq, per_seq_mod, target):
    if per_seq_mod or seq >= target:
        return seq
    return seq * _tile(m // seq, target // seq, unit=1)


def _resident(shape):
    return pl.BlockSpec(shape, lambda *_: (0,) * len(shape), pipeline_mode=pl.Buffered(1))


def _sigmoid(x):
    return 1.0 / (1.0 + jnp.exp(-x))


def _rms_scale(x, g):
    ms = jnp.mean(x * x, axis=-1, keepdims=True)
    return (x * lax.rsqrt(ms + RMS_EPS)) * g


def _dot(a, b):
    return jnp.dot(a, b, preferred_element_type=F32)


def _dot_exact(a, b):
    return jnp.dot(a, b, preferred_element_type=F32, precision=lax.Precision.HIGHEST)


def _split_bf16(table):
    hi = table.astype(BF16)
    lo = (table - hi.astype(np.float64)).astype(BF16)
    return jnp.asarray(hi), jnp.asarray(lo)


def _dot_split(a_hi_ref, a_lo_ref, b):
    b_hi = b.astype(BF16)
    b_lo = (b - b_hi.astype(F32)).astype(BF16)
    a_hi = a_hi_ref[...]
    return _dot(a_hi, b_hi) + (_dot(a_lo_ref[...], b_hi) + _dot(a_hi, b_lo))


def _dwconv3_rows(x, w, first, last):
    rows = x.shape[0]
    prev = jnp.where(first, 0.0, pltpu.roll(x, 1, 0))
    nxt = jnp.where(last, 0.0, pltpu.roll(x, rows - 1, 0))
    return prev * w[0:1] + x * w[1:2] + nxt * w[2:3]


def _ada_kernel(cc_ref, w_ref, b_ref, o_ref):
    cc = cc_ref[...]
    s = cc * _sigmoid(cc)
    o_ref[...] = _dot(s.astype(BF16), w_ref[...].astype(BF16)) + b_ref[...]


def _ada_mod(cc, w, b):
    d, n = w.shape
    tn = _tile(n, 1024)
    return pl.pallas_call(
        _ada_kernel,
        grid=(n // tn,),
        in_specs=[pl.BlockSpec((V7X_SUBLANES, d), lambda j: (0, 0)),
                  pl.BlockSpec((d, tn), lambda j: (0, j)),
                  pl.BlockSpec((1, tn), lambda j: (0, j))],
        out_specs=pl.BlockSpec((V7X_SUBLANES, tn), lambda j: (0, j)),
        out_shape=jax.ShapeDtypeStruct((V7X_SUBLANES, n), F32),
        compiler_params=_cparams("parallel"),
        name="ada_mod",
    )(cc, w, b)


def _in_proj_kernel(x_ref, g_ref, sc_ref, sh_ref, w_ref, hy_ref, ret_ref, h_ref, *, n_hy, n_qkv, n_g):
    j = pl.program_id(1)

    @pl.when(j == 0)
    def _():
        h = _rms_scale(x_ref[...], g_ref[...]) * (1.0 + sc_ref[...]) + sh_ref[...]
        h_ref[...] = h.astype(BF16)

    def tile():
        return _dot(h_ref[...], w_ref[...])

    @pl.when(j < n_hy)
    def _():
        hy_ref[...] = tile()

    @pl.when((j >= n_hy) & (j < n_hy + n_qkv))
    def _():
        ret_ref[...] = tile().astype(BF16)

    @pl.when((j >= n_hy + n_qkv) & (j < n_hy + n_qkv + n_g))
    def _():
        p = tile()
        ret_ref[...] = (p * _sigmoid(p)).astype(BF16)

    @pl.when(j >= n_hy + n_qkv + n_g)
    def _():
        ret_ref[...] = _sigmoid(tile()).astype(BF16)


def _in_proj(x, g, scale, shift, w, d_hy, d_ret, rows_per_mod, tm):
    m, d = x.shape
    n = w.shape[1]
    tn = _tile(math.gcd(3 * d_hy, d_ret, 2 * d), 1024)
    n_hy, n_qkv, n_g = 3 * d_hy // tn, 3 * d_ret // tn, d_ret // tn
    mod_spec = pl.BlockSpec((None, 1, d), lambda i, j: ((i * tm) // rows_per_mod, 0, 0))
    return pl.pallas_call(
        functools.partial(_in_proj_kernel, n_hy=n_hy, n_qkv=n_qkv, n_g=n_g),
        grid=(m // tm, n // tn),
        in_specs=[pl.BlockSpec((tm, d), lambda i, j: (i, 0)),
                  pl.BlockSpec((1, d), lambda i, j: (0, 0)),
                  mod_spec, mod_spec,
                  pl.BlockSpec((d, tn), lambda i, j: (0, j))],
        out_specs=[pl.BlockSpec((tm, tn), lambda i, j: (i, jnp.minimum(j, n_hy - 1))),
                   pl.BlockSpec((tm, tn), lambda i, j: (i, jnp.maximum(j - n_hy, 0)))],
        out_shape=[jax.ShapeDtypeStruct((m, 3 * d_hy), F32), jax.ShapeDtypeStruct((m, n - 3 * d_hy), BF16)],
        scratch_shapes=[pltpu.VMEM((tm, d), BF16)],
        compiler_params=_cparams("parallel", "arbitrary"),
        name="in_proj",
    )(x, g, scale, shift, w)


def _dft_tables(seq):
    n_fft = 2 * seq
    idx = np.arange(seq)
    ang = 2.0 * np.pi * ((idx[:, None] * idx[None, :]) % n_fft) / n_fft
    cos = np.cos(ang)
    msin = -np.sin(ang)
    sign = np.where(idx % 2 == 0, 1.0, -1.0)
    msin[0, :] = sign
    fwd = np.concatenate([cos, msin], axis=0)
    wgt = np.full((seq,), 2.0 / n_fft)
    wgt[0] = 1.0 / n_fft
    inv_re = cos.T * wgt[None, :]
    inv_im = msin.T * wgt[None, :]
    inv_im[:, 0] = sign / n_fft
    inv = np.concatenate([inv_re, inv_im], axis=1)
    return fwd, inv


def _filter_feats(seq):
    n = np.arange(seq, dtype=np.float64)
    t = n / seq
    f = np.linspace(1e-4, HY_BANDS - 1, HY_BANDS)
    w = 2.0 * math.pi * n / seq
    z = np.concatenate([t[:, None], np.cos(w[:, None] * f), np.sin(w[:, None] * f)], axis=-1)
    out = np.zeros((seq, V7X_LANES), np.float32)
    out[:, :z.shape[1]] = z
    return out


def _filter_kernel(z_ref, w1_ref, b1_ref, w2_ref, b2_ref, fr_ref, w3f_ref, w3b_ref, b3f_ref, b3b_ref,
                   decf_ref, decb_ref, chi_ref, clo_ref, shi_ref, slo_ref, kr_ref, kiz_ref, krn_ref, h2_ref):
    @pl.when(pl.program_id(0) == 0)
    def _():
        h1 = jnp.sin(fr_ref[0:1, :] * (_dot_exact(z_ref[...], w1_ref[...]) + b1_ref[...]))
        h2_ref[...] = jnp.sin(fr_ref[1:2, :] * (_dot_exact(h1, w2_ref[...]) + b2_ref[...]))

    h2 = h2_ref[...]
    seq, cw = kr_ref.shape
    t = z_ref[:, 0:1]
    row = lax.broadcasted_iota(jnp.int32, (seq, cw), 0)
    hf = (_dot_exact(h2, w3f_ref[...]) + b3f_ref[...]) * jnp.exp(-t * jnp.abs(decf_ref[...]))
    hb = (_dot_exact(h2, w3b_ref[...]) + b3b_ref[...]) * jnp.exp(-t * jnp.abs(decb_ref[...]))
    hb = jnp.where(row == 0, 0.0, hb)
    norm = (jnp.sum(jnp.abs(hf), axis=0, keepdims=True)
            + jnp.sum(jnp.abs(hb), axis=0, keepdims=True) + FILTER_EPS)
    inv = 1.0 / norm
    even = (hf + hb) * inv
    odd = (hf - hb) * inv
    kr = _dot_split(chi_ref, clo_ref, even)
    ki = _dot_split(shi_ref, slo_ref, odd)
    nyq = jnp.sum(jnp.where((row & 1) == 0, even, -even), axis=0, keepdims=True)
    kr_ref[...] = kr
    kiz_ref[...] = jnp.where(row == 0, 0.0, ki)
    krn_ref[...] = jnp.where(row == 0, nyq, kr)


def _hyena_filters(seq, w1, b1, w2, b2, w3, b3, freq, decay, d_hy):
    fh = w1.shape[1]
    pad = V7X_LANES
    w1p = jnp.zeros((pad, pad), F32).at[:w1.shape[0], :fh].set(w1)
    b1p = jnp.zeros((1, pad), F32).at[0, :fh].set(b1)
    w2p = jnp.zeros((pad, pad), F32).at[:fh, :fh].set(w2)
    b2p = jnp.zeros((1, pad), F32).at[0, :fh].set(b2)
    frp = jnp.zeros((2, pad), F32).at[:, :fh].set(freq)
    ncol = w3.shape[1] // 2
    w3p = jnp.zeros((pad, 2 * ncol), F32).at[:fh].set(w3)
    b3r = b3.reshape(1, 2 * ncol)
    dec = decay.reshape(1, 2 * ncol)
    fwd, _ = _dft_tables(seq)
    chi, clo = _split_bf16(fwd[:seq])
    shi, slo = _split_bf16(fwd[seq:])
    z = jnp.asarray(_filter_feats(seq))
    cw = _tile(ncol, 256)
    nb = ncol // cw
    full = lambda shape: pl.BlockSpec(shape, lambda j: (0, 0))
    colf = lambda rows: pl.BlockSpec((rows, cw), lambda j: (0, j))
    colb = lambda rows: pl.BlockSpec((rows, cw), lambda j: (0, nb + j))
    out = jax.ShapeDtypeStruct((seq, ncol), F32)
    return pl.pallas_call(
        _filter_kernel,
        grid=(nb,),
        in_specs=[full((seq, pad)), full((pad, pad)), full((1, pad)), full((pad, pad)), full((1, pad)),
                  full((2, pad)), colf(pad), colb(pad), colf(1), colb(1), colf(1), colb(1),
                  _resident((seq, seq)), _resident((seq, seq)), _resident((seq, seq)), _resident((seq, seq))],
        out_specs=[colf(seq), colf(seq), colf(seq)],
        out_shape=[out, out, out],
        scratch_shapes=[pltpu.VMEM((seq, pad), F32)],
        compiler_params=_cparams("arbitrary"),
        name=f"hy_filter_{seq}",
    )(z, w1p, b1p, w2p, b2p, frp, w3p, w3p, b3r, b3r, dec, dec, chi, clo, shi, slo)


def _hyena_kernel(x1_ref, x2_ref, v_ref, w1_ref, w2_ref, wv_ref, hb_ref,
                  kr0_ref, kiz0_ref, krn0_ref, kr1_ref, kiz1_ref, krn1_ref, f_ref, g_ref, o_ref, *, seq):
    cw = o_ref.shape[1]
    row = lax.broadcasted_iota(jnp.int32, (seq, cw), 0)
    first = row == 0
    last = row == seq - 1

    def long_conv(u, kr_ref, kiz_ref, krn_ref, bias):
        spec = _dot(f_ref[...], u.astype(BF16))
        ur = spec[:seq]
        ui = spec[seq:]
        kiz = kiz_ref[...]
        yr = ur * kr_ref[...] - ui * kiz
        yi = ur * kiz + ui * krn_ref[...]
        y = _dot(g_ref[:, :seq], yr.astype(BF16)) + _dot(g_ref[:, seq:], yi.astype(BF16))
        return y + u * bias

    for s in range(o_ref.shape[0] // seq):
        rs = slice(s * seq, (s + 1) * seq)
        x1 = _dwconv3_rows(x1_ref[rs], w1_ref[...], first, last)
        x2 = _dwconv3_rows(x2_ref[rs], w2_ref[...], first, last)
        v = _dwconv3_rows(v_ref[rs], wv_ref[...], first, last)
        z = x1 * long_conv(v, kr0_ref, kiz0_ref, krn0_ref, hb_ref[0:1])
        o_ref[rs] = (x2 * long_conv(z, kr1_ref, kiz1_ref, krn1_ref, hb_ref[1:2])).astype(BF16)


def _hyena(proj, batch, seq, short_w, hy_bias, tables, d_hy):
    kr, kiz, krn = tables
    cw = _tile(d_hy, 1024 if seq <= 256 else 256)
    nb = d_hy // cw
    fwd, inv = _dft_tables(seq)
    fmat = jnp.asarray(fwd, F32).astype(BF16)
    gmat = jnp.asarray(inv, F32).astype(BF16)
    bs = 2 if batch % 2 == 0 else 1
    col = lambda rows, off: pl.BlockSpec((rows, cw), lambda j, b: (0, off * nb + j))
    act = lambda off: pl.BlockSpec((bs * seq, cw), lambda j, b: (b, off * nb + j))
    return pl.pallas_call(
        functools.partial(_hyena_kernel, seq=seq),
        grid=(nb, batch // bs),
        in_specs=[act(0), act(1), act(2), col(3, 0), col(3, 1), col(3, 2), col(2, 0),
                  col(seq, 0), col(seq, 0), col(seq, 0), col(seq, 1), col(seq, 1), col(seq, 1),
                  _resident((2 * seq, seq)), _resident((seq, 2 * seq))],
        out_specs=pl.BlockSpec((bs * seq, cw), lambda j, b: (b, j)),
        out_shape=jax.ShapeDtypeStruct((batch * seq, d_hy), BF16),
        compiler_params=_cparams("parallel", "parallel"),
        name=f"hyena_{seq}",
    )(proj, proj, proj, short_w, short_w, short_w, hy_bias, kr, kiz, krn, kr, kiz, krn, fmat, gmat)


def _rope_tables(seq, dk):
    rows = seq // GRID_W
    row = jnp.repeat(jnp.arange(rows), GRID_W).astype(F32)
    col = jnp.tile(jnp.arange(GRID_W), rows).astype(F32)
    nfreq = dk // 4
    inv = ROPE_BASE ** (-jnp.arange(nfreq, dtype=F32) / nfreq)
    ang = jnp.concatenate([row[:, None] * inv, col[:, None] * inv], axis=-1)
    cos, sin = jnp.cos(ang), jnp.sin(ang)
    return jnp.concatenate([cos, cos], axis=-1), jnp.concatenate([-sin, sin], axis=-1)


def _retention_kernel(*refs, seq, hb, dk, use_rope, use_state, want_state):
    refs = list(refs)
    q_ref, k_ref, v_ref, g_ref, lg_ref, gn_ref = refs[:6]
    pos = 6
    if use_rope:
        cos_ref, sin_ref = refs[pos:pos + 2]
        pos += 2
    if use_state:
        s0_ref = refs[pos]
        pos += 1
    o_ref = refs[pos]
    pos += 1
    if want_state:
        st_ref = refs[pos]
        pos += 1
    d_ref = refs[pos]

    def log_gamma(hh, direction):
        return jnp.log(_sigmoid(lg_ref[hh, direction]))[:, 0:1]

    @pl.when(pl.program_id(1) == 0)
    def _():
        i = lax.broadcasted_iota(jnp.int32, (dk, dk), 0)
        j = lax.broadcasted_iota(jnp.int32, (dk, dk), 1)
        diff = (i - j).astype(F32)
        scale = dk ** -0.5
        for hh in range(hb):
            lf = log_gamma(hh, 0)
            lb = log_gamma(hh, 1)
            base_f = scale * jnp.exp(lf * diff)
            base_b = scale * jnp.exp(lb * (-diff))
            diag = jnp.where(diff >= 0, base_f, 0.0) + jnp.where(diff <= 0, base_b, 0.0)
            for bi in range(seq // dk):
                for bj in range(seq // dk):
                    if bi == bj:
                        blk = diag
                    elif bi > bj:
                        blk = base_f * jnp.exp(lf * float(dk * (bi - bj)))
                    else:
                        blk = base_b * jnp.exp(lb * float(dk * (bj - bi)))
                    d_ref[hh, bi * dk:(bi + 1) * dk, bj * dk:(bj + 1) * dk] = blk

    pos_f = lax.broadcasted_iota(jnp.int32, (seq, dk), 0).astype(F32)
    for hh in range(hb):
        sl = slice(hh * dk, (hh + 1) * dk)
        qb = q_ref[:, sl]
        kb16 = k_ref[:, sl]
        vb = v_ref[:, sl]
        if use_rope:
            cos = cos_ref[...]
            sin = sin_ref[...]
            q = qb.astype(F32)
            k = kb16.astype(F32)
            qb = (q * cos + pltpu.roll(q, dk // 2, 1) * sin).astype(BF16)
            kb16 = (k * cos + pltpu.roll(k, dk // 2, 1) * sin).astype(BF16)
        s = lax.dot_general(qb, kb16, (((1,), (1,)), ((), ())), preferred_element_type=F32)
        o = _dot((s * d_ref[hh]).astype(BF16), vb)
        lf = log_gamma(hh, 0)
        lb = log_gamma(hh, 1)
        if use_state:
            o = o + _dot(qb, s0_ref[0, hh].astype(BF16)) * jnp.exp(lf * (pos_f + 1.0))
            o = o + _dot(qb, s0_ref[1, hh].astype(BF16)) * jnp.exp(lb * (seq - pos_f))
        if want_state:
            k = kb16.astype(F32) * dk ** -0.5
            kf = (k * jnp.exp(lf * (seq - 1.0 - pos_f))).astype(BF16)
            kb = (k * jnp.exp(lb * pos_f)).astype(BF16)
            tn = (((0,), (0,)), ((), ()))
            st_ref[0, hh] = lax.dot_general(kf, vb, tn, preferred_element_type=F32)
            st_ref[1, hh] = lax.dot_general(kb, vb, tn, preferred_element_type=F32)
        mu = jnp.mean(o, axis=-1, keepdims=True)
        oc = o - mu
        var = jnp.mean(oc * oc, axis=-1, keepdims=True)
        y = (oc * lax.rsqrt(var + GN_EPS)) * gn_ref[:, sl] * g_ref[:, sl].astype(F32)
        o_ref[:, sl] = y.astype(BF16)


def _retention(proj, batch, seq, n_heads, dk, decay_logit, ret_gn, rope, state0, want_state):
    d_ret = n_heads * dk
    hb = n_heads if seq <= 256 else min(n_heads, 2)
    bw = hb * dk
    nhb = n_heads // hb
    lg = jnp.broadcast_to(decay_logit.T[:, :, None, None], (n_heads, 2, 1, V7X_LANES))
    act = lambda part: pl.BlockSpec((seq, bw), lambda h, b: (b, (part * d_ret) // bw + h))
    in_specs = [act(0), act(1), act(2), act(3),
                pl.BlockSpec((hb, 2, 1, V7X_LANES), lambda h, b: (h, 0, 0, 0)),
                pl.BlockSpec((1, bw), lambda h, b: (0, h))]
    args = [proj, proj, proj, proj, lg, ret_gn.reshape(1, d_ret)]
    if rope is not None:
        in_specs += [pl.BlockSpec((seq, dk), lambda h, b: (0, 0))] * 2
        args += list(rope)
    if state0 is not None:
        in_specs.append(pl.BlockSpec((None, 2, hb, dk, dk), lambda h, b: (b, 0, h, 0, 0)))
        args.append(state0)
    out_specs = [pl.BlockSpec((seq, bw), lambda h, b: (b, h))]
    out_shape = [jax.ShapeDtypeStruct((batch * seq, d_ret), BF16)]
    if want_state:
        out_specs.append(pl.BlockSpec((None, 2, hb, dk, dk), lambda h, b: (b, 0, h, 0, 0)))
        out_shape.append(jax.ShapeDtypeStruct((batch, 2, n_heads, dk, dk), F32))
    body = functools.partial(_retention_kernel, seq=seq, hb=hb, dk=dk, use_rope=rope is not None,
                             use_state=state0 is not None, want_state=want_state)
    return pl.pallas_call(
        body,
        grid=(nhb, batch),
        in_specs=in_specs,
        out_specs=out_specs,
        out_shape=out_shape,
        scratch_shapes=[pltpu.VMEM((hb, seq, seq), F32)],
        compiler_params=_cparams("parallel", "arbitrary"),
        name=f"retention_{seq}",
    )(*args)


def _merge_kernel(*refs, nblk):
    gate_refs = refs[:2 * nblk]
    (yhy_ref, yret_ref, x_ref, wbh_ref, wbr_ref, wo_ref, gpost_ref, gm_ref, gpre_ref, sc_ref, sh_ref,
     o_ref, h_ref) = refs[2 * nblk:]
    a = _dot(yhy_ref[...], wbh_ref[...])
    b = _dot(yret_ref[...], wbr_ref[...])
    wblk = gate_refs[0].shape[1]
    parts = []
    for kk in range(nblk):
        sl = slice(kk * wblk, (kk + 1) * wblk)
        g_hy = gate_refs[kk][...].astype(F32)
        g_ret = gate_refs[nblk + kk][...].astype(F32)
        parts.append((g_hy * a[:, sl] + g_ret * b[:, sl]).astype(BF16))
    merged = parts[0] if nblk == 1 else jnp.concatenate(parts, axis=1)
    out = _dot(merged, wo_ref[...])
    x1 = x_ref[...] + gm_ref[...] * _rms_scale(out, gpost_ref[...])
    o_ref[...] = x1
    h_ref[...] = (_rms_scale(x1, gpre_ref[...]) * (1.0 + sc_ref[...]) + sh_ref[...]).astype(BF16)


def _merge(proj, col0, y_hy, y_ret, x, w_br_hy, w_br_ret, w_out, g_post, gate_m, g_pre_f, scale_f, shift_f,
           rows_per_mod, tm):
    m, d = x.shape
    vec = pl.BlockSpec((1, d), lambda i: (0, 0))
    mod_spec = pl.BlockSpec((None, 1, d), lambda i: ((i * tm) // rows_per_mod, 0, 0))
    wblk = math.gcd(col0, d)
    nblk = d // wblk
    gate_spec = lambda kk: pl.BlockSpec((tm, wblk), lambda i: (i, col0 // wblk + kk))
    const = lambda arr: _resident(arr.shape)
    row = lambda width: pl.BlockSpec((tm, width), lambda i: (i, 0))
    return pl.pallas_call(
        functools.partial(_merge_kernel, nblk=nblk),
        grid=(m // tm,),
        in_specs=[gate_spec(kk) for kk in range(2 * nblk)]
        + [row(y_hy.shape[1]), row(y_ret.shape[1]), row(d), const(w_br_hy), const(w_br_ret), const(w_out),
           vec, mod_spec, vec, mod_spec, mod_spec],
        out_specs=[row(d), row(d)],
        out_shape=[jax.ShapeDtypeStruct((m, d), F32), jax.ShapeDtypeStruct((m, d), BF16)],
        compiler_params=_cparams("parallel"),
        name="merge",
    )(*([proj] * (2 * nblk)), y_hy, y_ret, x, w_br_hy, w_br_ret, w_out, g_post, gate_m, g_pre_f, scale_f, shift_f)


def _ffn_kernel(x_ref, h_ref, gate_ref, gpost_ref, wa_ref, wb_ref, ca_ref, cb_ref, wd_ref, o_ref, act_ref, *, seq):
    j = pl.program_id(1)
    nf = pl.num_programs(1) - 1
    tm = x_ref.shape[0]
    tf = wa_ref.shape[1]

    def up(slot):
        pos = lax.broadcasted_iota(jnp.int32, (tm, tf), 0) % seq
        first = pos == 0
        last = pos == seq - 1
        h = h_ref[...]
        a = _dwconv3_rows(_dot(h, wa_ref[...]), ca_ref[...], first, last)
        b = _dwconv3_rows(_dot(h, wb_ref[...]), cb_ref[...], first, last)
        gelu = 0.5 * a * (1.0 + jnp.tanh(math.sqrt(2.0 / math.pi) * (a + 0.044715 * (a * a * a))))
        act_ref[slot] = (gelu * b).astype(BF16)

    def down(slot):
        return _dot(act_ref[slot], wd_ref[...])

    @pl.when(j == 0)
    def _():
        up(0)

    @pl.when(j == 1)
    def _():
        o_ref[...] = down(0)
        up(1)

    @pl.when((j > 1) & (j < nf))
    def _():
        slot = j % 2
        o_ref[...] += down(1 - slot)
        up(slot)

    @pl.when(j == nf)
    def _():
        f = o_ref[...] + down((nf - 1) % 2)
        o_ref[...] = x_ref[...] + gate_ref[...] * _rms_scale(f, gpost_ref[...])


def _ffn(x, h, seq, gate, g_post, w_up, conv_w, w_down, rows_per_mod, tm):
    m, d = x.shape
    d_ff = w_down.shape[0]
    tf = _tile(d_ff, 512)
    nf = d_ff // tf
    assert nf >= 2, (d_ff, tf)
    once = pl.BlockSpec((tm, d), lambda i, j: (i, 0), pipeline_mode=pl.Buffered(1))
    up_blk = lambda j: jnp.minimum(j, nf - 1)
    down_blk = lambda j: jnp.maximum(j - 1, 0)
    return pl.pallas_call(
        functools.partial(_ffn_kernel, seq=seq),
        grid=(m // tm, nf + 1),
        in_specs=[once, once,
                  pl.BlockSpec((None, 1, d), lambda i, j: ((i * tm) // rows_per_mod, 0, 0)),
                  pl.BlockSpec((1, d), lambda i, j: (0, 0)),
                  pl.BlockSpec((d, tf), lambda i, j: (0, up_blk(j))),
                  pl.BlockSpec((d, tf), lambda i, j: (0, nf + up_blk(j))),
                  pl.BlockSpec((3, tf), lambda i, j: (0, up_blk(j))),
                  pl.BlockSpec((3, tf), lambda i, j: (0, nf + up_blk(j))),
                  pl.BlockSpec((tf, d), lambda i, j: (down_blk(j), 0))],
        out_specs=pl.BlockSpec((tm, d), lambda i, j: (i, 0)),
        out_shape=jax.ShapeDtypeStruct((m, d), F32),
        scratch_shapes=[pltpu.VMEM((2, tm, tf), BF16)],
        compiler_params=_cparams("parallel", "arbitrary"),
        name="ffn",
    )(x, h, gate, g_post, w_up, w_up, conv_w, conv_w, w_down)


def _block(x3, mod, rope, state0, want_state, p):
    batch, seq, d = x3.shape
    m = batch * seq
    x = x3.reshape(m, d)
    per_seq_mod = mod.shape[0] != 1
    rows_per_mod = seq if per_seq_mod else m
    tm = _row_tile(m, seq, per_seq_mod, 1024)
    tm_merge = _tile(tm, 512, unit=V7X_SUBLANES)
    shift_m, scale_m, gate_m, shift_f, scale_f, gate_f = (mod[:, i][:, None, :] for i in range(6))
    d_hy, d_ret, n_heads, dk = p["d_hy"], p["d_ret"], p["n_heads"], p["dk"]

    hy_in, ret_in = _in_proj(x, p["g_pre_m"], scale_m, shift_m, p["w_in"], d_hy, d_ret, rows_per_mod, tm)
    tables = _hyena_filters(seq, p["hy_w1"], p["hy_b1"], p["hy_w2"], p["hy_b2"], p["hy_w3"], p["hy_b3"],
                            p["hy_freq"], p["hy_decay"], d_hy)
    y_hy = _hyena(hy_in, batch, seq, p["hy_short_w"], p["hy_bias"], tables, d_hy)
    ret = _retention(ret_in, batch, seq, n_heads, dk, p["ret_decay_logit"], p["ret_gn"], rope, state0, want_state)
    y_ret = ret[0]
    x, h_ffn = _merge(ret_in, 4 * d_ret, y_hy, y_ret, x, p["w_br_hy"], p["w_br_ret"], p["w_out"],
                      p["g_post_m"], gate_m, p["g_pre_f"], scale_f, shift_f, rows_per_mod, tm_merge)
    x = _ffn(x, h_ffn, seq, gate_f, p["g_post_f"], p["ffn_w_up"], p["ffn_conv"], p["ffn_w_down"], rows_per_mod, tm)
    return x.reshape(batch, seq, d), (ret[1] if want_state else None)


def kernel(x_prompt, x_sample, state_ret, c, c_ctx, w_ada, b_ada, norm_pre_mix, norm_post_mix, norm_pre_ffn,
           norm_post_ffn, w_in, hy_short_w, hy_w1, hy_b1, hy_w2, hy_b2, hy_w3, hy_b3, hy_freq, hy_decay, hy_bias,
           ret_decay_logit, ret_gn, w_br_hy, w_br_ret, w_out, ffn_w_up, ffn_conv, ffn_w_down):
    depth = w_in.shape[0]
    d = x_prompt.shape[-1]
    n_dec = x_sample.shape[0]
    n_heads, dk = state_ret.shape[3], state_ret.shape[4]
    d_hy = hy_bias.shape[-1]
    cc = jnp.zeros((V7X_SUBLANES, d), F32).at[0].set(c_ctx).at[1:1 + n_dec].set(c)
    rope = _rope_tables(x_sample.shape[1], dk)
    x_p, x_s = x_prompt, x_sample
    states = []
    for l in range(depth):
        p = dict(
            d_hy=d_hy, d_ret=n_heads * dk, n_heads=n_heads, dk=dk,
            g_pre_m=norm_pre_mix[l][None], g_post_m=norm_post_mix[l][None],
            g_pre_f=norm_pre_ffn[l][None], g_post_f=norm_post_ffn[l][None],
            w_in=w_in[l].astype(BF16), hy_short_w=hy_short_w[l],
            hy_w1=hy_w1[l], hy_b1=hy_b1[l], hy_w2=hy_w2[l], hy_b2=hy_b2[l], hy_w3=hy_w3[l], hy_b3=hy_b3[l],
            hy_freq=hy_freq[l], hy_decay=hy_decay[l], hy_bias=hy_bias[l],
            ret_decay_logit=ret_decay_logit[l], ret_gn=ret_gn[l],
            w_br_hy=w_br_hy[l].astype(BF16), w_br_ret=w_br_ret[l].astype(BF16), w_out=w_out[l].astype(BF16),
            ffn_w_up=ffn_w_up[l].astype(BF16), ffn_conv=ffn_conv[l], ffn_w_down=ffn_w_down[l].astype(BF16),
        )
        mod = _ada_mod(cc, w_ada[l], b_ada[l][None]).reshape(V7X_SUBLANES, 6, d)
        x_p, st = _block(x_p, mod[0:1], None, None, True, p)
        x_s, _ = _block(x_s, mod[1:1 + n_dec], rope, state_ret[:, l], False, p)
        states.append(st)
    return x_p, x_s, jnp.stack(states, axis=1)
```

```python
import functools
import math

import jax
import jax.numpy as jnp
import numpy as np
from jax import lax
from jax.experimental import pallas as pl
from jax.experimental.pallas import tpu as pltpu

F32 = jnp.float32
BF16 = jnp.bfloat16

RMS_EPS = 1e-6
GN_EPS = 1e-5
FILTER_EPS = 1e-6
HY_BANDS = 16
GRID_W = 64
ROPE_BASE = 10000.0

V7X_VMEM_LIMIT_BYTES = 56 * 1024 * 1024
V7X_LANES = 128
V7X_SUBLANES = 8


def _cparams(*sem):
    return pltpu.CompilerParams(dimension_semantics=sem, vmem_limit_bytes=V7X_VMEM_LIMIT_BYTES)


def _tile(n, target, unit=V7X_LANES):
    if n <= target:
        return n
    best = unit
    for t in range(unit, target + 1, unit):
        if n % t == 0:
            best = t
    assert n % best == 0, (n, target, unit)
    return best


def _row_tile(m, seq, per_seq_mod, target):
    if per_seq_mod or seq >= target:
        return seq
    return seq * _tile(m // seq, target // seq, unit=1)


def _resident(shape):
    return pl.BlockSpec(shape, lambda *_: (0,) * len(shape), pipeline_mode=pl.Buffered(1))


def _sigmoid(x):
    return 1.0 / (1.0 + jnp.exp(-x))


def _rms_scale(x, g):
    ms = jnp.mean(x * x, axis=-1, keepdims=True)
    return (x * lax.rsqrt(ms + RMS_EPS)) * g


def _dot(a, b):
    return jnp.dot(a, b, preferred_element_type=F32)


def _dot_exact(a, b):
    return jnp.dot(a, b, preferred_element_type=F32, precision=lax.Precision.HIGHEST)


def _split_bf16(table):
    hi = table.astype(BF16)
    lo = (table - hi.astype(np.float64)).astype(BF16)
    return jnp.asarray(hi), jnp.asarray(lo)


def _dot_split(a_hi_ref, a_lo_ref, b):
    b_hi = b.astype(BF16)
    b_lo = (b - b_hi.astype(F32)).astype(BF16)
    a_hi = a_hi_ref[...]
    return _dot(a_hi, b_hi) + (_dot(a_lo_ref[...], b_hi) + _dot(a_hi, b_lo))


def _dwconv3_rows(x, w, first, last):
    rows = x.shape[0]
    prev = jnp.where(first, 0.0, pltpu.roll(x, 1, 0))
    nxt = jnp.where(last, 0.0, pltpu.roll(x, rows - 1, 0))
    return prev * w[0:1] + x * w[1:2] + nxt * w[2:3]


def _ada_kernel(cc_ref, w_ref, b_ref, o_ref):
    cc = cc_ref[...]
    s = cc * _sigmoid(cc)
    o_ref[...] = _dot(s.astype(BF16), w_ref[...].astype(BF16)) + b_ref[...]


def _ada_mod(cc, w, b):
    d, n = w.shape
    tn = _tile(n, 1024)
    return pl.pallas_call(
        _ada_kernel,
        grid=(n // tn,),
        in_specs=[pl.BlockSpec((V7X_SUBLANES, d), lambda j: (0, 0)),
                  pl.BlockSpec((d, tn), lambda j: (0, j)),
                  pl.BlockSpec((1, tn), lambda j: (0, j))],
        out_specs=pl.BlockSpec((V7X_SUBLANES, tn), lambda j: (0, j)),
        out_shape=jax.ShapeDtypeStruct((V7X_SUBLANES, n), F32),
        compiler_params=_cparams("parallel"),
        name="ada_mod",
    )(cc, w, b)


def _in_proj_kernel(*refs, n_hy, n_qkv, n_g, n_cast, emit_w):
    x_ref, g_ref, sc_ref, sh_ref, w_ref = refs[:5]
    cast_in = refs[5:5 + n_cast]
    hy_ref, ret_ref = refs[5 + n_cast:7 + n_cast]
    pos = 7 + n_cast
    if emit_w:
        wbf_ref = refs[pos]
        pos += 1
    cast_out = refs[pos:pos + n_cast]
    h_ref = refs[pos + n_cast]
    j = pl.program_id(1)

    @pl.when(j == 0)
    def _():
        h = _rms_scale(x_ref[...], g_ref[...]) * (1.0 + sc_ref[...]) + sh_ref[...]
        h_ref[...] = h.astype(BF16)

    def tile():
        w = w_ref[...]
        if emit_w:
            w = w.astype(BF16)

            @pl.when(pl.program_id(0) == 0)
            def _():
                wbf_ref[...] = w
        for src, dst in zip(cast_in, cast_out):
            dst[...] = src[...].astype(BF16)
        return _dot(h_ref[...], w)

    @pl.when(j < n_hy)
    def _():
        hy_ref[...] = tile()

    @pl.when((j >= n_hy) & (j < n_hy + n_qkv))
    def _():
        ret_ref[...] = tile().astype(BF16)

    @pl.when((j >= n_hy + n_qkv) & (j < n_hy + n_qkv + n_g))
    def _():
        p = tile()
        ret_ref[...] = (p * _sigmoid(p)).astype(BF16)

    @pl.when(j >= n_hy + n_qkv + n_g)
    def _():
        ret_ref[...] = _sigmoid(tile()).astype(BF16)


V7X_BF16_ROW_TILE = 2 * V7X_SUBLANES


def _cast_rows(arr, n_steps):
    rows = arr.shape[0]
    for blk in range(V7X_BF16_ROW_TILE, rows + 1, V7X_BF16_ROW_TILE):
        if rows % blk == 0 and rows // blk <= n_steps:
            return blk
    return None


def _in_proj(x, g, scale, shift, w, d_hy, d_ret, rows_per_mod, tm, cast=()):
    m, d = x.shape
    n = w.shape[1]
    emit_w = w.dtype != BF16
    tn = _tile(math.gcd(3 * d_hy, d_ret, 2 * d), 512 if emit_w else 1024)
    n_hy, n_qkv, n_g = 3 * d_hy // tn, 3 * d_ret // tn, d_ret // tn
    n_col = n // tn
    mod_spec = pl.BlockSpec((None, 1, d), lambda i, j: ((i * tm) // rows_per_mod, 0, 0))
    blks = [_cast_rows(a, (m // tm) * n_col) for a in cast]
    assert all(b is not None for b in blks), [a.shape for a in cast]
    cast_specs = [pl.BlockSpec((b, a.shape[1]), lambda i, j, nb=a.shape[0] // b: (jnp.minimum(i * n_col + j, nb - 1), 0))
                  for a, b in zip(cast, blks)]
    out_specs = [pl.BlockSpec((tm, tn), lambda i, j: (i, jnp.minimum(j, n_hy - 1))),
                 pl.BlockSpec((tm, tn), lambda i, j: (i, jnp.maximum(j - n_hy, 0)))]
    out_shape = [jax.ShapeDtypeStruct((m, 3 * d_hy), F32), jax.ShapeDtypeStruct((m, n - 3 * d_hy), BF16)]
    if emit_w:
        out_specs.append(pl.BlockSpec((d, tn), lambda i, j: (0, jnp.where(i == 0, j, n_col - 1))))
        out_shape.append(jax.ShapeDtypeStruct(w.shape, BF16))
    out_specs += cast_specs
    out_shape += [jax.ShapeDtypeStruct(a.shape, BF16) for a in cast]
    return pl.pallas_call(
        functools.partial(_in_proj_kernel, n_hy=n_hy, n_qkv=n_qkv, n_g=n_g, n_cast=len(cast), emit_w=emit_w),
        grid=(m // tm, n_col),
        in_specs=[pl.BlockSpec((tm, d), lambda i, j: (i, 0)),
                  pl.BlockSpec((1, d), lambda i, j: (0, 0)),
                  mod_spec, mod_spec,
                  pl.BlockSpec((d, tn), lambda i, j: (0, j))] + cast_specs,
        out_specs=out_specs,
        out_shape=out_shape,
        scratch_shapes=[pltpu.VMEM((tm, d), BF16)],
        compiler_params=_cparams("arbitrary", "arbitrary"),
        name="in_proj",
    )(x, g, scale, shift, w, *cast)


def _dft_tables(seq):
    n_fft = 2 * seq
    idx = np.arange(seq)
    ang = 2.0 * np.pi * ((idx[:, None] * idx[None, :]) % n_fft) / n_fft
    cos = np.cos(ang)
    msin = -np.sin(ang)
    sign = np.where(idx % 2 == 0, 1.0, -1.0)
    msin[0, :] = sign
    fwd = np.concatenate([cos, msin], axis=0)
    wgt = np.full((seq,), 2.0 / n_fft)
    wgt[0] = 1.0 / n_fft
    inv_re = cos.T * wgt[None, :]
    inv_im = msin.T * wgt[None, :]
    inv_im[:, 0] = sign / n_fft
    inv = np.concatenate([inv_re, inv_im], axis=1)
    return fwd, inv


def _filter_feats(seq):
    n = np.arange(seq, dtype=np.float64)
    t = n / seq
    f = np.linspace(1e-4, HY_BANDS - 1, HY_BANDS)
    w = 2.0 * math.pi * n / seq
    z = np.concatenate([t[:, None], np.cos(w[:, None] * f), np.sin(w[:, None] * f)], axis=-1)
    out = np.zeros((seq, V7X_LANES), np.float32)
    out[:, :z.shape[1]] = z
    return out


def _filter_kernel(z_ref, w1_ref, b1_ref, w2_ref, b2_ref, fr_ref, w3f_ref, w3b_ref, b3f_ref, b3b_ref,
                   decf_ref, decb_ref, chi_ref, clo_ref, shi_ref, slo_ref, kr_ref, kiz_ref, krn_ref, h2_ref):
    @pl.when(pl.program_id(0) == 0)
    def _():
        h1 = jnp.sin(fr_ref[0:1, :] * (_dot_exact(z_ref[...], w1_ref[...]) + b1_ref[...]))
        h2_ref[...] = jnp.sin(fr_ref[1:2, :] * (_dot_exact(h1, w2_ref[...]) + b2_ref[...]))

    h2 = h2_ref[...]
    seq, cw = kr_ref.shape
    t = z_ref[:, 0:1]
    row = lax.broadcasted_iota(jnp.int32, (seq, cw), 0)
    hf = (_dot_exact(h2, w3f_ref[...]) + b3f_ref[...]) * jnp.exp(-t * jnp.abs(decf_ref[...]))
    hb = (_dot_exact(h2, w3b_ref[...]) + b3b_ref[...]) * jnp.exp(-t * jnp.abs(decb_ref[...]))
    hb = jnp.where(row == 0, 0.0, hb)
    norm = (jnp.sum(jnp.abs(hf), axis=0, keepdims=True)
            + jnp.sum(jnp.abs(hb), axis=0, keepdims=True) + FILTER_EPS)
    inv = 1.0 / norm
    even = (hf + hb) * inv
    odd = (hf - hb) * inv
    kr = _dot_split(chi_ref, clo_ref, even)
    ki = _dot_split(shi_ref, slo_ref, odd)
    nyq = jnp.sum(jnp.where((row & 1) == 0, even, -even), axis=0, keepdims=True)
    kr_ref[...] = kr
    kiz_ref[...] = jnp.where(row == 0, 0.0, ki)
    krn_ref[...] = jnp.where(row == 0, nyq, kr)


def _hyena_filters(seq, w1, b1, w2, b2, w3, b3, freq, decay, d_hy):
    fh = w1.shape[1]
    pad = V7X_LANES
    w1p = jnp.zeros((pad, pad), F32).at[:w1.shape[0], :fh].set(w1)
    b1p = jnp.zeros((1, pad), F32).at[0, :fh].set(b1)
    w2p = jnp.zeros((pad, pad), F32).at[:fh, :fh].set(w2)
    b2p = jnp.zeros((1, pad), F32).at[0, :fh].set(b2)
    frp = jnp.zeros((2, pad), F32).at[:, :fh].set(freq)
    ncol = w3.shape[1] // 2
    w3p = jnp.zeros((pad, 2 * ncol), F32).at[:fh].set(w3)
    b3r = b3.reshape(1, 2 * ncol)
    dec = decay.reshape(1, 2 * ncol)
    fwd, _ = _dft_tables(seq)
    chi, clo = _split_bf16(fwd[:seq])
    shi, slo = _split_bf16(fwd[seq:])
    z = jnp.asarray(_filter_feats(seq))
    cw = _tile(ncol, 256)
    nb = ncol // cw
    full = lambda shape: pl.BlockSpec(shape, lambda j: (0, 0))
    colf = lambda rows: pl.BlockSpec((rows, cw), lambda j: (0, j))
    colb = lambda rows: pl.BlockSpec((rows, cw), lambda j: (0, nb + j))
    out = jax.ShapeDtypeStruct((seq, ncol), F32)
    return pl.pallas_call(
        _filter_kernel,
        grid=(nb,),
        in_specs=[full((seq, pad)), full((pad, pad)), full((1, pad)), full((pad, pad)), full((1, pad)),
                  full((2, pad)), colf(pad), colb(pad), colf(1), colb(1), colf(1), colb(1),
                  _resident((seq, seq)), _resident((seq, seq)), _resident((seq, seq)), _resident((seq, seq))],
        out_specs=[colf(seq), colf(seq), colf(seq)],
        out_shape=[out, out, out],
        scratch_shapes=[pltpu.VMEM((seq, pad), F32)],
        compiler_params=_cparams("arbitrary"),
        name=f"hy_filter_{seq}",
    )(z, w1p, b1p, w2p, b2p, frp, w3p, w3p, b3r, b3r, dec, dec, chi, clo, shi, slo)


def _hyena_kernel(x1_ref, x2_ref, v_ref, w1_ref, w2_ref, wv_ref, hb_ref,
                  kr0_ref, kiz0_ref, krn0_ref, kr1_ref, kiz1_ref, krn1_ref, f_ref, g_ref, o_ref, *, seq):
    cw = o_ref.shape[1]
    row = lax.broadcasted_iota(jnp.int32, (seq, cw), 0)
    first = row == 0
    last = row == seq - 1

    def long_conv(u, kr_ref, kiz_ref, krn_ref, bias):
        spec = _dot(f_ref[...], u.astype(BF16))
        ur = spec[:seq]
        ui = spec[seq:]
        kiz = kiz_ref[...]
        yr = ur * kr_ref[...] - ui * kiz
        yi = ur * kiz + ui * krn_ref[...]
        y = _dot(g_ref[:, :seq], yr.astype(BF16)) + _dot(g_ref[:, seq:], yi.astype(BF16))
        return y + u * bias

    for s in range(o_ref.shape[0] // seq):
        rs = slice(s * seq, (s + 1) * seq)
        x1 = _dwconv3_rows(x1_ref[rs], w1_ref[...], first, last)
        x2 = _dwconv3_rows(x2_ref[rs], w2_ref[...], first, last)
        v = _dwconv3_rows(v_ref[rs], wv_ref[...], first, last)
        z = x1 * long_conv(v, kr0_ref, kiz0_ref, krn0_ref, hb_ref[0:1])
        o_ref[rs] = (x2 * long_conv(z, kr1_ref, kiz1_ref, krn1_ref, hb_ref[1:2])).astype(BF16)


def _hyena(proj, batch, seq, short_w, hy_bias, tables, d_hy):
    kr, kiz, krn = tables
    cw = _tile(d_hy, 1024 if seq <= 256 else 256)
    nb = d_hy // cw
    fwd, inv = _dft_tables(seq)
    fmat = jnp.asarray(fwd, F32).astype(BF16)
    gmat = jnp.asarray(inv, F32).astype(BF16)
    bs = 2 if batch % 2 == 0 else 1
    col = lambda rows, off: pl.BlockSpec((rows, cw), lambda j, b: (0, off * nb + j))
    act = lambda off: pl.BlockSpec((bs * seq, cw), lambda j, b: (b, off * nb + j))
    return pl.pallas_call(
        functools.partial(_hyena_kernel, seq=seq),
        grid=(nb, batch // bs),
        in_specs=[act(0), act(1), act(2), col(3, 0), col(3, 1), col(3, 2), col(2, 0),
                  col(seq, 0), col(seq, 0), col(seq, 0), col(seq, 1), col(seq, 1), col(seq, 1),
                  _resident((2 * seq, seq)), _resident((seq, 2 * seq))],
        out_specs=pl.BlockSpec((bs * seq, cw), lambda j, b: (b, j)),
        out_shape=jax.ShapeDtypeStruct((batch * seq, d_hy), BF16),
        compiler_params=_cparams("parallel", "parallel"),
        name=f"hyena_{seq}",
    )(proj, proj, proj, short_w, short_w, short_w, hy_bias, kr, kiz, krn, kr, kiz, krn, fmat, gmat)


def _rope_tables(seq, dk):
    rows = seq // GRID_W
    row = jnp.repeat(jnp.arange(rows), GRID_W).astype(F32)
    col = jnp.tile(jnp.arange(GRID_W), rows).astype(F32)
    nfreq = dk // 4
    inv = ROPE_BASE ** (-jnp.arange(nfreq, dtype=F32) / nfreq)
    ang = jnp.concatenate([row[:, None] * inv, col[:, None] * inv], axis=-1)
    cos, sin = jnp.cos(ang), jnp.sin(ang)
    return jnp.concatenate([cos, cos], axis=-1), jnp.concatenate([-sin, sin], axis=-1)


def _retention_kernel(*refs, seq, hb, dk, use_rope, use_state, want_state):
    refs = list(refs)
    q_ref, k_ref, v_ref, g_ref, lg_ref, gn_ref = refs[:6]
    pos = 6
    if use_rope:
        cos_ref, sin_ref = refs[pos:pos + 2]
        pos += 2
    if use_state:
        s0_ref = refs[pos]
        pos += 1
    o_ref = refs[pos]
    pos += 1
    if want_state:
        st_ref = refs[pos]
        pos += 1
    d_ref = refs[pos]

    def log_gamma(hh, direction):
        return jnp.log(_sigmoid(lg_ref[hh, direction]))[:, 0:1]

    @pl.when(pl.program_id(1) == 0)
    def _():
        i = lax.broadcasted_iota(jnp.int32, (dk, dk), 0)
        j = lax.broadcasted_iota(jnp.int32, (dk, dk), 1)
        diff = (i - j).astype(F32)
        scale = dk ** -0.5
        for hh in range(hb):
            lf = log_gamma(hh, 0)
            lb = log_gamma(hh, 1)
            base_f = scale * jnp.exp(lf * diff)
            base_b = scale * jnp.exp(lb * (-diff))
            diag = jnp.where(diff >= 0, base_f, 0.0) + jnp.where(diff <= 0, base_b, 0.0)
            for bi in range(seq // dk):
                for bj in range(seq // dk):
                    if bi == bj:
                        blk = diag
                    elif bi > bj:
                        blk = base_f * jnp.exp(lf * float(dk * (bi - bj)))
                    else:
                        blk = base_b * jnp.exp(lb * float(dk * (bj - bi)))
                    d_ref[hh, bi * dk:(bi + 1) * dk, bj * dk:(bj + 1) * dk] = blk

    pos_f = lax.broadcasted_iota(jnp.int32, (seq, dk), 0).astype(F32)
    for hh in range(hb):
        sl = slice(hh * dk, (hh + 1) * dk)
        qb = q_ref[:, sl]
        kb16 = k_ref[:, sl]
        vb = v_ref[:, sl]
        if use_rope:
            cos = cos_ref[...]
            sin = sin_ref[...]
            q = qb.astype(F32)
            k = kb16.astype(F32)
            qb = (q * cos + pltpu.roll(q, dk // 2, 1) * sin).astype(BF16)
            kb16 = (k * cos + pltpu.roll(k, dk // 2, 1) * sin).astype(BF16)
        s = lax.dot_general(qb, kb16, (((1,), (1,)), ((), ())), preferred_element_type=F32)
        o = _dot((s * d_ref[hh]).astype(BF16), vb)
        lf = log_gamma(hh, 0)
        lb = log_gamma(hh, 1)
        if use_state:
            o = o + _dot(qb, s0_ref[0, hh].astype(BF16)) * jnp.exp(lf * (pos_f + 1.0))
            o = o + _dot(qb, s0_ref[1, hh].astype(BF16)) * jnp.exp(lb * (seq - pos_f))
        if want_state:
            k = kb16.astype(F32) * dk ** -0.5
            kf = (k * jnp.exp(lf * (seq - 1.0 - pos_f))).astype(BF16)
            kb = (k * jnp.exp(lb * pos_f)).astype(BF16)
            tn = (((0,), (0,)), ((), ()))
            st_ref[0, hh] = lax.dot_general(kf, vb, tn, preferred_element_type=F32)
            st_ref[1, hh] = lax.dot_general(kb, vb, tn, preferred_element_type=F32)
        mu = jnp.mean(o, axis=-1, keepdims=True)
        oc = o - mu
        var = jnp.mean(oc * oc, axis=-1, keepdims=True)
        y = (oc * lax.rsqrt(var + GN_EPS)) * gn_ref[:, sl] * g_ref[:, sl].astype(F32)
        o_ref[:, sl] = y.astype(BF16)


def _retention(proj, batch, seq, n_heads, dk, decay_logit, ret_gn, rope, state0, want_state):
    d_ret = n_heads * dk
    hb = n_heads if seq <= 256 else min(n_heads, 2)
    bw = hb * dk
    nhb = n_heads // hb
    lg = jnp.broadcast_to(decay_logit.T[:, :, None, None], (n_heads, 2, 1, V7X_LANES))
    act = lambda part: pl.BlockSpec((seq, bw), lambda h, b: (b, (part * d_ret) // bw + h))
    in_specs = [act(0), act(1), act(2), act(3),
                pl.BlockSpec((hb, 2, 1, V7X_LANES), lambda h, b: (h, 0, 0, 0)),
                pl.BlockSpec((1, bw), lambda h, b: (0, h))]
    args = [proj, proj, proj, proj, lg, ret_gn.reshape(1, d_ret)]
    if rope is not None:
        in_specs += [pl.BlockSpec((seq, dk), lambda h, b: (0, 0))] * 2
        args += list(rope)
    if state0 is not None:
        in_specs.append(pl.BlockSpec((None, 2, hb, dk, dk), lambda h, b: (b, 0, h, 0, 0)))
        args.append(state0)
    out_specs = [pl.BlockSpec((seq, bw), lambda h, b: (b, h))]
    out_shape = [jax.ShapeDtypeStruct((batch * seq, d_ret), BF16)]
    if want_state:
        out_specs.append(pl.BlockSpec((None, 2, hb, dk, dk), lambda h, b: (b, 0, h, 0, 0)))
        out_shape.append(jax.ShapeDtypeStruct((batch, 2, n_heads, dk, dk), F32))
    body = functools.partial(_retention_kernel, seq=seq, hb=hb, dk=dk, use_rope=rope is not None,
                             use_state=state0 is not None, want_state=want_state)
    return pl.pallas_call(
        body,
        grid=(nhb, batch),
        in_specs=in_specs,
        out_specs=out_specs,
        out_shape=out_shape,
        scratch_shapes=[pltpu.VMEM((hb, seq, seq), F32)],
        compiler_params=_cparams("parallel", "arbitrary"),
        name=f"retention_{seq}",
    )(*args)


def _merge_kernel(*refs, nblk):
    gate_refs = refs[:2 * nblk]
    (yhy_ref, yret_ref, x_ref, wbh_ref, wbr_ref, wo_ref, gpost_ref, gm_ref, gpre_ref, sc_ref, sh_ref,
     o_ref, h_ref) = refs[2 * nblk:]
    a = _dot(yhy_ref[...], wbh_ref[...])
    b = _dot(yret_ref[...], wbr_ref[...])
    wblk = gate_refs[0].shape[1]
    parts = []
    for kk in range(nblk):
        sl = slice(kk * wblk, (kk + 1) * wblk)
        g_hy = gate_refs[kk][...].astype(F32)
        g_ret = gate_refs[nblk + kk][...].astype(F32)
        parts.append((g_hy * a[:, sl] + g_ret * b[:, sl]).astype(BF16))
    merged = parts[0] if nblk == 1 else jnp.concatenate(parts, axis=1)
    out = _dot(merged, wo_ref[...])
    x1 = x_ref[...] + gm_ref[...] * _rms_scale(out, gpost_ref[...])
    o_ref[...] = x1
    h_ref[...] = (_rms_scale(x1, gpre_ref[...]) * (1.0 + sc_ref[...]) + sh_ref[...]).astype(BF16)


def _merge(proj, col0, y_hy, y_ret, x, w_br_hy, w_br_ret, w_out, g_post, gate_m, g_pre_f, scale_f, shift_f,
           rows_per_mod, tm):
    m, d = x.shape
    vec = pl.BlockSpec((1, d), lambda i: (0, 0))
    mod_spec = pl.BlockSpec((None, 1, d), lambda i: ((i * tm) // rows_per_mod, 0, 0))
    wblk = math.gcd(col0, d)
    nblk = d // wblk
    gate_spec = lambda kk: pl.BlockSpec((tm, wblk), lambda i: (i, col0 // wblk + kk))
    const = lambda arr: _resident(arr.shape)
    row = lambda width: pl.BlockSpec((tm, width), lambda i: (i, 0))
    return pl.pallas_call(
        functools.partial(_merge_kernel, nblk=nblk),
        grid=(m // tm,),
        in_specs=[gate_spec(kk) for kk in range(2 * nblk)]
        + [row(y_hy.shape[1]), row(y_ret.shape[1]), row(d), const(w_br_hy), const(w_br_ret), const(w_out),
           vec, mod_spec, vec, mod_spec, mod_spec],
        out_specs=[row(d), row(d)],
        out_shape=[jax.ShapeDtypeStruct((m, d), F32), jax.ShapeDtypeStruct((m, d), BF16)],
        compiler_params=_cparams("parallel"),
        name="merge",
    )(*([proj] * (2 * nblk)), y_hy, y_ret, x, w_br_hy, w_br_ret, w_out, g_post, gate_m, g_pre_f, scale_f, shift_f)


def _ffn_kernel(x_ref, h_ref, gate_ref, gpost_ref, wa_ref, wb_ref, ca_ref, cb_ref, wd_ref, o_ref, act_ref, *, seq):
    j = pl.program_id(1)
    nf = pl.num_programs(1) - 1
    tm = x_ref.shape[0]
    tf = wa_ref.shape[1]

    def up(slot):
        pos = lax.broadcasted_iota(jnp.int32, (tm, tf), 0) % seq
        first = pos == 0
        last = pos == seq - 1
        h = h_ref[...]
        a = _dwconv3_rows(_dot(h, wa_ref[...]), ca_ref[...], first, last)
        b = _dwconv3_rows(_dot(h, wb_ref[...]), cb_ref[...], first, last)
        gelu = 0.5 * a * (1.0 + jnp.tanh(math.sqrt(2.0 / math.pi) * (a + 0.044715 * (a * a * a))))
        act_ref[slot] = (gelu * b).astype(BF16)

    def down(slot):
        return _dot(act_ref[slot], wd_ref[...])

    @pl.when(j == 0)
    def _():
        up(0)

    @pl.when(j == 1)
    def _():
        o_ref[...] = down(0)
        up(1)

    @pl.when((j > 1) & (j < nf))
    def _():
        slot = j % 2
        o_ref[...] += down(1 - slot)
        up(slot)

    @pl.when(j == nf)
    def _():
        f = o_ref[...] + down((nf - 1) % 2)
        o_ref[...] = x_ref[...] + gate_ref[...] * _rms_scale(f, gpost_ref[...])


def _ffn(x, h, seq, gate, g_post, w_up, conv_w, w_down, rows_per_mod, tm):
    m, d = x.shape
    d_ff = w_down.shape[0]
    tf = _tile(d_ff, 512)
    nf = d_ff // tf
    assert nf >= 2, (d_ff, tf)
    once = pl.BlockSpec((tm, d), lambda i, j: (i, 0), pipeline_mode=pl.Buffered(1))
    up_blk = lambda j: jnp.minimum(j, nf - 1)
    down_blk = lambda j: jnp.maximum(j - 1, 0)
    return pl.pallas_call(
        functools.partial(_ffn_kernel, seq=seq),
        grid=(m // tm, nf + 1),
        in_specs=[once, once,
                  pl.BlockSpec((None, 1, d), lambda i, j: ((i * tm) // rows_per_mod, 0, 0)),
                  pl.BlockSpec((1, d), lambda i, j: (0, 0)),
                  pl.BlockSpec((d, tf), lambda i, j: (0, up_blk(j))),
                  pl.BlockSpec((d, tf), lambda i, j: (0, nf + up_blk(j))),
                  pl.BlockSpec((3, tf), lambda i, j: (0, up_blk(j))),
                  pl.BlockSpec((3, tf), lambda i, j: (0, nf + up_blk(j))),
                  pl.BlockSpec((tf, d), lambda i, j: (down_blk(j), 0))],
        out_specs=pl.BlockSpec((tm, d), lambda i, j: (i, 0)),
        out_shape=jax.ShapeDtypeStruct((m, d), F32),
        scratch_shapes=[pltpu.VMEM((2, tm, tf), BF16)],
        compiler_params=_cparams("parallel", "arbitrary"),
        name="ffn",
    )(x, h, gate, g_post, w_up, w_up, conv_w, conv_w, w_down)


class _Group:
    def __init__(self, x3, mod):
        self.batch, self.seq, self.d = x3.shape
        self.m = self.batch * self.seq
        self.x = x3.reshape(self.m, self.d)
        per_seq_mod = mod.shape[0] != 1
        self.rows_per_mod = self.seq if per_seq_mod else self.m
        self.tm = _row_tile(self.m, self.seq, per_seq_mod, 1024)
        self.tm_merge = _tile(self.tm, 512, unit=V7X_SUBLANES)
        (self.shift_m, self.scale_m, self.gate_m,
         self.shift_f, self.scale_f, self.gate_f) = (mod[:, i][:, None, :] for i in range(6))

    def in_proj(self, p, w_in, cast=()):
        return _in_proj(self.x, p["g_pre_m"], self.scale_m, self.shift_m, w_in, p["d_hy"], p["d_ret"],
                        self.rows_per_mod, self.tm, cast)


def _mix_and_ffn(grp, hy_in, ret_in, rope, state0, want_state, p):
    d_hy, d_ret, n_heads, dk = p["d_hy"], p["d_ret"], p["n_heads"], p["dk"]
    tables = _hyena_filters(grp.seq, p["hy_w1"], p["hy_b1"], p["hy_w2"], p["hy_b2"], p["hy_w3"], p["hy_b3"],
                            p["hy_freq"], p["hy_decay"], d_hy)
    y_hy = _hyena(hy_in, grp.batch, grp.seq, p["hy_short_w"], p["hy_bias"], tables, d_hy)
    ret = _retention(ret_in, grp.batch, grp.seq, n_heads, dk, p["ret_decay_logit"], p["ret_gn"], rope, state0,
                     want_state)
    x, h_ffn = _merge(ret_in, 4 * d_ret, y_hy, ret[0], grp.x, p["w_br_hy"], p["w_br_ret"], p["w_out"],
                      p["g_post_m"], grp.gate_m, p["g_pre_f"], grp.scale_f, grp.shift_f, grp.rows_per_mod,
                      grp.tm_merge)
    x = _ffn(x, h_ffn, grp.seq, grp.gate_f, p["g_post_f"], p["ffn_w_up"], p["ffn_conv"], p["ffn_w_down"],
             grp.rows_per_mod, grp.tm)
    return x.reshape(grp.batch, grp.seq, grp.d), (ret[1] if want_state else None)


def kernel(x_prompt, x_sample, state_ret, c, c_ctx, w_ada, b_ada, norm_pre_mix, norm_post_mix, norm_pre_ffn,
           norm_post_ffn, w_in, hy_short_w, hy_w1, hy_b1, hy_w2, hy_b2, hy_w3, hy_b3, hy_freq, hy_decay, hy_bias,
           ret_decay_logit, ret_gn, w_br_hy, w_br_ret, w_out, ffn_w_up, ffn_conv, ffn_w_down):
    depth = w_in.shape[0]
    d = x_prompt.shape[-1]
    n_dec = x_sample.shape[0]
    n_heads, dk = state_ret.shape[3], state_ret.shape[4]
    d_hy = hy_bias.shape[-1]
    cc = jnp.zeros((V7X_SUBLANES, d), F32).at[0].set(c_ctx).at[1:1 + n_dec].set(c)
    rope = _rope_tables(x_sample.shape[1], dk)
    x_p, x_s = x_prompt, x_sample
    states = []
    for l in range(depth):
        p = dict(
            d_hy=d_hy, d_ret=n_heads * dk, n_heads=n_heads, dk=dk,
            g_pre_m=norm_pre_mix[l][None], g_post_m=norm_post_mix[l][None],
            g_pre_f=norm_pre_ffn[l][None], g_post_f=norm_post_ffn[l][None],
            hy_short_w=hy_short_w[l],
            hy_w1=hy_w1[l], hy_b1=hy_b1[l], hy_w2=hy_w2[l], hy_b2=hy_b2[l], hy_w3=hy_w3[l], hy_b3=hy_b3[l],
            hy_freq=hy_freq[l], hy_decay=hy_decay[l], hy_bias=hy_bias[l],
            ret_decay_logit=ret_decay_logit[l], ret_gn=ret_gn[l], ffn_conv=ffn_conv[l],
        )
        mod = _ada_mod(cc, w_ada[l], b_ada[l][None]).reshape(V7X_SUBLANES, 6, d)
        ctx = _Group(x_p, mod[0:1])
        lat = _Group(x_s, mod[1:1 + n_dec])
        hy_s, ret_s, w_in_bf = lat.in_proj(p, w_in[l])
        hy_p, ret_p, *rest = ctx.in_proj(p, w_in_bf, (ffn_w_up[l], ffn_w_down[l], w_out[l], w_br_hy[l], w_br_ret[l]))
        p["ffn_w_up"], p["ffn_w_down"], p["w_out"], p["w_br_hy"], p["w_br_ret"] = rest
        x_p, st = _mix_and_ffn(ctx, hy_p, ret_p, None, None, True, p)
        x_s, _ = _mix_and_ffn(lat, hy_s, ret_s, rope, state_ret[:, l], False, p)
        states.append(st)
    return x_p, x_s, jnp.stack(states, axis=1)
```

```python
import functools
import math

import jax
import jax.numpy as jnp
import numpy as np
from jax import lax
from jax.experimental import pallas as pl
from jax.experimental.pallas import tpu as pltpu

F32 = jnp.float32
BF16 = jnp.bfloat16

RMS_EPS = 1e-6
GN_EPS = 1e-5
FILTER_EPS = 1e-6
HY_BANDS = 16
GRID_W = 64
ROPE_BASE = 10000.0

V7X_VMEM_LIMIT_BYTES = 58 * 1024 * 1024
V7X_LANES = 128
V7X_SUBLANES = 8


def _cparams(*sem):
    return pltpu.CompilerParams(dimension_semantics=sem, vmem_limit_bytes=V7X_VMEM_LIMIT_BYTES)


def _tile(n, target, unit=V7X_LANES):
    if n <= target:
        return n
    best = unit
    for t in range(unit, target + 1, unit):
        if n % t == 0:
            best = t
    assert n % best == 0, (n, target, unit)
    return best


def _row_tile(m, seq, per_seq_mod, target):
    if per_seq_mod or seq >= target:
        return seq
    return seq * _tile(m // seq, target // seq, unit=1)


def _resident(shape):
    return pl.BlockSpec(shape, lambda *_: (0,) * len(shape), pipeline_mode=pl.Buffered(1))


def _sigmoid(x):
    return 1.0 / (1.0 + jnp.exp(-x))


def _rms_scale(x, g):
    ms = jnp.mean(x * x, axis=-1, keepdims=True)
    return (x * lax.rsqrt(ms + RMS_EPS)) * g


def _dot(a, b):
    return jnp.dot(a, b, preferred_element_type=F32)


def _dot_exact(a, b):
    return jnp.dot(a, b, preferred_element_type=F32, precision=lax.Precision.HIGHEST)


def _split_bf16(table):
    hi = table.astype(BF16)
    lo = (table - hi.astype(np.float64)).astype(BF16)
    return jnp.asarray(hi), jnp.asarray(lo)


def _dot_split(a_hi_ref, a_lo_ref, b):
    b_hi = b.astype(BF16)
    b_lo = (b - b_hi.astype(F32)).astype(BF16)
    a_hi = a_hi_ref[...]
    return _dot(a_hi, b_hi) + (_dot(a_lo_ref[...], b_hi) + _dot(a_hi, b_lo))


def _dwconv3_rows(x, w, first, last):
    rows = x.shape[0]
    prev = jnp.where(first, 0.0, pltpu.roll(x, 1, 0))
    nxt = jnp.where(last, 0.0, pltpu.roll(x, rows - 1, 0))
    return prev * w[0:1] + x * w[1:2] + nxt * w[2:3]


def _ada_kernel(cc_ref, w_ref, b_ref, o_ref):
    cc = cc_ref[...]
    s = cc * _sigmoid(cc)
    o_ref[...] = _dot(s.astype(BF16), w_ref[...].astype(BF16)) + b_ref[...]


def _ada_mod(cc, w, b):
    d, n = w.shape
    tn = _tile(n, 1024)
    return pl.pallas_call(
        _ada_kernel,
        grid=(n // tn,),
        in_specs=[pl.BlockSpec((V7X_SUBLANES, d), lambda j: (0, 0)),
                  pl.BlockSpec((d, tn), lambda j: (0, j)),
                  pl.BlockSpec((1, tn), lambda j: (0, j))],
        out_specs=pl.BlockSpec((V7X_SUBLANES, tn), lambda j: (0, j)),
        out_shape=jax.ShapeDtypeStruct((V7X_SUBLANES, n), F32),
        compiler_params=_cparams("parallel"),
        name="ada_mod",
    )(cc, w, b)


def _in_proj_kernel(*refs, n_hy, n_qkv, n_g, n_cast, emit_w):
    x_ref, g_ref, sc_ref, sh_ref, w_ref = refs[:5]
    cast_in = refs[5:5 + n_cast]
    hy_ref, ret_ref = refs[5 + n_cast:7 + n_cast]
    pos = 7 + n_cast
    if emit_w:
        wbf_ref = refs[pos]
        pos += 1
    cast_out = refs[pos:pos + n_cast]
    h_ref = refs[pos + n_cast]
    j = pl.program_id(1)

    @pl.when(j == 0)
    def _():
        n_mod = sc_ref.shape[0]
        rows = x_ref.shape[0] // n_mod
        for b in range(n_mod):
            rs = slice(b * rows, (b + 1) * rows)
            h = _rms_scale(x_ref[rs], g_ref[...]) * (1.0 + sc_ref[b]) + sh_ref[b]
            h_ref[rs] = h.astype(BF16)

    def tile():
        w = w_ref[...]
        if emit_w:
            w = w.astype(BF16)
            wbf_ref[...] = w
        for src, dst in zip(cast_in, cast_out):
            dst[...] = src[...].astype(BF16)
        return _dot(h_ref[...], w)

    @pl.when(j < n_hy)
    def _():
        hy_ref[...] = tile()

    @pl.when((j >= n_hy) & (j < n_hy + n_qkv))
    def _():
        ret_ref[...] = tile().astype(BF16)

    @pl.when((j >= n_hy + n_qkv) & (j < n_hy + n_qkv + n_g))
    def _():
        p = tile()
        ret_ref[...] = (p * _sigmoid(p)).astype(BF16)

    @pl.when(j >= n_hy + n_qkv + n_g)
    def _():
        ret_ref[...] = _sigmoid(tile()).astype(BF16)


V7X_BF16_ROW_TILE = 2 * V7X_SUBLANES


def _cast_rows(arr, n_steps):
    rows = arr.shape[0]
    for blk in range(V7X_BF16_ROW_TILE, rows + 1, V7X_BF16_ROW_TILE):
        if rows % blk == 0 and rows // blk <= n_steps:
            return blk
    return None


def _in_proj(x, g, scale, shift, w, d_hy, d_ret, rows_per_mod, tm, cast=()):
    m, d = x.shape
    n = w.shape[1]
    emit_w = w.dtype != BF16
    if emit_w:
        tm = m
    tn = _tile(math.gcd(3 * d_hy, d_ret, 2 * d), 512 if emit_w else 1024)
    n_hy, n_qkv, n_g = 3 * d_hy // tn, 3 * d_ret // tn, d_ret // tn
    n_col = n // tn
    n_mod = max(1, tm // rows_per_mod)
    mod_spec = pl.BlockSpec((n_mod, 1, d), lambda i, j: ((i * tm) // rows_per_mod // n_mod, 0, 0))
    blks = [_cast_rows(a, (m // tm) * n_col) for a in cast]
    assert all(b is not None for b in blks), [a.shape for a in cast]
    cast_specs = [pl.BlockSpec((b, a.shape[1]), lambda i, j, nb=a.shape[0] // b: (jnp.minimum(i * n_col + j, nb - 1), 0))
                  for a, b in zip(cast, blks)]
    out_specs = [pl.BlockSpec((tm, tn), lambda i, j: (i, jnp.minimum(j, n_hy - 1))),
                 pl.BlockSpec((tm, tn), lambda i, j: (i, jnp.maximum(j - n_hy, 0)))]
    out_shape = [jax.ShapeDtypeStruct((m, 3 * d_hy), F32), jax.ShapeDtypeStruct((m, n - 3 * d_hy), BF16)]
    if emit_w:
        out_specs.append(pl.BlockSpec((d, tn), lambda i, j: (0, j)))
        out_shape.append(jax.ShapeDtypeStruct(w.shape, BF16))
    out_specs += cast_specs
    out_shape += [jax.ShapeDtypeStruct(a.shape, BF16) for a in cast]
    return pl.pallas_call(
        functools.partial(_in_proj_kernel, n_hy=n_hy, n_qkv=n_qkv, n_g=n_g, n_cast=len(cast), emit_w=emit_w),
        grid=(m // tm, n_col),
        in_specs=[pl.BlockSpec((tm, d), lambda i, j: (i, 0), pipeline_mode=pl.Buffered(1) if emit_w else None),
                  pl.BlockSpec((1, d), lambda i, j: (0, 0)),
                  mod_spec, mod_spec,
                  pl.BlockSpec((d, tn), lambda i, j: (0, j))] + cast_specs,
        out_specs=out_specs,
        out_shape=out_shape,
        scratch_shapes=[pltpu.VMEM((tm, d), BF16)],
        compiler_params=_cparams("arbitrary", "arbitrary"),
        name="in_proj",
    )(x, g, scale, shift, w, *cast)


def _dft_tables(seq):
    n_fft = 2 * seq
    idx = np.arange(seq)
    ang = 2.0 * np.pi * ((idx[:, None] * idx[None, :]) % n_fft) / n_fft
    cos = np.cos(ang)
    msin = -np.sin(ang)
    sign = np.where(idx % 2 == 0, 1.0, -1.0)
    msin[0, :] = sign
    fwd = np.concatenate([cos, msin], axis=0)
    wgt = np.full((seq,), 2.0 / n_fft)
    wgt[0] = 1.0 / n_fft
    inv_re = cos.T * wgt[None, :]
    inv_im = msin.T * wgt[None, :]
    inv_im[:, 0] = sign / n_fft
    inv = np.concatenate([inv_re, inv_im], axis=1)
    return fwd, inv


def _filter_feats(seq):
    n = np.arange(seq, dtype=np.float64)
    t = n / seq
    f = np.linspace(1e-4, HY_BANDS - 1, HY_BANDS)
    w = 2.0 * math.pi * n / seq
    z = np.concatenate([t[:, None], np.cos(w[:, None] * f), np.sin(w[:, None] * f)], axis=-1)
    out = np.zeros((seq, V7X_LANES), np.float32)
    out[:, :z.shape[1]] = z
    return out


def _filter_kernel(z_ref, w1_ref, b1_ref, w2_ref, b2_ref, fr_ref, w3f_ref, w3b_ref, b3f_ref, b3b_ref,
                   decf_ref, decb_ref, chi_ref, clo_ref, shi_ref, slo_ref, kr_ref, kiz_ref, krn_ref, h2_ref):
    @pl.when(pl.program_id(0) == 0)
    def _():
        h1 = jnp.sin(fr_ref[0:1, :] * (_dot_exact(z_ref[...], w1_ref[...]) + b1_ref[...]))
        h2_ref[...] = jnp.sin(fr_ref[1:2, :] * (_dot_exact(h1, w2_ref[...]) + b2_ref[...]))

    h2 = h2_ref[...]
    seq, cw = kr_ref.shape
    t = z_ref[:, 0:1]
    row = lax.broadcasted_iota(jnp.int32, (seq, cw), 0)
    hf = (_dot_exact(h2, w3f_ref[...]) + b3f_ref[...]) * jnp.exp(-t * jnp.abs(decf_ref[...]))
    hb = (_dot_exact(h2, w3b_ref[...]) + b3b_ref[...]) * jnp.exp(-t * jnp.abs(decb_ref[...]))
    hb = jnp.where(row == 0, 0.0, hb)
    norm = (jnp.sum(jnp.abs(hf), axis=0, keepdims=True)
            + jnp.sum(jnp.abs(hb), axis=0, keepdims=True) + FILTER_EPS)
    inv = 1.0 / norm
    even = (hf + hb) * inv
    odd = (hf - hb) * inv
    kr = _dot_split(chi_ref, clo_ref, even)
    ki = _dot_split(shi_ref, slo_ref, odd)
    nyq = jnp.sum(jnp.where((row & 1) == 0, even, -even), axis=0, keepdims=True)
    kr_ref[...] = kr
    kiz_ref[...] = jnp.where(row == 0, 0.0, ki)
    krn_ref[...] = jnp.where(row == 0, nyq, kr)


def _hyena_filters(seq, w1, b1, w2, b2, w3, b3, freq, decay, d_hy):
    fh = w1.shape[1]
    pad = V7X_LANES
    w1p = jnp.zeros((pad, pad), F32).at[:w1.shape[0], :fh].set(w1)
    b1p = jnp.zeros((1, pad), F32).at[0, :fh].set(b1)
    w2p = jnp.zeros((pad, pad), F32).at[:fh, :fh].set(w2)
    b2p = jnp.zeros((1, pad), F32).at[0, :fh].set(b2)
    frp = jnp.zeros((2, pad), F32).at[:, :fh].set(freq)
    ncol = w3.shape[1] // 2
    w3p = jnp.zeros((pad, 2 * ncol), F32).at[:fh].set(w3)
    b3r = b3.reshape(1, 2 * ncol)
    dec = decay.reshape(1, 2 * ncol)
    fwd, _ = _dft_tables(seq)
    chi, clo = _split_bf16(fwd[:seq])
    shi, slo = _split_bf16(fwd[seq:])
    z = jnp.asarray(_filter_feats(seq))
    cw = _tile(ncol, 256)
    nb = ncol // cw
    full = lambda shape: pl.BlockSpec(shape, lambda j: (0, 0))
    colf = lambda rows: pl.BlockSpec((rows, cw), lambda j: (0, j))
    colb = lambda rows: pl.BlockSpec((rows, cw), lambda j: (0, nb + j))
    out = jax.ShapeDtypeStruct((seq, ncol), F32)
    return pl.pallas_call(
        _filter_kernel,
        grid=(nb,),
        in_specs=[full((seq, pad)), full((pad, pad)), full((1, pad)), full((pad, pad)), full((1, pad)),
                  full((2, pad)), colf(pad), colb(pad), colf(1), colb(1), colf(1), colb(1),
                  _resident((seq, seq)), _resident((seq, seq)), _resident((seq, seq)), _resident((seq, seq))],
        out_specs=[colf(seq), colf(seq), colf(seq)],
        out_shape=[out, out, out],
        scratch_shapes=[pltpu.VMEM((seq, pad), F32)],
        compiler_params=_cparams("arbitrary"),
        name=f"hy_filter_{seq}",
    )(z, w1p, b1p, w2p, b2p, frp, w3p, w3p, b3r, b3r, dec, dec, chi, clo, shi, slo)


def _hyena_kernel(x1_ref, x2_ref, v_ref, w1_ref, w2_ref, wv_ref, hb_ref,
                  kr0_ref, kiz0_ref, krn0_ref, kr1_ref, kiz1_ref, krn1_ref, f_ref, g_ref, o_ref, *, seq):
    cw = o_ref.shape[1]
    row = lax.broadcasted_iota(jnp.int32, (seq, cw), 0)
    first = row == 0
    last = row == seq - 1

    def long_conv(u, kr_ref, kiz_ref, krn_ref, bias):
        spec = _dot(f_ref[...], u.astype(BF16))
        ur = spec[:seq]
        ui = spec[seq:]
        kiz = kiz_ref[...]
        yr = ur * kr_ref[...] - ui * kiz
        yi = ur * kiz + ui * krn_ref[...]
        y = _dot(g_ref[:, :seq], yr.astype(BF16)) + _dot(g_ref[:, seq:], yi.astype(BF16))
        return y + u * bias

    for s in range(o_ref.shape[0] // seq):
        rs = slice(s * seq, (s + 1) * seq)
        x1 = _dwconv3_rows(x1_ref[rs], w1_ref[...], first, last)
        x2 = _dwconv3_rows(x2_ref[rs], w2_ref[...], first, last)
        v = _dwconv3_rows(v_ref[rs], wv_ref[...], first, last)
        z = x1 * long_conv(v, kr0_ref, kiz0_ref, krn0_ref, hb_ref[0:1])
        o_ref[rs] = (x2 * long_conv(z, kr1_ref, kiz1_ref, krn1_ref, hb_ref[1:2])).astype(BF16)


def _hyena(proj, batch, seq, short_w, hy_bias, tables, d_hy):
    kr, kiz, krn = tables
    cw = _tile(d_hy, 1024 if seq <= 256 else 256)
    nb = d_hy // cw
    fwd, inv = _dft_tables(seq)
    fmat = jnp.asarray(fwd, F32).astype(BF16)
    gmat = jnp.asarray(inv, F32).astype(BF16)
    bs = 2 if batch % 2 == 0 else 1
    col = lambda rows, off: pl.BlockSpec((rows, cw), lambda j, b: (0, off * nb + j))
    act = lambda off: pl.BlockSpec((bs * seq, cw), lambda j, b: (b, off * nb + j))
    return pl.pallas_call(
        functools.partial(_hyena_kernel, seq=seq),
        grid=(nb, batch // bs),
        in_specs=[act(0), act(1), act(2), col(3, 0), col(3, 1), col(3, 2), col(2, 0),
                  col(seq, 0), col(seq, 0), col(seq, 0), col(seq, 1), col(seq, 1), col(seq, 1),
                  _resident((2 * seq, seq)), _resident((seq, 2 * seq))],
        out_specs=pl.BlockSpec((bs * seq, cw), lambda j, b: (b, j)),
        out_shape=jax.ShapeDtypeStruct((batch * seq, d_hy), BF16),
        compiler_params=_cparams("parallel", "parallel"),
        name=f"hyena_{seq}",
    )(proj, proj, proj, short_w, short_w, short_w, hy_bias, kr, kiz, krn, kr, kiz, krn, fmat, gmat)


def _rope_tables(seq, dk):
    rows = seq // GRID_W
    row = jnp.repeat(jnp.arange(rows), GRID_W).astype(F32)
    col = jnp.tile(jnp.arange(GRID_W), rows).astype(F32)
    nfreq = dk // 4
    inv = ROPE_BASE ** (-jnp.arange(nfreq, dtype=F32) / nfreq)
    ang = jnp.concatenate([row[:, None] * inv, col[:, None] * inv], axis=-1)
    cos, sin = jnp.cos(ang), jnp.sin(ang)
    return jnp.concatenate([cos, cos], axis=-1), jnp.concatenate([-sin, sin], axis=-1)


def _retention_kernel(*refs, seq, hb, dk, use_rope, use_state, want_state):
    refs = list(refs)
    q_ref, k_ref, v_ref, g_ref, lg_ref, gn_ref = refs[:6]
    pos = 6
    if use_rope:
        cos_ref, sin_ref = refs[pos:pos + 2]
        pos += 2
    if use_state:
        s0_ref = refs[pos]
        pos += 1
    o_ref = refs[pos]
    pos += 1
    if want_state:
        st_ref = refs[pos]
        pos += 1
    d_ref = refs[pos]

    def log_gamma(hh, direction):
        return jnp.log(_sigmoid(lg_ref[hh, direction]))[:, 0:1]

    @pl.when(pl.program_id(1) == 0)
    def _():
        i = lax.broadcasted_iota(jnp.int32, (dk, dk), 0)
        j = lax.broadcasted_iota(jnp.int32, (dk, dk), 1)
        diff = (i - j).astype(F32)
        scale = dk ** -0.5
        for hh in range(hb):
            lf = log_gamma(hh, 0)
            lb = log_gamma(hh, 1)
            base_f = scale * jnp.exp(lf * diff)
            base_b = scale * jnp.exp(lb * (-diff))
            diag = jnp.where(diff >= 0, base_f, 0.0) + jnp.where(diff <= 0, base_b, 0.0)
            for bi in range(seq // dk):
                for bj in range(seq // dk):
                    if bi == bj:
                        blk = diag
                    elif bi > bj:
                        blk = base_f * jnp.exp(lf * float(dk * (bi - bj)))
                    else:
                        blk = base_b * jnp.exp(lb * float(dk * (bj - bi)))
                    d_ref[hh, bi * dk:(bi + 1) * dk, bj * dk:(bj + 1) * dk] = blk

    pos_f = lax.broadcasted_iota(jnp.int32, (seq, dk), 0).astype(F32)
    for hh in range(hb):
        sl = slice(hh * dk, (hh + 1) * dk)
        qb = q_ref[:, sl]
        kb16 = k_ref[:, sl]
        vb = v_ref[:, sl]
        if use_rope:
            cos = cos_ref[...]
            sin = sin_ref[...]
            q = qb.astype(F32)
            k = kb16.astype(F32)
            qb = (q * cos + pltpu.roll(q, dk // 2, 1) * sin).astype(BF16)
            kb16 = (k * cos + pltpu.roll(k, dk // 2, 1) * sin).astype(BF16)
        s = lax.dot_general(qb, kb16, (((1,), (1,)), ((), ())), preferred_element_type=F32)
        o = _dot((s * d_ref[hh]).astype(BF16), vb)
        lf = log_gamma(hh, 0)
        lb = log_gamma(hh, 1)
        if use_state:
            o = o + _dot(qb, s0_ref[0, hh].astype(BF16)) * jnp.exp(lf * (pos_f + 1.0))
            o = o + _dot(qb, s0_ref[1, hh].astype(BF16)) * jnp.exp(lb * (seq - pos_f))
        if want_state:
            k = kb16.astype(F32) * dk ** -0.5
            kf = (k * jnp.exp(lf * (seq - 1.0 - pos_f))).astype(BF16)
            kb = (k * jnp.exp(lb * pos_f)).astype(BF16)
            tn = (((0,), (0,)), ((), ()))
            st_ref[0, hh] = lax.dot_general(kf, vb, tn, preferred_element_type=F32)
            st_ref[1, hh] = lax.dot_general(kb, vb, tn, preferred_element_type=F32)
        mu = jnp.mean(o, axis=-1, keepdims=True)
        oc = o - mu
        var = jnp.mean(oc * oc, axis=-1, keepdims=True)
        y = (oc * lax.rsqrt(var + GN_EPS)) * gn_ref[:, sl] * g_ref[:, sl].astype(F32)
        o_ref[:, sl] = y.astype(BF16)


def _retention(proj, batch, seq, n_heads, dk, decay_logit, ret_gn, rope, state0, want_state):
    d_ret = n_heads * dk
    hb = n_heads if seq <= 256 else min(n_heads, 2)
    bw = hb * dk
    nhb = n_heads // hb
    lg = jnp.broadcast_to(decay_logit.T[:, :, None, None], (n_heads, 2, 1, V7X_LANES))
    act = lambda part: pl.BlockSpec((seq, bw), lambda h, b: (b, (part * d_ret) // bw + h))
    in_specs = [act(0), act(1), act(2), act(3),
                pl.BlockSpec((hb, 2, 1, V7X_LANES), lambda h, b: (h, 0, 0, 0)),
                pl.BlockSpec((1, bw), lambda h, b: (0, h))]
    args = [proj, proj, proj, proj, lg, ret_gn.reshape(1, d_ret)]
    if rope is not None:
        in_specs += [pl.BlockSpec((seq, dk), lambda h, b: (0, 0))] * 2
        args += list(rope)
    if state0 is not None:
        in_specs.append(pl.BlockSpec((None, 2, hb, dk, dk), lambda h, b: (b, 0, h, 0, 0)))
        args.append(state0)
    out_specs = [pl.BlockSpec((seq, bw), lambda h, b: (b, h))]
    out_shape = [jax.ShapeDtypeStruct((batch * seq, d_ret), BF16)]
    if want_state:
        out_specs.append(pl.BlockSpec((None, 2, hb, dk, dk), lambda h, b: (b, 0, h, 0, 0)))
        out_shape.append(jax.ShapeDtypeStruct((batch, 2, n_heads, dk, dk), F32))
    body = functools.partial(_retention_kernel, seq=seq, hb=hb, dk=dk, use_rope=rope is not None,
                             use_state=state0 is not None, want_state=want_state)
    return pl.pallas_call(
        body,
        grid=(nhb, batch),
        in_specs=in_specs,
        out_specs=out_specs,
        out_shape=out_shape,
        scratch_shapes=[pltpu.VMEM((hb, seq, seq), F32)],
        compiler_params=_cparams("parallel", "arbitrary"),
        name=f"retention_{seq}",
    )(*args)


def _merge_kernel(*refs, nblk):
    gate_refs = refs[:2 * nblk]
    (yhy_ref, yret_ref, x_ref, wbh_ref, wbr_ref, wo_ref, gpost_ref, gm_ref, gpre_ref, sc_ref, sh_ref,
     o_ref, h_ref) = refs[2 * nblk:]
    a = _dot(yhy_ref[...], wbh_ref[...])
    b = _dot(yret_ref[...], wbr_ref[...])
    wblk = gate_refs[0].shape[1]
    parts = []
    for kk in range(nblk):
        sl = slice(kk * wblk, (kk + 1) * wblk)
        g_hy = gate_refs[kk][...].astype(F32)
        g_ret = gate_refs[nblk + kk][...].astype(F32)
        parts.append((g_hy * a[:, sl] + g_ret * b[:, sl]).astype(BF16))
    merged = parts[0] if nblk == 1 else jnp.concatenate(parts, axis=1)
    out = _dot(merged, wo_ref[...])
    x1 = x_ref[...] + gm_ref[...] * _rms_scale(out, gpost_ref[...])
    o_ref[...] = x1
    h_ref[...] = (_rms_scale(x1, gpre_ref[...]) * (1.0 + sc_ref[...]) + sh_ref[...]).astype(BF16)


def _merge(proj, col0, y_hy, y_ret, x, w_br_hy, w_br_ret, w_out, g_post, gate_m, g_pre_f, scale_f, shift_f,
           rows_per_mod, tm):
    m, d = x.shape
    vec = pl.BlockSpec((1, d), lambda i: (0, 0))
    mod_spec = pl.BlockSpec((None, 1, d), lambda i: ((i * tm) // rows_per_mod, 0, 0))
    wblk = math.gcd(col0, d)
    nblk = d // wblk
    gate_spec = lambda kk: pl.BlockSpec((tm, wblk), lambda i: (i, col0 // wblk + kk))
    const = lambda arr: _resident(arr.shape)
    row = lambda width: pl.BlockSpec((tm, width), lambda i: (i, 0))
    return pl.pallas_call(
        functools.partial(_merge_kernel, nblk=nblk),
        grid=(m // tm,),
        in_specs=[gate_spec(kk) for kk in range(2 * nblk)]
        + [row(y_hy.shape[1]), row(y_ret.shape[1]), row(d), const(w_br_hy), const(w_br_ret), const(w_out),
           vec, mod_spec, vec, mod_spec, mod_spec],
        out_specs=[row(d), row(d)],
        out_shape=[jax.ShapeDtypeStruct((m, d), F32), jax.ShapeDtypeStruct((m, d), BF16)],
        compiler_params=_cparams("parallel"),
        name="merge",
    )(*([proj] * (2 * nblk)), y_hy, y_ret, x, w_br_hy, w_br_ret, w_out, g_post, gate_m, g_pre_f, scale_f, shift_f)


def _ffn_kernel(x_ref, h_ref, gate_ref, gpost_ref, wa_ref, wb_ref, ca_ref, cb_ref, wd_ref, o_ref, act_ref, *, seq):
    j = pl.program_id(1)
    nf = pl.num_programs(1) - 1
    tm = x_ref.shape[0]
    tf = wa_ref.shape[1]

    def up(slot):
        pos = lax.broadcasted_iota(jnp.int32, (tm, tf), 0) % seq
        first = pos == 0
        last = pos == seq - 1
        h = h_ref[...]
        a = _dwconv3_rows(_dot(h, wa_ref[...]), ca_ref[...], first, last)
        b = _dwconv3_rows(_dot(h, wb_ref[...]), cb_ref[...], first, last)
        gelu = 0.5 * a * (1.0 + jnp.tanh(math.sqrt(2.0 / math.pi) * (a + 0.044715 * (a * a * a))))
        act_ref[slot] = (gelu * b).astype(BF16)

    def down(slot):
        return _dot(act_ref[slot], wd_ref[...])

    @pl.when(j == 0)
    def _():
        up(0)

    @pl.when(j == 1)
    def _():
        o_ref[...] = down(0)
        up(1)

    @pl.when((j > 1) & (j < nf))
    def _():
        slot = j % 2
        o_ref[...] += down(1 - slot)
        up(slot)

    @pl.when(j == nf)
    def _():
        f = o_ref[...] + down((nf - 1) % 2)
        o_ref[...] = x_ref[...] + gate_ref[...] * _rms_scale(f, gpost_ref[...])


def _ffn(x, h, seq, gate, g_post, w_up, conv_w, w_down, rows_per_mod, tm):
    m, d = x.shape
    d_ff = w_down.shape[0]
    tf = _tile(d_ff, 512)
    nf = d_ff // tf
    assert nf >= 2, (d_ff, tf)
    once = pl.BlockSpec((tm, d), lambda i, j: (i, 0), pipeline_mode=pl.Buffered(1))
    up_blk = lambda j: jnp.minimum(j, nf - 1)
    down_blk = lambda j: jnp.maximum(j - 1, 0)
    return pl.pallas_call(
        functools.partial(_ffn_kernel, seq=seq),
        grid=(m // tm, nf + 1),
        in_specs=[once, once,
                  pl.BlockSpec((None, 1, d), lambda i, j: ((i * tm) // rows_per_mod, 0, 0)),
                  pl.BlockSpec((1, d), lambda i, j: (0, 0)),
                  pl.BlockSpec((d, tf), lambda i, j: (0, up_blk(j))),
                  pl.BlockSpec((d, tf), lambda i, j: (0, nf + up_blk(j))),
                  pl.BlockSpec((3, tf), lambda i, j: (0, up_blk(j))),
                  pl.BlockSpec((3, tf), lambda i, j: (0, nf + up_blk(j))),
                  pl.BlockSpec((tf, d), lambda i, j: (down_blk(j), 0))],
        out_specs=pl.BlockSpec((tm, d), lambda i, j: (i, 0)),
        out_shape=jax.ShapeDtypeStruct((m, d), F32),
        scratch_shapes=[pltpu.VMEM((2, tm, tf), BF16)],
        compiler_params=_cparams("parallel", "arbitrary"),
        name="ffn",
    )(x, h, gate, g_post, w_up, w_up, conv_w, conv_w, w_down)


class _Group:
    def __init__(self, x3, mod):
        self.batch, self.seq, self.d = x3.shape
        self.m = self.batch * self.seq
        self.x = x3.reshape(self.m, self.d)
        per_seq_mod = mod.shape[0] != 1
        self.rows_per_mod = self.seq if per_seq_mod else self.m
        self.tm = _row_tile(self.m, self.seq, per_seq_mod, 1024)
        self.tm_merge = _tile(self.tm, 512, unit=V7X_SUBLANES)
        (self.shift_m, self.scale_m, self.gate_m,
         self.shift_f, self.scale_f, self.gate_f) = (mod[:, i][:, None, :] for i in range(6))

    def in_proj(self, p, w_in, cast=()):
        return _in_proj(self.x, p["g_pre_m"], self.scale_m, self.shift_m, w_in, p["d_hy"], p["d_ret"],
                        self.rows_per_mod, self.tm, cast)


def _mix_and_ffn(grp, hy_in, ret_in, rope, state0, want_state, p):
    d_hy, d_ret, n_heads, dk = p["d_hy"], p["d_ret"], p["n_heads"], p["dk"]
    tables = _hyena_filters(grp.seq, p["hy_w1"], p["hy_b1"], p["hy_w2"], p["hy_b2"], p["hy_w3"], p["hy_b3"],
                            p["hy_freq"], p["hy_decay"], d_hy)
    y_hy = _hyena(hy_in, grp.batch, grp.seq, p["hy_short_w"], p["hy_bias"], tables, d_hy)
    ret = _retention(ret_in, grp.batch, grp.seq, n_heads, dk, p["ret_decay_logit"], p["ret_gn"], rope, state0,
                     want_state)
    x, h_ffn = _merge(ret_in, 4 * d_ret, y_hy, ret[0], grp.x, p["w_br_hy"], p["w_br_ret"], p["w_out"],
                      p["g_post_m"], grp.gate_m, p["g_pre_f"], grp.scale_f, grp.shift_f, grp.rows_per_mod,
                      grp.tm_merge)
    x = _ffn(x, h_ffn, grp.seq, grp.gate_f, p["g_post_f"], p["ffn_w_up"], p["ffn_conv"], p["ffn_w_down"],
             grp.rows_per_mod, grp.tm)
    return x.reshape(grp.batch, grp.seq, grp.d), (ret[1] if want_state else None)


def kernel(x_prompt, x_sample, state_ret, c, c_ctx, w_ada, b_ada, norm_pre_mix, norm_post_mix, norm_pre_ffn,
           norm_post_ffn, w_in, hy_short_w, hy_w1, hy_b1, hy_w2, hy_b2, hy_w3, hy_b3, hy_freq, hy_decay, hy_bias,
           ret_decay_logit, ret_gn, w_br_hy, w_br_ret, w_out, ffn_w_up, ffn_conv, ffn_w_down):
    depth = w_in.shape[0]
    d = x_prompt.shape[-1]
    n_dec = x_sample.shape[0]
    n_heads, dk = state_ret.shape[3], state_ret.shape[4]
    d_hy = hy_bias.shape[-1]
    cc = jnp.zeros((V7X_SUBLANES, d), F32).at[0].set(c_ctx).at[1:1 + n_dec].set(c)
    rope = _rope_tables(x_sample.shape[1], dk)
    x_p, x_s = x_prompt, x_sample
    states = []
    for l in range(depth):
        p = dict(
            d_hy=d_hy, d_ret=n_heads * dk, n_heads=n_heads, dk=dk,
            g_pre_m=norm_pre_mix[l][None], g_post_m=norm_post_mix[l][None],
            g_pre_f=norm_pre_ffn[l][None], g_post_f=norm_post_ffn[l][None],
            hy_short_w=hy_short_w[l],
            hy_w1=hy_w1[l], hy_b1=hy_b1[l], hy_w2=hy_w2[l], hy_b2=hy_b2[l], hy_w3=hy_w3[l], hy_b3=hy_b3[l],
            hy_freq=hy_freq[l], hy_decay=hy_decay[l], hy_bias=hy_bias[l],
            ret_decay_logit=ret_decay_logit[l], ret_gn=ret_gn[l], ffn_conv=ffn_conv[l],
        )
        mod = _ada_mod(cc, w_ada[l], b_ada[l][None]).reshape(V7X_SUBLANES, 6, d)
        ctx = _Group(x_p, mod[0:1])
        lat = _Group(x_s, mod[1:1 + n_dec])
        hy_s, ret_s, w_in_bf = lat.in_proj(p, w_in[l])
        hy_p, ret_p, *rest = ctx.in_proj(p, w_in_bf, (ffn_w_up[l], ffn_w_down[l], w_out[l], w_br_hy[l], w_br_ret[l]))
        p["ffn_w_up"], p["ffn_w_down"], p["w_out"], p["w_br_hy"], p["w_br_ret"] = rest
        x_p, st = _mix_and_ffn(ctx, hy_p, ret_p, None, None, True, p)
        x_s, _ = _mix_and_ffn(lat, hy_s, ret_s, rope, state_ret[:, l], False, p)
        states.append(st)
    return x_p, x_s, jnp.stack(states, axis=1)
```

```python
import functools
import math

import jax
import jax.numpy as jnp
import numpy as np
from jax import lax
from jax.experimental import pallas as pl
from jax.experimental.pallas import tpu as pltpu

F32 = jnp.float32
BF16 = jnp.bfloat16

RMS_EPS = 1e-6
GN_EPS = 1e-5
FILTER_EPS = 1e-6
HY_BANDS = 16
GRID_W = 64
ROPE_BASE = 10000.0

V7X_VMEM_LIMIT_BYTES = 58 * 1024 * 1024
V7X_LANES = 128
V7X_SUBLANES = 8


def _cparams(*sem):
    return pltpu.CompilerParams(dimension_semantics=sem, vmem_limit_bytes=V7X_VMEM_LIMIT_BYTES)


def _tile(n, target, unit=V7X_LANES):
    if n <= target:
        return n
    best = unit
    for t in range(unit, target + 1, unit):
        if n % t == 0:
            best = t
    assert n % best == 0, (n, target, unit)
    return best


def _row_tile(m, seq, per_seq_mod, target):
    if per_seq_mod or seq >= target:
        return seq
    return seq * _tile(m // seq, target // seq, unit=1)


def _resident(shape):
    return pl.BlockSpec(shape, lambda *_: (0,) * len(shape), pipeline_mode=pl.Buffered(1))


def _sigmoid(x):
    return 1.0 / (1.0 + jnp.exp(-x))


def _rms_scale(x, g):
    ms = jnp.mean(x * x, axis=-1, keepdims=True)
    return (x * lax.rsqrt(ms + RMS_EPS)) * g


def _dot(a, b):
    return jnp.dot(a, b, preferred_element_type=F32)


def _dot_exact(a, b):
    return jnp.dot(a, b, preferred_element_type=F32, precision=lax.Precision.HIGHEST)


def _split_bf16(table):
    hi = table.astype(BF16)
    lo = (table - hi.astype(np.float64)).astype(BF16)
    return jnp.asarray(hi), jnp.asarray(lo)


def _dot_split(a_hi_ref, a_lo_ref, b):
    b_hi = b.astype(BF16)
    b_lo = (b - b_hi.astype(F32)).astype(BF16)
    a_hi = a_hi_ref[...]
    return _dot(a_hi, b_hi) + (_dot(a_lo_ref[...], b_hi) + _dot(a_hi, b_lo))


def _dwconv3_rows(x, w, first, last):
    rows = x.shape[0]
    prev = jnp.where(first, 0.0, pltpu.roll(x, 1, 0))
    nxt = jnp.where(last, 0.0, pltpu.roll(x, rows - 1, 0))
    return prev * w[0:1] + x * w[1:2] + nxt * w[2:3]


def _ada_kernel(cc_ref, w_ref, b_ref, o_ref):
    cc = cc_ref[...]
    s = cc * _sigmoid(cc)
    o_ref[...] = _dot(s.astype(BF16), w_ref[...].astype(BF16)) + b_ref[...]


def _ada_mod(cc, w, b):
    d, n = w.shape
    tn = _tile(n, 1024)
    return pl.pallas_call(
        _ada_kernel,
        grid=(n // tn,),
        in_specs=[pl.BlockSpec((V7X_SUBLANES, d), lambda j: (0, 0)),
                  pl.BlockSpec((d, tn), lambda j: (0, j)),
                  pl.BlockSpec((1, tn), lambda j: (0, j))],
        out_specs=pl.BlockSpec((V7X_SUBLANES, tn), lambda j: (0, j)),
        out_shape=jax.ShapeDtypeStruct((V7X_SUBLANES, n), F32),
        compiler_params=_cparams("parallel"),
        name="ada_mod",
    )(cc, w, b)


def _in_proj_kernel(*refs, n_hy, n_qkv, n_g, n_cast, emit_w):
    x_ref, g_ref, sc_ref, sh_ref, w_ref = refs[:5]
    cast_in = refs[5:5 + n_cast]
    hy_ref, ret_ref = refs[5 + n_cast:7 + n_cast]
    pos = 7 + n_cast
    if emit_w:
        wbf_ref = refs[pos]
        pos += 1
    cast_out = refs[pos:pos + n_cast]
    h_ref = refs[pos + n_cast]
    j = pl.program_id(1)

    @pl.when(j == 0)
    def _():
        n_mod = sc_ref.shape[0]
        rows = x_ref.shape[0] // n_mod
        for b in range(n_mod):
            rs = slice(b * rows, (b + 1) * rows)
            h = _rms_scale(x_ref[rs], g_ref[...]) * (1.0 + sc_ref[b]) + sh_ref[b]
            h_ref[rs] = h.astype(BF16)

    def tile():
        w = w_ref[...]
        if emit_w:
            w = w.astype(BF16)
            wbf_ref[...] = w
        for src, dst in zip(cast_in, cast_out):
            dst[...] = src[...].astype(BF16)
        return _dot(h_ref[...], w)

    @pl.when(j < n_hy)
    def _():
        hy_ref[...] = tile()

    @pl.when((j >= n_hy) & (j < n_hy + n_qkv))
    def _():
        ret_ref[...] = tile().astype(BF16)

    @pl.when((j >= n_hy + n_qkv) & (j < n_hy + n_qkv + n_g))
    def _():
        p = tile()
        ret_ref[...] = (p * _sigmoid(p)).astype(BF16)

    @pl.when(j >= n_hy + n_qkv + n_g)
    def _():
        ret_ref[...] = _sigmoid(tile()).astype(BF16)


V7X_BF16_ROW_TILE = 2 * V7X_SUBLANES


def _cast_rows(arr, n_steps):
    rows = arr.shape[0]
    for blk in range(V7X_BF16_ROW_TILE, rows + 1, V7X_BF16_ROW_TILE):
        if rows % blk == 0 and rows // blk <= n_steps:
            return blk
    return None


def _in_proj(x, g, scale, shift, w, d_hy, d_ret, rows_per_mod, tm, cast=()):
    m, d = x.shape
    n = w.shape[1]
    emit_w = w.dtype != BF16
    if emit_w:
        tm = m
    tn = _tile(math.gcd(3 * d_hy, d_ret, 2 * d), 512 if emit_w else 1024)
    n_hy, n_qkv, n_g = 3 * d_hy // tn, 3 * d_ret // tn, d_ret // tn
    n_col = n // tn
    n_mod = max(1, tm // rows_per_mod)
    mod_spec = pl.BlockSpec((n_mod, 1, d), lambda i, j: ((i * tm) // rows_per_mod // n_mod, 0, 0))
    blks = [_cast_rows(a, (m // tm) * n_col) for a in cast]
    assert all(b is not None for b in blks), [a.shape for a in cast]
    cast_specs = [pl.BlockSpec((b, a.shape[1]), lambda i, j, nb=a.shape[0] // b: (jnp.minimum(i * n_col + j, nb - 1), 0))
                  for a, b in zip(cast, blks)]
    out_specs = [pl.BlockSpec((tm, tn), lambda i, j: (i, jnp.minimum(j, n_hy - 1))),
                 pl.BlockSpec((tm, tn), lambda i, j: (i, jnp.maximum(j - n_hy, 0)))]
    out_shape = [jax.ShapeDtypeStruct((m, 3 * d_hy), F32), jax.ShapeDtypeStruct((m, n - 3 * d_hy), BF16)]
    if emit_w:
        out_specs.append(pl.BlockSpec((d, tn), lambda i, j: (0, j)))
        out_shape.append(jax.ShapeDtypeStruct(w.shape, BF16))
    out_specs += cast_specs
    out_shape += [jax.ShapeDtypeStruct(a.shape, BF16) for a in cast]
    return pl.pallas_call(
        functools.partial(_in_proj_kernel, n_hy=n_hy, n_qkv=n_qkv, n_g=n_g, n_cast=len(cast), emit_w=emit_w),
        grid=(m // tm, n_col),
        in_specs=[pl.BlockSpec((tm, d), lambda i, j: (i, 0), pipeline_mode=pl.Buffered(1) if emit_w else None),
                  pl.BlockSpec((1, d), lambda i, j: (0, 0)),
                  mod_spec, mod_spec,
                  pl.BlockSpec((d, tn), lambda i, j: (0, j))] + cast_specs,
        out_specs=out_specs,
        out_shape=out_shape,
        scratch_shapes=[pltpu.VMEM((tm, d), BF16)],
        compiler_params=_cparams("arbitrary", "arbitrary"),
        name="in_proj",
    )(x, g, scale, shift, w, *cast)


def _dft_tables(seq):
    n_fft = 2 * seq
    idx = np.arange(seq)
    ang = 2.0 * np.pi * ((idx[:, None] * idx[None, :]) % n_fft) / n_fft
    cos = np.cos(ang)
    msin = -np.sin(ang)
    sign = np.where(idx % 2 == 0, 1.0, -1.0)
    msin[0, :] = sign
    fwd = np.concatenate([cos, msin], axis=0)
    wgt = np.full((seq,), 2.0 / n_fft)
    wgt[0] = 1.0 / n_fft
    inv_re = cos.T * wgt[None, :]
    inv_im = msin.T * wgt[None, :]
    inv_im[:, 0] = sign / n_fft
    inv = np.concatenate([inv_re, inv_im], axis=1)
    return fwd, inv


def _filter_feats(seq):
    n = np.arange(seq, dtype=np.float64)
    t = n / seq
    f = np.linspace(1e-4, HY_BANDS - 1, HY_BANDS)
    w = 2.0 * math.pi * n / seq
    z = np.concatenate([t[:, None], np.cos(w[:, None] * f), np.sin(w[:, None] * f)], axis=-1)
    out = np.zeros((seq, V7X_LANES), np.float32)
    out[:, :z.shape[1]] = z
    return out


def _filter_kernel(z_ref, w1_ref, b1_ref, w2_ref, b2_ref, fr_ref, w3f_ref, w3b_ref, b3f_ref, b3b_ref,
                   decf_ref, decb_ref, bias_ref, chi_ref, clo_ref, shi_ref, slo_ref, kr_ref, kiz_ref, krn_ref, h2_ref):
    @pl.when(pl.program_id(0) == 0)
    def _():
        h1 = jnp.sin(fr_ref[0:1, :] * (_dot_exact(z_ref[...], w1_ref[...]) + b1_ref[...]))
        h2_ref[...] = jnp.sin(fr_ref[1:2, :] * (_dot_exact(h1, w2_ref[...]) + b2_ref[...]))

    h2 = h2_ref[...]
    seq, cw = kr_ref.shape
    t = z_ref[:, 0:1]
    row = lax.broadcasted_iota(jnp.int32, (seq, cw), 0)
    hf = (_dot_exact(h2, w3f_ref[...]) + b3f_ref[...]) * jnp.exp(-t * jnp.abs(decf_ref[...]))
    hb = (_dot_exact(h2, w3b_ref[...]) + b3b_ref[...]) * jnp.exp(-t * jnp.abs(decb_ref[...]))
    hb = jnp.where(row == 0, 0.0, hb)
    norm = (jnp.sum(jnp.abs(hf), axis=0, keepdims=True)
            + jnp.sum(jnp.abs(hb), axis=0, keepdims=True) + FILTER_EPS)
    inv = 1.0 / norm
    even = (hf + hb) * inv
    odd = (hf - hb) * inv
    bias = bias_ref[...]
    kr = _dot_split(chi_ref, clo_ref, even) + bias
    ki = _dot_split(shi_ref, slo_ref, odd)
    nyq = jnp.sum(jnp.where((row & 1) == 0, even, -even), axis=0, keepdims=True) + bias
    kr_ref[...] = kr.astype(BF16)
    kiz_ref[...] = jnp.where(row == 0, 0.0, ki).astype(BF16)
    krn_ref[...] = jnp.where(row == 0, nyq, kr).astype(BF16)


def _hyena_filters(seq, w1, b1, w2, b2, w3, b3, freq, decay, hy_bias, d_hy):
    fh = w1.shape[1]
    pad = V7X_LANES
    w1p = jnp.zeros((pad, pad), F32).at[:w1.shape[0], :fh].set(w1)
    b1p = jnp.zeros((1, pad), F32).at[0, :fh].set(b1)
    w2p = jnp.zeros((pad, pad), F32).at[:fh, :fh].set(w2)
    b2p = jnp.zeros((1, pad), F32).at[0, :fh].set(b2)
    frp = jnp.zeros((2, pad), F32).at[:, :fh].set(freq)
    ncol = w3.shape[1] // 2
    w3p = jnp.zeros((pad, 2 * ncol), F32).at[:fh].set(w3)
    b3r = b3.reshape(1, 2 * ncol)
    dec = decay.reshape(1, 2 * ncol)
    bias = hy_bias.reshape(1, ncol)
    fwd, _ = _dft_tables(seq)
    chi, clo = _split_bf16(fwd[:seq])
    shi, slo = _split_bf16(fwd[seq:])
    z = jnp.asarray(_filter_feats(seq))
    cw = _tile(ncol, 256)
    nb = ncol // cw
    full = lambda shape: pl.BlockSpec(shape, lambda j: (0, 0))
    colf = lambda rows: pl.BlockSpec((rows, cw), lambda j: (0, j))
    colb = lambda rows: pl.BlockSpec((rows, cw), lambda j: (0, nb + j))
    out = jax.ShapeDtypeStruct((seq, ncol), BF16)
    return pl.pallas_call(
        _filter_kernel,
        grid=(nb,),
        in_specs=[full((seq, pad)), full((pad, pad)), full((1, pad)), full((pad, pad)), full((1, pad)),
                  full((2, pad)), colf(pad), colb(pad), colf(1), colb(1), colf(1), colb(1), colf(1),
                  _resident((seq, seq)), _resident((seq, seq)), _resident((seq, seq)), _resident((seq, seq))],
        out_specs=[colf(seq), colf(seq), colf(seq)],
        out_shape=[out, out, out],
        scratch_shapes=[pltpu.VMEM((seq, pad), F32)],
        compiler_params=_cparams("arbitrary"),
        name=f"hy_filter_{seq}",
    )(z, w1p, b1p, w2p, b2p, frp, w3p, w3p, b3r, b3r, dec, dec, bias, chi, clo, shi, slo)


def _hyena_kernel(x1_ref, x2_ref, v_ref, w1_ref, w2_ref, wv_ref,
                  kr0_ref, kiz0_ref, krn0_ref, kr1_ref, kiz1_ref, krn1_ref, f_ref, g_ref, o_ref, *, seq):
    cw = o_ref.shape[1]
    row = lax.broadcasted_iota(jnp.int32, (seq, cw), 0)
    first = row == 0
    last = row == seq - 1

    def long_conv(u, kr_ref, kiz_ref, krn_ref):
        spec = _dot(f_ref[...], u.astype(BF16))
        ur = spec[:seq].astype(BF16)
        ui = spec[seq:].astype(BF16)
        kiz = kiz_ref[...]
        yr = ur * kr_ref[...] - ui * kiz
        yi = ur * kiz + ui * krn_ref[...]
        return _dot(g_ref[:, :seq], yr) + _dot(g_ref[:, seq:], yi)

    for s in range(o_ref.shape[0] // seq):
        rs = slice(s * seq, (s + 1) * seq)
        x1 = _dwconv3_rows(x1_ref[rs], w1_ref[...], first, last)
        x2 = _dwconv3_rows(x2_ref[rs], w2_ref[...], first, last)
        v = _dwconv3_rows(v_ref[rs], wv_ref[...], first, last)
        z = x1 * long_conv(v, kr0_ref, kiz0_ref, krn0_ref)
        o_ref[rs] = (x2 * long_conv(z, kr1_ref, kiz1_ref, krn1_ref)).astype(BF16)


def _hyena(proj, batch, seq, short_w, tables, d_hy):
    kr, kiz, krn = tables
    cw = _tile(d_hy, 1024 if seq <= 256 else 256)
    nb = d_hy // cw
    fwd, inv = _dft_tables(seq)
    fmat = jnp.asarray(fwd, F32).astype(BF16)
    gmat = jnp.asarray(inv, F32).astype(BF16)
    bs = 2 if batch % 2 == 0 else 1
    col = lambda rows, off: pl.BlockSpec((rows, cw), lambda j, b: (0, off * nb + j))
    act = lambda off: pl.BlockSpec((bs * seq, cw), lambda j, b: (b, off * nb + j))
    return pl.pallas_call(
        functools.partial(_hyena_kernel, seq=seq),
        grid=(nb, batch // bs),
        in_specs=[act(0), act(1), act(2), col(3, 0), col(3, 1), col(3, 2),
                  col(seq, 0), col(seq, 0), col(seq, 0), col(seq, 1), col(seq, 1), col(seq, 1),
                  _resident((2 * seq, seq)), _resident((seq, 2 * seq))],
        out_specs=pl.BlockSpec((bs * seq, cw), lambda j, b: (b, j)),
        out_shape=jax.ShapeDtypeStruct((batch * seq, d_hy), BF16),
        compiler_params=_cparams("parallel", "parallel"),
        name=f"hyena_{seq}",
    )(proj, proj, proj, short_w, short_w, short_w, kr, kiz, krn, kr, kiz, krn, fmat, gmat)


def _rope_tables(seq, dk):
    rows = seq // GRID_W
    row = jnp.repeat(jnp.arange(rows), GRID_W).astype(F32)
    col = jnp.tile(jnp.arange(GRID_W), rows).astype(F32)
    nfreq = dk // 4
    inv = ROPE_BASE ** (-jnp.arange(nfreq, dtype=F32) / nfreq)
    ang = jnp.concatenate([row[:, None] * inv, col[:, None] * inv], axis=-1)
    cos, sin = jnp.cos(ang), jnp.sin(ang)
    return jnp.concatenate([cos, cos], axis=-1), jnp.concatenate([-sin, sin], axis=-1)


def _retention_kernel(*refs, seq, hb, dk, use_rope, use_state, want_state):
    refs = list(refs)
    q_ref, k_ref, v_ref, g_ref, lg_ref, gn_ref = refs[:6]
    pos = 6
    if use_rope:
        cos_ref, sin_ref = refs[pos:pos + 2]
        pos += 2
    if use_state:
        s0_ref = refs[pos]
        pos += 1
    o_ref = refs[pos]
    pos += 1
    if want_state:
        st_ref = refs[pos]
        pos += 1
    d_ref = refs[pos]

    def log_gamma(hh, direction):
        return jnp.log(_sigmoid(lg_ref[hh, direction]))[:, 0:1]

    @pl.when(pl.program_id(1) == 0)
    def _():
        i = lax.broadcasted_iota(jnp.int32, (dk, dk), 0)
        j = lax.broadcasted_iota(jnp.int32, (dk, dk), 1)
        diff = (i - j).astype(F32)
        scale = dk ** -0.5
        for hh in range(hb):
            lf = log_gamma(hh, 0)
            lb = log_gamma(hh, 1)
            base_f = scale * jnp.exp(lf * diff)
            base_b = scale * jnp.exp(lb * (-diff))
            diag = jnp.where(diff >= 0, base_f, 0.0) + jnp.where(diff <= 0, base_b, 0.0)
            for bi in range(seq // dk):
                for bj in range(seq // dk):
                    if bi == bj:
                        blk = diag
                    elif bi > bj:
                        blk = base_f * jnp.exp(lf * float(dk * (bi - bj)))
                    else:
                        blk = base_b * jnp.exp(lb * float(dk * (bj - bi)))
                    d_ref[hh, bi * dk:(bi + 1) * dk, bj * dk:(bj + 1) * dk] = blk

    pos_f = lax.broadcasted_iota(jnp.int32, (seq, dk), 0).astype(F32)
    for hh in range(hb):
        sl = slice(hh * dk, (hh + 1) * dk)
        qb = q_ref[:, sl]
        kb16 = k_ref[:, sl]
        vb = v_ref[:, sl]
        if use_rope:
            cos = cos_ref[...]
            sin = sin_ref[...]
            q = qb.astype(F32)
            k = kb16.astype(F32)
            qb = (q * cos + pltpu.roll(q, dk // 2, 1) * sin).astype(BF16)
            kb16 = (k * cos + pltpu.roll(k, dk // 2, 1) * sin).astype(BF16)
        s = lax.dot_general(qb, kb16, (((1,), (1,)), ((), ())), preferred_element_type=F32)
        o = _dot((s * d_ref[hh]).astype(BF16), vb)
        lf = log_gamma(hh, 0)
        lb = log_gamma(hh, 1)
        if use_state:
            o = o + _dot(qb, s0_ref[0, hh].astype(BF16)) * jnp.exp(lf * (pos_f + 1.0))
            o = o + _dot(qb, s0_ref[1, hh].astype(BF16)) * jnp.exp(lb * (seq - pos_f))
        if want_state:
            k = kb16.astype(F32) * dk ** -0.5
            kf = (k * jnp.exp(lf * (seq - 1.0 - pos_f))).astype(BF16)
            kb = (k * jnp.exp(lb * pos_f)).astype(BF16)
            tn = (((0,), (0,)), ((), ()))
            st_ref[0, hh] = lax.dot_general(kf, vb, tn, preferred_element_type=F32)
            st_ref[1, hh] = lax.dot_general(kb, vb, tn, preferred_element_type=F32)
        mu = jnp.mean(o, axis=-1, keepdims=True)
        oc = o - mu
        var = jnp.mean(oc * oc, axis=-1, keepdims=True)
        y = (oc * lax.rsqrt(var + GN_EPS)) * gn_ref[:, sl] * g_ref[:, sl].astype(F32)
        o_ref[:, sl] = y.astype(BF16)


def _retention(proj, batch, seq, n_heads, dk, decay_logit, ret_gn, rope, state0, want_state):
    d_ret = n_heads * dk
    hb = n_heads if seq <= 256 else min(n_heads, 2)
    bw = hb * dk
    nhb = n_heads // hb
    lg = jnp.broadcast_to(decay_logit.T[:, :, None, None], (n_heads, 2, 1, V7X_LANES))
    act = lambda part: pl.BlockSpec((seq, bw), lambda h, b: (b, (part * d_ret) // bw + h))
    in_specs = [act(0), act(1), act(2), act(3),
                pl.BlockSpec((hb, 2, 1, V7X_LANES), lambda h, b: (h, 0, 0, 0)),
                pl.BlockSpec((1, bw), lambda h, b: (0, h))]
    args = [proj, proj, proj, proj, lg, ret_gn.reshape(1, d_ret)]
    if rope is not None:
        in_specs += [pl.BlockSpec((seq, dk), lambda h, b: (0, 0))] * 2
        args += list(rope)
    if state0 is not None:
        in_specs.append(pl.BlockSpec((None, 2, hb, dk, dk), lambda h, b: (b, 0, h, 0, 0)))
        args.append(state0)
    out_specs = [pl.BlockSpec((seq, bw), lambda h, b: (b, h))]
    out_shape = [jax.ShapeDtypeStruct((batch * seq, d_ret), BF16)]
    if want_state:
        out_specs.append(pl.BlockSpec((None, 2, hb, dk, dk), lambda h, b: (b, 0, h, 0, 0)))
        out_shape.append(jax.ShapeDtypeStruct((batch, 2, n_heads, dk, dk), F32))
    body = functools.partial(_retention_kernel, seq=seq, hb=hb, dk=dk, use_rope=rope is not None,
                             use_state=state0 is not None, want_state=want_state)
    return pl.pallas_call(
        body,
        grid=(nhb, batch),
        in_specs=in_specs,
        out_specs=out_specs,
        out_shape=out_shape,
        scratch_shapes=[pltpu.VMEM((hb, seq, seq), F32)],
        compiler_params=_cparams("parallel", "arbitrary"),
        name=f"retention_{seq}",
    )(*args)


def _merge_kernel(*refs, nblk):
    gate_refs = refs[:2 * nblk]
    (yhy_ref, yret_ref, x_ref, wbh_ref, wbr_ref, wo_ref, gpost_ref, gm_ref, gpre_ref, sc_ref, sh_ref,
     o_ref, h_ref) = refs[2 * nblk:]
    a = _dot(yhy_ref[...], wbh_ref[...])
    b = _dot(yret_ref[...], wbr_ref[...])
    wblk = gate_refs[0].shape[1]
    parts = []
    for kk in range(nblk):
        sl = slice(kk * wblk, (kk + 1) * wblk)
        g_hy = gate_refs[kk][...].astype(F32)
        g_ret = gate_refs[nblk + kk][...].astype(F32)
        parts.append((g_hy * a[:, sl] + g_ret * b[:, sl]).astype(BF16))
    merged = parts[0] if nblk == 1 else jnp.concatenate(parts, axis=1)
    out = _dot(merged, wo_ref[...])
    x1 = x_ref[...] + gm_ref[...] * _rms_scale(out, gpost_ref[...])
    o_ref[...] = x1
    h_ref[...] = (_rms_scale(x1, gpre_ref[...]) * (1.0 + sc_ref[...]) + sh_ref[...]).astype(BF16)


def _merge(proj, col0, y_hy, y_ret, x, w_br_hy, w_br_ret, w_out, g_post, gate_m, g_pre_f, scale_f, shift_f,
           rows_per_mod, tm):
    m, d = x.shape
    vec = pl.BlockSpec((1, d), lambda i: (0, 0))
    mod_spec = pl.BlockSpec((None, 1, d), lambda i: ((i * tm) // rows_per_mod, 0, 0))
    wblk = math.gcd(col0, d)
    nblk = d // wblk
    gate_spec = lambda kk: pl.BlockSpec((tm, wblk), lambda i: (i, col0 // wblk + kk))
    const = lambda arr: _resident(arr.shape)
    row = lambda width: pl.BlockSpec((tm, width), lambda i: (i, 0))
    return pl.pallas_call(
        functools.partial(_merge_kernel, nblk=nblk),
        grid=(m // tm,),
        in_specs=[gate_spec(kk) for kk in range(2 * nblk)]
        + [row(y_hy.shape[1]), row(y_ret.shape[1]), row(d), const(w_br_hy), const(w_br_ret), const(w_out),
           vec, mod_spec, vec, mod_spec, mod_spec],
        out_specs=[row(d), row(d)],
        out_shape=[jax.ShapeDtypeStruct((m, d), F32), jax.ShapeDtypeStruct((m, d), BF16)],
        compiler_params=_cparams("parallel"),
        name="merge",
    )(*([proj] * (2 * nblk)), y_hy, y_ret, x, w_br_hy, w_br_ret, w_out, g_post, gate_m, g_pre_f, scale_f, shift_f)


def _ffn_kernel(x_ref, h_ref, gate_ref, gpost_ref, wa_ref, wb_ref, ca_ref, cb_ref, wd_ref, o_ref, act_ref, *, seq):
    j = pl.program_id(1)
    nf = pl.num_programs(1) - 1
    tm = x_ref.shape[0]
    tf = wa_ref.shape[1]

    def up(slot):
        pos = lax.broadcasted_iota(jnp.int32, (tm, tf), 0) % seq
        first = pos == 0
        last = pos == seq - 1
        h = h_ref[...]
        a = _dwconv3_rows(_dot(h, wa_ref[...]), ca_ref[...], first, last)
        b = _dwconv3_rows(_dot(h, wb_ref[...]), cb_ref[...], first, last)
        c1 = math.sqrt(2.0 / math.pi)
        half = 0.5 * a
        gelu = half + half * jnp.tanh(a * (c1 + (c1 * 0.044715) * (a * a)))
        act_ref[slot] = (gelu * b).astype(BF16)

    def down(slot):
        return _dot(act_ref[slot], wd_ref[...])

    @pl.when(j == 0)
    def _():
        up(0)

    @pl.when(j == 1)
    def _():
        o_ref[...] = down(0)
        up(1)

    @pl.when((j > 1) & (j < nf))
    def _():
        slot = j % 2
        o_ref[...] += down(1 - slot)
        up(slot)

    @pl.when(j == nf)
    def _():
        f = o_ref[...] + down((nf - 1) % 2)
        o_ref[...] = x_ref[...] + gate_ref[...] * _rms_scale(f, gpost_ref[...])


def _ffn(x, h, seq, gate, g_post, w_up, conv_w, w_down, rows_per_mod, tm):
    m, d = x.shape
    d_ff = w_down.shape[0]
    tf = _tile(d_ff, 512)
    nf = d_ff // tf
    assert nf >= 2, (d_ff, tf)
    once = pl.BlockSpec((tm, d), lambda i, j: (i, 0), pipeline_mode=pl.Buffered(1))
    up_blk = lambda j: jnp.minimum(j, nf - 1)
    down_blk = lambda j: jnp.maximum(j - 1, 0)
    return pl.pallas_call(
        functools.partial(_ffn_kernel, seq=seq),
        grid=(m // tm, nf + 1),
        in_specs=[once, once,
                  pl.BlockSpec((None, 1, d), lambda i, j: ((i * tm) // rows_per_mod, 0, 0)),
                  pl.BlockSpec((1, d), lambda i, j: (0, 0)),
                  pl.BlockSpec((d, tf), lambda i, j: (0, up_blk(j))),
                  pl.BlockSpec((d, tf), lambda i, j: (0, nf + up_blk(j))),
                  pl.BlockSpec((3, tf), lambda i, j: (0, up_blk(j))),
                  pl.BlockSpec((3, tf), lambda i, j: (0, nf + up_blk(j))),
                  pl.BlockSpec((tf, d), lambda i, j: (down_blk(j), 0))],
        out_specs=pl.BlockSpec((tm, d), lambda i, j: (i, 0)),
        out_shape=jax.ShapeDtypeStruct((m, d), F32),
        scratch_shapes=[pltpu.VMEM((2, tm, tf), BF16)],
        compiler_params=_cparams("parallel", "arbitrary"),
        name="ffn",
    )(x, h, gate, g_post, w_up, w_up, conv_w, conv_w, w_down)


class _Group:
    def __init__(self, x3, mod):
        self.batch, self.seq, self.d = x3.shape
        self.m = self.batch * self.seq
        self.x = x3.reshape(self.m, self.d)
        per_seq_mod = mod.shape[0] != 1
        self.rows_per_mod = self.seq if per_seq_mod else self.m
        self.tm = _row_tile(self.m, self.seq, per_seq_mod, 1024)
        self.tm_merge = _tile(self.tm, 512, unit=V7X_SUBLANES)
        (self.shift_m, self.scale_m, self.gate_m,
         self.shift_f, self.scale_f, self.gate_f) = (mod[:, i][:, None, :] for i in range(6))

    def in_proj(self, p, w_in, cast=()):
        return _in_proj(self.x, p["g_pre_m"], self.scale_m, self.shift_m, w_in, p["d_hy"], p["d_ret"],
                        self.rows_per_mod, self.tm, cast)


def _mix_and_ffn(grp, hy_in, ret_in, rope, state0, want_state, p):
    d_hy, d_ret, n_heads, dk = p["d_hy"], p["d_ret"], p["n_heads"], p["dk"]
    tables = _hyena_filters(grp.seq, p["hy_w1"], p["hy_b1"], p["hy_w2"], p["hy_b2"], p["hy_w3"], p["hy_b3"],
                            p["hy_freq"], p["hy_decay"], p["hy_bias"], d_hy)
    y_hy = _hyena(hy_in, grp.batch, grp.seq, p["hy_short_w"], tables, d_hy)
    ret = _retention(ret_in, grp.batch, grp.seq, n_heads, dk, p["ret_decay_logit"], p["ret_gn"], rope, state0,
                     want_state)
    x, h_ffn = _merge(ret_in, 4 * d_ret, y_hy, ret[0], grp.x, p["w_br_hy"], p["w_br_ret"], p["w_out"],
                      p["g_post_m"], grp.gate_m, p["g_pre_f"], grp.scale_f, grp.shift_f, grp.rows_per_mod,
                      grp.tm_merge)
    x = _ffn(x, h_ffn, grp.seq, grp.gate_f, p["g_post_f"], p["ffn_w_up"], p["ffn_conv"], p["ffn_w_down"],
             grp.rows_per_mod, grp.tm)
    return x.reshape(grp.batch, grp.seq, grp.d), (ret[1] if want_state else None)


def kernel(x_prompt, x_sample, state_ret, c, c_ctx, w_ada, b_ada, norm_pre_mix, norm_post_mix, norm_pre_ffn,
           norm_post_ffn, w_in, hy_short_w, hy_w1, hy_b1, hy_w2, hy_b2, hy_w3, hy_b3, hy_freq, hy_decay, hy_bias,
           ret_decay_logit, ret_gn, w_br_hy, w_br_ret, w_out, ffn_w_up, ffn_conv, ffn_w_down):
    depth = w_in.shape[0]
    d = x_prompt.shape[-1]
    n_dec = x_sample.shape[0]
    n_heads, dk = state_ret.shape[3], state_ret.shape[4]
    d_hy = hy_bias.shape[-1]
    cc = jnp.zeros((V7X_SUBLANES, d), F32).at[0].set(c_ctx).at[1:1 + n_dec].set(c)
    rope = _rope_tables(x_sample.shape[1], dk)
    x_p, x_s = x_prompt, x_sample
    states = []
    for l in range(depth):
        p = dict(
            d_hy=d_hy, d_ret=n_heads * dk, n_heads=n_heads, dk=dk,
            g_pre_m=norm_pre_mix[l][None], g_post_m=norm_post_mix[l][None],
            g_pre_f=norm_pre_ffn[l][None], g_post_f=norm_post_ffn[l][None],
            hy_short_w=hy_short_w[l],
            hy_w1=hy_w1[l], hy_b1=hy_b1[l], hy_w2=hy_w2[l], hy_b2=hy_b2[l], hy_w3=hy_w3[l], hy_b3=hy_b3[l],
            hy_freq=hy_freq[l], hy_decay=hy_decay[l], hy_bias=hy_bias[l],
            ret_decay_logit=ret_decay_logit[l], ret_gn=ret_gn[l], ffn_conv=ffn_conv[l],
        )
        mod = _ada_mod(cc, w_ada[l], b_ada[l][None]).reshape(V7X_SUBLANES, 6, d)
        ctx = _Group(x_p, mod[0:1])
        lat = _Group(x_s, mod[1:1 + n_dec])
        hy_s, ret_s, w_in_bf = lat.in_proj(p, w_in[l])
        hy_p, ret_p, *rest = ctx.in_proj(p, w_in_bf, (ffn_w_up[l], ffn_w_down[l], w_out[l], w_br_hy[l], w_br_ret[l]))
        p["ffn_w_up"], p["ffn_w_down"], p["w_out"], p["w_br_hy"], p["w_br_ret"] = rest
        x_p, st = _mix_and_ffn(ctx, hy_p, ret_p, None, None, True, p)
        x_s, _ = _mix_and_ffn(lat, hy_s, ret_s, rope, state_ret[:, l], False, p)
        states.append(st)
    return x_p, x_s, jnp.stack(states, axis=1)
```

```python
import functools
import math

import jax
import jax.numpy as jnp
import numpy as np
from jax import lax
from jax.experimental import pallas as pl
from jax.experimental.pallas import tpu as pltpu

F32 = jnp.float32
BF16 = jnp.bfloat16

RMS_EPS = 1e-6
GN_EPS = 1e-5
FILTER_EPS = 1e-6
HY_BANDS = 16
GRID_W = 64
ROPE_BASE = 10000.0

V7X_VMEM_LIMIT_BYTES = 58 * 1024 * 1024
V7X_LANES = 128
V7X_SUBLANES = 8


def _cparams(*sem):
    return pltpu.CompilerParams(dimension_semantics=sem, vmem_limit_bytes=V7X_VMEM_LIMIT_BYTES)


def _tile(n, target, unit=V7X_LANES):
    if n <= target:
        return n
    best = unit
    for t in range(unit, target + 1, unit):
        if n % t == 0:
            best = t
    assert n % best == 0, (n, target, unit)
    return best


def _row_tile(m, seq, per_seq_mod, target):
    if per_seq_mod or seq >= target:
        return seq
    return seq * _tile(m // seq, target // seq, unit=1)


def _resident(shape):
    return pl.BlockSpec(shape, lambda *_: (0,) * len(shape), pipeline_mode=pl.Buffered(1))


def _sigmoid(x):
    return 1.0 / (1.0 + jnp.exp(-x))


def _rms_scale(x, g):
    ms = jnp.mean(x * x, axis=-1, keepdims=True)
    return (x * lax.rsqrt(ms + RMS_EPS)) * g


def _dot(a, b):
    return jnp.dot(a, b, preferred_element_type=F32)


def _dot_exact(a, b):
    return jnp.dot(a, b, preferred_element_type=F32, precision=lax.Precision.HIGHEST)


def _split_bf16(table):
    hi = table.astype(BF16)
    lo = (table - hi.astype(np.float64)).astype(BF16)
    return jnp.asarray(hi), jnp.asarray(lo)


def _dot_split(a_hi_ref, a_lo_ref, b):
    b_hi = b.astype(BF16)
    b_lo = (b - b_hi.astype(F32)).astype(BF16)
    a_hi = a_hi_ref[...]
    return _dot(a_hi, b_hi) + (_dot(a_lo_ref[...], b_hi) + _dot(a_hi, b_lo))


def _dwconv3_rows(x, w, first, last):
    rows = x.shape[0]
    prev = jnp.where(first, 0.0, pltpu.roll(x, 1, 0))
    nxt = jnp.where(last, 0.0, pltpu.roll(x, rows - 1, 0))
    return prev * w[0:1] + x * w[1:2] + nxt * w[2:3]


def _ada_kernel(cc_ref, w_ref, b_ref, o_ref):
    cc = cc_ref[...]
    s = cc * _sigmoid(cc)
    o_ref[...] = _dot(s.astype(BF16), w_ref[...].astype(BF16)) + b_ref[...]


def _ada_mod(cc, w, b):
    d, n = w.shape
    tn = _tile(n, 1024)
    return pl.pallas_call(
        _ada_kernel,
        grid=(n // tn,),
        in_specs=[pl.BlockSpec((V7X_SUBLANES, d), lambda j: (0, 0)),
                  pl.BlockSpec((d, tn), lambda j: (0, j)),
                  pl.BlockSpec((1, tn), lambda j: (0, j))],
        out_specs=pl.BlockSpec((V7X_SUBLANES, tn), lambda j: (0, j)),
        out_shape=jax.ShapeDtypeStruct((V7X_SUBLANES, n), F32),
        compiler_params=_cparams("parallel"),
        name="ada_mod",
    )(cc, w, b)


def _in_proj_kernel(*refs, n_hy, n_qkv, n_g, n_cast, emit_w):
    x_ref, g_ref, sc_ref, sh_ref, w_ref = refs[:5]
    cast_in = refs[5:5 + n_cast]
    hy_ref, ret_ref = refs[5 + n_cast:7 + n_cast]
    pos = 7 + n_cast
    if emit_w:
        wbf_ref = refs[pos]
        pos += 1
    cast_out = refs[pos:pos + n_cast]
    h_ref = refs[pos + n_cast]
    j = pl.program_id(1)

    @pl.when(j == 0)
    def _():
        n_mod = sc_ref.shape[0]
        rows = x_ref.shape[0] // n_mod
        for b in range(n_mod):
            rs = slice(b * rows, (b + 1) * rows)
            h = _rms_scale(x_ref[rs], g_ref[...]) * (1.0 + sc_ref[b]) + sh_ref[b]
            h_ref[rs] = h.astype(BF16)

    def tile():
        w = w_ref[...]
        if emit_w:
            w = w.astype(BF16)
            wbf_ref[...] = w
        for src, dst in zip(cast_in, cast_out):
            dst[...] = src[...].astype(BF16)
        return _dot(h_ref[...], w)

    @pl.when(j < n_hy)
    def _():
        hy_ref[...] = tile()

    @pl.when((j >= n_hy) & (j < n_hy + n_qkv))
    def _():
        ret_ref[...] = tile().astype(BF16)

    @pl.when((j >= n_hy + n_qkv) & (j < n_hy + n_qkv + n_g))
    def _():
        p = tile()
        ret_ref[...] = (p * _sigmoid(p)).astype(BF16)

    @pl.when(j >= n_hy + n_qkv + n_g)
    def _():
        ret_ref[...] = _sigmoid(tile()).astype(BF16)


V7X_BF16_ROW_TILE = 2 * V7X_SUBLANES


def _cast_rows(arr, n_steps):
    rows = arr.shape[0]
    for blk in range(V7X_BF16_ROW_TILE, rows + 1, V7X_BF16_ROW_TILE):
        if rows % blk == 0 and rows // blk <= n_steps:
            return blk
    return None


def _in_proj(x, g, scale, shift, w, d_hy, d_ret, rows_per_mod, tm, cast=()):
    m, d = x.shape
    n = w.shape[1]
    emit_w = w.dtype != BF16
    if emit_w:
        tm = m
    tn = _tile(math.gcd(3 * d_hy, d_ret, 2 * d), 512 if emit_w else 1024)
    n_hy, n_qkv, n_g = 3 * d_hy // tn, 3 * d_ret // tn, d_ret // tn
    n_col = n // tn
    n_mod = max(1, tm // rows_per_mod)
    mod_spec = pl.BlockSpec((n_mod, 1, d), lambda i, j: ((i * tm) // rows_per_mod // n_mod, 0, 0))
    blks = [_cast_rows(a, (m // tm) * n_col) for a in cast]
    assert all(b is not None for b in blks), [a.shape for a in cast]
    cast_specs = [pl.BlockSpec((b, a.shape[1]), lambda i, j, nb=a.shape[0] // b: (jnp.minimum(i * n_col + j, nb - 1), 0))
                  for a, b in zip(cast, blks)]
    out_specs = [pl.BlockSpec((tm, tn), lambda i, j: (i, jnp.minimum(j, n_hy - 1))),
                 pl.BlockSpec((tm, tn), lambda i, j: (i, jnp.maximum(j - n_hy, 0)))]
    out_shape = [jax.ShapeDtypeStruct((m, 3 * d_hy), F32), jax.ShapeDtypeStruct((m, n - 3 * d_hy), BF16)]
    if emit_w:
        out_specs.append(pl.BlockSpec((d, tn), lambda i, j: (0, j)))
        out_shape.append(jax.ShapeDtypeStruct(w.shape, BF16))
    out_specs += cast_specs
    out_shape += [jax.ShapeDtypeStruct(a.shape, BF16) for a in cast]
    return pl.pallas_call(
        functools.partial(_in_proj_kernel, n_hy=n_hy, n_qkv=n_qkv, n_g=n_g, n_cast=len(cast), emit_w=emit_w),
        grid=(m // tm, n_col),
        in_specs=[pl.BlockSpec((tm, d), lambda i, j: (i, 0), pipeline_mode=pl.Buffered(1) if emit_w else None),
                  pl.BlockSpec((1, d), lambda i, j: (0, 0)),
                  mod_spec, mod_spec,
                  pl.BlockSpec((d, tn), lambda i, j: (0, j))] + cast_specs,
        out_specs=out_specs,
        out_shape=out_shape,
        scratch_shapes=[pltpu.VMEM((tm, d), BF16)],
        compiler_params=_cparams("arbitrary", "arbitrary"),
        name="in_proj",
    )(x, g, scale, shift, w, *cast)


def _dft_tables(seq):
    n_fft = 2 * seq
    idx = np.arange(seq)
    ang = 2.0 * np.pi * ((idx[:, None] * idx[None, :]) % n_fft) / n_fft
    cos = np.cos(ang)
    msin = -np.sin(ang)
    sign = np.where(idx % 2 == 0, 1.0, -1.0)
    msin[0, :] = sign
    fwd = np.concatenate([cos, msin], axis=0)
    wgt = np.full((seq,), 2.0 / n_fft)
    wgt[0] = 1.0 / n_fft
    inv_re = cos.T * wgt[None, :]
    inv_im = msin.T * wgt[None, :]
    inv_im[:, 0] = sign / n_fft
    inv = np.concatenate([inv_re, inv_im], axis=1)
    return fwd, inv


def _filter_feats(seq):
    n = np.arange(seq, dtype=np.float64)
    t = n / seq
    f = np.linspace(1e-4, HY_BANDS - 1, HY_BANDS)
    w = 2.0 * math.pi * n / seq
    z = np.concatenate([t[:, None], np.cos(w[:, None] * f), np.sin(w[:, None] * f)], axis=-1)
    out = np.zeros((seq, V7X_LANES), np.float32)
    out[:, :z.shape[1]] = z
    return out


def _filter_kernel(z_ref, w1_ref, b1_ref, w2_ref, b2_ref, fr_ref, w3f_ref, w3b_ref, b3f_ref, b3b_ref,
                   decf_ref, decb_ref, bias_ref, chi_ref, clo_ref, shi_ref, slo_ref, kr_ref, kiz_ref, krn_ref, h2_ref):
    @pl.when(pl.program_id(0) == 0)
    def _():
        h1 = jnp.sin(fr_ref[0:1, :] * (_dot_exact(z_ref[...], w1_ref[...]) + b1_ref[...]))
        h2_ref[...] = jnp.sin(fr_ref[1:2, :] * (_dot_exact(h1, w2_ref[...]) + b2_ref[...]))

    h2 = h2_ref[...]
    seq, cw = kr_ref.shape
    t = z_ref[:, 0:1]
    row = lax.broadcasted_iota(jnp.int32, (seq, cw), 0)
    hf = (_dot_exact(h2, w3f_ref[...]) + b3f_ref[...]) * jnp.exp(-t * jnp.abs(decf_ref[...]))
    hb = (_dot_exact(h2, w3b_ref[...]) + b3b_ref[...]) * jnp.exp(-t * jnp.abs(decb_ref[...]))
    hb = jnp.where(row == 0, 0.0, hb)
    norm = (jnp.sum(jnp.abs(hf), axis=0, keepdims=True)
            + jnp.sum(jnp.abs(hb), axis=0, keepdims=True) + FILTER_EPS)
    inv = 1.0 / norm
    even = (hf + hb) * inv
    odd = (hf - hb) * inv
    bias = bias_ref[...]
    kr = _dot_split(chi_ref, clo_ref, even) + bias
    ki = _dot_split(shi_ref, slo_ref, odd)
    nyq = jnp.sum(jnp.where((row & 1) == 0, even, -even), axis=0, keepdims=True) + bias
    kr_ref[...] = kr.astype(BF16)
    kiz_ref[...] = jnp.where(row == 0, 0.0, ki).astype(BF16)
    krn_ref[...] = jnp.where(row == 0, nyq, kr).astype(BF16)


def _hyena_filters(seq, w1, b1, w2, b2, w3, b3, freq, decay, hy_bias, d_hy):
    fh = w1.shape[1]
    pad = V7X_LANES
    w1p = jnp.zeros((pad, pad), F32).at[:w1.shape[0], :fh].set(w1)
    b1p = jnp.zeros((1, pad), F32).at[0, :fh].set(b1)
    w2p = jnp.zeros((pad, pad), F32).at[:fh, :fh].set(w2)
    b2p = jnp.zeros((1, pad), F32).at[0, :fh].set(b2)
    frp = jnp.zeros((2, pad), F32).at[:, :fh].set(freq)
    ncol = w3.shape[1] // 2
    w3p = jnp.zeros((pad, 2 * ncol), F32).at[:fh].set(w3)
    b3r = b3.reshape(1, 2 * ncol)
    dec = decay.reshape(1, 2 * ncol)
    bias = hy_bias.reshape(1, ncol)
    fwd, _ = _dft_tables(seq)
    chi, clo = _split_bf16(fwd[:seq])
    shi, slo = _split_bf16(fwd[seq:])
    z = jnp.asarray(_filter_feats(seq))
    cw = _tile(ncol, 512)
    nb = ncol // cw
    full = lambda shape: pl.BlockSpec(shape, lambda j: (0, 0))
    colf = lambda rows: pl.BlockSpec((rows, cw), lambda j: (0, j))
    colb = lambda rows: pl.BlockSpec((rows, cw), lambda j: (0, nb + j))
    out = jax.ShapeDtypeStruct((seq, ncol), BF16)
    return pl.pallas_call(
        _filter_kernel,
        grid=(nb,),
        in_specs=[full((seq, pad)), full((pad, pad)), full((1, pad)), full((pad, pad)), full((1, pad)),
                  full((2, pad)), colf(pad), colb(pad), colf(1), colb(1), colf(1), colb(1), colf(1),
                  _resident((seq, seq)), _resident((seq, seq)), _resident((seq, seq)), _resident((seq, seq))],
        out_specs=[colf(seq), colf(seq), colf(seq)],
        out_shape=[out, out, out],
        scratch_shapes=[pltpu.VMEM((seq, pad), F32)],
        compiler_params=_cparams("arbitrary"),
        name=f"hy_filter_{seq}",
    )(z, w1p, b1p, w2p, b2p, frp, w3p, w3p, b3r, b3r, dec, dec, bias, chi, clo, shi, slo)


def _hyena_kernel(x1_ref, x2_ref, v_ref, w1_ref, w2_ref, wv_ref,
                  kr0_ref, kiz0_ref, krn0_ref, kr1_ref, kiz1_ref, krn1_ref, f_ref, g_ref, o_ref, *, seq):
    cw = o_ref.shape[1]
    row = lax.broadcasted_iota(jnp.int32, (seq, cw), 0)
    first = row == 0
    last = row == seq - 1

    def long_conv(u, kr_ref, kiz_ref, krn_ref):
        spec = _dot(f_ref[...], u.astype(BF16))
        ur = spec[:seq].astype(BF16)
        ui = spec[seq:].astype(BF16)
        kiz = kiz_ref[...]
        yr = ur * kr_ref[...] - ui * kiz
        yi = ur * kiz + ui * krn_ref[...]
        return _dot(g_ref[:, :seq], yr) + _dot(g_ref[:, seq:], yi)

    for s in range(o_ref.shape[0] // seq):
        rs = slice(s * seq, (s + 1) * seq)
        x1 = _dwconv3_rows(x1_ref[rs], w1_ref[...], first, last)
        x2 = _dwconv3_rows(x2_ref[rs], w2_ref[...], first, last)
        v = _dwconv3_rows(v_ref[rs], wv_ref[...], first, last)
        z = x1 * long_conv(v, kr0_ref, kiz0_ref, krn0_ref)
        o_ref[rs] = (x2 * long_conv(z, kr1_ref, kiz1_ref, krn1_ref)).astype(BF16)


def _hyena(proj, batch, seq, short_w, tables, d_hy):
    kr, kiz, krn = tables
    cw = _tile(d_hy, 1024 if seq <= 256 else 512)
    nb = d_hy // cw
    fwd, inv = _dft_tables(seq)
    fmat = jnp.asarray(fwd, F32).astype(BF16)
    gmat = jnp.asarray(inv, F32).astype(BF16)
    bs = 2 if batch % 2 == 0 and seq <= 256 else 1
    col = lambda rows, off: pl.BlockSpec((rows, cw), lambda j, b: (0, off * nb + j))
    act = lambda off: pl.BlockSpec((bs * seq, cw), lambda j, b: (b, off * nb + j))
    return pl.pallas_call(
        functools.partial(_hyena_kernel, seq=seq),
        grid=(nb, batch // bs),
        in_specs=[act(0), act(1), act(2), col(3, 0), col(3, 1), col(3, 2),
                  col(seq, 0), col(seq, 0), col(seq, 0), col(seq, 1), col(seq, 1), col(seq, 1),
                  _resident((2 * seq, seq)), _resident((seq, 2 * seq))],
        out_specs=pl.BlockSpec((bs * seq, cw), lambda j, b: (b, j)),
        out_shape=jax.ShapeDtypeStruct((batch * seq, d_hy), BF16),
        compiler_params=_cparams("parallel", "parallel"),
        name=f"hyena_{seq}",
    )(proj, proj, proj, short_w, short_w, short_w, kr, kiz, krn, kr, kiz, krn, fmat, gmat)


def _rope_tables(seq, dk):
    rows = seq // GRID_W
    row = jnp.repeat(jnp.arange(rows), GRID_W).astype(F32)
    col = jnp.tile(jnp.arange(GRID_W), rows).astype(F32)
    nfreq = dk // 4
    inv = ROPE_BASE ** (-jnp.arange(nfreq, dtype=F32) / nfreq)
    ang = jnp.concatenate([row[:, None] * inv, col[:, None] * inv], axis=-1)
    cos, sin = jnp.cos(ang), jnp.sin(ang)
    return jnp.concatenate([cos, cos], axis=-1), jnp.concatenate([-sin, sin], axis=-1)


def _retention_kernel(*refs, seq, hb, dk, use_rope, use_state, want_state):
    refs = list(refs)
    q_ref, k_ref, v_ref, g_ref, lg_ref, gn_ref = refs[:6]
    pos = 6
    if use_rope:
        cos_ref, sin_ref = refs[pos:pos + 2]
        pos += 2
    if use_state:
        s0_ref = refs[pos]
        pos += 1
    o_ref = refs[pos]
    pos += 1
    if want_state:
        st_ref = refs[pos]
        pos += 1
    d_ref = refs[pos]

    def log_gamma(hh, direction):
        return jnp.log(_sigmoid(lg_ref[hh, direction]))[:, 0:1]

    @pl.when(pl.program_id(1) == 0)
    def _():
        i = lax.broadcasted_iota(jnp.int32, (dk, dk), 0)
        j = lax.broadcasted_iota(jnp.int32, (dk, dk), 1)
        diff = (i - j).astype(F32)
        scale = dk ** -0.5
        for hh in range(hb):
            lf = log_gamma(hh, 0)
            lb = log_gamma(hh, 1)
            base_f = scale * jnp.exp(lf * diff)
            base_b = scale * jnp.exp(lb * (-diff))
            diag = jnp.where(diff >= 0, base_f, 0.0) + jnp.where(diff <= 0, base_b, 0.0)
            for bi in range(seq // dk):
                for bj in range(seq // dk):
                    if bi == bj:
                        blk = diag
                    elif bi > bj:
                        blk = base_f * jnp.exp(lf * float(dk * (bi - bj)))
                    else:
                        blk = base_b * jnp.exp(lb * float(dk * (bj - bi)))
                    d_ref[hh, bi * dk:(bi + 1) * dk, bj * dk:(bj + 1) * dk] = blk

    pos_f = lax.broadcasted_iota(jnp.int32, (seq, dk), 0).astype(F32)
    for hh in range(hb):
        sl = slice(hh * dk, (hh + 1) * dk)
        qb = q_ref[:, sl]
        kb16 = k_ref[:, sl]
        vb = v_ref[:, sl]
        if use_rope:
            cos = cos_ref[...]
            sin = sin_ref[...]
            q = qb.astype(F32)
            k = kb16.astype(F32)
            qb = (q * cos + pltpu.roll(q, dk // 2, 1) * sin).astype(BF16)
            kb16 = (k * cos + pltpu.roll(k, dk // 2, 1) * sin).astype(BF16)
        s = lax.dot_general(qb, kb16, (((1,), (1,)), ((), ())), preferred_element_type=F32)
        o = _dot((s * d_ref[hh]).astype(BF16), vb)
        lf = log_gamma(hh, 0)
        lb = log_gamma(hh, 1)
        if use_state:
            o = o + _dot(qb, s0_ref[0, hh].astype(BF16)) * jnp.exp(lf * (pos_f + 1.0))
            o = o + _dot(qb, s0_ref[1, hh].astype(BF16)) * jnp.exp(lb * (seq - pos_f))
        if want_state:
            k = kb16.astype(F32) * dk ** -0.5
            kf = (k * jnp.exp(lf * (seq - 1.0 - pos_f))).astype(BF16)
            kb = (k * jnp.exp(lb * pos_f)).astype(BF16)
            tn = (((0,), (0,)), ((), ()))
            st_ref[0, hh] = lax.dot_general(kf, vb, tn, preferred_element_type=F32)
            st_ref[1, hh] = lax.dot_general(kb, vb, tn, preferred_element_type=F32)
        mu = jnp.mean(o, axis=-1, keepdims=True)
        oc = o - mu
        var = jnp.mean(oc * oc, axis=-1, keepdims=True)
        y = (oc * lax.rsqrt(var + GN_EPS)) * gn_ref[:, sl] * g_ref[:, sl].astype(F32)
        o_ref[:, sl] = y.astype(BF16)


def _retention(proj, batch, seq, n_heads, dk, decay_logit, ret_gn, rope, state0, want_state):
    d_ret = n_heads * dk
    hb = n_heads if seq <= 256 else min(n_heads, 2)
    bw = hb * dk
    nhb = n_heads // hb
    lg = jnp.broadcast_to(decay_logit.T[:, :, None, None], (n_heads, 2, 1, V7X_LANES))
    act = lambda part: pl.BlockSpec((seq, bw), lambda h, b: (b, (part * d_ret) // bw + h))
    in_specs = [act(0), act(1), act(2), act(3),
                pl.BlockSpec((hb, 2, 1, V7X_LANES), lambda h, b: (h, 0, 0, 0)),
                pl.BlockSpec((1, bw), lambda h, b: (0, h))]
    args = [proj, proj, proj, proj, lg, ret_gn.reshape(1, d_ret)]
    if rope is not None:
        in_specs += [pl.BlockSpec((seq, dk), lambda h, b: (0, 0))] * 2
        args += list(rope)
    if state0 is not None:
        in_specs.append(pl.BlockSpec((None, 2, hb, dk, dk), lambda h, b: (b, 0, h, 0, 0)))
        args.append(state0)
    out_specs = [pl.BlockSpec((seq, bw), lambda h, b: (b, h))]
    out_shape = [jax.ShapeDtypeStruct((batch * seq, d_ret), BF16)]
    if want_state:
        out_specs.append(pl.BlockSpec((None, 2, hb, dk, dk), lambda h, b: (b, 0, h, 0, 0)))
        out_shape.append(jax.ShapeDtypeStruct((batch, 2, n_heads, dk, dk), F32))
    body = functools.partial(_retention_kernel, seq=seq, hb=hb, dk=dk, use_rope=rope is not None,
                             use_state=state0 is not None, want_state=want_state)
    return pl.pallas_call(
        body,
        grid=(nhb, batch),
        in_specs=in_specs,
        out_specs=out_specs,
        out_shape=out_shape,
        scratch_shapes=[pltpu.VMEM((hb, seq, seq), F32)],
        compiler_params=_cparams("parallel", "arbitrary"),
        name=f"retention_{seq}",
    )(*args)


def _merge_kernel(*refs, nblk):
    gate_refs = refs[:2 * nblk]
    (yhy_ref, yret_ref, x_ref, wbh_ref, wbr_ref, wo_ref, gpost_ref, gm_ref, gpre_ref, sc_ref, sh_ref,
     o_ref, h_ref) = refs[2 * nblk:]
    a = _dot(yhy_ref[...], wbh_ref[...])
    b = _dot(yret_ref[...], wbr_ref[...])
    wblk = gate_refs[0].shape[1]
    parts = []
    for kk in range(nblk):
        sl = slice(kk * wblk, (kk + 1) * wblk)
        g_hy = gate_refs[kk][...].astype(F32)
        g_ret = gate_refs[nblk + kk][...].astype(F32)
        parts.append((g_hy * a[:, sl] + g_ret * b[:, sl]).astype(BF16))
    merged = parts[0] if nblk == 1 else jnp.concatenate(parts, axis=1)
    out = _dot(merged, wo_ref[...])
    x1 = x_ref[...] + gm_ref[...] * _rms_scale(out, gpost_ref[...])
    o_ref[...] = x1
    h_ref[...] = (_rms_scale(x1, gpre_ref[...]) * (1.0 + sc_ref[...]) + sh_ref[...]).astype(BF16)


def _merge(proj, col0, y_hy, y_ret, x, w_br_hy, w_br_ret, w_out, g_post, gate_m, g_pre_f, scale_f, shift_f,
           rows_per_mod, tm):
    m, d = x.shape
    vec = pl.BlockSpec((1, d), lambda i: (0, 0))
    mod_spec = pl.BlockSpec((None, 1, d), lambda i: ((i * tm) // rows_per_mod, 0, 0))
    wblk = math.gcd(col0, d)
    nblk = d // wblk
    gate_spec = lambda kk: pl.BlockSpec((tm, wblk), lambda i: (i, col0 // wblk + kk))
    const = lambda arr: _resident(arr.shape)
    row = lambda width: pl.BlockSpec((tm, width), lambda i: (i, 0))
    return pl.pallas_call(
        functools.partial(_merge_kernel, nblk=nblk),
        grid=(m // tm,),
        in_specs=[gate_spec(kk) for kk in range(2 * nblk)]
        + [row(y_hy.shape[1]), row(y_ret.shape[1]), row(d), const(w_br_hy), const(w_br_ret), const(w_out),
           vec, mod_spec, vec, mod_spec, mod_spec],
        out_specs=[row(d), row(d)],
        out_shape=[jax.ShapeDtypeStruct((m, d), F32), jax.ShapeDtypeStruct((m, d), BF16)],
        compiler_params=_cparams("parallel"),
        name="merge",
    )(*([proj] * (2 * nblk)), y_hy, y_ret, x, w_br_hy, w_br_ret, w_out, g_post, gate_m, g_pre_f, scale_f, shift_f)


def _ffn_kernel(x_ref, h_ref, gate_ref, gpost_ref, wa_ref, wb_ref, ca_ref, cb_ref, wd_ref, o_ref, act_ref, *, seq):
    j = pl.program_id(1)
    nf = pl.num_programs(1) - 1
    tm = x_ref.shape[0]
    tf = wa_ref.shape[1]

    def up(slot):
        pos = lax.broadcasted_iota(jnp.int32, (tm, tf), 0) % seq
        first = pos == 0
        last = pos == seq - 1
        h = h_ref[...]
        a = _dwconv3_rows(_dot(h, wa_ref[...]), ca_ref[...], first, last)
        b = _dwconv3_rows(_dot(h, wb_ref[...]), cb_ref[...], first, last)
        c1 = math.sqrt(2.0 / math.pi)
        half = 0.5 * a
        gelu = half + half * jnp.tanh(a * (c1 + (c1 * 0.044715) * (a * a)))
        act_ref[slot] = (gelu * b).astype(BF16)

    def down(slot):
        return _dot(act_ref[slot], wd_ref[...])

    @pl.when(j == 0)
    def _():
        up(0)

    @pl.when(j == 1)
    def _():
        o_ref[...] = down(0)
        up(1)

    @pl.when((j > 1) & (j < nf))
    def _():
        slot = j % 2
        o_ref[...] += down(1 - slot)
        up(slot)

    @pl.when(j == nf)
    def _():
        f = o_ref[...] + down((nf - 1) % 2)
        o_ref[...] = x_ref[...] + gate_ref[...] * _rms_scale(f, gpost_ref[...])


def _ffn(x, h, seq, gate, g_post, w_up, conv_w, w_down, rows_per_mod, tm):
    m, d = x.shape
    d_ff = w_down.shape[0]
    tf = _tile(d_ff, 512)
    nf = d_ff // tf
    assert nf >= 2, (d_ff, tf)
    once = pl.BlockSpec((tm, d), lambda i, j: (i, 0), pipeline_mode=pl.Buffered(1))
    up_blk = lambda j: jnp.minimum(j, nf - 1)
    down_blk = lambda j: jnp.maximum(j - 1, 0)
    return pl.pallas_call(
        functools.partial(_ffn_kernel, seq=seq),
        grid=(m // tm, nf + 1),
        in_specs=[once, once,
                  pl.BlockSpec((None, 1, d), lambda i, j: ((i * tm) // rows_per_mod, 0, 0)),
                  pl.BlockSpec((1, d), lambda i, j: (0, 0)),
                  pl.BlockSpec((d, tf), lambda i, j: (0, up_blk(j))),
                  pl.BlockSpec((d, tf), lambda i, j: (0, nf + up_blk(j))),
                  pl.BlockSpec((3, tf), lambda i, j: (0, up_blk(j))),
                  pl.BlockSpec((3, tf), lambda i, j: (0, nf + up_blk(j))),
                  pl.BlockSpec((tf, d), lambda i, j: (down_blk(j), 0))],
        out_specs=pl.BlockSpec((tm, d), lambda i, j: (i, 0)),
        out_shape=jax.ShapeDtypeStruct((m, d), F32),
        scratch_shapes=[pltpu.VMEM((2, tm, tf), BF16)],
        compiler_params=_cparams("parallel", "arbitrary"),
        name="ffn",
    )(x, h, gate, g_post, w_up, w_up, conv_w, conv_w, w_down)


class _Group:
    def __init__(self, x3, mod):
        self.batch, self.seq, self.d = x3.shape
        self.m = self.batch * self.seq
        self.x = x3.reshape(self.m, self.d)
        per_seq_mod = mod.shape[0] != 1
        self.rows_per_mod = self.seq if per_seq_mod else self.m
        self.tm = _row_tile(self.m, self.seq, per_seq_mod, 1024)
        self.tm_merge = _tile(self.tm, 512, unit=V7X_SUBLANES)
        (self.shift_m, self.scale_m, self.gate_m,
         self.shift_f, self.scale_f, self.gate_f) = (mod[:, i][:, None, :] for i in range(6))

    def in_proj(self, p, w_in, cast=()):
        return _in_proj(self.x, p["g_pre_m"], self.scale_m, self.shift_m, w_in, p["d_hy"], p["d_ret"],
                        self.rows_per_mod, self.tm, cast)


def _mix_and_ffn(grp, hy_in, ret_in, rope, state0, want_state, p):
    d_hy, d_ret, n_heads, dk = p["d_hy"], p["d_ret"], p["n_heads"], p["dk"]
    tables = _hyena_filters(grp.seq, p["hy_w1"], p["hy_b1"], p["hy_w2"], p["hy_b2"], p["hy_w3"], p["hy_b3"],
                            p["hy_freq"], p["hy_decay"], p["hy_bias"], d_hy)
    y_hy = _hyena(hy_in, grp.batch, grp.seq, p["hy_short_w"], tables, d_hy)
    ret = _retention(ret_in, grp.batch, grp.seq, n_heads, dk, p["ret_decay_logit"], p["ret_gn"], rope, state0,
                     want_state)
    x, h_ffn = _merge(ret_in, 4 * d_ret, y_hy, ret[0], grp.x, p["w_br_hy"], p["w_br_ret"], p["w_out"],
                      p["g_post_m"], grp.gate_m, p["g_pre_f"], grp.scale_f, grp.shift_f, grp.rows_per_mod,
                      grp.tm_merge)
    x = _ffn(x, h_ffn, grp.seq, grp.gate_f, p["g_post_f"], p["ffn_w_up"], p["ffn_conv"], p["ffn_w_down"],
             grp.rows_per_mod, grp.tm)
    return x.reshape(grp.batch, grp.seq, grp.d), (ret[1] if want_state else None)


def kernel(x_prompt, x_sample, state_ret, c, c_ctx, w_ada, b_ada, norm_pre_mix, norm_post_mix, norm_pre_ffn,
           norm_post_ffn, w_in, hy_short_w, hy_w1, hy_b1, hy_w2, hy_b2, hy_w3, hy_b3, hy_freq, hy_decay, hy_bias,
           ret_decay_logit, ret_gn, w_br_hy, w_br_ret, w_out, ffn_w_up, ffn_conv, ffn_w_down):
    depth = w_in.shape[0]
    d = x_prompt.shape[-1]
    n_dec = x_sample.shape[0]
    n_heads, dk = state_ret.shape[3], state_ret.shape[4]
    d_hy = hy_bias.shape[-1]
    cc = jnp.zeros((V7X_SUBLANES, d), F32).at[0].set(c_ctx).at[1:1 + n_dec].set(c)
    rope = _rope_tables(x_sample.shape[1], dk)
    x_p, x_s = x_prompt, x_sample
    states = []
    for l in range(depth):
        p = dict(
            d_hy=d_hy, d_ret=n_heads * dk, n_heads=n_heads, dk=dk,
            g_pre_m=norm_pre_mix[l][None], g_post_m=norm_post_mix[l][None],
            g_pre_f=norm_pre_ffn[l][None], g_post_f=norm_post_ffn[l][None],
            hy_short_w=hy_short_w[l],
            hy_w1=hy_w1[l], hy_b1=hy_b1[l], hy_w2=hy_w2[l], hy_b2=hy_b2[l], hy_w3=hy_w3[l], hy_b3=hy_b3[l],
            hy_freq=hy_freq[l], hy_decay=hy_decay[l], hy_bias=hy_bias[l],
            ret_decay_logit=ret_decay_logit[l], ret_gn=ret_gn[l], ffn_conv=ffn_conv[l],
        )
        mod = _ada_mod(cc, w_ada[l], b_ada[l][None]).reshape(V7X_SUBLANES, 6, d)
        ctx = _Group(x_p, mod[0:1])
        lat = _Group(x_s, mod[1:1 + n_dec])
        hy_s, ret_s, w_in_bf = lat.in_proj(p, w_in[l])
        hy_p, ret_p, *rest = ctx.in_proj(p, w_in_bf, (ffn_w_up[l], ffn_w_down[l], w_out[l], w_br_hy[l], w_br_ret[l]))
        p["ffn_w_up"], p["ffn_w_down"], p["w_out"], p["w_br_hy"], p["w_br_ret"] = rest
        x_p, st = _mix_and_ffn(ctx, hy_p, ret_p, None, None, True, p)
        x_s, _ = _mix_and_ffn(lat, hy_s, ret_s, rope, state_ret[:, l], False, p)
        states.append(st)
    return x_p, x_s, jnp.stack(states, axis=1)
```

```python
import functools
import math

import jax
import jax.numpy as jnp
import numpy as np
from jax import lax
from jax.experimental import pallas as pl
from jax.experimental.pallas import tpu as pltpu

F32 = jnp.float32
BF16 = jnp.bfloat16

RMS_EPS = 1e-6
GN_EPS = 1e-5
FILTER_EPS = 1e-6
HY_BANDS = 16
GRID_W = 64
ROPE_BASE = 10000.0

V7X_VMEM_LIMIT_BYTES = 58 * 1024 * 1024
V7X_LANES = 128
V7X_SUBLANES = 8


def _cparams(*sem):
    return pltpu.CompilerParams(dimension_semantics=sem, vmem_limit_bytes=V7X_VMEM_LIMIT_BYTES)


def _tile(n, target, unit=V7X_LANES):
    if n <= target:
        return n
    best = unit
    for t in range(unit, target + 1, unit):
        if n % t == 0:
            best = t
    assert n % best == 0, (n, target, unit)
    return best


def _row_tile(m, seq, per_seq_mod, target):
    if per_seq_mod or seq >= target:
        return seq
    return seq * _tile(m // seq, target // seq, unit=1)


def _resident(shape):
    return pl.BlockSpec(shape, lambda *_: (0,) * len(shape), pipeline_mode=pl.Buffered(1))


def _sigmoid(x):
    return 1.0 / (1.0 + jnp.exp(-x))


def _rms_scale(x, g):
    ms = jnp.mean(x * x, axis=-1, keepdims=True)
    return (x * lax.rsqrt(ms + RMS_EPS)) * g


def _dot(a, b):
    return jnp.dot(a, b, preferred_element_type=F32)


def _dot_exact(a, b):
    return jnp.dot(a, b, preferred_element_type=F32, precision=lax.Precision.HIGHEST)


def _split_bf16(table):
    hi = table.astype(BF16)
    lo = (table - hi.astype(np.float64)).astype(BF16)
    return jnp.asarray(hi), jnp.asarray(lo)


def _dot_split(a_hi_ref, a_lo_ref, b):
    b_hi = b.astype(BF16)
    b_lo = (b - b_hi.astype(F32)).astype(BF16)
    a_hi = a_hi_ref[...]
    return _dot(a_hi, b_hi) + (_dot(a_lo_ref[...], b_hi) + _dot(a_hi, b_lo))


def _dwconv3_rows(x, w, first, last):
    rows = x.shape[0]
    prev = jnp.where(first, 0.0, pltpu.roll(x, 1, 0))
    nxt = jnp.where(last, 0.0, pltpu.roll(x, rows - 1, 0))
    return prev * w[0:1] + x * w[1:2] + nxt * w[2:3]


def _ada_kernel(cc_ref, w_ref, b_ref, o_ref):
    cc = cc_ref[...]
    s = cc * _sigmoid(cc)
    o_ref[...] = _dot(s.astype(BF16), w_ref[...].astype(BF16)) + b_ref[...]


def _ada_mod(cc, w, b):
    d, n = w.shape
    tn = _tile(n, 1024)
    return pl.pallas_call(
        _ada_kernel,
        grid=(n // tn,),
        in_specs=[pl.BlockSpec((V7X_SUBLANES, d), lambda j: (0, 0)),
                  pl.BlockSpec((d, tn), lambda j: (0, j)),
                  pl.BlockSpec((1, tn), lambda j: (0, j))],
        out_specs=pl.BlockSpec((V7X_SUBLANES, tn), lambda j: (0, j)),
        out_shape=jax.ShapeDtypeStruct((V7X_SUBLANES, n), F32),
        compiler_params=_cparams("parallel"),
        name="ada_mod",
    )(cc, w, b)


def _in_proj_kernel(*refs, seq, n_hy, n_qkv, n_g, n_cast, emit_w):
    x_ref, g_ref, sc_ref, sh_ref, w_ref, cw_ref = refs[:6]
    cast_in = refs[6:6 + n_cast]
    hy_ref, ret_ref = refs[6 + n_cast:8 + n_cast]
    pos = 8 + n_cast
    if emit_w:
        wbf_ref = refs[pos]
        pos += 1
    cast_out = refs[pos:pos + n_cast]
    h_ref = refs[pos + n_cast]
    j = pl.program_id(1)

    @pl.when(j == 0)
    def _():
        n_mod = sc_ref.shape[0]
        rows = x_ref.shape[0] // n_mod
        for b in range(n_mod):
            rs = slice(b * rows, (b + 1) * rows)
            h = _rms_scale(x_ref[rs], g_ref[...] * (1.0 + sc_ref[b])) + sh_ref[b]
            h_ref[rs] = h.astype(BF16)

    def tile():
        w = w_ref[...]
        if emit_w:
            w = w.astype(BF16)
            wbf_ref[...] = w
        for src, dst in zip(cast_in, cast_out):
            dst[...] = src[...].astype(BF16)
        return _dot(h_ref[...], w)

    @pl.when(j < n_hy)
    def _():
        pos = lax.broadcasted_iota(jnp.int32, hy_ref.shape, 0) % seq
        hy_ref[...] = _dwconv3_rows(tile(), cw_ref[...], pos == 0, pos == seq - 1)

    @pl.when((j >= n_hy) & (j < n_hy + n_qkv))
    def _():
        ret_ref[...] = tile().astype(BF16)

    @pl.when((j >= n_hy + n_qkv) & (j < n_hy + n_qkv + n_g))
    def _():
        p = tile()
        ret_ref[...] = (p * _sigmoid(p)).astype(BF16)

    @pl.when(j >= n_hy + n_qkv + n_g)
    def _():
        ret_ref[...] = _sigmoid(tile()).astype(BF16)


V7X_BF16_ROW_TILE = 2 * V7X_SUBLANES


def _cast_rows(arr, n_steps):
    rows = arr.shape[0]
    for blk in range(V7X_BF16_ROW_TILE, rows + 1, V7X_BF16_ROW_TILE):
        if rows % blk == 0 and rows // blk <= n_steps:
            return blk
    return None


def _in_proj(x, g, scale, shift, w, short_w, seq, d_hy, d_ret, rows_per_mod, tm, cast=()):
    m, d = x.shape
    n = w.shape[1]
    emit_w = w.dtype != BF16
    if emit_w:
        tm = m
    tn = _tile(math.gcd(3 * d_hy, d_ret, 2 * d), 512 if emit_w else 1024)
    n_hy, n_qkv, n_g = 3 * d_hy // tn, 3 * d_ret // tn, d_ret // tn
    n_col = n // tn
    n_mod = max(1, tm // rows_per_mod)
    mod_spec = pl.BlockSpec((n_mod, 1, d), lambda i, j: ((i * tm) // rows_per_mod // n_mod, 0, 0))
    blks = [_cast_rows(a, (m // tm) * n_col) for a in cast]
    assert all(b is not None for b in blks), [a.shape for a in cast]
    cast_specs = [pl.BlockSpec((b, a.shape[1]), lambda i, j, nb=a.shape[0] // b: (jnp.minimum(i * n_col + j, nb - 1), 0))
                  for a, b in zip(cast, blks)]
    out_specs = [pl.BlockSpec((tm, tn), lambda i, j: (i, jnp.minimum(j, n_hy - 1))),
                 pl.BlockSpec((tm, tn), lambda i, j: (i, jnp.maximum(j - n_hy, 0)))]
    out_shape = [jax.ShapeDtypeStruct((m, 3 * d_hy), F32), jax.ShapeDtypeStruct((m, n - 3 * d_hy), BF16)]
    if emit_w:
        out_specs.append(pl.BlockSpec((d, tn), lambda i, j: (0, j)))
        out_shape.append(jax.ShapeDtypeStruct(w.shape, BF16))
    out_specs += cast_specs
    out_shape += [jax.ShapeDtypeStruct(a.shape, BF16) for a in cast]
    return pl.pallas_call(
        functools.partial(_in_proj_kernel, seq=seq, n_hy=n_hy, n_qkv=n_qkv, n_g=n_g, n_cast=len(cast),
                          emit_w=emit_w),
        grid=(m // tm, n_col),
        in_specs=[pl.BlockSpec((tm, d), lambda i, j: (i, 0), pipeline_mode=pl.Buffered(1) if emit_w else None),
                  pl.BlockSpec((1, d), lambda i, j: (0, 0)),
                  mod_spec, mod_spec,
                  pl.BlockSpec((d, tn), lambda i, j: (0, j)),
                  pl.BlockSpec((3, tn), lambda i, j: (0, jnp.minimum(j, n_hy - 1)))] + cast_specs,
        out_specs=out_specs,
        out_shape=out_shape,
        scratch_shapes=[pltpu.VMEM((tm, d), BF16)],
        compiler_params=_cparams("arbitrary", "arbitrary"),
        name="in_proj",
    )(x, g, scale, shift, w, short_w, *cast)


def _dft_tables(seq):
    n_fft = 2 * seq
    idx = np.arange(seq)
    ang = 2.0 * np.pi * ((idx[:, None] * idx[None, :]) % n_fft) / n_fft
    cos = np.cos(ang)
    msin = -np.sin(ang)
    sign = np.where(idx % 2 == 0, 1.0, -1.0)
    msin[0, :] = sign
    fwd = np.concatenate([cos, msin], axis=0)
    wgt = np.full((seq,), 2.0 / n_fft)
    wgt[0] = 1.0 / n_fft
    inv_re = cos.T * wgt[None, :]
    inv_im = msin.T * wgt[None, :]
    inv_im[:, 0] = sign / n_fft
    inv = np.concatenate([inv_re, inv_im], axis=1)
    return fwd, inv


def _filter_feats(seq):
    n = np.arange(seq, dtype=np.float64)
    t = n / seq
    f = np.linspace(1e-4, HY_BANDS - 1, HY_BANDS)
    w = 2.0 * math.pi * n / seq
    z = np.concatenate([t[:, None], np.cos(w[:, None] * f), np.sin(w[:, None] * f)], axis=-1)
    out = np.zeros((seq, V7X_LANES), np.float32)
    out[:, :z.shape[1]] = z
    return out


def _filter_kernel(z_ref, w1_ref, b1_ref, w2_ref, b2_ref, fr_ref, w3f_ref, w3b_ref, b3f_ref, b3b_ref,
                   decf_ref, decb_ref, bias_ref, chi_ref, clo_ref, shi_ref, slo_ref, kr_ref, kiz_ref, krn_ref, h2_ref):
    @pl.when(pl.program_id(0) == 0)
    def _():
        h1 = jnp.sin(fr_ref[0:1, :] * (_dot_exact(z_ref[...], w1_ref[...]) + b1_ref[...]))
        h2_ref[...] = jnp.sin(fr_ref[1:2, :] * (_dot_exact(h1, w2_ref[...]) + b2_ref[...]))

    h2 = h2_ref[...]
    seq, cw = kr_ref.shape
    t = z_ref[:, 0:1]
    row = lax.broadcasted_iota(jnp.int32, (seq, cw), 0)
    hf = (_dot_exact(h2, w3f_ref[...]) + b3f_ref[...]) * jnp.exp(-t * jnp.abs(decf_ref[...]))
    hb = (_dot_exact(h2, w3b_ref[...]) + b3b_ref[...]) * jnp.exp(-t * jnp.abs(decb_ref[...]))
    hb = jnp.where(row == 0, 0.0, hb)
    norm = (jnp.sum(jnp.abs(hf), axis=0, keepdims=True)
            + jnp.sum(jnp.abs(hb), axis=0, keepdims=True) + FILTER_EPS)
    inv = 1.0 / norm
    even = (hf + hb) * inv
    odd = (hf - hb) * inv
    bias = bias_ref[...]
    kr = _dot_split(chi_ref, clo_ref, even) + bias
    ki = _dot_split(shi_ref, slo_ref, odd)
    nyq = jnp.sum(jnp.where((row & 1) == 0, even, -even), axis=0, keepdims=True) + bias
    kr_ref[...] = kr.astype(BF16)
    kiz_ref[...] = jnp.where(row == 0, 0.0, ki).astype(BF16)
    krn_ref[...] = jnp.where(row == 0, nyq, kr).astype(BF16)


def _hyena_filters(seq, w1, b1, w2, b2, w3, b3, freq, decay, hy_bias, d_hy):
    fh = w1.shape[1]
    pad = V7X_LANES
    w1p = jnp.zeros((pad, pad), F32).at[:w1.shape[0], :fh].set(w1)
    b1p = jnp.zeros((1, pad), F32).at[0, :fh].set(b1)
    w2p = jnp.zeros((pad, pad), F32).at[:fh, :fh].set(w2)
    b2p = jnp.zeros((1, pad), F32).at[0, :fh].set(b2)
    frp = jnp.zeros((2, pad), F32).at[:, :fh].set(freq)
    ncol = w3.shape[1] // 2
    w3p = jnp.zeros((pad, 2 * ncol), F32).at[:fh].set(w3)
    b3r = b3.reshape(1, 2 * ncol)
    dec = decay.reshape(1, 2 * ncol)
    bias = hy_bias.reshape(1, ncol)
    fwd, _ = _dft_tables(seq)
    chi, clo = _split_bf16(fwd[:seq])
    shi, slo = _split_bf16(fwd[seq:])
    z = jnp.asarray(_filter_feats(seq))
    cw = _tile(ncol, 512)
    nb = ncol // cw
    full = lambda shape: pl.BlockSpec(shape, lambda j: (0, 0))
    colf = lambda rows: pl.BlockSpec((rows, cw), lambda j: (0, j))
    colb = lambda rows: pl.BlockSpec((rows, cw), lambda j: (0, nb + j))
    out = jax.ShapeDtypeStruct((seq, ncol), BF16)
    return pl.pallas_call(
        _filter_kernel,
        grid=(nb,),
        in_specs=[full((seq, pad)), full((pad, pad)), full((1, pad)), full((pad, pad)), full((1, pad)),
                  full((2, pad)), colf(pad), colb(pad), colf(1), colb(1), colf(1), colb(1), colf(1),
                  _resident((seq, seq)), _resident((seq, seq)), _resident((seq, seq)), _resident((seq, seq))],
        out_specs=[colf(seq), colf(seq), colf(seq)],
        out_shape=[out, out, out],
        scratch_shapes=[pltpu.VMEM((seq, pad), F32)],
        compiler_params=_cparams("arbitrary"),
        name=f"hy_filter_{seq}",
    )(z, w1p, b1p, w2p, b2p, frp, w3p, w3p, b3r, b3r, dec, dec, bias, chi, clo, shi, slo)


def _hyena_kernel(x1_ref, x2_ref, v_ref, kr0_ref, kiz0_ref, krn0_ref, kr1_ref, kiz1_ref, krn1_ref, f_ref, g_ref,
                  o_ref, *, seq):
    def long_conv(u, kr_ref, kiz_ref, krn_ref):
        spec = _dot(f_ref[...], u.astype(BF16))
        ur = spec[:seq].astype(BF16)
        ui = spec[seq:].astype(BF16)
        kiz = kiz_ref[...]
        yr = ur * kr_ref[...] - ui * kiz
        yi = ur * kiz + ui * krn_ref[...]
        return _dot(g_ref[:, :seq], yr) + _dot(g_ref[:, seq:], yi)

    for s in range(o_ref.shape[0] // seq):
        rs = slice(s * seq, (s + 1) * seq)
        z = x1_ref[rs] * long_conv(v_ref[rs], kr0_ref, kiz0_ref, krn0_ref)
        o_ref[rs] = (x2_ref[rs] * long_conv(z, kr1_ref, kiz1_ref, krn1_ref)).astype(BF16)


def _hyena(proj, batch, seq, tables, d_hy):
    kr, kiz, krn = tables
    cw = _tile(d_hy, 1024 if seq <= 256 else 512)
    nb = d_hy // cw
    fwd, inv = _dft_tables(seq)
    fmat = jnp.asarray(fwd, F32).astype(BF16)
    gmat = jnp.asarray(inv, F32).astype(BF16)
    bs = 2 if batch % 2 == 0 and seq <= 256 else 1
    col = lambda rows, off: pl.BlockSpec((rows, cw), lambda j, b: (0, off * nb + j))
    act = lambda off: pl.BlockSpec((bs * seq, cw), lambda j, b: (b, off * nb + j))
    return pl.pallas_call(
        functools.partial(_hyena_kernel, seq=seq),
        grid=(nb, batch // bs),
        in_specs=[act(0), act(1), act(2),
                  col(seq, 0), col(seq, 0), col(seq, 0), col(seq, 1), col(seq, 1), col(seq, 1),
                  _resident((2 * seq, seq)), _resident((seq, 2 * seq))],
        out_specs=pl.BlockSpec((bs * seq, cw), lambda j, b: (b, j)),
        out_shape=jax.ShapeDtypeStruct((batch * seq, d_hy), BF16),
        compiler_params=_cparams("parallel", "parallel"),
        name=f"hyena_{seq}",
    )(proj, proj, proj, kr, kiz, krn, kr, kiz, krn, fmat, gmat)


def _rope_tables(seq, dk):
    rows = seq // GRID_W
    row = jnp.repeat(jnp.arange(rows), GRID_W).astype(F32)
    col = jnp.tile(jnp.arange(GRID_W), rows).astype(F32)
    nfreq = dk // 4
    inv = ROPE_BASE ** (-jnp.arange(nfreq, dtype=F32) / nfreq)
    ang = jnp.concatenate([row[:, None] * inv, col[:, None] * inv], axis=-1)
    cos, sin = jnp.cos(ang), jnp.sin(ang)
    return jnp.concatenate([cos, cos], axis=-1), jnp.concatenate([-sin, sin], axis=-1)


def _retention_kernel(*refs, seq, hb, dk, use_rope, use_state, want_state):
    refs = list(refs)
    q_ref, k_ref, v_ref, g_ref, lg_ref, gn_ref = refs[:6]
    pos = 6
    if use_rope:
        cos_ref, sin_ref = refs[pos:pos + 2]
        pos += 2
    if use_state:
        s0_ref = refs[pos]
        pos += 1
    o_ref = refs[pos]
    pos += 1
    if want_state:
        st_ref = refs[pos]
        pos += 1
    d_ref = refs[pos]

    def log_gamma(hh, direction):
        return jnp.log(_sigmoid(lg_ref[hh, direction]))[:, 0:1]

    @pl.when(pl.program_id(1) == 0)
    def _():
        i = lax.broadcasted_iota(jnp.int32, (dk, dk), 0)
        j = lax.broadcasted_iota(jnp.int32, (dk, dk), 1)
        diff = (i - j).astype(F32)
        scale = dk ** -0.5
        for hh in range(hb):
            lf = log_gamma(hh, 0)
            lb = log_gamma(hh, 1)
            base_f = scale * jnp.exp(lf * diff)
            base_b = scale * jnp.exp(lb * (-diff))
            diag = jnp.where(diff >= 0, base_f, 0.0) + jnp.where(diff <= 0, base_b, 0.0)
            for bi in range(seq // dk):
                for bj in range(seq // dk):
                    if bi == bj:
                        blk = diag
                    elif bi > bj:
                        blk = base_f * jnp.exp(lf * float(dk * (bi - bj)))
                    else:
                        blk = base_b * jnp.exp(lb * float(dk * (bj - bi)))
                    d_ref[hh, bi * dk:(bi + 1) * dk, bj * dk:(bj + 1) * dk] = blk

    pos_f = lax.broadcasted_iota(jnp.int32, (seq, dk), 0).astype(F32)
    for hh in range(hb):
        sl = slice(hh * dk, (hh + 1) * dk)
        qb = q_ref[:, sl]
        kb16 = k_ref[:, sl]
        vb = v_ref[:, sl]
        if use_rope:
            cos = cos_ref[...]
            sin = sin_ref[...]
            q = qb.astype(F32)
            k = kb16.astype(F32)
            qb = (q * cos + pltpu.roll(q, dk // 2, 1) * sin).astype(BF16)
            kb16 = (k * cos + pltpu.roll(k, dk // 2, 1) * sin).astype(BF16)
        s = lax.dot_general(qb, kb16, (((1,), (1,)), ((), ())), preferred_element_type=F32)
        o = _dot((s * d_ref[hh]).astype(BF16), vb)
        lf = log_gamma(hh, 0)
        lb = log_gamma(hh, 1)
        if use_state:
            o = o + _dot(qb, s0_ref[0, hh].astype(BF16)) * jnp.exp(lf * (pos_f + 1.0))
            o = o + _dot(qb, s0_ref[1, hh].astype(BF16)) * jnp.exp(lb * (seq - pos_f))
        if want_state:
            k = kb16.astype(F32) * dk ** -0.5
            kf = (k * jnp.exp(lf * (seq - 1.0 - pos_f))).astype(BF16)
            kb = (k * jnp.exp(lb * pos_f)).astype(BF16)
            tn = (((0,), (0,)), ((), ()))
            st_ref[0, hh] = lax.dot_general(kf, vb, tn, preferred_element_type=F32)
            st_ref[1, hh] = lax.dot_general(kb, vb, tn, preferred_element_type=F32)
        mu = jnp.mean(o, axis=-1, keepdims=True)
        oc = o - mu
        var = jnp.mean(oc * oc, axis=-1, keepdims=True)
        y = (oc * lax.rsqrt(var + GN_EPS)) * gn_ref[:, sl] * g_ref[:, sl].astype(F32)
        o_ref[:, sl] = y.astype(BF16)


def _retention(proj, batch, seq, n_heads, dk, decay_logit, ret_gn, rope, state0, want_state):
    d_ret = n_heads * dk
    hb = n_heads if seq <= 256 else min(n_heads, 2)
    bw = hb * dk
    nhb = n_heads // hb
    lg = jnp.broadcast_to(decay_logit.T[:, :, None, None], (n_heads, 2, 1, V7X_LANES))
    act = lambda part: pl.BlockSpec((seq, bw), lambda h, b: (b, (part * d_ret) // bw + h))
    in_specs = [act(0), act(1), act(2), act(3),
                pl.BlockSpec((hb, 2, 1, V7X_LANES), lambda h, b: (h, 0, 0, 0)),
                pl.BlockSpec((1, bw), lambda h, b: (0, h))]
    args = [proj, proj, proj, proj, lg, ret_gn.reshape(1, d_ret)]
    if rope is not None:
        in_specs += [pl.BlockSpec((seq, dk), lambda h, b: (0, 0))] * 2
        args += list(rope)
    if state0 is not None:
        in_specs.append(pl.BlockSpec((None, 2, hb, dk, dk), lambda h, b: (b, 0, h, 0, 0)))
        args.append(state0)
    out_specs = [pl.BlockSpec((seq, bw), lambda h, b: (b, h))]
    out_shape = [jax.ShapeDtypeStruct((batch * seq, d_ret), BF16)]
    if want_state:
        out_specs.append(pl.BlockSpec((None, 2, hb, dk, dk), lambda h, b: (b, 0, h, 0, 0)))
        out_shape.append(jax.ShapeDtypeStruct((batch, 2, n_heads, dk, dk), F32))
    body = functools.partial(_retention_kernel, seq=seq, hb=hb, dk=dk, use_rope=rope is not None,
                             use_state=state0 is not None, want_state=want_state)
    return pl.pallas_call(
        body,
        grid=(nhb, batch),
        in_specs=in_specs,
        out_specs=out_specs,
        out_shape=out_shape,
        scratch_shapes=[pltpu.VMEM((hb, seq, seq), F32)],
        compiler_params=_cparams("parallel", "arbitrary"),
        name=f"retention_{seq}",
    )(*args)


def _merge_kernel(*refs, nblk):
    gate_refs = refs[:2 * nblk]
    (yhy_ref, yret_ref, x_ref, wbh_ref, wbr_ref, wo_ref, gpost_ref, gm_ref, gpre_ref, sc_ref, sh_ref,
     o_ref, h_ref) = refs[2 * nblk:]
    a = _dot(yhy_ref[...], wbh_ref[...])
    b = _dot(yret_ref[...], wbr_ref[...])
    wblk = gate_refs[0].shape[1]
    parts = []
    for kk in range(nblk):
        sl = slice(kk * wblk, (kk + 1) * wblk)
        g_hy = gate_refs[kk][...].astype(F32)
        g_ret = gate_refs[nblk + kk][...].astype(F32)
        parts.append((g_hy * a[:, sl] + g_ret * b[:, sl]).astype(BF16))
    merged = parts[0] if nblk == 1 else jnp.concatenate(parts, axis=1)
    out = _dot(merged, wo_ref[...])
    x1 = x_ref[...] + _rms_scale(out, gm_ref[...] * gpost_ref[...])
    o_ref[...] = x1
    h_ref[...] = (_rms_scale(x1, gpre_ref[...] * (1.0 + sc_ref[...])) + sh_ref[...]).astype(BF16)


def _merge(proj, col0, y_hy, y_ret, x, w_br_hy, w_br_ret, w_out, g_post, gate_m, g_pre_f, scale_f, shift_f,
           rows_per_mod, tm):
    m, d = x.shape
    vec = pl.BlockSpec((1, d), lambda i: (0, 0))
    mod_spec = pl.BlockSpec((None, 1, d), lambda i: ((i * tm) // rows_per_mod, 0, 0))
    wblk = math.gcd(col0, d)
    nblk = d // wblk
    gate_spec = lambda kk: pl.BlockSpec((tm, wblk), lambda i: (i, col0 // wblk + kk))
    const = lambda arr: _resident(arr.shape)
    row = lambda width: pl.BlockSpec((tm, width), lambda i: (i, 0))
    return pl.pallas_call(
        functools.partial(_merge_kernel, nblk=nblk),
        grid=(m // tm,),
        in_specs=[gate_spec(kk) for kk in range(2 * nblk)]
        + [row(y_hy.shape[1]), row(y_ret.shape[1]), row(d), const(w_br_hy), const(w_br_ret), const(w_out),
           vec, mod_spec, vec, mod_spec, mod_spec],
        out_specs=[row(d), row(d)],
        out_shape=[jax.ShapeDtypeStruct((m, d), F32), jax.ShapeDtypeStruct((m, d), BF16)],
        compiler_params=_cparams("parallel"),
        name="merge",
    )(*([proj] * (2 * nblk)), y_hy, y_ret, x, w_br_hy, w_br_ret, w_out, g_post, gate_m, g_pre_f, scale_f, shift_f)


def _ffn_kernel(x_ref, h_ref, gate_ref, gpost_ref, wa_ref, wb_ref, ca_ref, cb_ref, wd_ref, o_ref, act_ref, *, seq):
    j = pl.program_id(1)
    nf = pl.num_programs(1) - 1
    tm = x_ref.shape[0]
    tf = wa_ref.shape[1]

    def up(slot):
        pos = lax.broadcasted_iota(jnp.int32, (tm, tf), 0) % seq
        first = pos == 0
        last = pos == seq - 1
        h = h_ref[...]
        a = _dwconv3_rows(_dot(h, wa_ref[...]), ca_ref[...], first, last)
        b = _dwconv3_rows(_dot(h, wb_ref[...]), cb_ref[...], first, last)
        c1 = math.sqrt(2.0 / math.pi)
        half = 0.5 * a
        gelu = half + half * jnp.tanh(a * (c1 + (c1 * 0.044715) * (a * a)))
        act_ref[slot] = (gelu * b).astype(BF16)

    def down(slot):
        return _dot(act_ref[slot], wd_ref[...])

    @pl.when(j == 0)
    def _():
        up(0)

    @pl.when(j == 1)
    def _():
        o_ref[...] = down(0)
        up(1)

    @pl.when((j > 1) & (j < nf))
    def _():
        slot = j % 2
        o_ref[...] += down(1 - slot)
        up(slot)

    @pl.when(j == nf)
    def _():
        f = o_ref[...] + down((nf - 1) % 2)
        o_ref[...] = x_ref[...] + _rms_scale(f, gate_ref[...] * gpost_ref[...])


def _ffn(x, h, seq, gate, g_post, w_up, conv_w, w_down, rows_per_mod, tm):
    m, d = x.shape
    d_ff = w_down.shape[0]
    tf = _tile(d_ff, 512)
    nf = d_ff // tf
    assert nf >= 2, (d_ff, tf)
    once = pl.BlockSpec((tm, d), lambda i, j: (i, 0), pipeline_mode=pl.Buffered(1))
    up_blk = lambda j: jnp.minimum(j, nf - 1)
    down_blk = lambda j: jnp.maximum(j - 1, 0)
    return pl.pallas_call(
        functools.partial(_ffn_kernel, seq=seq),
        grid=(m // tm, nf + 1),
        in_specs=[once, once,
                  pl.BlockSpec((None, 1, d), lambda i, j: ((i * tm) // rows_per_mod, 0, 0)),
                  pl.BlockSpec((1, d), lambda i, j: (0, 0)),
                  pl.BlockSpec((d, tf), lambda i, j: (0, up_blk(j))),
                  pl.BlockSpec((d, tf), lambda i, j: (0, nf + up_blk(j))),
                  pl.BlockSpec((3, tf), lambda i, j: (0, up_blk(j))),
                  pl.BlockSpec((3, tf), lambda i, j: (0, nf + up_blk(j))),
                  pl.BlockSpec((tf, d), lambda i, j: (down_blk(j), 0))],
        out_specs=pl.BlockSpec((tm, d), lambda i, j: (i, 0)),
        out_shape=jax.ShapeDtypeStruct((m, d), F32),
        scratch_shapes=[pltpu.VMEM((2, tm, tf), BF16)],
        compiler_params=_cparams("parallel", "arbitrary"),
        name="ffn",
    )(x, h, gate, g_post, w_up, w_up, conv_w, conv_w, w_down)


class _Group:
    def __init__(self, x3, mod):
        self.batch, self.seq, self.d = x3.shape
        self.m = self.batch * self.seq
        self.x = x3.reshape(self.m, self.d)
        per_seq_mod = mod.shape[0] != 1
        self.rows_per_mod = self.seq if per_seq_mod else self.m
        self.tm = _row_tile(self.m, self.seq, per_seq_mod, 1024)
        self.tm_merge = _tile(self.tm, 512, unit=V7X_SUBLANES)
        (self.shift_m, self.scale_m, self.gate_m,
         self.shift_f, self.scale_f, self.gate_f) = (mod[:, i][:, None, :] for i in range(6))

    def in_proj(self, p, w_in, cast=()):
        return _in_proj(self.x, p["g_pre_m"], self.scale_m, self.shift_m, w_in, p["hy_short_w"], self.seq,
                        p["d_hy"], p["d_ret"], self.rows_per_mod, self.tm, cast)


def _mix_and_ffn(grp, hy_in, ret_in, rope, state0, want_state, p):
    d_hy, d_ret, n_heads, dk = p["d_hy"], p["d_ret"], p["n_heads"], p["dk"]
    tables = _hyena_filters(grp.seq, p["hy_w1"], p["hy_b1"], p["hy_w2"], p["hy_b2"], p["hy_w3"], p["hy_b3"],
                            p["hy_freq"], p["hy_decay"], p["hy_bias"], d_hy)
    y_hy = _hyena(hy_in, grp.batch, grp.seq, tables, d_hy)
    ret = _retention(ret_in, grp.batch, grp.seq, n_heads, dk, p["ret_decay_logit"], p["ret_gn"], rope, state0,
                     want_state)
    x, h_ffn = _merge(ret_in, 4 * d_ret, y_hy, ret[0], grp.x, p["w_br_hy"], p["w_br_ret"], p["w_out"],
                      p["g_post_m"], grp.gate_m, p["g_pre_f"], grp.scale_f, grp.shift_f, grp.rows_per_mod,
                      grp.tm_merge)
    x = _ffn(x, h_ffn, grp.seq, grp.gate_f, p["g_post_f"], p["ffn_w_up"], p["ffn_conv"], p["ffn_w_down"],
             grp.rows_per_mod, grp.tm)
    return x.reshape(grp.batch, grp.seq, grp.d), (ret[1] if want_state else None)


def kernel(x_prompt, x_sample, state_ret, c, c_ctx, w_ada, b_ada, norm_pre_mix, norm_post_mix, norm_pre_ffn,
           norm_post_ffn, w_in, hy_short_w, hy_w1, hy_b1, hy_w2, hy_b2, hy_w3, hy_b3, hy_freq, hy_decay, hy_bias,
           ret_decay_logit, ret_gn, w_br_hy, w_br_ret, w_out, ffn_w_up, ffn_conv, ffn_w_down):
    depth = w_in.shape[0]
    d = x_prompt.shape[-1]
    n_dec = x_sample.shape[0]
    n_heads, dk = state_ret.shape[3], state_ret.shape[4]
    d_hy = hy_bias.shape[-1]
    cc = jnp.zeros((V7X_SUBLANES, d), F32).at[0].set(c_ctx).at[1:1 + n_dec].set(c)
    rope = _rope_tables(x_sample.shape[1], dk)
    x_p, x_s = x_prompt, x_sample
    states = []
    for l in range(depth):
        p = dict(
            d_hy=d_hy, d_ret=n_heads * dk, n_heads=n_heads, dk=dk,
            g_pre_m=norm_pre_mix[l][None], g_post_m=norm_post_mix[l][None],
            g_pre_f=norm_pre_ffn[l][None], g_post_f=norm_post_ffn[l][None],
            hy_short_w=hy_short_w[l],
            hy_w1=hy_w1[l], hy_b1=hy_b1[l], hy_w2=hy_w2[l], hy_b2=hy_b2[l], hy_w3=hy_w3[l], hy_b3=hy_b3[l],
            hy_freq=hy_freq[l], hy_decay=hy_decay[l], hy_bias=hy_bias[l],
            ret_decay_logit=ret_decay_logit[l], ret_gn=ret_gn[l], ffn_conv=ffn_conv[l],
        )
        mod = _ada_mod(cc, w_ada[l], b_ada[l][None]).reshape(V7X_SUBLANES, 6, d)
        ctx = _Group(x_p, mod[0:1])
        lat = _Group(x_s, mod[1:1 + n_dec])
        hy_s, ret_s, w_in_bf = lat.in_proj(p, w_in[l])
        hy_p, ret_p, *rest = ctx.in_proj(p, w_in_bf, (ffn_w_up[l], ffn_w_down[l], w_out[l], w_br_hy[l], w_br_ret[l]))
        p["ffn_w_up"], p["ffn_w_down"], p["w_out"], p["w_br_hy"], p["w_br_ret"] = rest
        x_p, st = _mix_and_ffn(ctx, hy_p, ret_p, None, None, True, p)
        x_s, _ = _mix_and_ffn(lat, hy_s, ret_s, rope, state_ret[:, l], False, p)
        states.append(st)
    return x_p, x_s, jnp.stack(states, axis=1)
```

```python
import functools
import math

import jax
import jax.numpy as jnp
import numpy as np
from jax import lax
from jax.experimental import pallas as pl
from jax.experimental.pallas import tpu as pltpu

F32 = jnp.float32
BF16 = jnp.bfloat16

RMS_EPS = 1e-6
GN_EPS = 1e-5
FILTER_EPS = 1e-6
HY_BANDS = 16
GRID_W = 64
ROPE_BASE = 10000.0

V7X_VMEM_LIMIT_BYTES = 58 * 1024 * 1024
V7X_LANES = 128
V7X_SUBLANES = 8


def _cparams(*sem):
    return pltpu.CompilerParams(dimension_semantics=sem, vmem_limit_bytes=V7X_VMEM_LIMIT_BYTES)


def _tile(n, target, unit=V7X_LANES):
    if n <= target:
        return n
    best = unit
    for t in range(unit, target + 1, unit):
        if n % t == 0:
            best = t
    assert n % best == 0, (n, target, unit)
    return best


def _row_tile(m, seq, per_seq_mod, target):
    if per_seq_mod or seq >= target:
        return seq
    return seq * _tile(m // seq, target // seq, unit=1)


def _resident(shape):
    return pl.BlockSpec(shape, lambda *_: (0,) * len(shape), pipeline_mode=pl.Buffered(1))


def _sigmoid(x):
    return 1.0 / (1.0 + jnp.exp(-x))


def _sigmoid_tanh(x):
    return 0.5 * jnp.tanh(0.5 * x) + 0.5


def _rms_scale(x, g):
    ms = jnp.mean(x * x, axis=-1, keepdims=True)
    return (x * lax.rsqrt(ms + RMS_EPS)) * g


def _dot(a, b):
    return jnp.dot(a, b, preferred_element_type=F32)


def _dot_exact(a, b):
    return jnp.dot(a, b, preferred_element_type=F32, precision=lax.Precision.HIGHEST)


def _split_bf16(table):
    hi = table.astype(BF16)
    lo = (table - hi.astype(np.float64)).astype(BF16)
    return jnp.asarray(hi), jnp.asarray(lo)


def _dot_split(a_hi_ref, a_lo_ref, b):
    b_hi = b.astype(BF16)
    b_lo = (b - b_hi.astype(F32)).astype(BF16)
    a_hi = a_hi_ref[...]
    return _dot(a_hi, b_hi) + (_dot(a_lo_ref[...], b_hi) + _dot(a_hi, b_lo))


def _dwconv3_rows(x, w, first, last):
    rows = x.shape[0]
    prev = jnp.where(first, 0.0, pltpu.roll(x, 1, 0))
    nxt = jnp.where(last, 0.0, pltpu.roll(x, rows - 1, 0))
    return prev * w[0:1] + x * w[1:2] + nxt * w[2:3]


def _ada_kernel(cc_ref, w_ref, b_ref, o_ref):
    cc = cc_ref[...]
    s = cc * _sigmoid(cc)
    o_ref[...] = _dot(s.astype(BF16), w_ref[...].astype(BF16)) + b_ref[...]


def _ada_mod(cc, w, b):
    d, n = w.shape
    tn = _tile(n, 1024)
    return pl.pallas_call(
        _ada_kernel,
        grid=(n // tn,),
        in_specs=[pl.BlockSpec((V7X_SUBLANES, d), lambda j: (0, 0)),
                  pl.BlockSpec((d, tn), lambda j: (0, j)),
                  pl.BlockSpec((1, tn), lambda j: (0, j))],
        out_specs=pl.BlockSpec((V7X_SUBLANES, tn), lambda j: (0, j)),
        out_shape=jax.ShapeDtypeStruct((V7X_SUBLANES, n), F32),
        compiler_params=_cparams("parallel"),
        name="ada_mod",
    )(cc, w, b)


def _in_proj_kernel(*refs, seq, n_hy, n_qkv, n_g, n_cast, emit_w):
    x_ref, g_ref, sc_ref, sh_ref, w_ref, cw_ref = refs[:6]
    cast_in = refs[6:6 + n_cast]
    hy_ref, ret_ref = refs[6 + n_cast:8 + n_cast]
    pos = 8 + n_cast
    if emit_w:
        wbf_ref = refs[pos]
        pos += 1
    cast_out = refs[pos:pos + n_cast]
    h_ref = refs[pos + n_cast]
    j = pl.program_id(1)

    @pl.when(j == 0)
    def _():
        n_mod = sc_ref.shape[0]
        rows = x_ref.shape[0] // n_mod
        for b in range(n_mod):
            rs = slice(b * rows, (b + 1) * rows)
            h = _rms_scale(x_ref[rs], g_ref[...] * (1.0 + sc_ref[b])) + sh_ref[b]
            h_ref[rs] = h.astype(BF16)

    def tile():
        w = w_ref[...]
        if emit_w:
            w = w.astype(BF16)
            wbf_ref[...] = w
        for src, dst in zip(cast_in, cast_out):
            dst[...] = src[...].astype(BF16)
        return _dot(h_ref[...], w)

    @pl.when(j < n_hy)
    def _():
        pos = lax.broadcasted_iota(jnp.int32, hy_ref.shape, 0) % seq
        hy_ref[...] = _dwconv3_rows(tile(), cw_ref[...], pos == 0, pos == seq - 1)

    @pl.when((j >= n_hy) & (j < n_hy + n_qkv))
    def _():
        ret_ref[...] = tile().astype(BF16)

    @pl.when((j >= n_hy + n_qkv) & (j < n_hy + n_qkv + n_g))
    def _():
        p = tile()
        ret_ref[...] = (p * _sigmoid_tanh(p)).astype(BF16)

    @pl.when(j >= n_hy + n_qkv + n_g)
    def _():
        ret_ref[...] = _sigmoid_tanh(tile()).astype(BF16)


V7X_BF16_ROW_TILE = 2 * V7X_SUBLANES


def _cast_rows(arr, n_steps):
    rows = arr.shape[0]
    for blk in range(V7X_BF16_ROW_TILE, rows + 1, V7X_BF16_ROW_TILE):
        if rows % blk == 0 and rows // blk <= n_steps:
            return blk
    return None


def _in_proj(x, g, scale, shift, w, short_w, seq, d_hy, d_ret, rows_per_mod, tm, cast=()):
    m, d = x.shape
    n = w.shape[1]
    emit_w = w.dtype != BF16
    if emit_w:
        tm = m
    tn = _tile(math.gcd(3 * d_hy, d_ret, 2 * d), 512 if emit_w else 1024)
    n_hy, n_qkv, n_g = 3 * d_hy // tn, 3 * d_ret // tn, d_ret // tn
    n_col = n // tn
    n_mod = max(1, tm // rows_per_mod)
    mod_spec = pl.BlockSpec((n_mod, 1, d), lambda i, j: ((i * tm) // rows_per_mod // n_mod, 0, 0))
    blks = [_cast_rows(a, (m // tm) * n_col) for a in cast]
    assert all(b is not None for b in blks), [a.shape for a in cast]
    cast_specs = [pl.BlockSpec((b, a.shape[1]), lambda i, j, nb=a.shape[0] // b: (jnp.minimum(i * n_col + j, nb - 1), 0))
                  for a, b in zip(cast, blks)]
    out_specs = [pl.BlockSpec((tm, tn), lambda i, j: (i, jnp.minimum(j, n_hy - 1))),
                 pl.BlockSpec((tm, tn), lambda i, j: (i, jnp.maximum(j - n_hy, 0)))]
    out_shape = [jax.ShapeDtypeStruct((m, 3 * d_hy), F32), jax.ShapeDtypeStruct((m, n - 3 * d_hy), BF16)]
    if emit_w:
        out_specs.append(pl.BlockSpec((d, tn), lambda i, j: (0, j)))
        out_shape.append(jax.ShapeDtypeStruct(w.shape, BF16))
    out_specs += cast_specs
    out_shape += [jax.ShapeDtypeStruct(a.shape, BF16) for a in cast]
    return pl.pallas_call(
        functools.partial(_in_proj_kernel, seq=seq, n_hy=n_hy, n_qkv=n_qkv, n_g=n_g, n_cast=len(cast),
                          emit_w=emit_w),
        grid=(m // tm, n_col),
        in_specs=[pl.BlockSpec((tm, d), lambda i, j: (i, 0), pipeline_mode=pl.Buffered(1) if emit_w else None),
                  pl.BlockSpec((1, d), lambda i, j: (0, 0)),
                  mod_spec, mod_spec,
                  pl.BlockSpec((d, tn), lambda i, j: (0, j)),
                  pl.BlockSpec((3, tn), lambda i, j: (0, jnp.minimum(j, n_hy - 1)))] + cast_specs,
        out_specs=out_specs,
        out_shape=out_shape,
        scratch_shapes=[pltpu.VMEM((tm, d), BF16)],
        compiler_params=_cparams("arbitrary", "arbitrary"),
        name="in_proj",
    )(x, g, scale, shift, w, short_w, *cast)


def _dft_tables(seq):
    n_fft = 2 * seq
    idx = np.arange(seq)
    ang = 2.0 * np.pi * ((idx[:, None] * idx[None, :]) % n_fft) / n_fft
    cos = np.cos(ang)
    msin = -np.sin(ang)
    sign = np.where(idx % 2 == 0, 1.0, -1.0)
    msin[0, :] = sign
    fwd = np.concatenate([cos, msin], axis=0)
    wgt = np.full((seq,), 2.0 / n_fft)
    wgt[0] = 1.0 / n_fft
    inv_re = cos.T * wgt[None, :]
    inv_im = msin.T * wgt[None, :]
    inv_im[:, 0] = sign / n_fft
    inv = np.concatenate([inv_re, inv_im], axis=1)
    return fwd, inv


def _filter_feats(seq):
    n = np.arange(seq, dtype=np.float64)
    t = n / seq
    f = np.linspace(1e-4, HY_BANDS - 1, HY_BANDS)
    w = 2.0 * math.pi * n / seq
    z = np.concatenate([t[:, None], np.cos(w[:, None] * f), np.sin(w[:, None] * f)], axis=-1)
    out = np.zeros((seq, V7X_LANES), np.float32)
    out[:, :z.shape[1]] = z
    return out


def _filter_kernel(z_ref, w1_ref, b1_ref, w2_ref, b2_ref, fr_ref, w3f_ref, w3b_ref, b3f_ref, b3b_ref,
                   decf_ref, decb_ref, bias_ref, chi_ref, clo_ref, shi_ref, slo_ref, kr_ref, kiz_ref, krn_ref, h2_ref):
    @pl.when(pl.program_id(0) == 0)
    def _():
        h1 = jnp.sin(fr_ref[0:1, :] * (_dot_exact(z_ref[...], w1_ref[...]) + b1_ref[...]))
        h2_ref[...] = jnp.sin(fr_ref[1:2, :] * (_dot_exact(h1, w2_ref[...]) + b2_ref[...]))

    h2 = h2_ref[...]
    seq, cw = kr_ref.shape
    t = z_ref[:, 0:1]
    row = lax.broadcasted_iota(jnp.int32, (seq, cw), 0)
    hf = (_dot_exact(h2, w3f_ref[...]) + b3f_ref[...]) * jnp.exp(-t * jnp.abs(decf_ref[...]))
    hb = (_dot_exact(h2, w3b_ref[...]) + b3b_ref[...]) * jnp.exp(-t * jnp.abs(decb_ref[...]))
    hb = jnp.where(row == 0, 0.0, hb)
    norm = (jnp.sum(jnp.abs(hf), axis=0, keepdims=True)
            + jnp.sum(jnp.abs(hb), axis=0, keepdims=True) + FILTER_EPS)
    inv = 1.0 / norm
    even = (hf + hb) * inv
    odd = (hf - hb) * inv
    bias = bias_ref[...]
    kr = _dot_split(chi_ref, clo_ref, even) + bias
    ki = _dot_split(shi_ref, slo_ref, odd)
    nyq = jnp.sum(jnp.where((row & 1) == 0, even, -even), axis=0, keepdims=True) + bias
    kr_ref[...] = kr.astype(BF16)
    kiz_ref[...] = jnp.where(row == 0, 0.0, ki).astype(BF16)
    krn_ref[...] = jnp.where(row == 0, nyq, kr).astype(BF16)


def _hyena_filters(seq, w1, b1, w2, b2, w3, b3, freq, decay, hy_bias, d_hy):
    fh = w1.shape[1]
    pad = V7X_LANES
    w1p = jnp.zeros((pad, pad), F32).at[:w1.shape[0], :fh].set(w1)
    b1p = jnp.zeros((1, pad), F32).at[0, :fh].set(b1)
    w2p = jnp.zeros((pad, pad), F32).at[:fh, :fh].set(w2)
    b2p = jnp.zeros((1, pad), F32).at[0, :fh].set(b2)
    frp = jnp.zeros((2, pad), F32).at[:, :fh].set(freq)
    ncol = w3.shape[1] // 2
    w3p = jnp.zeros((pad, 2 * ncol), F32).at[:fh].set(w3)
    b3r = b3.reshape(1, 2 * ncol)
    dec = decay.reshape(1, 2 * ncol)
    bias = hy_bias.reshape(1, ncol)
    fwd, _ = _dft_tables(seq)
    chi, clo = _split_bf16(fwd[:seq])
    shi, slo = _split_bf16(fwd[seq:])
    z = jnp.asarray(_filter_feats(seq))
    cw = _tile(ncol, 512)
    nb = ncol // cw
    full = lambda shape: pl.BlockSpec(shape, lambda j: (0, 0))
    colf = lambda rows: pl.BlockSpec((rows, cw), lambda j: (0, j))
    colb = lambda rows: pl.BlockSpec((rows, cw), lambda j: (0, nb + j))
    out = jax.ShapeDtypeStruct((seq, ncol), BF16)
    return pl.pallas_call(
        _filter_kernel,
        grid=(nb,),
        in_specs=[full((seq, pad)), full((pad, pad)), full((1, pad)), full((pad, pad)), full((1, pad)),
                  full((2, pad)), colf(pad), colb(pad), colf(1), colb(1), colf(1), colb(1), colf(1),
                  _resident((seq, seq)), _resident((seq, seq)), _resident((seq, seq)), _resident((seq, seq))],
        out_specs=[colf(seq), colf(seq), colf(seq)],
        out_shape=[out, out, out],
        scratch_shapes=[pltpu.VMEM((seq, pad), F32)],
        compiler_params=_cparams("arbitrary"),
        name=f"hy_filter_{seq}",
    )(z, w1p, b1p, w2p, b2p, frp, w3p, w3p, b3r, b3r, dec, dec, bias, chi, clo, shi, slo)


def _hyena_kernel(x1_ref, x2_ref, v_ref, kr0_ref, kiz0_ref, krn0_ref, kr1_ref, kiz1_ref, krn1_ref, f_ref, g_ref,
                  o_ref, *, seq):
    def long_conv(u, kr_ref, kiz_ref, krn_ref):
        spec = _dot(f_ref[...], u.astype(BF16))
        ur = spec[:seq].astype(BF16)
        ui = spec[seq:].astype(BF16)
        kiz = kiz_ref[...]
        yr = ur * kr_ref[...] - ui * kiz
        yi = ur * kiz + ui * krn_ref[...]
        return _dot(g_ref[:, :seq], yr) + _dot(g_ref[:, seq:], yi)

    for s in range(o_ref.shape[0] // seq):
        rs = slice(s * seq, (s + 1) * seq)
        z = x1_ref[rs] * long_conv(v_ref[rs], kr0_ref, kiz0_ref, krn0_ref)
        o_ref[rs] = (x2_ref[rs] * long_conv(z, kr1_ref, kiz1_ref, krn1_ref)).astype(BF16)


def _hyena(proj, batch, seq, tables, d_hy):
    kr, kiz, krn = tables
    cw = _tile(d_hy, 1024 if seq <= 256 else 512)
    nb = d_hy // cw
    fwd, inv = _dft_tables(seq)
    fmat = jnp.asarray(fwd, F32).astype(BF16)
    gmat = jnp.asarray(inv, F32).astype(BF16)
    bs = 2 if batch % 2 == 0 and seq <= 256 else 1
    col = lambda rows, off: pl.BlockSpec((rows, cw), lambda j, b: (0, off * nb + j))
    act = lambda off: pl.BlockSpec((bs * seq, cw), lambda j, b: (b, off * nb + j))
    return pl.pallas_call(
        functools.partial(_hyena_kernel, seq=seq),
        grid=(nb, batch // bs),
        in_specs=[act(0), act(1), act(2),
                  col(seq, 0), col(seq, 0), col(seq, 0), col(seq, 1), col(seq, 1), col(seq, 1),
                  _resident((2 * seq, seq)), _resident((seq, 2 * seq))],
        out_specs=pl.BlockSpec((bs * seq, cw), lambda j, b: (b, j)),
        out_shape=jax.ShapeDtypeStruct((batch * seq, d_hy), BF16),
        compiler_params=_cparams("parallel", "parallel"),
        name=f"hyena_{seq}",
    )(proj, proj, proj, kr, kiz, krn, kr, kiz, krn, fmat, gmat)


def _rope_tables(seq, dk):
    rows = seq // GRID_W
    row = jnp.repeat(jnp.arange(rows), GRID_W).astype(F32)
    col = jnp.tile(jnp.arange(GRID_W), rows).astype(F32)
    nfreq = dk // 4
    inv = ROPE_BASE ** (-jnp.arange(nfreq, dtype=F32) / nfreq)
    ang = jnp.concatenate([row[:, None] * inv, col[:, None] * inv], axis=-1)
    cos, sin = jnp.cos(ang), jnp.sin(ang)
    return jnp.concatenate([cos, cos], axis=-1), jnp.concatenate([-sin, sin], axis=-1)


def _retention_kernel(*refs, seq, hb, dk, use_rope, use_state, want_state):
    refs = list(refs)
    q_ref, k_ref, v_ref, g_ref, lg_ref, gn_ref = refs[:6]
    pos = 6
    if use_rope:
        cos_ref, sin_ref = refs[pos:pos + 2]
        pos += 2
    if use_state:
        s0_ref = refs[pos]
        pos += 1
    o_ref = refs[pos]
    pos += 1
    if want_state:
        st_ref = refs[pos]
        pos += 1
    d_ref = refs[pos]
    if want_state:
        wt_ref = refs[pos + 1]

    def log_gamma(hh, direction):
        return jnp.log(_sigmoid(lg_ref[hh, direction]))[:, 0:1]

    @pl.when(pl.program_id(1) == 0)
    def _():
        i = lax.broadcasted_iota(jnp.int32, (dk, dk), 0)
        j = lax.broadcasted_iota(jnp.int32, (dk, dk), 1)
        diff = (i - j).astype(F32)
        scale = dk ** -0.5
        for hh in range(hb):
            lf = log_gamma(hh, 0)
            lb = log_gamma(hh, 1)
            base_f = scale * jnp.exp(lf * diff)
            base_b = scale * jnp.exp(lb * (-diff))
            diag = jnp.where(diff >= 0, base_f, 0.0) + jnp.where(diff <= 0, base_b, 0.0)
            for bi in range(seq // dk):
                for bj in range(seq // dk):
                    if bi == bj:
                        blk = diag
                    elif bi > bj:
                        blk = base_f * jnp.exp(lf * float(dk * (bi - bj)))
                    else:
                        blk = base_b * jnp.exp(lb * float(dk * (bj - bi)))
                    d_ref[hh, bi * dk:(bi + 1) * dk, bj * dk:(bj + 1) * dk] = blk
            if want_state:
                t = lax.broadcasted_iota(jnp.int32, (seq, dk), 0).astype(F32)
                wt_ref[hh, 0] = scale * jnp.exp(lf * (seq - 1.0 - t))
                wt_ref[hh, 1] = scale * jnp.exp(lb * t)

    pos_f = lax.broadcasted_iota(jnp.int32, (seq, dk), 0).astype(F32)
    for hh in range(hb):
        sl = slice(hh * dk, (hh + 1) * dk)
        qb = q_ref[:, sl]
        kb16 = k_ref[:, sl]
        vb = v_ref[:, sl]
        if use_rope:
            cos = cos_ref[...]
            sin = sin_ref[...]
            q = qb.astype(F32)
            k = kb16.astype(F32)
            qb = (q * cos + pltpu.roll(q, dk // 2, 1) * sin).astype(BF16)
            kb16 = (k * cos + pltpu.roll(k, dk // 2, 1) * sin).astype(BF16)
        s = lax.dot_general(qb, kb16, (((1,), (1,)), ((), ())), preferred_element_type=F32)
        o = _dot((s * d_ref[hh]).astype(BF16), vb)
        lf = log_gamma(hh, 0)
        lb = log_gamma(hh, 1)
        if use_state:
            o = o + _dot(qb, s0_ref[0, hh].astype(BF16)) * jnp.exp(lf * (pos_f + 1.0))
            o = o + _dot(qb, s0_ref[1, hh].astype(BF16)) * jnp.exp(lb * (seq - pos_f))
        if want_state:
            k = kb16.astype(F32)
            kf = (k * wt_ref[hh, 0]).astype(BF16)
            kb = (k * wt_ref[hh, 1]).astype(BF16)
            tn = (((0,), (0,)), ((), ()))
            st_ref[0, hh] = lax.dot_general(kf, vb, tn, preferred_element_type=F32)
            st_ref[1, hh] = lax.dot_general(kb, vb, tn, preferred_element_type=F32)
        mu = jnp.mean(o, axis=-1, keepdims=True)
        oc = o - mu
        var = jnp.mean(oc * oc, axis=-1, keepdims=True)
        y = (oc * lax.rsqrt(var + GN_EPS)) * gn_ref[:, sl] * g_ref[:, sl].astype(F32)
        o_ref[:, sl] = y.astype(BF16)


def _retention(proj, batch, seq, n_heads, dk, decay_logit, ret_gn, rope, state0, want_state):
    d_ret = n_heads * dk
    hb = n_heads if seq <= 256 else min(n_heads, 2)
    bw = hb * dk
    nhb = n_heads // hb
    lg = jnp.broadcast_to(decay_logit.T[:, :, None, None], (n_heads, 2, 1, V7X_LANES))
    act = lambda part: pl.BlockSpec((seq, bw), lambda h, b: (b, (part * d_ret) // bw + h))
    in_specs = [act(0), act(1), act(2), act(3),
                pl.BlockSpec((hb, 2, 1, V7X_LANES), lambda h, b: (h, 0, 0, 0)),
                pl.BlockSpec((1, bw), lambda h, b: (0, h))]
    args = [proj, proj, proj, proj, lg, ret_gn.reshape(1, d_ret)]
    if rope is not None:
        in_specs += [pl.BlockSpec((seq, dk), lambda h, b: (0, 0))] * 2
        args += list(rope)
    if state0 is not None:
        in_specs.append(pl.BlockSpec((None, 2, hb, dk, dk), lambda h, b: (b, 0, h, 0, 0)))
        args.append(state0)
    out_specs = [pl.BlockSpec((seq, bw), lambda h, b: (b, h))]
    out_shape = [jax.ShapeDtypeStruct((batch * seq, d_ret), BF16)]
    if want_state:
        out_specs.append(pl.BlockSpec((None, 2, hb, dk, dk), lambda h, b: (b, 0, h, 0, 0)))
        out_shape.append(jax.ShapeDtypeStruct((batch, 2, n_heads, dk, dk), F32))
    body = functools.partial(_retention_kernel, seq=seq, hb=hb, dk=dk, use_rope=rope is not None,
                             use_state=state0 is not None, want_state=want_state)
    return pl.pallas_call(
        body,
        grid=(nhb, batch),
        in_specs=in_specs,
        out_specs=out_specs,
        out_shape=out_shape,
        scratch_shapes=[pltpu.VMEM((hb, seq, seq), F32)]
        + ([pltpu.VMEM((hb, 2, seq, dk), F32)] if want_state else []),
        compiler_params=_cparams("parallel", "arbitrary"),
        name=f"retention_{seq}",
    )(*args)


def _merge_kernel(*refs, nblk):
    gate_refs = refs[:2 * nblk]
    (yhy_ref, yret_ref, x_ref, wbh_ref, wbr_ref, wo_ref, gpost_ref, gm_ref, gpre_ref, sc_ref, sh_ref,
     o_ref, h_ref) = refs[2 * nblk:]
    a = _dot(yhy_ref[...], wbh_ref[...])
    b = _dot(yret_ref[...], wbr_ref[...])
    wblk = gate_refs[0].shape[1]
    parts = []
    for kk in range(nblk):
        sl = slice(kk * wblk, (kk + 1) * wblk)
        g_hy = gate_refs[kk][...].astype(F32)
        g_ret = gate_refs[nblk + kk][...].astype(F32)
        parts.append((g_hy * a[:, sl] + g_ret * b[:, sl]).astype(BF16))
    merged = parts[0] if nblk == 1 else jnp.concatenate(parts, axis=1)
    half = merged.shape[0] // 2
    for c in range(2):
        rs = slice(c * half, (c + 1) * half)
        out = _dot(merged[rs], wo_ref[...])
        x1 = x_ref[rs] + _rms_scale(out, gm_ref[...] * gpost_ref[...])
        o_ref[rs] = x1
        h_ref[rs] = (_rms_scale(x1, gpre_ref[...] * (1.0 + sc_ref[...])) + sh_ref[...]).astype(BF16)


def _merge(proj, col0, y_hy, y_ret, x, w_br_hy, w_br_ret, w_out, g_post, gate_m, g_pre_f, scale_f, shift_f,
           rows_per_mod, tm):
    m, d = x.shape
    vec = pl.BlockSpec((1, d), lambda i: (0, 0))
    mod_spec = pl.BlockSpec((None, 1, d), lambda i: ((i * tm) // rows_per_mod, 0, 0))
    wblk = math.gcd(col0, d)
    nblk = d // wblk
    gate_spec = lambda kk: pl.BlockSpec((tm, wblk), lambda i: (i, col0 // wblk + kk))
    const = lambda arr: _resident(arr.shape)
    row = lambda width: pl.BlockSpec((tm, width), lambda i: (i, 0))
    return pl.pallas_call(
        functools.partial(_merge_kernel, nblk=nblk),
        grid=(m // tm,),
        in_specs=[gate_spec(kk) for kk in range(2 * nblk)]
        + [row(y_hy.shape[1]), row(y_ret.shape[1]), row(d), const(w_br_hy), const(w_br_ret), const(w_out),
           vec, mod_spec, vec, mod_spec, mod_spec],
        out_specs=[row(d), row(d)],
        out_shape=[jax.ShapeDtypeStruct((m, d), F32), jax.ShapeDtypeStruct((m, d), BF16)],
        compiler_params=_cparams("parallel"),
        name="merge",
    )(*([proj] * (2 * nblk)), y_hy, y_ret, x, w_br_hy, w_br_ret, w_out, g_post, gate_m, g_pre_f, scale_f, shift_f)


def _ffn_kernel(x_ref, h_ref, gate_ref, gpost_ref, wa_ref, wb_ref, ca_ref, cb_ref, wd_ref, o_ref, act_ref, *, seq):
    j = pl.program_id(1)
    nf = pl.num_programs(1) - 1
    tm = x_ref.shape[0]
    tf = wa_ref.shape[1]

    def up(slot):
        pos = lax.broadcasted_iota(jnp.int32, (tm, tf), 0) % seq
        first = pos == 0
        last = pos == seq - 1
        h = h_ref[...]
        a = _dwconv3_rows(_dot(h, wa_ref[...]), ca_ref[...], first, last)
        b = _dwconv3_rows(_dot(h, wb_ref[...]), cb_ref[...], first, last)
        c1 = math.sqrt(2.0 / math.pi)
        half = 0.5 * a
        gelu = half + half * jnp.tanh(a * (c1 + (c1 * 0.044715) * (a * a)))
        act_ref[slot] = (gelu * b).astype(BF16)

    def down(slot):
        return _dot(act_ref[slot], wd_ref[...])

    @pl.when(j == 0)
    def _():
        up(0)

    @pl.when(j == 1)
    def _():
        o_ref[...] = down(0)
        up(1)

    @pl.when((j > 1) & (j < nf))
    def _():
        slot = j % 2
        o_ref[...] += down(1 - slot)
        up(slot)

    @pl.when(j == nf)
    def _():
        f = o_ref[...] + down((nf - 1) % 2)
        o_ref[...] = x_ref[...] + _rms_scale(f, gate_ref[...] * gpost_ref[...])


def _ffn(x, h, seq, gate, g_post, w_up, conv_w, w_down, rows_per_mod, tm):
    m, d = x.shape
    d_ff = w_down.shape[0]
    tf = _tile(d_ff, 512)
    nf = d_ff // tf
    assert nf >= 2, (d_ff, tf)
    once = pl.BlockSpec((tm, d), lambda i, j: (i, 0), pipeline_mode=pl.Buffered(1))
    up_blk = lambda j: jnp.minimum(j, nf - 1)
    down_blk = lambda j: jnp.maximum(j - 1, 0)
    return pl.pallas_call(
        functools.partial(_ffn_kernel, seq=seq),
        grid=(m // tm, nf + 1),
        in_specs=[once, once,
                  pl.BlockSpec((None, 1, d), lambda i, j: ((i * tm) // rows_per_mod, 0, 0)),
                  pl.BlockSpec((1, d), lambda i, j: (0, 0)),
                  pl.BlockSpec((d, tf), lambda i, j: (0, up_blk(j))),
                  pl.BlockSpec((d, tf), lambda i, j: (0, nf + up_blk(j))),
                  pl.BlockSpec((3, tf), lambda i, j: (0, up_blk(j))),
                  pl.BlockSpec((3, tf), lambda i, j: (0, nf + up_blk(j))),
                  pl.BlockSpec((tf, d), lambda i, j: (down_blk(j), 0))],
        out_specs=pl.BlockSpec((tm, d), lambda i, j: (i, 0)),
        out_shape=jax.ShapeDtypeStruct((m, d), F32),
        scratch_shapes=[pltpu.VMEM((2, tm, tf), BF16)],
        compiler_params=_cparams("parallel", "arbitrary"),
        name="ffn",
    )(x, h, gate, g_post, w_up, w_up, conv_w, conv_w, w_down)


class _Group:
    def __init__(self, x3, mod):
        self.batch, self.seq, self.d = x3.shape
        self.m = self.batch * self.seq
        self.x = x3.reshape(self.m, self.d)
        per_seq_mod = mod.shape[0] != 1
        self.rows_per_mod = self.seq if per_seq_mod else self.m
        self.tm = _row_tile(self.m, self.seq, per_seq_mod, 1024)
        self.tm_merge = _tile(self.tm, 512, unit=V7X_SUBLANES)
        (self.shift_m, self.scale_m, self.gate_m,
         self.shift_f, self.scale_f, self.gate_f) = (mod[:, i][:, None, :] for i in range(6))

    def in_proj(self, p, w_in, cast=()):
        return _in_proj(self.x, p["g_pre_m"], self.scale_m, self.shift_m, w_in, p["hy_short_w"], self.seq,
                        p["d_hy"], p["d_ret"], self.rows_per_mod, self.tm, cast)


def _mix_and_ffn(grp, hy_in, ret_in, rope, state0, want_state, p):
    d_hy, d_ret, n_heads, dk = p["d_hy"], p["d_ret"], p["n_heads"], p["dk"]
    tables = _hyena_filters(grp.seq, p["hy_w1"], p["hy_b1"], p["hy_w2"], p["hy_b2"], p["hy_w3"], p["hy_b3"],
                            p["hy_freq"], p["hy_decay"], p["hy_bias"], d_hy)
    y_hy = _hyena(hy_in, grp.batch, grp.seq, tables, d_hy)
    ret = _retention(ret_in, grp.batch, grp.seq, n_heads, dk, p["ret_decay_logit"], p["ret_gn"], rope, state0,
                     want_state)
    x, h_ffn = _merge(ret_in, 4 * d_ret, y_hy, ret[0], grp.x, p["w_br_hy"], p["w_br_ret"], p["w_out"],
                      p["g_post_m"], grp.gate_m, p["g_pre_f"], grp.scale_f, grp.shift_f, grp.rows_per_mod,
                      grp.tm_merge)
    x = _ffn(x, h_ffn, grp.seq, grp.gate_f, p["g_post_f"], p["ffn_w_up"], p["ffn_conv"], p["ffn_w_down"],
             grp.rows_per_mod, grp.tm)
    return x.reshape(grp.batch, grp.seq, grp.d), (ret[1] if want_state else None)


def kernel(x_prompt, x_sample, state_ret, c, c_ctx, w_ada, b_ada, norm_pre_mix, norm_post_mix, norm_pre_ffn,
           norm_post_ffn, w_in, hy_short_w, hy_w1, hy_b1, hy_w2, hy_b2, hy_w3, hy_b3, hy_freq, hy_decay, hy_bias,
           ret_decay_logit, ret_gn, w_br_hy, w_br_ret, w_out, ffn_w_up, ffn_conv, ffn_w_down):
    depth = w_in.shape[0]
    d = x_prompt.shape[-1]
    n_dec = x_sample.shape[0]
    n_heads, dk = state_ret.shape[3], state_ret.shape[4]
    d_hy = hy_bias.shape[-1]
    cc = jnp.zeros((V7X_SUBLANES, d), F32).at[0].set(c_ctx).at[1:1 + n_dec].set(c)
    rope = _rope_tables(x_sample.shape[1], dk)
    x_p, x_s = x_prompt, x_sample
    states = []
    for l in range(depth):
        p = dict(
            d_hy=d_hy, d_ret=n_heads * dk, n_heads=n_heads, dk=dk,
            g_pre_m=norm_pre_mix[l][None], g_post_m=norm_post_mix[l][None],
            g_pre_f=norm_pre_ffn[l][None], g_post_f=norm_post_ffn[l][None],
            hy_short_w=hy_short_w[l],
            hy_w1=hy_w1[l], hy_b1=hy_b1[l], hy_w2=hy_w2[l], hy_b2=hy_b2[l], hy_w3=hy_w3[l], hy_b3=hy_b3[l],
            hy_freq=hy_freq[l], hy_decay=hy_decay[l], hy_bias=hy_bias[l],
            ret_decay_logit=ret_decay_logit[l], ret_gn=ret_gn[l], ffn_conv=ffn_conv[l],
        )
        mod = _ada_mod(cc, w_ada[l], b_ada[l][None]).reshape(V7X_SUBLANES, 6, d)
        ctx = _Group(x_p, mod[0:1])
        lat = _Group(x_s, mod[1:1 + n_dec])
        hy_s, ret_s, w_in_bf = lat.in_proj(p, w_in[l])
        hy_p, ret_p, *rest = ctx.in_proj(p, w_in_bf, (ffn_w_up[l], ffn_w_down[l], w_out[l], w_br_hy[l], w_br_ret[l]))
        p["ffn_w_up"], p["ffn_w_down"], p["w_out"], p["w_br_hy"], p["w_br_ret"] = rest
        x_p, st = _mix_and_ffn(ctx, hy_p, ret_p, None, None, True, p)
        x_s, _ = _mix_and_ffn(lat, hy_s, ret_s, rope, state_ret[:, l], False, p)
        states.append(st)
    return x_p, x_s, jnp.stack(states, axis=1)
```

```python
import functools
import math

import jax
import jax.numpy as jnp
import numpy as np
from jax import lax
from jax.experimental import pallas as pl
from jax.experimental.pallas import tpu as pltpu

F32 = jnp.float32
BF16 = jnp.bfloat16

RMS_EPS = 1e-6
GN_EPS = 1e-5
FILTER_EPS = 1e-6
HY_BANDS = 16
GRID_W = 64
ROPE_BASE = 10000.0

V7X_VMEM_LIMIT_BYTES = 58 * 1024 * 1024
V7X_LANES = 128
V7X_SUBLANES = 8


def _cparams(*sem):
    return pltpu.CompilerParams(dimension_semantics=sem, vmem_limit_bytes=V7X_VMEM_LIMIT_BYTES)


def _tile(n, target, unit=V7X_LANES):
    if n <= target:
        return n
    best = unit
    for t in range(unit, target + 1, unit):
        if n % t == 0:
            best = t
    assert n % best == 0, (n, target, unit)
    return best


def _row_tile(m, seq, per_seq_mod, target):
    if per_seq_mod or seq >= target:
        return seq
    return seq * _tile(m // seq, target // seq, unit=1)


def _resident(shape):
    return pl.BlockSpec(shape, lambda *_: (0,) * len(shape), pipeline_mode=pl.Buffered(1))


def _sigmoid(x):
    return 1.0 / (1.0 + jnp.exp(-x))


def _sigmoid_tanh(x):
    return 0.5 * jnp.tanh(0.5 * x) + 0.5


def _rms_scale(x, g):
    ms = jnp.mean(x * x, axis=-1, keepdims=True)
    return (x * lax.rsqrt(ms + RMS_EPS)) * g


def _dot(a, b):
    return jnp.dot(a, b, preferred_element_type=F32)


def _dot_exact(a, b):
    a_hi = a.astype(BF16)
    a_lo = (a - a_hi.astype(F32)).astype(BF16)
    b_hi = b.astype(BF16)
    b_lo = (b - b_hi.astype(F32)).astype(BF16)
    return _dot(a_hi, b_hi) + (_dot(a_lo, b_hi) + _dot(a_hi, b_lo))


def _split_bf16(table):
    hi = table.astype(BF16)
    lo = (table - hi.astype(np.float64)).astype(BF16)
    return jnp.asarray(hi), jnp.asarray(lo)


def _dot_split(a_hi_ref, a_lo_ref, b):
    b_hi = b.astype(BF16)
    b_lo = (b - b_hi.astype(F32)).astype(BF16)
    a_hi = a_hi_ref[...]
    return _dot(a_hi, b_hi) + (_dot(a_lo_ref[...], b_hi) + _dot(a_hi, b_lo))


def _dwconv3_rows(x, w, first, last):
    rows = x.shape[0]
    prev = jnp.where(first, 0.0, pltpu.roll(x, 1, 0))
    nxt = jnp.where(last, 0.0, pltpu.roll(x, rows - 1, 0))
    return prev * w[0:1] + x * w[1:2] + nxt * w[2:3]


def _ada_kernel(cc_ref, w_ref, b_ref, o_ref):
    cc = cc_ref[...]
    s = cc * _sigmoid(cc)
    o_ref[...] = _dot(s.astype(BF16), w_ref[...].astype(BF16)) + b_ref[...]


def _ada_mod(cc, w, b):
    d, n = w.shape
    tn = _tile(n, 1024)
    return pl.pallas_call(
        _ada_kernel,
        grid=(n // tn,),
        in_specs=[pl.BlockSpec((V7X_SUBLANES, d), lambda j: (0, 0)),
                  pl.BlockSpec((d, tn), lambda j: (0, j)),
                  pl.BlockSpec((1, tn), lambda j: (0, j))],
        out_specs=pl.BlockSpec((V7X_SUBLANES, tn), lambda j: (0, j)),
        out_shape=jax.ShapeDtypeStruct((V7X_SUBLANES, n), F32),
        compiler_params=_cparams("parallel"),
        name="ada_mod",
    )(cc, w, b)


def _in_proj_kernel(*refs, seq, n_hy, n_qkv, n_g, n_cast, emit_w):
    x_ref, g_ref, sc_ref, sh_ref, w_ref, cw_ref = refs[:6]
    cast_in = refs[6:6 + n_cast]
    hy_ref, ret_ref = refs[6 + n_cast:8 + n_cast]
    pos = 8 + n_cast
    if emit_w:
        wbf_ref = refs[pos]
        pos += 1
    cast_out = refs[pos:pos + n_cast]
    h_ref = refs[pos + n_cast]
    j = pl.program_id(1)

    @pl.when(j == 0)
    def _():
        n_mod = sc_ref.shape[0]
        rows = x_ref.shape[0] // n_mod
        for b in range(n_mod):
            rs = slice(b * rows, (b + 1) * rows)
            h = _rms_scale(x_ref[rs], g_ref[...] * (1.0 + sc_ref[b])) + sh_ref[b]
            h_ref[rs] = h.astype(BF16)

    def tile():
        w = w_ref[...]
        if emit_w:
            w = w.astype(BF16)
            wbf_ref[...] = w
        for src, dst in zip(cast_in, cast_out):
            dst[...] = src[...].astype(BF16)
        return _dot(h_ref[...], w)

    @pl.when(j < n_hy)
    def _():
        pos = lax.broadcasted_iota(jnp.int32, hy_ref.shape, 0) % seq
        hy_ref[...] = _dwconv3_rows(tile(), cw_ref[...], pos == 0, pos == seq - 1)

    @pl.when((j >= n_hy) & (j < n_hy + n_qkv))
    def _():
        ret_ref[...] = tile().astype(BF16)

    @pl.when((j >= n_hy + n_qkv) & (j < n_hy + n_qkv + n_g))
    def _():
        p = tile()
        ret_ref[...] = (p * _sigmoid_tanh(p)).astype(BF16)

    @pl.when(j >= n_hy + n_qkv + n_g)
    def _():
        ret_ref[...] = _sigmoid_tanh(tile()).astype(BF16)


V7X_BF16_ROW_TILE = 2 * V7X_SUBLANES


def _cast_rows(arr, n_steps):
    rows = arr.shape[0]
    for blk in range(V7X_BF16_ROW_TILE, rows + 1, V7X_BF16_ROW_TILE):
        if rows % blk == 0 and rows // blk <= n_steps:
            return blk
    return None


def _in_proj(x, g, scale, shift, w, short_w, seq, d_hy, d_ret, rows_per_mod, tm, cast=()):
    m, d = x.shape
    n = w.shape[1]
    emit_w = w.dtype != BF16
    if emit_w:
        tm = m
    tn = _tile(math.gcd(3 * d_hy, d_ret, 2 * d), 512 if emit_w else 1024)
    n_hy, n_qkv, n_g = 3 * d_hy // tn, 3 * d_ret // tn, d_ret // tn
    n_col = n // tn
    n_mod = max(1, tm // rows_per_mod)
    mod_spec = pl.BlockSpec((n_mod, 1, d), lambda i, j: ((i * tm) // rows_per_mod // n_mod, 0, 0))
    blks = [_cast_rows(a, (m // tm) * n_col) for a in cast]
    assert all(b is not None for b in blks), [a.shape for a in cast]
    cast_specs = [pl.BlockSpec((b, a.shape[1]), lambda i, j, nb=a.shape[0] // b: (jnp.minimum(i * n_col + j, nb - 1), 0))
                  for a, b in zip(cast, blks)]
    out_specs = [pl.BlockSpec((tm, tn), lambda i, j: (i, jnp.minimum(j, n_hy - 1))),
                 pl.BlockSpec((tm, tn), lambda i, j: (i, jnp.maximum(j - n_hy, 0)))]
    out_shape = [jax.ShapeDtypeStruct((m, 3 * d_hy), F32), jax.ShapeDtypeStruct((m, n - 3 * d_hy), BF16)]
    if emit_w:
        out_specs.append(pl.BlockSpec((d, tn), lambda i, j: (0, j)))
        out_shape.append(jax.ShapeDtypeStruct(w.shape, BF16))
    out_specs += cast_specs
    out_shape += [jax.ShapeDtypeStruct(a.shape, BF16) for a in cast]
    return pl.pallas_call(
        functools.partial(_in_proj_kernel, seq=seq, n_hy=n_hy, n_qkv=n_qkv, n_g=n_g, n_cast=len(cast),
                          emit_w=emit_w),
        grid=(m // tm, n_col),
        in_specs=[pl.BlockSpec((tm, d), lambda i, j: (i, 0), pipeline_mode=pl.Buffered(1) if emit_w else None),
                  pl.BlockSpec((1, d), lambda i, j: (0, 0)),
                  mod_spec, mod_spec,
                  pl.BlockSpec((d, tn), lambda i, j: (0, j)),
                  pl.BlockSpec((3, tn), lambda i, j: (0, jnp.minimum(j, n_hy - 1)))] + cast_specs,
        out_specs=out_specs,
        out_shape=out_shape,
        scratch_shapes=[pltpu.VMEM((tm, d), BF16)],
        compiler_params=_cparams("arbitrary", "arbitrary"),
        name="in_proj",
    )(x, g, scale, shift, w, short_w, *cast)


def _dft_tables(seq):
    n_fft = 2 * seq
    idx = np.arange(seq)
    ang = 2.0 * np.pi * ((idx[:, None] * idx[None, :]) % n_fft) / n_fft
    cos = np.cos(ang)
    msin = -np.sin(ang)
    sign = np.where(idx % 2 == 0, 1.0, -1.0)
    msin[0, :] = sign
    fwd = np.concatenate([cos, msin], axis=0)
    wgt = np.full((seq,), 2.0 / n_fft)
    wgt[0] = 1.0 / n_fft
    inv_re = cos.T * wgt[None, :]
    inv_im = msin.T * wgt[None, :]
    inv_im[:, 0] = sign / n_fft
    inv = np.concatenate([inv_re, inv_im], axis=1)
    return fwd, inv


def _filter_feats(seq):
    n = np.arange(seq, dtype=np.float64)
    t = n / seq
    f = np.linspace(1e-4, HY_BANDS - 1, HY_BANDS)
    w = 2.0 * math.pi * n / seq
    z = np.concatenate([t[:, None], np.cos(w[:, None] * f), np.sin(w[:, None] * f)], axis=-1)
    out = np.zeros((seq, V7X_LANES), np.float32)
    out[:, :z.shape[1]] = z
    return out


def _filter_kernel(z_ref, w1_ref, b1_ref, w2_ref, b2_ref, fr_ref, w3f_ref, w3b_ref, b3f_ref, b3b_ref,
                   decf_ref, decb_ref, bias_ref, chi_ref, clo_ref, shi_ref, slo_ref, kr_ref, kiz_ref, krn_ref, h2_ref):
    @pl.when(pl.program_id(0) == 0)
    def _():
        h1 = jnp.sin(fr_ref[0:1, :] * (_dot_exact(z_ref[...], w1_ref[...]) + b1_ref[...]))
        h2_ref[...] = jnp.sin(fr_ref[1:2, :] * (_dot_exact(h1, w2_ref[...]) + b2_ref[...]))

    h2 = h2_ref[...]
    seq, cw = kr_ref.shape
    t = z_ref[:, 0:1]
    row = lax.broadcasted_iota(jnp.int32, (seq, cw), 0)
    hf = (_dot_exact(h2, w3f_ref[...]) + b3f_ref[...]) * jnp.exp(-t * jnp.abs(decf_ref[...]))
    hb = (_dot_exact(h2, w3b_ref[...]) + b3b_ref[...]) * jnp.exp(-t * jnp.abs(decb_ref[...]))
    hb = jnp.where(row == 0, 0.0, hb)
    norm = (jnp.sum(jnp.abs(hf), axis=0, keepdims=True)
            + jnp.sum(jnp.abs(hb), axis=0, keepdims=True) + FILTER_EPS)
    inv = 1.0 / norm
    even = (hf + hb) * inv
    odd = (hf - hb) * inv
    bias = bias_ref[...]
    kr = _dot_split(chi_ref, clo_ref, even) + bias
    ki = _dot_split(shi_ref, slo_ref, odd)
    nyq = jnp.sum(jnp.where((row & 1) == 0, even, -even), axis=0, keepdims=True) + bias
    kr_ref[...] = kr.astype(BF16)
    kiz_ref[...] = jnp.where(row == 0, 0.0, ki).astype(BF16)
    krn_ref[...] = jnp.where(row == 0, nyq, kr).astype(BF16)


def _hyena_filters(seq, w1, b1, w2, b2, w3, b3, freq, decay, hy_bias, d_hy):
    fh = w1.shape[1]
    pad = V7X_LANES
    w1p = jnp.zeros((pad, pad), F32).at[:w1.shape[0], :fh].set(w1)
    b1p = jnp.zeros((1, pad), F32).at[0, :fh].set(b1)
    w2p = jnp.zeros((pad, pad), F32).at[:fh, :fh].set(w2)
    b2p = jnp.zeros((1, pad), F32).at[0, :fh].set(b2)
    frp = jnp.zeros((2, pad), F32).at[:, :fh].set(freq)
    ncol = w3.shape[1] // 2
    w3p = jnp.zeros((pad, 2 * ncol), F32).at[:fh].set(w3)
    b3r = b3.reshape(1, 2 * ncol)
    dec = decay.reshape(1, 2 * ncol)
    bias = hy_bias.reshape(1, ncol)
    fwd, _ = _dft_tables(seq)
    chi, clo = _split_bf16(fwd[:seq])
    shi, slo = _split_bf16(fwd[seq:])
    z = jnp.asarray(_filter_feats(seq))
    cw = _tile(ncol, 512)
    nb = ncol // cw
    full = lambda shape: pl.BlockSpec(shape, lambda j: (0, 0))
    colf = lambda rows: pl.BlockSpec((rows, cw), lambda j: (0, j))
    colb = lambda rows: pl.BlockSpec((rows, cw), lambda j: (0, nb + j))
    out = jax.ShapeDtypeStruct((seq, ncol), BF16)
    return pl.pallas_call(
        _filter_kernel,
        grid=(nb,),
        in_specs=[full((seq, pad)), full((pad, pad)), full((1, pad)), full((pad, pad)), full((1, pad)),
                  full((2, pad)), colf(pad), colb(pad), colf(1), colb(1), colf(1), colb(1), colf(1),
                  _resident((seq, seq)), _resident((seq, seq)), _resident((seq, seq)), _resident((seq, seq))],
        out_specs=[colf(seq), colf(seq), colf(seq)],
        out_shape=[out, out, out],
        scratch_shapes=[pltpu.VMEM((seq, pad), F32)],
        compiler_params=_cparams("arbitrary"),
        name=f"hy_filter_{seq}",
    )(z, w1p, b1p, w2p, b2p, frp, w3p, w3p, b3r, b3r, dec, dec, bias, chi, clo, shi, slo)


def _hyena_kernel(x1_ref, x2_ref, v_ref, kr0_ref, kiz0_ref, krn0_ref, kr1_ref, kiz1_ref, krn1_ref, f_ref, g_ref,
                  o_ref, *, seq):
    def long_conv(u, kr_ref, kiz_ref, krn_ref):
        spec = _dot(f_ref[...], u.astype(BF16))
        ur = spec[:seq].astype(BF16)
        ui = spec[seq:].astype(BF16)
        kiz = kiz_ref[...]
        yr = ur * kr_ref[...] - ui * kiz
        yi = ur * kiz + ui * krn_ref[...]
        return _dot(g_ref[:, :seq], yr) + _dot(g_ref[:, seq:], yi)

    for s in range(o_ref.shape[0] // seq):
        rs = slice(s * seq, (s + 1) * seq)
        z = x1_ref[rs] * long_conv(v_ref[rs], kr0_ref, kiz0_ref, krn0_ref)
        o_ref[rs] = (x2_ref[rs] * long_conv(z, kr1_ref, kiz1_ref, krn1_ref)).astype(BF16)


def _hyena(proj, batch, seq, tables, d_hy):
    kr, kiz, krn = tables
    cw = _tile(d_hy, 1024 if seq <= 256 else 512)
    nb = d_hy // cw
    fwd, inv = _dft_tables(seq)
    fmat = jnp.asarray(fwd, F32).astype(BF16)
    gmat = jnp.asarray(inv, F32).astype(BF16)
    bs = 2 if batch % 2 == 0 and seq <= 256 else 1
    col = lambda rows, off: pl.BlockSpec((rows, cw), lambda j, b: (0, off * nb + j))
    act = lambda off: pl.BlockSpec((bs * seq, cw), lambda j, b: (b, off * nb + j))
    return pl.pallas_call(
        functools.partial(_hyena_kernel, seq=seq),
        grid=(nb, batch // bs),
        in_specs=[act(0), act(1), act(2),
                  col(seq, 0), col(seq, 0), col(seq, 0), col(seq, 1), col(seq, 1), col(seq, 1),
                  _resident((2 * seq, seq)), _resident((seq, 2 * seq))],
        out_specs=pl.BlockSpec((bs * seq, cw), lambda j, b: (b, j)),
        out_shape=jax.ShapeDtypeStruct((batch * seq, d_hy), BF16),
        compiler_params=_cparams("parallel", "parallel"),
        name=f"hyena_{seq}",
    )(proj, proj, proj, kr, kiz, krn, kr, kiz, krn, fmat, gmat)


def _rope_tables(seq, dk):
    rows = seq // GRID_W
    row = jnp.repeat(jnp.arange(rows), GRID_W).astype(F32)
    col = jnp.tile(jnp.arange(GRID_W), rows).astype(F32)
    nfreq = dk // 4
    inv = ROPE_BASE ** (-jnp.arange(nfreq, dtype=F32) / nfreq)
    ang = jnp.concatenate([row[:, None] * inv, col[:, None] * inv], axis=-1)
    cos, sin = jnp.cos(ang), jnp.sin(ang)
    return jnp.concatenate([cos, cos], axis=-1), jnp.concatenate([-sin, sin], axis=-1)


def _retention_kernel(*refs, seq, hb, dk, use_rope, use_state, want_state):
    refs = list(refs)
    q_ref, k_ref, v_ref, g_ref, lg_ref, gn_ref = refs[:6]
    pos = 6
    if use_rope:
        cos_ref, sin_ref = refs[pos:pos + 2]
        pos += 2
    if use_state:
        s0_ref = refs[pos]
        pos += 1
    o_ref = refs[pos]
    pos += 1
    if want_state:
        st_ref = refs[pos]
        pos += 1
    d_ref = refs[pos]
    if want_state:
        wt_ref = refs[pos + 1]

    def log_gamma(hh, direction):
        return jnp.log(_sigmoid(lg_ref[hh, direction]))[:, 0:1]

    @pl.when(pl.program_id(1) == 0)
    def _():
        i = lax.broadcasted_iota(jnp.int32, (dk, dk), 0)
        j = lax.broadcasted_iota(jnp.int32, (dk, dk), 1)
        diff = (i - j).astype(F32)
        scale = dk ** -0.5
        for hh in range(hb):
            lf = log_gamma(hh, 0)
            lb = log_gamma(hh, 1)
            base_f = scale * jnp.exp(lf * diff)
            base_b = scale * jnp.exp(lb * (-diff))
            diag = jnp.where(diff >= 0, base_f, 0.0) + jnp.where(diff <= 0, base_b, 0.0)
            for bi in range(seq // dk):
                for bj in range(seq // dk):
                    if bi == bj:
                        blk = diag
                    elif bi > bj:
                        blk = base_f * jnp.exp(lf * float(dk * (bi - bj)))
                    else:
                        blk = base_b * jnp.exp(lb * float(dk * (bj - bi)))
                    d_ref[hh, bi * dk:(bi + 1) * dk, bj * dk:(bj + 1) * dk] = blk
            if want_state:
                t = lax.broadcasted_iota(jnp.int32, (seq, dk), 0).astype(F32)
                wt_ref[hh, 0] = scale * jnp.exp(lf * (seq - 1.0 - t))
                wt_ref[hh, 1] = scale * jnp.exp(lb * t)

    pos_f = lax.broadcasted_iota(jnp.int32, (seq, dk), 0).astype(F32)
    for hh in range(hb):
        sl = slice(hh * dk, (hh + 1) * dk)
        qb = q_ref[:, sl]
        kb16 = k_ref[:, sl]
        vb = v_ref[:, sl]
        if use_rope:
            cos = cos_ref[...]
            sin = sin_ref[...]
            q = qb.astype(F32)
            k = kb16.astype(F32)
            qb = (q * cos + pltpu.roll(q, dk // 2, 1) * sin).astype(BF16)
            kb16 = (k * cos + pltpu.roll(k, dk // 2, 1) * sin).astype(BF16)
        s = lax.dot_general(qb, kb16, (((1,), (1,)), ((), ())), preferred_element_type=F32)
        o = _dot((s * d_ref[hh]).astype(BF16), vb)
        lf = log_gamma(hh, 0)
        lb = log_gamma(hh, 1)
        if use_state:
            o = o + _dot(qb, s0_ref[0, hh].astype(BF16)) * jnp.exp(lf * (pos_f + 1.0))
            o = o + _dot(qb, s0_ref[1, hh].astype(BF16)) * jnp.exp(lb * (seq - pos_f))
        if want_state:
            k = kb16.astype(F32)
            kf = (k * wt_ref[hh, 0]).astype(BF16)
            kb = (k * wt_ref[hh, 1]).astype(BF16)
            tn = (((0,), (0,)), ((), ()))
            st_ref[0, hh] = lax.dot_general(kf, vb, tn, preferred_element_type=F32)
            st_ref[1, hh] = lax.dot_general(kb, vb, tn, preferred_element_type=F32)
        mu = jnp.mean(o, axis=-1, keepdims=True)
        oc = o - mu
        var = jnp.mean(oc * oc, axis=-1, keepdims=True)
        y = (oc * lax.rsqrt(var + GN_EPS)) * gn_ref[:, sl] * g_ref[:, sl].astype(F32)
        o_ref[:, sl] = y.astype(BF16)


def _retention(proj, batch, seq, n_heads, dk, decay_logit, ret_gn, rope, state0, want_state):
    d_ret = n_heads * dk
    hb = n_heads if seq <= 256 else min(n_heads, 2)
    bw = hb * dk
    nhb = n_heads // hb
    lg = jnp.broadcast_to(decay_logit.T[:, :, None, None], (n_heads, 2, 1, V7X_LANES))
    act = lambda part: pl.BlockSpec((seq, bw), lambda h, b: (b, (part * d_ret) // bw + h))
    in_specs = [act(0), act(1), act(2), act(3),
                pl.BlockSpec((hb, 2, 1, V7X_LANES), lambda h, b: (h, 0, 0, 0)),
                pl.BlockSpec((1, bw), lambda h, b: (0, h))]
    args = [proj, proj, proj, proj, lg, ret_gn.reshape(1, d_ret)]
    if rope is not None:
        in_specs += [pl.BlockSpec((seq, dk), lambda h, b: (0, 0))] * 2
        args += list(rope)
    if state0 is not None:
        in_specs.append(pl.BlockSpec((None, 2, hb, dk, dk), lambda h, b: (b, 0, h, 0, 0)))
        args.append(state0)
    out_specs = [pl.BlockSpec((seq, bw), lambda h, b: (b, h))]
    out_shape = [jax.ShapeDtypeStruct((batch * seq, d_ret), BF16)]
    if want_state:
        out_specs.append(pl.BlockSpec((None, 2, hb, dk, dk), lambda h, b: (b, 0, h, 0, 0)))
        out_shape.append(jax.ShapeDtypeStruct((batch, 2, n_heads, dk, dk), F32))
    body = functools.partial(_retention_kernel, seq=seq, hb=hb, dk=dk, use_rope=rope is not None,
                             use_state=state0 is not None, want_state=want_state)
    return pl.pallas_call(
        body,
        grid=(nhb, batch),
        in_specs=in_specs,
        out_specs=out_specs,
        out_shape=out_shape,
        scratch_shapes=[pltpu.VMEM((hb, seq, seq), F32)]
        + ([pltpu.VMEM((hb, 2, seq, dk), F32)] if want_state else []),
        compiler_params=_cparams("parallel", "arbitrary"),
        name=f"retention_{seq}",
    )(*args)


def _merge_kernel(*refs, nblk):
    gate_refs = refs[:2 * nblk]
    (yhy_ref, yret_ref, x_ref, wbh_ref, wbr_ref, wo_ref, gpost_ref, gm_ref, gpre_ref, sc_ref, sh_ref,
     o_ref, h_ref) = refs[2 * nblk:]
    a = _dot(yhy_ref[...], wbh_ref[...])
    b = _dot(yret_ref[...], wbr_ref[...])
    wblk = gate_refs[0].shape[1]
    parts = []
    for kk in range(nblk):
        sl = slice(kk * wblk, (kk + 1) * wblk)
        g_hy = gate_refs[kk][...].astype(F32)
        g_ret = gate_refs[nblk + kk][...].astype(F32)
        parts.append((g_hy * a[:, sl] + g_ret * b[:, sl]).astype(BF16))
    merged = parts[0] if nblk == 1 else jnp.concatenate(parts, axis=1)
    half = merged.shape[0] // 2
    for c in range(2):
        rs = slice(c * half, (c + 1) * half)
        out = _dot(merged[rs], wo_ref[...])
        x1 = x_ref[rs] + _rms_scale(out, gm_ref[...] * gpost_ref[...])
        o_ref[rs] = x1
        h_ref[rs] = (_rms_scale(x1, gpre_ref[...] * (1.0 + sc_ref[...])) + sh_ref[...]).astype(BF16)


def _merge(proj, col0, y_hy, y_ret, x, w_br_hy, w_br_ret, w_out, g_post, gate_m, g_pre_f, scale_f, shift_f,
           rows_per_mod, tm):
    m, d = x.shape
    vec = pl.BlockSpec((1, d), lambda i: (0, 0))
    mod_spec = pl.BlockSpec((None, 1, d), lambda i: ((i * tm) // rows_per_mod, 0, 0))
    wblk = math.gcd(col0, d)
    nblk = d // wblk
    gate_spec = lambda kk: pl.BlockSpec((tm, wblk), lambda i: (i, col0 // wblk + kk))
    const = lambda arr: _resident(arr.shape)
    row = lambda width: pl.BlockSpec((tm, width), lambda i: (i, 0))
    return pl.pallas_call(
        functools.partial(_merge_kernel, nblk=nblk),
        grid=(m // tm,),
        in_specs=[gate_spec(kk) for kk in range(2 * nblk)]
        + [row(y_hy.shape[1]), row(y_ret.shape[1]), row(d), const(w_br_hy), const(w_br_ret), const(w_out),
           vec, mod_spec, vec, mod_spec, mod_spec],
        out_specs=[row(d), row(d)],
        out_shape=[jax.ShapeDtypeStruct((m, d), F32), jax.ShapeDtypeStruct((m, d), BF16)],
        compiler_params=_cparams("parallel"),
        name="merge",
    )(*([proj] * (2 * nblk)), y_hy, y_ret, x, w_br_hy, w_br_ret, w_out, g_post, gate_m, g_pre_f, scale_f, shift_f)


def _ffn_kernel(x_ref, h_ref, gate_ref, gpost_ref, wa_ref, wb_ref, ca_ref, cb_ref, wd_ref, o_ref, act_ref, *, seq):
    j = pl.program_id(1)
    nf = pl.num_programs(1) - 1
    tm = x_ref.shape[0]
    tf = wa_ref.shape[1]

    def up(slot):
        pos = lax.broadcasted_iota(jnp.int32, (tm, tf), 0) % seq
        first = pos == 0
        last = pos == seq - 1
        h = h_ref[...]
        a = _dwconv3_rows(_dot(h, wa_ref[...]), ca_ref[...], first, last)
        b = _dwconv3_rows(_dot(h, wb_ref[...]), cb_ref[...], first, last)
        c1 = math.sqrt(2.0 / math.pi)
        half = 0.5 * a
        gelu = half + half * jnp.tanh(a * (c1 + (c1 * 0.044715) * (a * a)))
        act_ref[slot] = (gelu * b).astype(BF16)

    def down(slot):
        return _dot(act_ref[slot], wd_ref[...])

    @pl.when(j == 0)
    def _():
        up(0)

    @pl.when(j == 1)
    def _():
        o_ref[...] = down(0)
        up(1)

    @pl.when((j > 1) & (j < nf))
    def _():
        slot = j % 2
        o_ref[...] += down(1 - slot)
        up(slot)

    @pl.when(j == nf)
    def _():
        slot = (nf - 1) % 2
        half = tm // 2
        for c in range(2):
            rs = slice(c * half, (c + 1) * half)
            f = o_ref[rs] + _dot(act_ref[slot, rs], wd_ref[...])
            o_ref[rs] = x_ref[rs] + _rms_scale(f, gate_ref[...] * gpost_ref[...])


def _ffn(x, h, seq, gate, g_post, w_up, conv_w, w_down, rows_per_mod, tm):
    m, d = x.shape
    d_ff = w_down.shape[0]
    tf = _tile(d_ff, 512)
    nf = d_ff // tf
    assert nf >= 2, (d_ff, tf)
    once = pl.BlockSpec((tm, d), lambda i, j: (i, 0), pipeline_mode=pl.Buffered(1))
    up_blk = lambda j: jnp.minimum(j, nf - 1)
    down_blk = lambda j: jnp.maximum(j - 1, 0)
    return pl.pallas_call(
        functools.partial(_ffn_kernel, seq=seq),
        grid=(m // tm, nf + 1),
        in_specs=[once, once,
                  pl.BlockSpec((None, 1, d), lambda i, j: ((i * tm) // rows_per_mod, 0, 0)),
                  pl.BlockSpec((1, d), lambda i, j: (0, 0)),
                  pl.BlockSpec((d, tf), lambda i, j: (0, up_blk(j))),
                  pl.BlockSpec((d, tf), lambda i, j: (0, nf + up_blk(j))),
                  pl.BlockSpec((3, tf), lambda i, j: (0, up_blk(j))),
                  pl.BlockSpec((3, tf), lambda i, j: (0, nf + up_blk(j))),
                  pl.BlockSpec((tf, d), lambda i, j: (down_blk(j), 0))],
        out_specs=pl.BlockSpec((tm, d), lambda i, j: (i, 0)),
        out_shape=jax.ShapeDtypeStruct((m, d), F32),
        scratch_shapes=[pltpu.VMEM((2, tm, tf), BF16)],
        compiler_params=_cparams("parallel", "arbitrary"),
        name="ffn",
    )(x, h, gate, g_post, w_up, w_up, conv_w, conv_w, w_down)


class _Group:
    def __init__(self, x3, mod):
        self.batch, self.seq, self.d = x3.shape
        self.m = self.batch * self.seq
        self.x = x3.reshape(self.m, self.d)
        per_seq_mod = mod.shape[0] != 1
        self.rows_per_mod = self.seq if per_seq_mod else self.m
        self.tm = _row_tile(self.m, self.seq, per_seq_mod, 1024)
        self.tm_merge = _tile(self.tm, 512, unit=V7X_SUBLANES)
        (self.shift_m, self.scale_m, self.gate_m,
         self.shift_f, self.scale_f, self.gate_f) = (mod[:, i][:, None, :] for i in range(6))

    def in_proj(self, p, w_in, cast=()):
        return _in_proj(self.x, p["g_pre_m"], self.scale_m, self.shift_m, w_in, p["hy_short_w"], self.seq,
                        p["d_hy"], p["d_ret"], self.rows_per_mod, self.tm, cast)


def _mix_and_ffn(grp, hy_in, ret_in, rope, state0, want_state, p):
    d_hy, d_ret, n_heads, dk = p["d_hy"], p["d_ret"], p["n_heads"], p["dk"]
    tables = _hyena_filters(grp.seq, p["hy_w1"], p["hy_b1"], p["hy_w2"], p["hy_b2"], p["hy_w3"], p["hy_b3"],
                            p["hy_freq"], p["hy_decay"], p["hy_bias"], d_hy)
    y_hy = _hyena(hy_in, grp.batch, grp.seq, tables, d_hy)
    ret = _retention(ret_in, grp.batch, grp.seq, n_heads, dk, p["ret_decay_logit"], p["ret_gn"], rope, state0,
                     want_state)
    x, h_ffn = _merge(ret_in, 4 * d_ret, y_hy, ret[0], grp.x, p["w_br_hy"], p["w_br_ret"], p["w_out"],
                      p["g_post_m"], grp.gate_m, p["g_pre_f"], grp.scale_f, grp.shift_f, grp.rows_per_mod,
                      grp.tm_merge)
    x = _ffn(x, h_ffn, grp.seq, grp.gate_f, p["g_post_f"], p["ffn_w_up"], p["ffn_conv"], p["ffn_w_down"],
             grp.rows_per_mod, grp.tm)
    return x.reshape(grp.batch, grp.seq, grp.d), (ret[1] if want_state else None)


def kernel(x_prompt, x_sample, state_ret, c, c_ctx, w_ada, b_ada, norm_pre_mix, norm_post_mix, norm_pre_ffn,
           norm_post_ffn, w_in, hy_short_w, hy_w1, hy_b1, hy_w2, hy_b2, hy_w3, hy_b3, hy_freq, hy_decay, hy_bias,
           ret_decay_logit, ret_gn, w_br_hy, w_br_ret, w_out, ffn_w_up, ffn_conv, ffn_w_down):
    depth = w_in.shape[0]
    d = x_prompt.shape[-1]
    n_dec = x_sample.shape[0]
    n_heads, dk = state_ret.shape[3], state_ret.shape[4]
    d_hy = hy_bias.shape[-1]
    cc = jnp.zeros((V7X_SUBLANES, d), F32).at[0].set(c_ctx).at[1:1 + n_dec].set(c)
    rope = _rope_tables(x_sample.shape[1], dk)
    x_p, x_s = x_prompt, x_sample
    states = []
    for l in range(depth):
        p = dict(
            d_hy=d_hy, d_ret=n_heads * dk, n_heads=n_heads, dk=dk,
            g_pre_m=norm_pre_mix[l][None], g_post_m=norm_post_mix[l][None],
            g_pre_f=norm_pre_ffn[l][None], g_post_f=norm_post_ffn[l][None],
            hy_short_w=hy_short_w[l],
            hy_w1=hy_w1[l], hy_b1=hy_b1[l], hy_w2=hy_w2[l], hy_b2=hy_b2[l], hy_w3=hy_w3[l], hy_b3=hy_b3[l],
            hy_freq=hy_freq[l], hy_decay=hy_decay[l], hy_bias=hy_bias[l],
            ret_decay_logit=ret_decay_logit[l], ret_gn=ret_gn[l], ffn_conv=ffn_conv[l],
        )
        mod = _ada_mod(cc, w_ada[l], b_ada[l][None]).reshape(V7X_SUBLANES, 6, d)
        ctx = _Group(x_p, mod[0:1])
        lat = _Group(x_s, mod[1:1 + n_dec])
        hy_s, ret_s, w_in_bf = lat.in_proj(p, w_in[l])
        hy_p, ret_p, *rest = ctx.in_proj(p, w_in_bf, (ffn_w_up[l], ffn_w_down[l], w_out[l], w_br_hy[l], w_br_ret[l]))
        p["ffn_w_up"], p["ffn_w_down"], p["w_out"], p["w_br_hy"], p["w_br_ret"] = rest
        x_p, st = _mix_and_ffn(ctx, hy_p, ret_p, None, None, True, p)
        x_s, _ = _mix_and_ffn(lat, hy_s, ret_s, rope, state_ret[:, l], False, p)
        states.append(st)
    return x_p, x_s, jnp.stack(states, axis=1)
```

```python
import functools
import math

import jax
import jax.numpy as jnp
import numpy as np
from jax import lax
from jax.experimental import pallas as pl
from jax.experimental.pallas import tpu as pltpu

F32 = jnp.float32
BF16 = jnp.bfloat16

RMS_EPS = 1e-6
GN_EPS = 1e-5
FILTER_EPS = 1e-6
HY_BANDS = 16
GRID_W = 64
ROPE_BASE = 10000.0

V7X_VMEM_LIMIT_BYTES = 58 * 1024 * 1024
V7X_LANES = 128
V7X_SUBLANES = 8


def _cparams(*sem):
    return pltpu.CompilerParams(dimension_semantics=sem, vmem_limit_bytes=V7X_VMEM_LIMIT_BYTES)


def _tile(n, target, unit=V7X_LANES):
    if n <= target:
        return n
    best = unit
    for t in range(unit, target + 1, unit):
        if n % t == 0:
            best = t
    assert n % best == 0, (n, target, unit)
    return best


def _row_tile(m, seq, per_seq_mod, target):
    if per_seq_mod or seq >= target:
        return seq
    return seq * _tile(m // seq, target // seq, unit=1)


def _resident(shape):
    return pl.BlockSpec(shape, lambda *_: (0,) * len(shape), pipeline_mode=pl.Buffered(1))


def _sigmoid(x):
    return 1.0 / (1.0 + jnp.exp(-x))


def _sigmoid_tanh(x):
    return 0.5 * jnp.tanh(0.5 * x) + 0.5


def _rms_scale(x, g):
    ms = jnp.mean(x * x, axis=-1, keepdims=True)
    return (x * lax.rsqrt(ms + RMS_EPS)) * g


def _dot(a, b):
    return jnp.dot(a, b, preferred_element_type=F32)


def _dot_hilo(a, b):
    a_hi = a.astype(BF16)
    a_lo = (a - a_hi.astype(F32)).astype(BF16)
    b_hi = b.astype(BF16)
    b_lo = (b - b_hi.astype(F32)).astype(BF16)
    return _dot(a_hi, b_hi) + (_dot(a_lo, b_hi) + _dot(a_hi, b_lo))


def _split_bf16(table):
    hi = table.astype(BF16)
    lo = (table - hi.astype(np.float64)).astype(BF16)
    return jnp.asarray(hi), jnp.asarray(lo)


def _dot_split(a_hi_ref, a_lo_ref, b):
    b_hi = b.astype(BF16)
    b_lo = (b - b_hi.astype(F32)).astype(BF16)
    a_hi = a_hi_ref[...]
    return _dot(a_hi, b_hi) + (_dot(a_lo_ref[...], b_hi) + _dot(a_hi, b_lo))


def _dwconv3_rows(x, w, first, last):
    rows = x.shape[0]
    prev = jnp.where(first, 0.0, pltpu.roll(x, 1, 0))
    nxt = jnp.where(last, 0.0, pltpu.roll(x, rows - 1, 0))
    return prev * w[0:1] + x * w[1:2] + nxt * w[2:3]


def _ada_kernel(cc_ref, w_ref, b_ref, o_ref):
    cc = cc_ref[...]
    s = cc * _sigmoid(cc)
    o_ref[...] = _dot(s.astype(BF16), w_ref[...].astype(BF16)) + b_ref[...]


def _ada_mod(cc, w, b):
    d, n = w.shape
    tn = _tile(n, 1024)
    return pl.pallas_call(
        _ada_kernel,
        grid=(n // tn,),
        in_specs=[pl.BlockSpec((V7X_SUBLANES, d), lambda j: (0, 0)),
                  pl.BlockSpec((d, tn), lambda j: (0, j)),
                  pl.BlockSpec((1, tn), lambda j: (0, j))],
        out_specs=pl.BlockSpec((V7X_SUBLANES, tn), lambda j: (0, j)),
        out_shape=jax.ShapeDtypeStruct((V7X_SUBLANES, n), F32),
        compiler_params=_cparams("parallel"),
        name="ada_mod",
    )(cc, w, b)


def _in_proj_kernel(*refs, seq, n_hy, n_qkv, n_g, n_cast, emit_w):
    x_ref, g_ref, sc_ref, sh_ref, w_ref, cw_ref = refs[:6]
    cast_in = refs[6:6 + n_cast]
    hy_ref, ret_ref = refs[6 + n_cast:8 + n_cast]
    pos = 8 + n_cast
    if emit_w:
        wbf_ref = refs[pos]
        pos += 1
    cast_out = refs[pos:pos + n_cast]
    h_ref = refs[pos + n_cast]
    j = pl.program_id(1)

    @pl.when(j == 0)
    def _():
        n_mod = sc_ref.shape[0]
        rows = x_ref.shape[0] // n_mod
        for b in range(n_mod):
            rs = slice(b * rows, (b + 1) * rows)
            h = _rms_scale(x_ref[rs], g_ref[...] * (1.0 + sc_ref[b])) + sh_ref[b]
            h_ref[rs] = h.astype(BF16)

    def tile():
        w = w_ref[...]
        if emit_w:
            w = w.astype(BF16)
            wbf_ref[...] = w
        for src, dst in zip(cast_in, cast_out):
            dst[...] = src[...].astype(BF16)
        return _dot(h_ref[...], w)

    @pl.when(j < n_hy)
    def _():
        pos = lax.broadcasted_iota(jnp.int32, hy_ref.shape, 0) % seq
        hy_ref[...] = _dwconv3_rows(tile(), cw_ref[...], pos == 0, pos == seq - 1)

    @pl.when((j >= n_hy) & (j < n_hy + n_qkv))
    def _():
        ret_ref[...] = tile().astype(BF16)

    @pl.when((j >= n_hy + n_qkv) & (j < n_hy + n_qkv + n_g))
    def _():
        p = tile()
        ret_ref[...] = (p * _sigmoid_tanh(p)).astype(BF16)

    @pl.when(j >= n_hy + n_qkv + n_g)
    def _():
        ret_ref[...] = _sigmoid_tanh(tile()).astype(BF16)


V7X_BF16_ROW_TILE = 2 * V7X_SUBLANES


def _cast_rows(arr, n_steps):
    rows = arr.shape[0]
    for blk in range(V7X_BF16_ROW_TILE, rows + 1, V7X_BF16_ROW_TILE):
        if rows % blk == 0 and rows // blk <= n_steps:
            return blk
    return None


def _in_proj(x, g, scale, shift, w, short_w, seq, d_hy, d_ret, rows_per_mod, tm, cast=()):
    m, d = x.shape
    n = w.shape[1]
    emit_w = w.dtype != BF16
    if emit_w:
        tm = m
    tn = _tile(math.gcd(3 * d_hy, d_ret, 2 * d), 512 if emit_w else 1024)
    n_hy, n_qkv, n_g = 3 * d_hy // tn, 3 * d_ret // tn, d_ret // tn
    n_col = n // tn
    n_mod = max(1, tm // rows_per_mod)
    mod_spec = pl.BlockSpec((n_mod, 1, d), lambda i, j: ((i * tm) // rows_per_mod // n_mod, 0, 0))
    blks = [_cast_rows(a, (m // tm) * n_col) for a in cast]
    assert all(b is not None for b in blks), [a.shape for a in cast]
    cast_specs = [pl.BlockSpec((b, a.shape[1]), lambda i, j, nb=a.shape[0] // b: (jnp.minimum(i * n_col + j, nb - 1), 0))
                  for a, b in zip(cast, blks)]
    out_specs = [pl.BlockSpec((tm, tn), lambda i, j: (i, jnp.minimum(j, n_hy - 1))),
                 pl.BlockSpec((tm, tn), lambda i, j: (i, jnp.maximum(j - n_hy, 0)))]
    out_shape = [jax.ShapeDtypeStruct((m, 3 * d_hy), F32), jax.ShapeDtypeStruct((m, n - 3 * d_hy), BF16)]
    if emit_w:
        out_specs.append(pl.BlockSpec((d, tn), lambda i, j: (0, j)))
        out_shape.append(jax.ShapeDtypeStruct(w.shape, BF16))
    out_specs += cast_specs
    out_shape += [jax.ShapeDtypeStruct(a.shape, BF16) for a in cast]
    return pl.pallas_call(
        functools.partial(_in_proj_kernel, seq=seq, n_hy=n_hy, n_qkv=n_qkv, n_g=n_g, n_cast=len(cast),
                          emit_w=emit_w),
        grid=(m // tm, n_col),
        in_specs=[pl.BlockSpec((tm, d), lambda i, j: (i, 0), pipeline_mode=pl.Buffered(1) if emit_w else None),
                  pl.BlockSpec((1, d), lambda i, j: (0, 0)),
                  mod_spec, mod_spec,
                  pl.BlockSpec((d, tn), lambda i, j: (0, j)),
                  pl.BlockSpec((3, tn), lambda i, j: (0, jnp.minimum(j, n_hy - 1)))] + cast_specs,
        out_specs=out_specs,
        out_shape=out_shape,
        scratch_shapes=[pltpu.VMEM((tm, d), BF16)],
        compiler_params=_cparams("arbitrary", "arbitrary"),
        name="in_proj",
    )(x, g, scale, shift, w, short_w, *cast)


def _dft_tables(seq):
    n_fft = 2 * seq
    idx = np.arange(seq)
    ang = 2.0 * np.pi * ((idx[:, None] * idx[None, :]) % n_fft) / n_fft
    cos = np.cos(ang)
    msin = -np.sin(ang)
    sign = np.where(idx % 2 == 0, 1.0, -1.0)
    msin[0, :] = sign
    fwd = np.concatenate([cos, msin], axis=0)
    wgt = np.full((seq,), 2.0 / n_fft)
    wgt[0] = 1.0 / n_fft
    inv_re = cos.T * wgt[None, :]
    inv_im = msin.T * wgt[None, :]
    inv_im[:, 0] = sign / n_fft
    inv = np.concatenate([inv_re, inv_im], axis=1)
    return fwd, inv


def _filter_feats(seq):
    n = np.arange(seq, dtype=np.float64)
    t = n / seq
    f = np.linspace(1e-4, HY_BANDS - 1, HY_BANDS)
    w = 2.0 * math.pi * n / seq
    z = np.concatenate([t[:, None], np.cos(w[:, None] * f), np.sin(w[:, None] * f)], axis=-1)
    out = np.zeros((seq, V7X_LANES), np.float32)
    out[:, :z.shape[1]] = z
    return out


def _filter_kernel(z_ref, w1_ref, b1_ref, w2_ref, b2_ref, fr_ref, w3f_ref, w3b_ref, b3f_ref, b3b_ref,
                   decf_ref, decb_ref, bias_ref, chi_ref, clo_ref, shi_ref, slo_ref, kr_ref, kiz_ref, krn_ref, h2_ref):
    @pl.when(pl.program_id(0) == 0)
    def _():
        h1 = jnp.sin(fr_ref[0:1, :] * (_dot_hilo(z_ref[...], w1_ref[...]) + b1_ref[...]))
        h2_ref[...] = jnp.sin(fr_ref[1:2, :] * (_dot_hilo(h1, w2_ref[...]) + b2_ref[...]))

    h2 = h2_ref[...]
    seq, cw = kr_ref.shape
    t = z_ref[:, 0:1]
    row = lax.broadcasted_iota(jnp.int32, (seq, cw), 0)
    hf = (_dot_hilo(h2, w3f_ref[...]) + b3f_ref[...]) * jnp.exp(-t * jnp.abs(decf_ref[...]))
    hb = (_dot_hilo(h2, w3b_ref[...]) + b3b_ref[...]) * jnp.exp(-t * jnp.abs(decb_ref[...]))
    hb = jnp.where(row == 0, 0.0, hb)
    norm = (jnp.sum(jnp.abs(hf), axis=0, keepdims=True)
            + jnp.sum(jnp.abs(hb), axis=0, keepdims=True) + FILTER_EPS)
    inv = 1.0 / norm
    even = (hf + hb) * inv
    odd = (hf - hb) * inv
    bias = bias_ref[...]
    kr = _dot_split(chi_ref, clo_ref, even) + bias
    ki = _dot_split(shi_ref, slo_ref, odd)
    nyq = jnp.sum(jnp.where((row & 1) == 0, even, -even), axis=0, keepdims=True) + bias
    kr_ref[...] = kr.astype(BF16)
    kiz_ref[...] = jnp.where(row == 0, 0.0, ki).astype(BF16)
    krn_ref[...] = jnp.where(row == 0, nyq, kr).astype(BF16)


def _filter_params(w1, b1, w2, b2, w3, b3, freq, decay, hy_bias):
    fh = w1.shape[1]
    pad = V7X_LANES
    ncol2 = w3.shape[1]
    return (jnp.zeros((pad, pad), F32).at[:w1.shape[0], :fh].set(w1),
            jnp.zeros((1, pad), F32).at[0, :fh].set(b1),
            jnp.zeros((pad, pad), F32).at[:fh, :fh].set(w2),
            jnp.zeros((1, pad), F32).at[0, :fh].set(b2),
            jnp.zeros((2, pad), F32).at[:, :fh].set(freq),
            jnp.zeros((pad, ncol2), F32).at[:fh].set(w3),
            b3.reshape(1, ncol2), decay.reshape(1, ncol2), hy_bias.reshape(1, ncol2 // 2))


def _hyena_filters(seq, params):
    w1p, b1p, w2p, b2p, frp, w3p, b3r, dec, bias = params
    pad = V7X_LANES
    ncol = bias.shape[1]
    fwd, _ = _dft_tables(seq)
    chi, clo = _split_bf16(fwd[:seq])
    shi, slo = _split_bf16(fwd[seq:])
    z = jnp.asarray(_filter_feats(seq))
    cw = _tile(ncol, 512)
    nb = ncol // cw
    full = lambda shape: pl.BlockSpec(shape, lambda j: (0, 0))
    colf = lambda rows: pl.BlockSpec((rows, cw), lambda j: (0, j))
    colb = lambda rows: pl.BlockSpec((rows, cw), lambda j: (0, nb + j))
    out = jax.ShapeDtypeStruct((seq, ncol), BF16)
    return pl.pallas_call(
        _filter_kernel,
        grid=(nb,),
        in_specs=[full((seq, pad)), full((pad, pad)), full((1, pad)), full((pad, pad)), full((1, pad)),
                  full((2, pad)), colf(pad), colb(pad), colf(1), colb(1), colf(1), colb(1), colf(1),
                  _resident((seq, seq)), _resident((seq, seq)), _resident((seq, seq)), _resident((seq, seq))],
        out_specs=[colf(seq), colf(seq), colf(seq)],
        out_shape=[out, out, out],
        scratch_shapes=[pltpu.VMEM((seq, pad), F32)],
        compiler_params=_cparams("arbitrary"),
        name=f"hy_filter_{seq}",
    )(z, w1p, b1p, w2p, b2p, frp, w3p, w3p, b3r, b3r, dec, dec, bias, chi, clo, shi, slo)


def _hyena_kernel(x1_ref, x2_ref, v_ref, kr0_ref, kiz0_ref, krn0_ref, kr1_ref, kiz1_ref, krn1_ref, f_ref, g_ref,
                  o_ref, *, seq):
    def long_conv(u, kr_ref, kiz_ref, krn_ref):
        spec = _dot(f_ref[...], u.astype(BF16))
        ur = spec[:seq].astype(BF16)
        ui = spec[seq:].astype(BF16)
        kiz = kiz_ref[...]
        yr = ur * kr_ref[...] - ui * kiz
        yi = ur * kiz + ui * krn_ref[...]
        return _dot(g_ref[:, :seq], yr) + _dot(g_ref[:, seq:], yi)

    for s in range(o_ref.shape[0] // seq):
        rs = slice(s * seq, (s + 1) * seq)
        z = x1_ref[rs] * long_conv(v_ref[rs], kr0_ref, kiz0_ref, krn0_ref)
        o_ref[rs] = (x2_ref[rs] * long_conv(z, kr1_ref, kiz1_ref, krn1_ref)).astype(BF16)


def _hyena(proj, batch, seq, tables, d_hy):
    kr, kiz, krn = tables
    cw = _tile(d_hy, 1024 if seq <= 256 else 512)
    nb = d_hy // cw
    fwd, inv = _dft_tables(seq)
    fmat = jnp.asarray(fwd, F32).astype(BF16)
    gmat = jnp.asarray(inv, F32).astype(BF16)
    bs = 2 if batch % 2 == 0 and seq <= 256 else 1
    col = lambda rows, off: pl.BlockSpec((rows, cw), lambda j, b: (0, off * nb + j))
    act = lambda off: pl.BlockSpec((bs * seq, cw), lambda j, b: (b, off * nb + j))
    return pl.pallas_call(
        functools.partial(_hyena_kernel, seq=seq),
        grid=(nb, batch // bs),
        in_specs=[act(0), act(1), act(2),
                  col(seq, 0), col(seq, 0), col(seq, 0), col(seq, 1), col(seq, 1), col(seq, 1),
                  _resident((2 * seq, seq)), _resident((seq, 2 * seq))],
        out_specs=pl.BlockSpec((bs * seq, cw), lambda j, b: (b, j)),
        out_shape=jax.ShapeDtypeStruct((batch * seq, d_hy), BF16),
        compiler_params=_cparams("parallel", "parallel"),
        name=f"hyena_{seq}",
    )(proj, proj, proj, kr, kiz, krn, kr, kiz, krn, fmat, gmat)


def _rope_tables(seq, dk):
    rows = seq // GRID_W
    row = jnp.repeat(jnp.arange(rows), GRID_W).astype(F32)
    col = jnp.tile(jnp.arange(GRID_W), rows).astype(F32)
    nfreq = dk // 4
    inv = ROPE_BASE ** (-jnp.arange(nfreq, dtype=F32) / nfreq)
    ang = jnp.concatenate([row[:, None] * inv, col[:, None] * inv], axis=-1)
    cos, sin = jnp.cos(ang), jnp.sin(ang)
    return jnp.concatenate([cos, cos], axis=-1), jnp.concatenate([-sin, sin], axis=-1)


def _retention_kernel(*refs, seq, hb, dk, use_rope, use_state, want_state):
    refs = list(refs)
    q_ref, k_ref, v_ref, g_ref, lg_ref, gn_ref = refs[:6]
    pos = 6
    if use_rope:
        cos_ref, sin_ref = refs[pos:pos + 2]
        pos += 2
    if use_state:
        s0_ref = refs[pos]
        pos += 1
    o_ref = refs[pos]
    pos += 1
    if want_state:
        st_ref = refs[pos]
        pos += 1
    d_ref = refs[pos]
    if want_state:
        wt_ref = refs[pos + 1]

    def log_gamma(hh, direction):
        return jnp.log(_sigmoid(lg_ref[hh, direction]))[:, 0:1]

    @pl.when(pl.program_id(1) == 0)
    def _():
        i = lax.broadcasted_iota(jnp.int32, (dk, dk), 0)
        j = lax.broadcasted_iota(jnp.int32, (dk, dk), 1)
        diff = (i - j).astype(F32)
        scale = dk ** -0.5
        for hh in range(hb):
            lf = log_gamma(hh, 0)
            lb = log_gamma(hh, 1)
            base_f = scale * jnp.exp(lf * diff)
            base_b = scale * jnp.exp(lb * (-diff))
            diag = jnp.where(diff >= 0, base_f, 0.0) + jnp.where(diff <= 0, base_b, 0.0)
            for bi in range(seq // dk):
                for bj in range(seq // dk):
                    if bi == bj:
                        blk = diag
                    elif bi > bj:
                        blk = base_f * jnp.exp(lf * float(dk * (bi - bj)))
                    else:
                        blk = base_b * jnp.exp(lb * float(dk * (bj - bi)))
                    d_ref[hh, bi * dk:(bi + 1) * dk, bj * dk:(bj + 1) * dk] = blk
            if want_state:
                t = lax.broadcasted_iota(jnp.int32, (seq, dk), 0).astype(F32)
                wt_ref[hh, 0] = scale * jnp.exp(lf * (seq - 1.0 - t))
                wt_ref[hh, 1] = scale * jnp.exp(lb * t)

    pos_f = lax.broadcasted_iota(jnp.int32, (seq, dk), 0).astype(F32)
    for hh in range(hb):
        sl = slice(hh * dk, (hh + 1) * dk)
        qb = q_ref[:, sl]
        kb16 = k_ref[:, sl]
        vb = v_ref[:, sl]
        if use_rope:
            cos = cos_ref[...]
            sin = sin_ref[...]
            q = qb.astype(F32)
            k = kb16.astype(F32)
            qb = (q * cos + pltpu.roll(q, dk // 2, 1) * sin).astype(BF16)
            kb16 = (k * cos + pltpu.roll(k, dk // 2, 1) * sin).astype(BF16)
        s = lax.dot_general(qb, kb16, (((1,), (1,)), ((), ())), preferred_element_type=F32)
        o = _dot((s * d_ref[hh]).astype(BF16), vb)
        lf = log_gamma(hh, 0)
        lb = log_gamma(hh, 1)
        if use_state:
            o = o + _dot(qb, s0_ref[0, hh].astype(BF16)) * jnp.exp(lf * (pos_f + 1.0))
            o = o + _dot(qb, s0_ref[1, hh].astype(BF16)) * jnp.exp(lb * (seq - pos_f))
        if want_state:
            k = kb16.astype(F32)
            kf = (k * wt_ref[hh, 0]).astype(BF16)
            kb = (k * wt_ref[hh, 1]).astype(BF16)
            tn = (((0,), (0,)), ((), ()))
            st_ref[0, hh] = lax.dot_general(kf, vb, tn, preferred_element_type=F32)
            st_ref[1, hh] = lax.dot_general(kb, vb, tn, preferred_element_type=F32)
        mu = jnp.mean(o, axis=-1, keepdims=True)
        oc = o - mu
        var = jnp.mean(oc * oc, axis=-1, keepdims=True)
        y = (oc * lax.rsqrt(var + GN_EPS)) * gn_ref[:, sl] * g_ref[:, sl].astype(F32)
        o_ref[:, sl] = y.astype(BF16)


def _retention(proj, batch, seq, n_heads, dk, decay_logit, ret_gn, rope, state0, want_state):
    d_ret = n_heads * dk
    hb = n_heads if seq <= 256 else min(n_heads, 2)
    bw = hb * dk
    nhb = n_heads // hb
    lg = jnp.broadcast_to(decay_logit.T[:, :, None, None], (n_heads, 2, 1, V7X_LANES))
    act = lambda part: pl.BlockSpec((seq, bw), lambda h, b: (b, (part * d_ret) // bw + h))
    in_specs = [act(0), act(1), act(2), act(3),
                pl.BlockSpec((hb, 2, 1, V7X_LANES), lambda h, b: (h, 0, 0, 0)),
                pl.BlockSpec((1, bw), lambda h, b: (0, h))]
    args = [proj, proj, proj, proj, lg, ret_gn.reshape(1, d_ret)]
    if rope is not None:
        in_specs += [pl.BlockSpec((seq, dk), lambda h, b: (0, 0))] * 2
        args += list(rope)
    if state0 is not None:
        in_specs.append(pl.BlockSpec((None, 2, hb, dk, dk), lambda h, b: (b, 0, h, 0, 0)))
        args.append(state0)
    out_specs = [pl.BlockSpec((seq, bw), lambda h, b: (b, h))]
    out_shape = [jax.ShapeDtypeStruct((batch * seq, d_ret), BF16)]
    if want_state:
        out_specs.append(pl.BlockSpec((None, 2, hb, dk, dk), lambda h, b: (b, 0, h, 0, 0)))
        out_shape.append(jax.ShapeDtypeStruct((batch, 2, n_heads, dk, dk), F32))
    body = functools.partial(_retention_kernel, seq=seq, hb=hb, dk=dk, use_rope=rope is not None,
                             use_state=state0 is not None, want_state=want_state)
    return pl.pallas_call(
        body,
        grid=(nhb, batch),
        in_specs=in_specs,
        out_specs=out_specs,
        out_shape=out_shape,
        scratch_shapes=[pltpu.VMEM((hb, seq, seq), F32)]
        + ([pltpu.VMEM((hb, 2, seq, dk), F32)] if want_state else []),
        compiler_params=_cparams("parallel", "arbitrary"),
        name=f"retention_{seq}",
    )(*args)


def _merge_kernel(*refs, nblk):
    gate_refs = refs[:2 * nblk]
    (yhy_ref, yret_ref, x_ref, wbh_ref, wbr_ref, wo_ref, gpost_ref, gm_ref, gpre_ref, sc_ref, sh_ref,
     o_ref, h_ref) = refs[2 * nblk:]
    a = _dot(yhy_ref[...], wbh_ref[...])
    b = _dot(yret_ref[...], wbr_ref[...])
    wblk = gate_refs[0].shape[1]
    parts = []
    for kk in range(nblk):
        sl = slice(kk * wblk, (kk + 1) * wblk)
        g_hy = gate_refs[kk][...].astype(F32)
        g_ret = gate_refs[nblk + kk][...].astype(F32)
        parts.append((g_hy * a[:, sl] + g_ret * b[:, sl]).astype(BF16))
    merged = parts[0] if nblk == 1 else jnp.concatenate(parts, axis=1)
    half = merged.shape[0] // 2
    for c in range(2):
        rs = slice(c * half, (c + 1) * half)
        out = _dot(merged[rs], wo_ref[...])
        x1 = x_ref[rs] + _rms_scale(out, gm_ref[...] * gpost_ref[...])
        o_ref[rs] = x1
        h_ref[rs] = (_rms_scale(x1, gpre_ref[...] * (1.0 + sc_ref[...])) + sh_ref[...]).astype(BF16)


def _merge(proj, col0, y_hy, y_ret, x, w_br_hy, w_br_ret, w_out, g_post, gate_m, g_pre_f, scale_f, shift_f,
           rows_per_mod, tm):
    m, d = x.shape
    vec = pl.BlockSpec((1, d), lambda i: (0, 0))
    mod_spec = pl.BlockSpec((None, 1, d), lambda i: ((i * tm) // rows_per_mod, 0, 0))
    wblk = math.gcd(col0, d)
    nblk = d // wblk
    gate_spec = lambda kk: pl.BlockSpec((tm, wblk), lambda i: (i, col0 // wblk + kk))
    const = lambda arr: _resident(arr.shape)
    row = lambda width: pl.BlockSpec((tm, width), lambda i: (i, 0))
    return pl.pallas_call(
        functools.partial(_merge_kernel, nblk=nblk),
        grid=(m // tm,),
        in_specs=[gate_spec(kk) for kk in range(2 * nblk)]
        + [row(y_hy.shape[1]), row(y_ret.shape[1]), row(d), const(w_br_hy), const(w_br_ret), const(w_out),
           vec, mod_spec, vec, mod_spec, mod_spec],
        out_specs=[row(d), row(d)],
        out_shape=[jax.ShapeDtypeStruct((m, d), F32), jax.ShapeDtypeStruct((m, d), BF16)],
        compiler_params=_cparams("parallel"),
        name="merge",
    )(*([proj] * (2 * nblk)), y_hy, y_ret, x, w_br_hy, w_br_ret, w_out, g_post, gate_m, g_pre_f, scale_f, shift_f)


def _ffn_kernel(x_ref, h_ref, gate_ref, gpost_ref, wa_ref, wb_ref, ca_ref, cb_ref, wd_ref, o_ref, act_ref, *, seq):
    j = pl.program_id(1)
    nf = pl.num_programs(1) - 1
    tm = x_ref.shape[0]
    tf = wa_ref.shape[1]

    def up(slot):
        pos = lax.broadcasted_iota(jnp.int32, (tm, tf), 0) % seq
        first = pos == 0
        last = pos == seq - 1
        h = h_ref[...]
        a = _dwconv3_rows(_dot(h, wa_ref[...]), ca_ref[...], first, last)
        b = _dwconv3_rows(_dot(h, wb_ref[...]), cb_ref[...], first, last)
        c1 = math.sqrt(2.0 / math.pi)
        half = 0.5 * a
        gelu = half + half * jnp.tanh(a * (c1 + (c1 * 0.044715) * (a * a)))
        act_ref[slot] = (gelu * b).astype(BF16)

    def down(slot):
        return _dot(act_ref[slot], wd_ref[...])

    @pl.when(j == 0)
    def _():
        up(0)

    @pl.when(j == 1)
    def _():
        o_ref[...] = down(0)
        up(1)

    @pl.when((j > 1) & (j < nf))
    def _():
        slot = j % 2
        o_ref[...] += down(1 - slot)
        up(slot)

    @pl.when(j == nf)
    def _():
        slot = (nf - 1) % 2
        half = tm // 2
        for c in range(2):
            rs = slice(c * half, (c + 1) * half)
            f = o_ref[rs] + _dot(act_ref[slot, rs], wd_ref[...])
            o_ref[rs] = x_ref[rs] + _rms_scale(f, gate_ref[...] * gpost_ref[...])


def _ffn(x, h, seq, gate, g_post, w_up, conv_w, w_down, rows_per_mod, tm):
    m, d = x.shape
    d_ff = w_down.shape[0]
    tf = _tile(d_ff, 512)
    nf = d_ff // tf
    assert nf >= 2, (d_ff, tf)
    once = pl.BlockSpec((tm, d), lambda i, j: (i, 0), pipeline_mode=pl.Buffered(1))
    up_blk = lambda j: jnp.minimum(j, nf - 1)
    down_blk = lambda j: jnp.maximum(j - 1, 0)
    return pl.pallas_call(
        functools.partial(_ffn_kernel, seq=seq),
        grid=(m // tm, nf + 1),
        in_specs=[once, once,
                  pl.BlockSpec((None, 1, d), lambda i, j: ((i * tm) // rows_per_mod, 0, 0)),
                  pl.BlockSpec((1, d), lambda i, j: (0, 0)),
                  pl.BlockSpec((d, tf), lambda i, j: (0, up_blk(j))),
                  pl.BlockSpec((d, tf), lambda i, j: (0, nf + up_blk(j))),
                  pl.BlockSpec((3, tf), lambda i, j: (0, up_blk(j))),
                  pl.BlockSpec((3, tf), lambda i, j: (0, nf + up_blk(j))),
                  pl.BlockSpec((tf, d), lambda i, j: (down_blk(j), 0))],
        out_specs=pl.BlockSpec((tm, d), lambda i, j: (i, 0)),
        out_shape=jax.ShapeDtypeStruct((m, d), F32),
        scratch_shapes=[pltpu.VMEM((2, tm, tf), BF16)],
        compiler_params=_cparams("parallel", "arbitrary"),
        name="ffn",
    )(x, h, gate, g_post, w_up, w_up, conv_w, conv_w, w_down)


class _Group:
    def __init__(self, x3, mod):
        self.batch, self.seq, self.d = x3.shape
        self.m = self.batch * self.seq
        self.x = x3.reshape(self.m, self.d)
        per_seq_mod = mod.shape[0] != 1
        self.rows_per_mod = self.seq if per_seq_mod else self.m
        self.tm = _row_tile(self.m, self.seq, per_seq_mod, 1024)
        self.tm_merge = _tile(self.tm, 512, unit=V7X_SUBLANES)
        (self.shift_m, self.scale_m, self.gate_m,
         self.shift_f, self.scale_f, self.gate_f) = (mod[:, i][:, None, :] for i in range(6))

    def in_proj(self, p, w_in, cast=()):
        return _in_proj(self.x, p["g_pre_m"], self.scale_m, self.shift_m, w_in, p["hy_short_w"], self.seq,
                        p["d_hy"], p["d_ret"], self.rows_per_mod, self.tm, cast)


def _mix_and_ffn(grp, hy_in, ret_in, rope, state0, want_state, p):
    d_hy, d_ret, n_heads, dk = p["d_hy"], p["d_ret"], p["n_heads"], p["dk"]
    tables = _hyena_filters(grp.seq, p["filter_params"])
    y_hy = _hyena(hy_in, grp.batch, grp.seq, tables, d_hy)
    ret = _retention(ret_in, grp.batch, grp.seq, n_heads, dk, p["ret_decay_logit"], p["ret_gn"], rope, state0,
                     want_state)
    x, h_ffn = _merge(ret_in, 4 * d_ret, y_hy, ret[0], grp.x, p["w_br_hy"], p["w_br_ret"], p["w_out"],
                      p["g_post_m"], grp.gate_m, p["g_pre_f"], grp.scale_f, grp.shift_f, grp.rows_per_mod,
                      grp.tm_merge)
    x = _ffn(x, h_ffn, grp.seq, grp.gate_f, p["g_post_f"], p["ffn_w_up"], p["ffn_conv"], p["ffn_w_down"],
             grp.rows_per_mod, grp.tm)
    return x.reshape(grp.batch, grp.seq, grp.d), (ret[1] if want_state else None)


def kernel(x_prompt, x_sample, state_ret, c, c_ctx, w_ada, b_ada, norm_pre_mix, norm_post_mix, norm_pre_ffn,
           norm_post_ffn, w_in, hy_short_w, hy_w1, hy_b1, hy_w2, hy_b2, hy_w3, hy_b3, hy_freq, hy_decay, hy_bias,
           ret_decay_logit, ret_gn, w_br_hy, w_br_ret, w_out, ffn_w_up, ffn_conv, ffn_w_down):
    depth = w_in.shape[0]
    d = x_prompt.shape[-1]
    n_dec = x_sample.shape[0]
    n_heads, dk = state_ret.shape[3], state_ret.shape[4]
    d_hy = hy_bias.shape[-1]
    assert 1 + n_dec <= V7X_SUBLANES, n_dec
    cc = jnp.concatenate([c_ctx[None], c, jnp.zeros((V7X_SUBLANES - 1 - n_dec, d), F32)], axis=0)
    rope = _rope_tables(x_sample.shape[1], dk)
    x_p, x_s = x_prompt, x_sample
    states = []
    for l in range(depth):
        p = dict(
            d_hy=d_hy, d_ret=n_heads * dk, n_heads=n_heads, dk=dk,
            g_pre_m=norm_pre_mix[l][None], g_post_m=norm_post_mix[l][None],
            g_pre_f=norm_pre_ffn[l][None], g_post_f=norm_post_ffn[l][None],
            hy_short_w=hy_short_w[l],
            filter_params=_filter_params(hy_w1[l], hy_b1[l], hy_w2[l], hy_b2[l], hy_w3[l], hy_b3[l], hy_freq[l],
                                         hy_decay[l], hy_bias[l]),
            ret_decay_logit=ret_decay_logit[l], ret_gn=ret_gn[l], ffn_conv=ffn_conv[l],
        )
        mod = _ada_mod(cc, w_ada[l], b_ada[l][None]).reshape(V7X_SUBLANES, 6, d)
        ctx = _Group(x_p, mod[0:1])
        lat = _Group(x_s, mod[1:1 + n_dec])
        hy_s, ret_s, w_in_bf = lat.in_proj(p, w_in[l])
        hy_p, ret_p, *rest = ctx.in_proj(p, w_in_bf, (ffn_w_up[l], ffn_w_down[l], w_out[l], w_br_hy[l], w_br_ret[l]))
        p["ffn_w_up"], p["ffn_w_down"], p["w_out"], p["w_br_hy"], p["w_br_ret"] = rest
        x_p, st = _mix_and_ffn(ctx, hy_p, ret_p, None, None, True, p)
        x_s, _ = _mix_and_ffn(lat, hy_s, ret_s, rope, state_ret[:, l], False, p)
        states.append(st)
    return x_p, x_s, jnp.stack(states, axis=1)
```

```python
import functools
import math

import jax
import jax.numpy as jnp
import numpy as np
from jax import lax
from jax.experimental import pallas as pl
from jax.experimental.pallas import tpu as pltpu

F32 = jnp.float32
BF16 = jnp.bfloat16

RMS_EPS = 1e-6
GN_EPS = 1e-5
FILTER_EPS = 1e-6
HY_BANDS = 16
GRID_W = 64
ROPE_BASE = 10000.0

V7X_VMEM_LIMIT_BYTES = 58 * 1024 * 1024
V7X_LANES = 128
V7X_SUBLANES = 8


def _cparams(*sem):
    return pltpu.CompilerParams(dimension_semantics=sem, vmem_limit_bytes=V7X_VMEM_LIMIT_BYTES)


def _tile(n, target, unit=V7X_LANES):
    if n <= target:
        return n
    best = unit
    for t in range(unit, target + 1, unit):
        if n % t == 0:
            best = t
    assert n % best == 0, (n, target, unit)
    return best


def _row_tile(m, seq, per_seq_mod, target):
    if per_seq_mod or seq >= target:
        return seq
    return seq * _tile(m // seq, target // seq, unit=1)


def _resident(shape):
    return pl.BlockSpec(shape, lambda *_: (0,) * len(shape), pipeline_mode=pl.Buffered(1))


def _sigmoid(x):
    return 1.0 / (1.0 + jnp.exp(-x))


def _sigmoid_tanh(x):
    return 0.5 * jnp.tanh(0.5 * x) + 0.5


def _rms_scale(x, g):
    ms = jnp.mean(x * x, axis=-1, keepdims=True)
    return (x * lax.rsqrt(ms + RMS_EPS)) * g


def _dot(a, b):
    return jnp.dot(a, b, preferred_element_type=F32)


def _dot_hilo(a, b):
    a_hi = a.astype(BF16)
    a_lo = (a - a_hi.astype(F32)).astype(BF16)
    b_hi = b.astype(BF16)
    b_lo = (b - b_hi.astype(F32)).astype(BF16)
    return _dot(a_hi, b_hi) + (_dot(a_lo, b_hi) + _dot(a_hi, b_lo))


def _split_bf16(table):
    hi = table.astype(BF16)
    lo = (table - hi.astype(np.float64)).astype(BF16)
    return jnp.asarray(hi), jnp.asarray(lo)


def _dot_split(a_hi_ref, a_lo_ref, b):
    b_hi = b.astype(BF16)
    b_lo = (b - b_hi.astype(F32)).astype(BF16)
    a_hi = a_hi_ref[...]
    return _dot(a_hi, b_hi) + (_dot(a_lo_ref[...], b_hi) + _dot(a_hi, b_lo))


def _dwconv3_rows(x, w, first, last):
    rows = x.shape[0]
    prev = jnp.where(first, 0.0, pltpu.roll(x, 1, 0))
    nxt = jnp.where(last, 0.0, pltpu.roll(x, rows - 1, 0))
    return prev * w[0:1] + x * w[1:2] + nxt * w[2:3]


def _ada_kernel(cc_ref, w_ref, b_ref, o_ref):
    cc = cc_ref[...]
    s = cc * _sigmoid(cc)
    o_ref[...] = _dot(s.astype(BF16), w_ref[...].astype(BF16)) + b_ref[...]


def _ada_mod(cc, w, b):
    d, n = w.shape
    tn = _tile(n, 1024)
    return pl.pallas_call(
        _ada_kernel,
        grid=(n // tn,),
        in_specs=[pl.BlockSpec((V7X_SUBLANES, d), lambda j: (0, 0)),
                  pl.BlockSpec((d, tn), lambda j: (0, j)),
                  pl.BlockSpec((1, tn), lambda j: (0, j))],
        out_specs=pl.BlockSpec((V7X_SUBLANES, tn), lambda j: (0, j)),
        out_shape=jax.ShapeDtypeStruct((V7X_SUBLANES, n), F32),
        compiler_params=_cparams("parallel"),
        name="ada_mod",
    )(cc, w, b)


def _in_proj_kernel(*refs, seq, n_hy, n_qkv, n_g, n_cast, emit_w):
    x_ref, g_ref, sc_ref, sh_ref, w_ref, cw_ref = refs[:6]
    cast_in = refs[6:6 + n_cast]
    o_ref = refs[6 + n_cast]
    pos = 7 + n_cast
    if emit_w:
        wbf_ref = refs[pos]
        pos += 1
    cast_out = refs[pos:pos + n_cast]
    h_ref = refs[pos + n_cast]
    j = pl.program_id(1)

    @pl.when(j == 0)
    def _():
        n_mod = sc_ref.shape[0]
        rows = x_ref.shape[0] // n_mod
        for b in range(n_mod):
            rs = slice(b * rows, (b + 1) * rows)
            h = _rms_scale(x_ref[rs], g_ref[...] * (1.0 + sc_ref[b])) + sh_ref[b]
            h_ref[rs] = h.astype(BF16)

    def tile():
        w = w_ref[...]
        if emit_w:
            w = w.astype(BF16)
            wbf_ref[...] = w
        for src, dst in zip(cast_in, cast_out):
            dst[...] = src[...].astype(BF16)
        return _dot(h_ref[...], w)

    @pl.when(j < n_hy)
    def _():
        pos = lax.broadcasted_iota(jnp.int32, o_ref.shape, 0) % seq
        o_ref[...] = _dwconv3_rows(tile(), cw_ref[...], pos == 0, pos == seq - 1).astype(BF16)

    @pl.when((j >= n_hy) & (j < n_hy + n_qkv))
    def _():
        o_ref[...] = tile().astype(BF16)

    @pl.when((j >= n_hy + n_qkv) & (j < n_hy + n_qkv + n_g))
    def _():
        p = tile()
        o_ref[...] = (p * _sigmoid_tanh(p)).astype(BF16)

    @pl.when(j >= n_hy + n_qkv + n_g)
    def _():
        o_ref[...] = _sigmoid_tanh(tile()).astype(BF16)


V7X_BF16_ROW_TILE = 2 * V7X_SUBLANES


def _cast_rows(arr, n_steps):
    rows = arr.shape[0]
    for blk in range(V7X_BF16_ROW_TILE, rows + 1, V7X_BF16_ROW_TILE):
        if rows % blk == 0 and rows // blk <= n_steps:
            return blk
    return None


def _in_proj(x, g, scale, shift, w, short_w, seq, d_hy, d_ret, rows_per_mod, tm, cast=()):
    m, d = x.shape
    n = w.shape[1]
    emit_w = w.dtype != BF16
    if emit_w:
        tm = m
    tn = _tile(math.gcd(3 * d_hy, d_ret, 2 * d), 512 if emit_w else 1024)
    n_hy, n_qkv, n_g = 3 * d_hy // tn, 3 * d_ret // tn, d_ret // tn
    n_col = n // tn
    n_mod = max(1, tm // rows_per_mod)
    mod_spec = pl.BlockSpec((n_mod, 1, d), lambda i, j: ((i * tm) // rows_per_mod // n_mod, 0, 0))
    blks = [_cast_rows(a, (m // tm) * n_col) for a in cast]
    assert all(b is not None for b in blks), [a.shape for a in cast]
    cast_specs = [pl.BlockSpec((b, a.shape[1]), lambda i, j, nb=a.shape[0] // b: (jnp.minimum(i * n_col + j, nb - 1), 0))
                  for a, b in zip(cast, blks)]
    out_specs = [pl.BlockSpec((tm, tn), lambda i, j: (i, j))]
    out_shape = [jax.ShapeDtypeStruct((m, n), BF16)]
    if emit_w:
        out_specs.append(pl.BlockSpec((d, tn), lambda i, j: (0, j)))
        out_shape.append(jax.ShapeDtypeStruct(w.shape, BF16))
    out_specs += cast_specs
    out_shape += [jax.ShapeDtypeStruct(a.shape, BF16) for a in cast]
    return pl.pallas_call(
        functools.partial(_in_proj_kernel, seq=seq, n_hy=n_hy, n_qkv=n_qkv, n_g=n_g, n_cast=len(cast),
                          emit_w=emit_w),
        grid=(m // tm, n_col),
        in_specs=[pl.BlockSpec((tm, d), lambda i, j: (i, 0), pipeline_mode=pl.Buffered(1) if emit_w else None),
                  pl.BlockSpec((1, d), lambda i, j: (0, 0)),
                  mod_spec, mod_spec,
                  pl.BlockSpec((d, tn), lambda i, j: (0, j)),
                  pl.BlockSpec((3, tn), lambda i, j: (0, jnp.minimum(j, n_hy - 1)))] + cast_specs,
        out_specs=out_specs,
        out_shape=out_shape,
        scratch_shapes=[pltpu.VMEM((tm, d), BF16)],
        compiler_params=_cparams("arbitrary", "arbitrary"),
        name="in_proj",
    )(x, g, scale, shift, w, short_w, *cast)


def _dft_tables(seq):
    n_fft = 2 * seq
    idx = np.arange(seq)
    ang = 2.0 * np.pi * ((idx[:, None] * idx[None, :]) % n_fft) / n_fft
    cos = np.cos(ang)
    msin = -np.sin(ang)
    sign = np.where(idx % 2 == 0, 1.0, -1.0)
    msin[0, :] = sign
    fwd = np.concatenate([cos, msin], axis=0)
    wgt = np.full((seq,), 2.0 / n_fft)
    wgt[0] = 1.0 / n_fft
    inv_re = cos.T * wgt[None, :]
    inv_im = msin.T * wgt[None, :]
    inv_im[:, 0] = sign / n_fft
    inv = np.concatenate([inv_re, inv_im], axis=1)
    return fwd, inv


def _filter_feats(seq):
    n = np.arange(seq, dtype=np.float64)
    t = n / seq
    f = np.linspace(1e-4, HY_BANDS - 1, HY_BANDS)
    w = 2.0 * math.pi * n / seq
    z = np.concatenate([t[:, None], np.cos(w[:, None] * f), np.sin(w[:, None] * f)], axis=-1)
    out = np.zeros((seq, V7X_LANES), np.float32)
    out[:, :z.shape[1]] = z
    return out


def _filter_kernel(z_ref, w1_ref, b1_ref, w2_ref, b2_ref, fr_ref, w3f_ref, w3b_ref, b3f_ref, b3b_ref,
                   decf_ref, decb_ref, bias_ref, chi_ref, clo_ref, shi_ref, slo_ref, kr_ref, kiz_ref, krn_ref, h2_ref):
    @pl.when(pl.program_id(0) == 0)
    def _():
        h1 = jnp.sin(fr_ref[0:1, :] * (_dot_hilo(z_ref[...], w1_ref[...]) + b1_ref[...]))
        h2_ref[...] = jnp.sin(fr_ref[1:2, :] * (_dot_hilo(h1, w2_ref[...]) + b2_ref[...]))

    h2 = h2_ref[...]
    seq, cw = kr_ref.shape
    t = z_ref[:, 0:1]
    row = lax.broadcasted_iota(jnp.int32, (seq, cw), 0)
    hf = (_dot_hilo(h2, w3f_ref[...]) + b3f_ref[...]) * jnp.exp(-t * jnp.abs(decf_ref[...]))
    hb = (_dot_hilo(h2, w3b_ref[...]) + b3b_ref[...]) * jnp.exp(-t * jnp.abs(decb_ref[...]))
    hb = jnp.where(row == 0, 0.0, hb)
    norm = (jnp.sum(jnp.abs(hf), axis=0, keepdims=True)
            + jnp.sum(jnp.abs(hb), axis=0, keepdims=True) + FILTER_EPS)
    inv = 1.0 / norm
    even = (hf + hb) * inv
    odd = (hf - hb) * inv
    bias = bias_ref[...]
    kr = _dot_split(chi_ref, clo_ref, even) + bias
    ki = _dot_split(shi_ref, slo_ref, odd)
    nyq = jnp.sum(jnp.where((row & 1) == 0, even, -even), axis=0, keepdims=True) + bias
    kr_ref[...] = kr.astype(BF16)
    kiz_ref[...] = jnp.where(row == 0, 0.0, ki).astype(BF16)
    krn_ref[...] = jnp.where(row == 0, nyq, kr).astype(BF16)


def _filter_params(w1, b1, w2, b2, w3, b3, freq, decay, hy_bias):
    fh = w1.shape[1]
    pad = V7X_LANES
    ncol2 = w3.shape[1]
    return (jnp.zeros((pad, pad), F32).at[:w1.shape[0], :fh].set(w1),
            jnp.zeros((1, pad), F32).at[0, :fh].set(b1),
            jnp.zeros((pad, pad), F32).at[:fh, :fh].set(w2),
            jnp.zeros((1, pad), F32).at[0, :fh].set(b2),
            jnp.zeros((2, pad), F32).at[:, :fh].set(freq),
            jnp.zeros((pad, ncol2), F32).at[:fh].set(w3),
            b3.reshape(1, ncol2), decay.reshape(1, ncol2), hy_bias.reshape(1, ncol2 // 2))


def _hyena_filters(seq, params):
    w1p, b1p, w2p, b2p, frp, w3p, b3r, dec, bias = params
    pad = V7X_LANES
    ncol = bias.shape[1]
    fwd, _ = _dft_tables(seq)
    chi, clo = _split_bf16(fwd[:seq])
    shi, slo = _split_bf16(fwd[seq:])
    z = jnp.asarray(_filter_feats(seq))
    cw = _tile(ncol, 512)
    nb = ncol // cw
    full = lambda shape: pl.BlockSpec(shape, lambda j: (0, 0))
    colf = lambda rows: pl.BlockSpec((rows, cw), lambda j: (0, j))
    colb = lambda rows: pl.BlockSpec((rows, cw), lambda j: (0, nb + j))
    out = jax.ShapeDtypeStruct((seq, ncol), BF16)
    return pl.pallas_call(
        _filter_kernel,
        grid=(nb,),
        in_specs=[full((seq, pad)), full((pad, pad)), full((1, pad)), full((pad, pad)), full((1, pad)),
                  full((2, pad)), colf(pad), colb(pad), colf(1), colb(1), colf(1), colb(1), colf(1),
                  _resident((seq, seq)), _resident((seq, seq)), _resident((seq, seq)), _resident((seq, seq))],
        out_specs=[colf(seq), colf(seq), colf(seq)],
        out_shape=[out, out, out],
        scratch_shapes=[pltpu.VMEM((seq, pad), F32)],
        compiler_params=_cparams("arbitrary"),
        name=f"hy_filter_{seq}",
    )(z, w1p, b1p, w2p, b2p, frp, w3p, w3p, b3r, b3r, dec, dec, bias, chi, clo, shi, slo)


def _hyena_kernel(x1_ref, x2_ref, v_ref, kr0_ref, kiz0_ref, krn0_ref, kr1_ref, kiz1_ref, krn1_ref, f_ref, g_ref,
                  o_ref, *, seq):
    def long_conv(u, kr_ref, kiz_ref, krn_ref):
        spec = _dot(f_ref[...], u.astype(BF16))
        ur = spec[:seq].astype(BF16)
        ui = spec[seq:].astype(BF16)
        kiz = kiz_ref[...]
        yr = ur * kr_ref[...] - ui * kiz
        yi = ur * kiz + ui * krn_ref[...]
        return _dot(g_ref[:, :seq], yr) + _dot(g_ref[:, seq:], yi)

    for s in range(o_ref.shape[0] // seq):
        rs = slice(s * seq, (s + 1) * seq)
        z = x1_ref[rs].astype(F32) * long_conv(v_ref[rs], kr0_ref, kiz0_ref, krn0_ref)
        o_ref[rs] = (x2_ref[rs].astype(F32) * long_conv(z, kr1_ref, kiz1_ref, krn1_ref)).astype(BF16)


def _hyena(proj, batch, seq, tables, d_hy):
    kr, kiz, krn = tables
    cw = _tile(d_hy, 1024 if seq <= 256 else 512)
    nb = d_hy // cw
    fwd, inv = _dft_tables(seq)
    fmat = jnp.asarray(fwd, F32).astype(BF16)
    gmat = jnp.asarray(inv, F32).astype(BF16)
    bs = 2 if batch % 2 == 0 and seq <= 256 else 1
    col = lambda rows, off: pl.BlockSpec((rows, cw), lambda j, b: (0, off * nb + j))
    act = lambda off: pl.BlockSpec((bs * seq, cw), lambda j, b: (b, off * nb + j))
    return pl.pallas_call(
        functools.partial(_hyena_kernel, seq=seq),
        grid=(nb, batch // bs),
        in_specs=[act(0), act(1), act(2),
                  col(seq, 0), col(seq, 0), col(seq, 0), col(seq, 1), col(seq, 1), col(seq, 1),
                  _resident((2 * seq, seq)), _resident((seq, 2 * seq))],
        out_specs=pl.BlockSpec((bs * seq, cw), lambda j, b: (b, j)),
        out_shape=jax.ShapeDtypeStruct((batch * seq, d_hy), BF16),
        compiler_params=_cparams("parallel", "parallel"),
        name=f"hyena_{seq}",
    )(proj, proj, proj, kr, kiz, krn, kr, kiz, krn, fmat, gmat)


def _rope_tables(seq, dk):
    rows = seq // GRID_W
    row = jnp.repeat(jnp.arange(rows), GRID_W).astype(F32)
    col = jnp.tile(jnp.arange(GRID_W), rows).astype(F32)
    nfreq = dk // 4
    inv = ROPE_BASE ** (-jnp.arange(nfreq, dtype=F32) / nfreq)
    ang = jnp.concatenate([row[:, None] * inv, col[:, None] * inv], axis=-1)
    cos, sin = jnp.cos(ang), jnp.sin(ang)
    return jnp.concatenate([cos, cos], axis=-1), jnp.concatenate([-sin, sin], axis=-1)


def _retention_kernel(*refs, seq, hb, dk, use_rope, use_state, want_state):
    refs = list(refs)
    q_ref, k_ref, v_ref, g_ref, lg_ref, gn_ref = refs[:6]
    pos = 6
    if use_rope:
        cos_ref, sin_ref = refs[pos:pos + 2]
        pos += 2
    if use_state:
        s0_ref = refs[pos]
        pos += 1
    o_ref = refs[pos]
    pos += 1
    if want_state:
        st_ref = refs[pos]
        pos += 1
    d_ref = refs[pos]
    if want_state:
        wt_ref = refs[pos + 1]

    def log_gamma(hh, direction):
        return jnp.log(_sigmoid(lg_ref[hh, direction]))[:, 0:1]

    @pl.when(pl.program_id(1) == 0)
    def _():
        i = lax.broadcasted_iota(jnp.int32, (dk, dk), 0)
        j = lax.broadcasted_iota(jnp.int32, (dk, dk), 1)
        diff = (i - j).astype(F32)
        scale = dk ** -0.5
        for hh in range(hb):
            lf = log_gamma(hh, 0)
            lb = log_gamma(hh, 1)
            base_f = scale * jnp.exp(lf * diff)
            base_b = scale * jnp.exp(lb * (-diff))
            diag = jnp.where(diff >= 0, base_f, 0.0) + jnp.where(diff <= 0, base_b, 0.0)
            for bi in range(seq // dk):
                for bj in range(seq // dk):
                    if bi == bj:
                        blk = diag
                    elif bi > bj:
                        blk = base_f * jnp.exp(lf * float(dk * (bi - bj)))
                    else:
                        blk = base_b * jnp.exp(lb * float(dk * (bj - bi)))
                    d_ref[hh, bi * dk:(bi + 1) * dk, bj * dk:(bj + 1) * dk] = blk
            if want_state:
                t = lax.broadcasted_iota(jnp.int32, (seq, dk), 0).astype(F32)
                wt_ref[hh, 0] = scale * jnp.exp(lf * (seq - 1.0 - t))
                wt_ref[hh, 1] = scale * jnp.exp(lb * t)

    pos_f = lax.broadcasted_iota(jnp.int32, (seq, dk), 0).astype(F32)
    for hh in range(hb):
        sl = slice(hh * dk, (hh + 1) * dk)
        qb = q_ref[:, sl]
        kb16 = k_ref[:, sl]
        vb = v_ref[:, sl]
        if use_rope:
            cos = cos_ref[...]
            sin = sin_ref[...]
            q = qb.astype(F32)
            k = kb16.astype(F32)
            qb = (q * cos + pltpu.roll(q, dk // 2, 1) * sin).astype(BF16)
            kb16 = (k * cos + pltpu.roll(k, dk // 2, 1) * sin).astype(BF16)
        s = lax.dot_general(qb, kb16, (((1,), (1,)), ((), ())), preferred_element_type=F32)
        o = _dot((s * d_ref[hh]).astype(BF16), vb)
        lf = log_gamma(hh, 0)
        lb = log_gamma(hh, 1)
        if use_state:
            o = o + _dot(qb, s0_ref[0, hh].astype(BF16)) * jnp.exp(lf * (pos_f + 1.0))
            o = o + _dot(qb, s0_ref[1, hh].astype(BF16)) * jnp.exp(lb * (seq - pos_f))
        if want_state:
            k = kb16.astype(F32)
            kf = (k * wt_ref[hh, 0]).astype(BF16)
            kb = (k * wt_ref[hh, 1]).astype(BF16)
            tn = (((0,), (0,)), ((), ()))
            st_ref[0, hh] = lax.dot_general(kf, vb, tn, preferred_element_type=F32)
            st_ref[1, hh] = lax.dot_general(kb, vb, tn, preferred_element_type=F32)
        mu = jnp.mean(o, axis=-1, keepdims=True)
        oc = o - mu
        var = jnp.mean(oc * oc, axis=-1, keepdims=True)
        y = (oc * lax.rsqrt(var + GN_EPS)) * gn_ref[:, sl] * g_ref[:, sl].astype(F32)
        o_ref[:, sl] = y.astype(BF16)


def _retention(proj, batch, seq, col0, n_heads, dk, decay_logit, ret_gn, rope, state0, want_state):
    d_ret = n_heads * dk
    hb = n_heads if seq <= 256 else min(n_heads, 2)
    bw = hb * dk
    nhb = n_heads // hb
    lg = jnp.broadcast_to(decay_logit.T[:, :, None, None], (n_heads, 2, 1, V7X_LANES))
    act = lambda part: pl.BlockSpec((seq, bw), lambda h, b: (b, (col0 + part * d_ret) // bw + h))
    in_specs = [act(0), act(1), act(2), act(3),
                pl.BlockSpec((hb, 2, 1, V7X_LANES), lambda h, b: (h, 0, 0, 0)),
                pl.BlockSpec((1, bw), lambda h, b: (0, h))]
    args = [proj, proj, proj, proj, lg, ret_gn.reshape(1, d_ret)]
    if rope is not None:
        in_specs += [pl.BlockSpec((seq, dk), lambda h, b: (0, 0))] * 2
        args += list(rope)
    if state0 is not None:
        in_specs.append(pl.BlockSpec((None, 2, hb, dk, dk), lambda h, b: (b, 0, h, 0, 0)))
        args.append(state0)
    out_specs = [pl.BlockSpec((seq, bw), lambda h, b: (b, h))]
    out_shape = [jax.ShapeDtypeStruct((batch * seq, d_ret), BF16)]
    if want_state:
        out_specs.append(pl.BlockSpec((None, 2, hb, dk, dk), lambda h, b: (b, 0, h, 0, 0)))
        out_shape.append(jax.ShapeDtypeStruct((batch, 2, n_heads, dk, dk), F32))
    body = functools.partial(_retention_kernel, seq=seq, hb=hb, dk=dk, use_rope=rope is not None,
                             use_state=state0 is not None, want_state=want_state)
    return pl.pallas_call(
        body,
        grid=(nhb, batch),
        in_specs=in_specs,
        out_specs=out_specs,
        out_shape=out_shape,
        scratch_shapes=[pltpu.VMEM((hb, seq, seq), F32)]
        + ([pltpu.VMEM((hb, 2, seq, dk), F32)] if want_state else []),
        compiler_params=_cparams("parallel", "arbitrary"),
        name=f"retention_{seq}",
    )(*args)


def _merge_kernel(*refs, nblk):
    gate_refs = refs[:2 * nblk]
    (yhy_ref, yret_ref, x_ref, wbh_ref, wbr_ref, wo_ref, gpost_ref, gm_ref, gpre_ref, sc_ref, sh_ref,
     o_ref, h_ref) = refs[2 * nblk:]
    a = _dot(yhy_ref[...], wbh_ref[...])
    b = _dot(yret_ref[...], wbr_ref[...])
    wblk = gate_refs[0].shape[1]
    parts = []
    for kk in range(nblk):
        sl = slice(kk * wblk, (kk + 1) * wblk)
        g_hy = gate_refs[kk][...].astype(F32)
        g_ret = gate_refs[nblk + kk][...].astype(F32)
        parts.append((g_hy * a[:, sl] + g_ret * b[:, sl]).astype(BF16))
    merged = parts[0] if nblk == 1 else jnp.concatenate(parts, axis=1)
    half = merged.shape[0] // 2
    for c in range(2):
        rs = slice(c * half, (c + 1) * half)
        out = _dot(merged[rs], wo_ref[...])
        x1 = x_ref[rs] + _rms_scale(out, gm_ref[...] * gpost_ref[...])
        o_ref[rs] = x1
        h_ref[rs] = (_rms_scale(x1, gpre_ref[...] * (1.0 + sc_ref[...])) + sh_ref[...]).astype(BF16)


def _merge(proj, col0, y_hy, y_ret, x, w_br_hy, w_br_ret, w_out, g_post, gate_m, g_pre_f, scale_f, shift_f,
           rows_per_mod, tm):
    m, d = x.shape
    vec = pl.BlockSpec((1, d), lambda i: (0, 0))
    mod_spec = pl.BlockSpec((None, 1, d), lambda i: ((i * tm) // rows_per_mod, 0, 0))
    wblk = math.gcd(col0, d)
    nblk = d // wblk
    gate_spec = lambda kk: pl.BlockSpec((tm, wblk), lambda i: (i, col0 // wblk + kk))
    const = lambda arr: _resident(arr.shape)
    row = lambda width: pl.BlockSpec((tm, width), lambda i: (i, 0))
    return pl.pallas_call(
        functools.partial(_merge_kernel, nblk=nblk),
        grid=(m // tm,),
        in_specs=[gate_spec(kk) for kk in range(2 * nblk)]
        + [row(y_hy.shape[1]), row(y_ret.shape[1]), row(d), const(w_br_hy), const(w_br_ret), const(w_out),
           vec, mod_spec, vec, mod_spec, mod_spec],
        out_specs=[row(d), row(d)],
        out_shape=[jax.ShapeDtypeStruct((m, d), F32), jax.ShapeDtypeStruct((m, d), BF16)],
        compiler_params=_cparams("parallel"),
        name="merge",
    )(*([proj] * (2 * nblk)), y_hy, y_ret, x, w_br_hy, w_br_ret, w_out, g_post, gate_m, g_pre_f, scale_f, shift_f)


def _ffn_kernel(x_hbm, h_ref, gate_ref, gpost_ref, wa_ref, wb_ref, ca_ref, cb_ref, wd_ref, o_ref, act_ref, x_ref,
                x_sem, *, seq):
    j = pl.program_id(1)
    nf = pl.num_programs(1) - 1
    tm = x_ref.shape[0]
    tf = wa_ref.shape[1]

    def x_copy():
        return pltpu.make_async_copy(x_hbm.at[pl.ds(pl.program_id(0) * tm, tm)], x_ref, x_sem)

    def up(slot):
        pos = lax.broadcasted_iota(jnp.int32, (tm, tf), 0) % seq
        first = pos == 0
        last = pos == seq - 1
        h = h_ref[...]
        a = _dwconv3_rows(_dot(h, wa_ref[...]), ca_ref[...], first, last)
        b = _dwconv3_rows(_dot(h, wb_ref[...]), cb_ref[...], first, last)
        c1 = math.sqrt(2.0 / math.pi)
        half = 0.5 * a
        gelu = half + half * jnp.tanh(a * (c1 + (c1 * 0.044715) * (a * a)))
        act_ref[slot] = (gelu * b).astype(BF16)

    def down(slot):
        return _dot(act_ref[slot], wd_ref[...])

    @pl.when(j == 0)
    def _():
        x_copy().start()
        up(0)

    @pl.when(j == 1)
    def _():
        o_ref[...] = down(0)
        up(1)

    @pl.when((j > 1) & (j < nf))
    def _():
        slot = j % 2
        o_ref[...] += down(1 - slot)
        up(slot)

    @pl.when(j == nf)
    def _():
        x_copy().wait()
        slot = (nf - 1) % 2
        half = tm // 2
        for c in range(2):
            rs = slice(c * half, (c + 1) * half)
            f = o_ref[rs] + _dot(act_ref[slot, rs], wd_ref[...])
            o_ref[rs] = x_ref[rs] + _rms_scale(f, gate_ref[...] * gpost_ref[...])


def _ffn(x, h, seq, gate, g_post, w_up, conv_w, w_down, rows_per_mod, tm):
    m, d = x.shape
    d_ff = w_down.shape[0]
    tf = _tile(d_ff, 512)
    nf = d_ff // tf
    assert nf >= 2, (d_ff, tf)
    up_blk = lambda j: jnp.minimum(j, nf - 1)
    down_blk = lambda j: jnp.maximum(j - 1, 0)
    return pl.pallas_call(
        functools.partial(_ffn_kernel, seq=seq),
        grid=(m // tm, nf + 1),
        in_specs=[pl.BlockSpec(memory_space=pl.ANY), pl.BlockSpec((tm, d), lambda i, j: (i, 0)),
                  pl.BlockSpec((None, 1, d), lambda i, j: ((i * tm) // rows_per_mod, 0, 0)),
                  pl.BlockSpec((1, d), lambda i, j: (0, 0)),
                  pl.BlockSpec((d, tf), lambda i, j: (0, up_blk(j))),
                  pl.BlockSpec((d, tf), lambda i, j: (0, nf + up_blk(j))),
                  pl.BlockSpec((3, tf), lambda i, j: (0, up_blk(j))),
                  pl.BlockSpec((3, tf), lambda i, j: (0, nf + up_blk(j))),
                  pl.BlockSpec((tf, d), lambda i, j: (down_blk(j), 0))],
        out_specs=pl.BlockSpec((tm, d), lambda i, j: (i, 0)),
        out_shape=jax.ShapeDtypeStruct((m, d), F32),
        scratch_shapes=[pltpu.VMEM((2, tm, tf), BF16), pltpu.VMEM((tm, d), F32), pltpu.SemaphoreType.DMA(())],
        compiler_params=_cparams("parallel", "arbitrary"),
        name="ffn",
    )(x, h, gate, g_post, w_up, w_up, conv_w, conv_w, w_down)


class _Group:
    def __init__(self, x3, mod):
        self.batch, self.seq, self.d = x3.shape
        self.m = self.batch * self.seq
        self.x = x3.reshape(self.m, self.d)
        per_seq_mod = mod.shape[0] != 1
        self.rows_per_mod = self.seq if per_seq_mod else self.m
        self.tm = _row_tile(self.m, self.seq, per_seq_mod, 1024)
        self.tm_merge = _tile(self.tm, 512, unit=V7X_SUBLANES)
        (self.shift_m, self.scale_m, self.gate_m,
         self.shift_f, self.scale_f, self.gate_f) = (mod[:, i][:, None, :] for i in range(6))

    def in_proj(self, p, w_in, cast=()):
        return _in_proj(self.x, p["g_pre_m"], self.scale_m, self.shift_m, w_in, p["hy_short_w"], self.seq,
                        p["d_hy"], p["d_ret"], self.rows_per_mod, self.tm, cast)


def _mix_and_ffn(grp, proj, rope, state0, want_state, p):
    d_hy, d_ret, n_heads, dk = p["d_hy"], p["d_ret"], p["n_heads"], p["dk"]
    tables = _hyena_filters(grp.seq, p["filter_params"])
    y_hy = _hyena(proj, grp.batch, grp.seq, tables, d_hy)
    ret = _retention(proj, grp.batch, grp.seq, 3 * d_hy, n_heads, dk, p["ret_decay_logit"], p["ret_gn"], rope,
                     state0, want_state)
    x, h_ffn = _merge(proj, 3 * d_hy + 4 * d_ret, y_hy, ret[0], grp.x, p["w_br_hy"], p["w_br_ret"], p["w_out"],
                      p["g_post_m"], grp.gate_m, p["g_pre_f"], grp.scale_f, grp.shift_f, grp.rows_per_mod,
                      grp.tm_merge)
    x = _ffn(x, h_ffn, grp.seq, grp.gate_f, p["g_post_f"], p["ffn_w_up"], p["ffn_conv"], p["ffn_w_down"],
             grp.rows_per_mod, grp.tm)
    return x.reshape(grp.batch, grp.seq, grp.d), (ret[1] if want_state else None)


def kernel(x_prompt, x_sample, state_ret, c, c_ctx, w_ada, b_ada, norm_pre_mix, norm_post_mix, norm_pre_ffn,
           norm_post_ffn, w_in, hy_short_w, hy_w1, hy_b1, hy_w2, hy_b2, hy_w3, hy_b3, hy_freq, hy_decay, hy_bias,
           ret_decay_logit, ret_gn, w_br_hy, w_br_ret, w_out, ffn_w_up, ffn_conv, ffn_w_down):
    depth = w_in.shape[0]
    d = x_prompt.shape[-1]
    n_dec = x_sample.shape[0]
    n_heads, dk = state_ret.shape[3], state_ret.shape[4]
    d_hy = hy_bias.shape[-1]
    assert 1 + n_dec <= V7X_SUBLANES, n_dec
    cc = jnp.concatenate([c_ctx[None], c, jnp.zeros((V7X_SUBLANES - 1 - n_dec, d), F32)], axis=0)
    rope = _rope_tables(x_sample.shape[1], dk)
    x_p, x_s = x_prompt, x_sample
    states = []
    for l in range(depth):
        p = dict(
            d_hy=d_hy, d_ret=n_heads * dk, n_heads=n_heads, dk=dk,
            g_pre_m=norm_pre_mix[l][None], g_post_m=norm_post_mix[l][None],
            g_pre_f=norm_pre_ffn[l][None], g_post_f=norm_post_ffn[l][None],
            hy_short_w=hy_short_w[l],
            filter_params=_filter_params(hy_w1[l], hy_b1[l], hy_w2[l], hy_b2[l], hy_w3[l], hy_b3[l], hy_freq[l],
                                         hy_decay[l], hy_bias[l]),
            ret_decay_logit=ret_decay_logit[l], ret_gn=ret_gn[l], ffn_conv=ffn_conv[l],
        )
        mod = _ada_mod(cc, w_ada[l], b_ada[l][None]).reshape(V7X_SUBLANES, 6, d)
        ctx = _Group(x_p, mod[0:1])
        lat = _Group(x_s, mod[1:1 + n_dec])
        proj_s, w_in_bf = lat.in_proj(p, w_in[l])
        proj_p, *rest = ctx.in_proj(p, w_in_bf, (ffn_w_up[l], ffn_w_down[l], w_out[l], w_br_hy[l], w_br_ret[l]))
        p["ffn_w_up"], p["ffn_w_down"], p["w_out"], p["w_br_hy"], p["w_br_ret"] = rest
        x_p, st = _mix_and_ffn(ctx, proj_p, None, None, True, p)
        x_s, _ = _mix_and_ffn(lat, proj_s, rope, state_ret[:, l], False, p)
        states.append(st)
    return x_p, x_s, jnp.stack(states, axis=1)
```

```python
import functools
import math

import jax
import jax.numpy as jnp
import numpy as np
from jax import lax
from jax.experimental import pallas as pl
from jax.experimental.pallas import tpu as pltpu

F32 = jnp.float32
BF16 = jnp.bfloat16

RMS_EPS = 1e-6
GN_EPS = 1e-5
FILTER_EPS = 1e-6
HY_BANDS = 16
GRID_W = 64
ROPE_BASE = 10000.0

V7X_VMEM_LIMIT_BYTES = 58 * 1024 * 1024
V7X_LANES = 128
V7X_SUBLANES = 8


def _cparams(*sem):
    return pltpu.CompilerParams(dimension_semantics=sem, vmem_limit_bytes=V7X_VMEM_LIMIT_BYTES)


def _tile(n, target, unit=V7X_LANES):
    if n <= target:
        return n
    best = unit
    for t in range(unit, target + 1, unit):
        if n % t == 0:
            best = t
    assert n % best == 0, (n, target, unit)
    return best


def _row_tile(m, seq, per_seq_mod, target):
    if per_seq_mod or seq >= target:
        return seq
    return seq * _tile(m // seq, target // seq, unit=1)


def _resident(shape):
    return pl.BlockSpec(shape, lambda *_: (0,) * len(shape), pipeline_mode=pl.Buffered(1))


def _sigmoid(x):
    return 1.0 / (1.0 + jnp.exp(-x))


def _sigmoid_tanh(x):
    return 0.5 * jnp.tanh(0.5 * x) + 0.5


def _rms_scale(x, g):
    ms = jnp.mean(x * x, axis=-1, keepdims=True)
    return (x * lax.rsqrt(ms + RMS_EPS)) * g


def _dot(a, b):
    return jnp.dot(a, b, preferred_element_type=F32)


def _dot_hilo(a, b):
    a_hi = a.astype(BF16)
    a_lo = (a - a_hi.astype(F32)).astype(BF16)
    b_hi = b.astype(BF16)
    b_lo = (b - b_hi.astype(F32)).astype(BF16)
    return _dot(a_hi, b_hi) + (_dot(a_lo, b_hi) + _dot(a_hi, b_lo))


def _split_bf16(table):
    hi = table.astype(BF16)
    lo = (table - hi.astype(np.float64)).astype(BF16)
    return jnp.asarray(hi), jnp.asarray(lo)


def _dot_split(a_hi_ref, a_lo_ref, b):
    b_hi = b.astype(BF16)
    b_lo = (b - b_hi.astype(F32)).astype(BF16)
    a_hi = a_hi_ref[...]
    return _dot(a_hi, b_hi) + (_dot(a_lo_ref[...], b_hi) + _dot(a_hi, b_lo))


def _dwconv3_rows(x, w, first, last):
    rows = x.shape[0]
    prev = jnp.where(first, 0.0, pltpu.roll(x, 1, 0))
    nxt = jnp.where(last, 0.0, pltpu.roll(x, rows - 1, 0))
    return prev * w[0:1] + x * w[1:2] + nxt * w[2:3]


def _ada_kernel(cc_ref, w_ref, b_ref, o_ref):
    cc = cc_ref[...]
    s = cc * _sigmoid(cc)
    o_ref[...] = _dot(s.astype(BF16), w_ref[...].astype(BF16)) + b_ref[...]


def _ada_mod(cc, w, b):
    d, n = w.shape
    tn = _tile(n, 1024)
    return pl.pallas_call(
        _ada_kernel,
        grid=(n // tn,),
        in_specs=[pl.BlockSpec((V7X_SUBLANES, d), lambda j: (0, 0)),
                  pl.BlockSpec((d, tn), lambda j: (0, j)),
                  pl.BlockSpec((1, tn), lambda j: (0, j))],
        out_specs=pl.BlockSpec((V7X_SUBLANES, tn), lambda j: (0, j)),
        out_shape=jax.ShapeDtypeStruct((V7X_SUBLANES, n), F32),
        compiler_params=_cparams("parallel"),
        name="ada_mod",
    )(cc, w, b)


def _in_proj_kernel(*refs, seq, n_hy, n_qkv, n_g, n_cast, emit_w):
    x_ref, g_ref, sc_ref, sh_ref, w_ref, cw_ref = refs[:6]
    cast_in = refs[6:6 + n_cast]
    o_ref = refs[6 + n_cast]
    pos = 7 + n_cast
    if emit_w:
        wbf_ref = refs[pos]
        pos += 1
    cast_out = refs[pos:pos + n_cast]
    h_ref = refs[pos + n_cast]
    j = pl.program_id(1)

    @pl.when(j == 0)
    def _():
        n_mod = sc_ref.shape[0]
        rows = x_ref.shape[0] // n_mod
        for b in range(n_mod):
            rs = slice(b * rows, (b + 1) * rows)
            h = _rms_scale(x_ref[rs], g_ref[...] * (1.0 + sc_ref[b])) + sh_ref[b]
            h_ref[rs] = h.astype(BF16)

    def tile():
        w = w_ref[...]
        if emit_w:
            w = w.astype(BF16)
            wbf_ref[...] = w
        for src, dst in zip(cast_in, cast_out):
            dst[...] = src[...].astype(BF16)
        return _dot(h_ref[...], w)

    @pl.when(j < n_hy)
    def _():
        pos = lax.broadcasted_iota(jnp.int32, o_ref.shape, 0) % seq
        o_ref[...] = _dwconv3_rows(tile(), cw_ref[...], pos == 0, pos == seq - 1).astype(BF16)

    @pl.when((j >= n_hy) & (j < n_hy + n_qkv))
    def _():
        o_ref[...] = tile().astype(BF16)

    @pl.when((j >= n_hy + n_qkv) & (j < n_hy + n_qkv + n_g))
    def _():
        p = tile()
        o_ref[...] = (p * _sigmoid_tanh(p)).astype(BF16)

    @pl.when(j >= n_hy + n_qkv + n_g)
    def _():
        o_ref[...] = _sigmoid_tanh(tile()).astype(BF16)


V7X_BF16_ROW_TILE = 2 * V7X_SUBLANES


def _cast_rows(arr, n_steps):
    rows = arr.shape[0]
    for blk in range(V7X_BF16_ROW_TILE, rows + 1, V7X_BF16_ROW_TILE):
        if rows % blk == 0 and rows // blk <= n_steps:
            return blk
    return None


def _cast_specs(cast, n_steps, step):
    blks = [_cast_rows(a, n_steps) for a in cast]
    assert all(b is not None for b in blks), [a.shape for a in cast]
    specs = [pl.BlockSpec((b, a.shape[1]), lambda *ids, nb=a.shape[0] // b: (jnp.minimum(step(*ids), nb - 1), 0))
             for a, b in zip(cast, blks)]
    return specs, [jax.ShapeDtypeStruct(a.shape, BF16) for a in cast]


def _in_proj(x, g, scale, shift, w, short_w, seq, d_hy, d_ret, rows_per_mod, tm, cast=()):
    m, d = x.shape
    n = w.shape[1]
    emit_w = w.dtype != BF16
    if emit_w:
        tm = m
    tn = _tile(math.gcd(3 * d_hy, d_ret, 2 * d), 512 if emit_w else 1024)
    n_hy, n_qkv, n_g = 3 * d_hy // tn, 3 * d_ret // tn, d_ret // tn
    n_col = n // tn
    n_mod = max(1, tm // rows_per_mod)
    mod_spec = pl.BlockSpec((n_mod, 1, d), lambda i, j: ((i * tm) // rows_per_mod // n_mod, 0, 0))
    cast_specs, cast_shapes = _cast_specs(cast, (m // tm) * n_col, lambda i, j: i * n_col + j)
    out_specs = [pl.BlockSpec((tm, tn), lambda i, j: (i, j))]
    out_shape = [jax.ShapeDtypeStruct((m, n), BF16)]
    if emit_w:
        out_specs.append(pl.BlockSpec((d, tn), lambda i, j: (0, j)))
        out_shape.append(jax.ShapeDtypeStruct(w.shape, BF16))
    out_specs += cast_specs
    out_shape += cast_shapes
    return pl.pallas_call(
        functools.partial(_in_proj_kernel, seq=seq, n_hy=n_hy, n_qkv=n_qkv, n_g=n_g, n_cast=len(cast),
                          emit_w=emit_w),
        grid=(m // tm, n_col),
        in_specs=[pl.BlockSpec((tm, d), lambda i, j: (i, 0), pipeline_mode=pl.Buffered(1) if emit_w else None),
                  pl.BlockSpec((1, d), lambda i, j: (0, 0)),
                  mod_spec, mod_spec,
                  pl.BlockSpec((d, tn), lambda i, j: (0, j)),
                  pl.BlockSpec((3, tn), lambda i, j: (0, jnp.minimum(j, n_hy - 1)))] + cast_specs,
        out_specs=out_specs,
        out_shape=out_shape,
        scratch_shapes=[pltpu.VMEM((tm, d), BF16)],
        compiler_params=_cparams("arbitrary", "arbitrary"),
        name="in_proj",
    )(x, g, scale, shift, w, short_w, *cast)


def _dft_tables(seq):
    n_fft = 2 * seq
    idx = np.arange(seq)
    ang = 2.0 * np.pi * ((idx[:, None] * idx[None, :]) % n_fft) / n_fft
    cos = np.cos(ang)
    msin = -np.sin(ang)
    sign = np.where(idx % 2 == 0, 1.0, -1.0)
    msin[0, :] = sign
    fwd = np.concatenate([cos, msin], axis=0)
    wgt = np.full((seq,), 2.0 / n_fft)
    wgt[0] = 1.0 / n_fft
    inv_re = cos.T * wgt[None, :]
    inv_im = msin.T * wgt[None, :]
    inv_im[:, 0] = sign / n_fft
    inv = np.concatenate([inv_re, inv_im], axis=1)
    return fwd, inv


def _filter_feats(seq):
    n = np.arange(seq, dtype=np.float64)
    t = n / seq
    f = np.linspace(1e-4, HY_BANDS - 1, HY_BANDS)
    w = 2.0 * math.pi * n / seq
    z = np.concatenate([t[:, None], np.cos(w[:, None] * f), np.sin(w[:, None] * f)], axis=-1)
    out = np.zeros((seq, V7X_LANES), np.float32)
    out[:, :z.shape[1]] = z
    return out


def _filter_kernel(z_ref, w1_ref, b1_ref, w2_ref, b2_ref, fr_ref, w3f_ref, w3b_ref, b3f_ref, b3b_ref,
                   decf_ref, decb_ref, bias_ref, chi_ref, clo_ref, shi_ref, slo_ref, kr_ref, kiz_ref, krn_ref, h2_ref):
    @pl.when(pl.program_id(0) == 0)
    def _():
        h1 = jnp.sin(fr_ref[0:1, :] * (_dot_hilo(z_ref[...], w1_ref[...]) + b1_ref[...]))
        h2_ref[...] = jnp.sin(fr_ref[1:2, :] * (_dot_hilo(h1, w2_ref[...]) + b2_ref[...]))

    h2 = h2_ref[...]
    seq, cw = kr_ref.shape
    t = z_ref[:, 0:1]
    row = lax.broadcasted_iota(jnp.int32, (seq, cw), 0)
    hf = (_dot_hilo(h2, w3f_ref[...]) + b3f_ref[...]) * jnp.exp(-t * jnp.abs(decf_ref[...]))
    hb = (_dot_hilo(h2, w3b_ref[...]) + b3b_ref[...]) * jnp.exp(-t * jnp.abs(decb_ref[...]))
    hb = jnp.where(row == 0, 0.0, hb)
    norm = (jnp.sum(jnp.abs(hf), axis=0, keepdims=True)
            + jnp.sum(jnp.abs(hb), axis=0, keepdims=True) + FILTER_EPS)
    inv = 1.0 / norm
    even = (hf + hb) * inv
    odd = (hf - hb) * inv
    bias = bias_ref[...]
    kr = _dot_split(chi_ref, clo_ref, even) + bias
    ki = _dot_split(shi_ref, slo_ref, odd)
    nyq = jnp.sum(jnp.where((row & 1) == 0, even, -even), axis=0, keepdims=True) + bias
    kr_ref[...] = kr.astype(BF16)
    kiz_ref[...] = jnp.where(row == 0, 0.0, ki).astype(BF16)
    krn_ref[...] = jnp.where(row == 0, nyq, kr).astype(BF16)


def _filter_params(w1, b1, w2, b2, w3, b3, freq, decay, hy_bias):
    fh = w1.shape[1]
    pad = V7X_LANES
    ncol2 = w3.shape[1]
    return (jnp.zeros((pad, pad), F32).at[:w1.shape[0], :fh].set(w1),
            jnp.zeros((1, pad), F32).at[0, :fh].set(b1),
            jnp.zeros((pad, pad), F32).at[:fh, :fh].set(w2),
            jnp.zeros((1, pad), F32).at[0, :fh].set(b2),
            jnp.zeros((2, pad), F32).at[:, :fh].set(freq),
            jnp.zeros((pad, ncol2), F32).at[:fh].set(w3),
            b3.reshape(1, ncol2), decay.reshape(1, ncol2), hy_bias.reshape(1, ncol2 // 2))


def _hyena_filters(seq, params):
    w1p, b1p, w2p, b2p, frp, w3p, b3r, dec, bias = params
    pad = V7X_LANES
    ncol = bias.shape[1]
    fwd, _ = _dft_tables(seq)
    chi, clo = _split_bf16(fwd[:seq])
    shi, slo = _split_bf16(fwd[seq:])
    z = jnp.asarray(_filter_feats(seq))
    cw = _tile(ncol, 512)
    nb = ncol // cw
    full = lambda shape: pl.BlockSpec(shape, lambda j: (0, 0))
    colf = lambda rows: pl.BlockSpec((rows, cw), lambda j: (0, j))
    colb = lambda rows: pl.BlockSpec((rows, cw), lambda j: (0, nb + j))
    out = jax.ShapeDtypeStruct((seq, ncol), BF16)
    return pl.pallas_call(
        _filter_kernel,
        grid=(nb,),
        in_specs=[full((seq, pad)), full((pad, pad)), full((1, pad)), full((pad, pad)), full((1, pad)),
                  full((2, pad)), colf(pad), colb(pad), colf(1), colb(1), colf(1), colb(1), colf(1),
                  _resident((seq, seq)), _resident((seq, seq)), _resident((seq, seq)), _resident((seq, seq))],
        out_specs=[colf(seq), colf(seq), colf(seq)],
        out_shape=[out, out, out],
        scratch_shapes=[pltpu.VMEM((seq, pad), F32)],
        compiler_params=_cparams("arbitrary"),
        name=f"hy_filter_{seq}",
    )(z, w1p, b1p, w2p, b2p, frp, w3p, w3p, b3r, b3r, dec, dec, bias, chi, clo, shi, slo)


def _hyena_kernel(*refs, seq, n_cast):
    (x1_ref, x2_ref, v_ref, kr0_ref, kiz0_ref, krn0_ref, kr1_ref, kiz1_ref, krn1_ref, f_ref, g_ref) = refs[:11]
    o_ref = refs[11 + n_cast]
    for src, dst in zip(refs[11:11 + n_cast], refs[12 + n_cast:]):
        dst[...] = src[...].astype(BF16)

    def long_conv(u, kr_ref, kiz_ref, krn_ref):
        spec = _dot(f_ref[...], u.astype(BF16))
        ur = spec[:seq].astype(BF16)
        ui = spec[seq:].astype(BF16)
        kiz = kiz_ref[...]
        yr = ur * kr_ref[...] - ui * kiz
        yi = ur * kiz + ui * krn_ref[...]
        return _dot(g_ref[:, :seq], yr) + _dot(g_ref[:, seq:], yi)

    for s in range(o_ref.shape[0] // seq):
        rs = slice(s * seq, (s + 1) * seq)
        z = x1_ref[rs].astype(F32) * long_conv(v_ref[rs], kr0_ref, kiz0_ref, krn0_ref)
        o_ref[rs] = (x2_ref[rs].astype(F32) * long_conv(z, kr1_ref, kiz1_ref, krn1_ref)).astype(BF16)


def _hyena(proj, batch, seq, tables, d_hy, cast=()):
    kr, kiz, krn = tables
    cw = _tile(d_hy, 1024 if seq <= 256 else 512)
    nb = d_hy // cw
    fwd, inv = _dft_tables(seq)
    fmat = jnp.asarray(fwd, F32).astype(BF16)
    gmat = jnp.asarray(inv, F32).astype(BF16)
    bs = 2 if batch % 2 == 0 and seq <= 256 else 1
    col = lambda rows, off: pl.BlockSpec((rows, cw), lambda j, b: (0, off * nb + j))
    act = lambda off: pl.BlockSpec((bs * seq, cw), lambda j, b: (b, off * nb + j))
    n_b = batch // bs
    cast_specs, cast_shapes = _cast_specs(cast, nb * n_b, lambda j, b: j * n_b + b)
    out = pl.pallas_call(
        functools.partial(_hyena_kernel, seq=seq, n_cast=len(cast)),
        grid=(nb, n_b),
        in_specs=[act(0), act(1), act(2),
                  col(seq, 0), col(seq, 0), col(seq, 0), col(seq, 1), col(seq, 1), col(seq, 1),
                  _resident((2 * seq, seq)), _resident((seq, 2 * seq))] + cast_specs,
        out_specs=[pl.BlockSpec((bs * seq, cw), lambda j, b: (b, j))] + cast_specs,
        out_shape=[jax.ShapeDtypeStruct((batch * seq, d_hy), BF16)] + cast_shapes,
        compiler_params=_cparams("arbitrary", "arbitrary"),
        name=f"hyena_{seq}",
    )(proj, proj, proj, kr, kiz, krn, kr, kiz, krn, fmat, gmat, *cast)
    return out[0], out[1:]


def _rope_tables(seq, dk):
    rows = seq // GRID_W
    row = jnp.repeat(jnp.arange(rows), GRID_W).astype(F32)
    col = jnp.tile(jnp.arange(GRID_W), rows).astype(F32)
    nfreq = dk // 4
    inv = ROPE_BASE ** (-jnp.arange(nfreq, dtype=F32) / nfreq)
    ang = jnp.concatenate([row[:, None] * inv, col[:, None] * inv], axis=-1)
    cos, sin = jnp.cos(ang), jnp.sin(ang)
    return jnp.concatenate([cos, cos], axis=-1), jnp.concatenate([-sin, sin], axis=-1)


def _retention_kernel(*refs, seq, hb, dk, use_rope, use_state, want_state):
    refs = list(refs)
    q_ref, k_ref, v_ref, g_ref, lg_ref, gn_ref = refs[:6]
    pos = 6
    if use_rope:
        cos_ref, sin_ref = refs[pos:pos + 2]
        pos += 2
    if use_state:
        s0_ref = refs[pos]
        pos += 1
    o_ref = refs[pos]
    pos += 1
    if want_state:
        st_ref = refs[pos]
        pos += 1
    d_ref = refs[pos]
    if want_state:
        wt_ref = refs[pos + 1]

    def log_gamma(hh, direction):
        return jnp.log(_sigmoid(lg_ref[hh, direction]))[:, 0:1]

    @pl.when(pl.program_id(1) == 0)
    def _():
        i = lax.broadcasted_iota(jnp.int32, (dk, dk), 0)
        j = lax.broadcasted_iota(jnp.int32, (dk, dk), 1)
        diff = (i - j).astype(F32)
        scale = dk ** -0.5
        for hh in range(hb):
            lf = log_gamma(hh, 0)
            lb = log_gamma(hh, 1)
            base_f = scale * jnp.exp(lf * diff)
            base_b = scale * jnp.exp(lb * (-diff))
            diag = jnp.where(diff >= 0, base_f, 0.0) + jnp.where(diff <= 0, base_b, 0.0)
            for bi in range(seq // dk):
                for bj in range(seq // dk):
                    if bi == bj:
                        blk = diag
                    elif bi > bj:
                        blk = base_f * jnp.exp(lf * float(dk * (bi - bj)))
                    else:
                        blk = base_b * jnp.exp(lb * float(dk * (bj - bi)))
                    d_ref[hh, bi * dk:(bi + 1) * dk, bj * dk:(bj + 1) * dk] = blk
            if want_state:
                t = lax.broadcasted_iota(jnp.int32, (seq, dk), 0).astype(F32)
                wt_ref[hh, 0] = scale * jnp.exp(lf * (seq - 1.0 - t))
                wt_ref[hh, 1] = scale * jnp.exp(lb * t)

    pos_f = lax.broadcasted_iota(jnp.int32, (seq, dk), 0).astype(F32)
    for hh in range(hb):
        sl = slice(hh * dk, (hh + 1) * dk)
        qb = q_ref[:, sl]
        kb16 = k_ref[:, sl]
        vb = v_ref[:, sl]
        if use_rope:
            cos = cos_ref[...]
            sin = sin_ref[...]
            q = qb.astype(F32)
            k = kb16.astype(F32)
            qb = (q * cos + pltpu.roll(q, dk // 2, 1) * sin).astype(BF16)
            kb16 = (k * cos + pltpu.roll(k, dk // 2, 1) * sin).astype(BF16)
        s = lax.dot_general(qb, kb16, (((1,), (1,)), ((), ())), preferred_element_type=F32)
        o = _dot((s * d_ref[hh]).astype(BF16), vb)
        lf = log_gamma(hh, 0)
        lb = log_gamma(hh, 1)
        if use_state:
            o = o + _dot(qb, s0_ref[0, hh].astype(BF16)) * jnp.exp(lf * (pos_f + 1.0))
            o = o + _dot(qb, s0_ref[1, hh].astype(BF16)) * jnp.exp(lb * (seq - pos_f))
        if want_state:
            k = kb16.astype(F32)
            kf = (k * wt_ref[hh, 0]).astype(BF16)
            kb = (k * wt_ref[hh, 1]).astype(BF16)
            tn = (((0,), (0,)), ((), ()))
            st_ref[0, hh] = lax.dot_general(kf, vb, tn, preferred_element_type=F32)
            st_ref[1, hh] = lax.dot_general(kb, vb, tn, preferred_element_type=F32)
        mu = jnp.mean(o, axis=-1, keepdims=True)
        oc = o - mu
        var = jnp.mean(oc * oc, axis=-1, keepdims=True)
        y = (oc * lax.rsqrt(var + GN_EPS)) * gn_ref[:, sl] * g_ref[:, sl].astype(F32)
        o_ref[:, sl] = y.astype(BF16)


def _retention(proj, batch, seq, col0, n_heads, dk, decay_logit, ret_gn, rope, state0, want_state):
    d_ret = n_heads * dk
    hb = n_heads if seq <= 256 else min(n_heads, 2)
    bw = hb * dk
    nhb = n_heads // hb
    lg = jnp.broadcast_to(decay_logit.T[:, :, None, None], (n_heads, 2, 1, V7X_LANES))
    act = lambda part: pl.BlockSpec((seq, bw), lambda h, b: (b, (col0 + part * d_ret) // bw + h))
    in_specs = [act(0), act(1), act(2), act(3),
                pl.BlockSpec((hb, 2, 1, V7X_LANES), lambda h, b: (h, 0, 0, 0)),
                pl.BlockSpec((1, bw), lambda h, b: (0, h))]
    args = [proj, proj, proj, proj, lg, ret_gn.reshape(1, d_ret)]
    if rope is not None:
        in_specs += [pl.BlockSpec((seq, dk), lambda h, b: (0, 0))] * 2
        args += list(rope)
    if state0 is not None:
        in_specs.append(pl.BlockSpec((None, 2, hb, dk, dk), lambda h, b: (b, 0, h, 0, 0)))
        args.append(state0)
    out_specs = [pl.BlockSpec((seq, bw), lambda h, b: (b, h))]
    out_shape = [jax.ShapeDtypeStruct((batch * seq, d_ret), BF16)]
    if want_state:
        out_specs.append(pl.BlockSpec((None, 2, hb, dk, dk), lambda h, b: (b, 0, h, 0, 0)))
        out_shape.append(jax.ShapeDtypeStruct((batch, 2, n_heads, dk, dk), F32))
    body = functools.partial(_retention_kernel, seq=seq, hb=hb, dk=dk, use_rope=rope is not None,
                             use_state=state0 is not None, want_state=want_state)
    return pl.pallas_call(
        body,
        grid=(nhb, batch),
        in_specs=in_specs,
        out_specs=out_specs,
        out_shape=out_shape,
        scratch_shapes=[pltpu.VMEM((hb, seq, seq), F32)]
        + ([pltpu.VMEM((hb, 2, seq, dk), F32)] if want_state else []),
        compiler_params=_cparams("parallel", "arbitrary"),
        name=f"retention_{seq}",
    )(*args)


def _merge_kernel(*refs, nblk):
    gate_refs = refs[:2 * nblk]
    (yhy_ref, yret_ref, x_ref, wbh_ref, wbr_ref, wo_ref, gpost_ref, gm_ref, gpre_ref, sc_ref, sh_ref,
     o_ref, h_ref) = refs[2 * nblk:]
    a = _dot(yhy_ref[...], wbh_ref[...])
    b = _dot(yret_ref[...], wbr_ref[...])
    wblk = gate_refs[0].shape[1]
    parts = []
    for kk in range(nblk):
        sl = slice(kk * wblk, (kk + 1) * wblk)
        g_hy = gate_refs[kk][...].astype(F32)
        g_ret = gate_refs[nblk + kk][...].astype(F32)
        parts.append((g_hy * a[:, sl] + g_ret * b[:, sl]).astype(BF16))
    merged = parts[0] if nblk == 1 else jnp.concatenate(parts, axis=1)
    half = merged.shape[0] // 2
    for c in range(2):
        rs = slice(c * half, (c + 1) * half)
        out = _dot(merged[rs], wo_ref[...])
        x1 = x_ref[rs] + _rms_scale(out, gm_ref[...] * gpost_ref[...])
        o_ref[rs] = x1
        h_ref[rs] = (_rms_scale(x1, gpre_ref[...] * (1.0 + sc_ref[...])) + sh_ref[...]).astype(BF16)


def _merge(proj, col0, y_hy, y_ret, x, w_br_hy, w_br_ret, w_out, g_post, gate_m, g_pre_f, scale_f, shift_f,
           rows_per_mod, tm):
    m, d = x.shape
    vec = pl.BlockSpec((1, d), lambda i: (0, 0))
    mod_spec = pl.BlockSpec((None, 1, d), lambda i: ((i * tm) // rows_per_mod, 0, 0))
    wblk = math.gcd(col0, d)
    nblk = d // wblk
    gate_spec = lambda kk: pl.BlockSpec((tm, wblk), lambda i: (i, col0 // wblk + kk))
    const = lambda arr: _resident(arr.shape)
    row = lambda width: pl.BlockSpec((tm, width), lambda i: (i, 0))
    return pl.pallas_call(
        functools.partial(_merge_kernel, nblk=nblk),
        grid=(m // tm,),
        in_specs=[gate_spec(kk) for kk in range(2 * nblk)]
        + [row(y_hy.shape[1]), row(y_ret.shape[1]), row(d), const(w_br_hy), const(w_br_ret), const(w_out),
           vec, mod_spec, vec, mod_spec, mod_spec],
        out_specs=[row(d), row(d)],
        out_shape=[jax.ShapeDtypeStruct((m, d), F32), jax.ShapeDtypeStruct((m, d), BF16)],
        compiler_params=_cparams("parallel"),
        name="merge",
    )(*([proj] * (2 * nblk)), y_hy, y_ret, x, w_br_hy, w_br_ret, w_out, g_post, gate_m, g_pre_f, scale_f, shift_f)


def _ffn_kernel(x_hbm, h_ref, gate_ref, gpost_ref, wa_ref, wb_ref, ca_ref, cb_ref, wd_ref, o_ref, act_ref, x_ref,
                x_sem, *, seq):
    j = pl.program_id(1)
    nf = pl.num_programs(1) - 1
    tm = x_ref.shape[0]
    tf = wa_ref.shape[1]

    def x_copy():
        return pltpu.make_async_copy(x_hbm.at[pl.ds(pl.program_id(0) * tm, tm)], x_ref, x_sem)

    def up(slot):
        pos = lax.broadcasted_iota(jnp.int32, (tm, tf), 0) % seq
        first = pos == 0
        last = pos == seq - 1
        h = h_ref[...]
        a = _dwconv3_rows(_dot(h, wa_ref[...]), ca_ref[...], first, last)
        b = _dwconv3_rows(_dot(h, wb_ref[...]), cb_ref[...], first, last)
        c1 = math.sqrt(2.0 / math.pi)
        half = 0.5 * a
        gelu = half + half * jnp.tanh(a * (c1 + (c1 * 0.044715) * (a * a)))
        act_ref[slot] = (gelu * b).astype(BF16)

    def down(slot):
        return _dot(act_ref[slot], wd_ref[...])

    @pl.when(j == 0)
    def _():
        x_copy().start()
        up(0)

    @pl.when(j == 1)
    def _():
        o_ref[...] = down(0)
        up(1)

    @pl.when((j > 1) & (j < nf))
    def _():
        slot = j % 2
        o_ref[...] += down(1 - slot)
        up(slot)

    @pl.when(j == nf)
    def _():
        x_copy().wait()
        slot = (nf - 1) % 2
        half = tm // 2
        for c in range(2):
            rs = slice(c * half, (c + 1) * half)
            f = o_ref[rs] + _dot(act_ref[slot, rs], wd_ref[...])
            o_ref[rs] = x_ref[rs] + _rms_scale(f, gate_ref[...] * gpost_ref[...])


def _ffn(x, h, seq, gate, g_post, w_up, conv_w, w_down, rows_per_mod, tm):
    m, d = x.shape
    d_ff = w_down.shape[0]
    tf = _tile(d_ff, 512)
    nf = d_ff // tf
    assert nf >= 2, (d_ff, tf)
    up_blk = lambda j: jnp.minimum(j, nf - 1)
    down_blk = lambda j: jnp.maximum(j - 1, 0)
    return pl.pallas_call(
        functools.partial(_ffn_kernel, seq=seq),
        grid=(m // tm, nf + 1),
        in_specs=[pl.BlockSpec(memory_space=pl.ANY), pl.BlockSpec((tm, d), lambda i, j: (i, 0)),
                  pl.BlockSpec((None, 1, d), lambda i, j: ((i * tm) // rows_per_mod, 0, 0)),
                  pl.BlockSpec((1, d), lambda i, j: (0, 0)),
                  pl.BlockSpec((d, tf), lambda i, j: (0, up_blk(j))),
                  pl.BlockSpec((d, tf), lambda i, j: (0, nf + up_blk(j))),
                  pl.BlockSpec((3, tf), lambda i, j: (0, up_blk(j))),
                  pl.BlockSpec((3, tf), lambda i, j: (0, nf + up_blk(j))),
                  pl.BlockSpec((tf, d), lambda i, j: (down_blk(j), 0))],
        out_specs=pl.BlockSpec((tm, d), lambda i, j: (i, 0)),
        out_shape=jax.ShapeDtypeStruct((m, d), F32),
        scratch_shapes=[pltpu.VMEM((2, tm, tf), BF16), pltpu.VMEM((tm, d), F32), pltpu.SemaphoreType.DMA(())],
        compiler_params=_cparams("parallel", "arbitrary"),
        name="ffn",
    )(x, h, gate, g_post, w_up, w_up, conv_w, conv_w, w_down)


class _Group:
    def __init__(self, x3, mod):
        self.batch, self.seq, self.d = x3.shape
        self.m = self.batch * self.seq
        self.x = x3.reshape(self.m, self.d)
        per_seq_mod = mod.shape[0] != 1
        self.rows_per_mod = self.seq if per_seq_mod else self.m
        self.tm = _row_tile(self.m, self.seq, per_seq_mod, 1024)
        self.tm_merge = _tile(self.tm, 512, unit=V7X_SUBLANES)
        (self.shift_m, self.scale_m, self.gate_m,
         self.shift_f, self.scale_f, self.gate_f) = (mod[:, i][:, None, :] for i in range(6))

    def in_proj(self, p, w_in, cast=()):
        return _in_proj(self.x, p["g_pre_m"], self.scale_m, self.shift_m, w_in, p["hy_short_w"], self.seq,
                        p["d_hy"], p["d_ret"], self.rows_per_mod, self.tm, cast)


def _mix_and_ffn(grp, proj, rope, state0, want_state, p):
    d_hy, d_ret, n_heads, dk = p["d_hy"], p["d_ret"], p["n_heads"], p["dk"]
    tables = _hyena_filters(grp.seq, p["filter_params"])
    late = [k for k in ("ffn_w_down", "w_out", "w_br_hy", "w_br_ret") if p[k].dtype != BF16]
    y_hy, converted = _hyena(proj, grp.batch, grp.seq, tables, d_hy, tuple(p[k] for k in late))
    p.update(zip(late, converted))
    ret = _retention(proj, grp.batch, grp.seq, 3 * d_hy, n_heads, dk, p["ret_decay_logit"], p["ret_gn"], rope,
                     state0, want_state)
    x, h_ffn = _merge(proj, 3 * d_hy + 4 * d_ret, y_hy, ret[0], grp.x, p["w_br_hy"], p["w_br_ret"], p["w_out"],
                      p["g_post_m"], grp.gate_m, p["g_pre_f"], grp.scale_f, grp.shift_f, grp.rows_per_mod,
                      grp.tm_merge)
    x = _ffn(x, h_ffn, grp.seq, grp.gate_f, p["g_post_f"], p["ffn_w_up"], p["ffn_conv"], p["ffn_w_down"],
             grp.rows_per_mod, grp.tm)
    return x.reshape(grp.batch, grp.seq, grp.d), (ret[1] if want_state else None)


def kernel(x_prompt, x_sample, state_ret, c, c_ctx, w_ada, b_ada, norm_pre_mix, norm_post_mix, norm_pre_ffn,
           norm_post_ffn, w_in, hy_short_w, hy_w1, hy_b1, hy_w2, hy_b2, hy_w3, hy_b3, hy_freq, hy_decay, hy_bias,
           ret_decay_logit, ret_gn, w_br_hy, w_br_ret, w_out, ffn_w_up, ffn_conv, ffn_w_down):
    depth = w_in.shape[0]
    d = x_prompt.shape[-1]
    n_dec = x_sample.shape[0]
    n_heads, dk = state_ret.shape[3], state_ret.shape[4]
    d_hy = hy_bias.shape[-1]
    assert 1 + n_dec <= V7X_SUBLANES, n_dec
    cc = jnp.concatenate([c_ctx[None], c, jnp.zeros((V7X_SUBLANES - 1 - n_dec, d), F32)], axis=0)
    rope = _rope_tables(x_sample.shape[1], dk)
    x_p, x_s = x_prompt, x_sample
    states = []
    for l in range(depth):
        p = dict(
            d_hy=d_hy, d_ret=n_heads * dk, n_heads=n_heads, dk=dk,
            g_pre_m=norm_pre_mix[l][None], g_post_m=norm_post_mix[l][None],
            g_pre_f=norm_pre_ffn[l][None], g_post_f=norm_post_ffn[l][None],
            hy_short_w=hy_short_w[l],
            filter_params=_filter_params(hy_w1[l], hy_b1[l], hy_w2[l], hy_b2[l], hy_w3[l], hy_b3[l], hy_freq[l],
                                         hy_decay[l], hy_bias[l]),
            ret_decay_logit=ret_decay_logit[l], ret_gn=ret_gn[l], ffn_conv=ffn_conv[l],
        )
        mod = _ada_mod(cc, w_ada[l], b_ada[l][None]).reshape(V7X_SUBLANES, 6, d)
        ctx = _Group(x_p, mod[0:1])
        lat = _Group(x_s, mod[1:1 + n_dec])
        proj_s, w_in_bf = lat.in_proj(p, w_in[l])
        proj_p, p["ffn_w_up"] = ctx.in_proj(p, w_in_bf, (ffn_w_up[l],))
        p.update(ffn_w_down=ffn_w_down[l], w_out=w_out[l], w_br_hy=w_br_hy[l], w_br_ret=w_br_ret[l])
        x_p, st = _mix_and_ffn(ctx, proj_p, None, None, True, p)
        x_s, _ = _mix_and_ffn(lat, proj_s, rope, state_ret[:, l], False, p)
        states.append(st)
    return x_p, x_s, jnp.stack(states, axis=1)
```

```python
import functools
import math

import jax
import jax.numpy as jnp
import numpy as np
from jax import lax
from jax.experimental import pallas as pl
from jax.experimental.pallas import tpu as pltpu

F32 = jnp.float32
BF16 = jnp.bfloat16

RMS_EPS = 1e-6
GN_EPS = 1e-5
FILTER_EPS = 1e-6
HY_BANDS = 16
GRID_W = 64
ROPE_BASE = 10000.0

V7X_VMEM_LIMIT_BYTES = 58 * 1024 * 1024
V7X_LANES = 128
V7X_SUBLANES = 8


def _cparams(*sem):
    return pltpu.CompilerParams(dimension_semantics=sem, vmem_limit_bytes=V7X_VMEM_LIMIT_BYTES)


def _tile(n, target, unit=V7X_LANES):
    if n <= target:
        return n
    best = unit
    for t in range(unit, target + 1, unit):
        if n % t == 0:
            best = t
    assert n % best == 0, (n, target, unit)
    return best


def _row_tile(m, seq, per_seq_mod, target):
    if per_seq_mod or seq >= target:
        return seq
    return seq * _tile(m // seq, target // seq, unit=1)


def _resident(shape):
    return pl.BlockSpec(shape, lambda *_: (0,) * len(shape), pipeline_mode=pl.Buffered(1))


def _sigmoid(x):
    return 1.0 / (1.0 + jnp.exp(-x))


def _sigmoid_tanh(x):
    return 0.5 * jnp.tanh(0.5 * x) + 0.5


def _rms_scale(x, g):
    ms = jnp.mean(x * x, axis=-1, keepdims=True)
    return (x * lax.rsqrt(ms + RMS_EPS)) * g


def _dot(a, b):
    return jnp.dot(a, b, preferred_element_type=F32)


def _dot_hilo(a, b):
    a_hi = a.astype(BF16)
    a_lo = (a - a_hi.astype(F32)).astype(BF16)
    b_hi = b.astype(BF16)
    b_lo = (b - b_hi.astype(F32)).astype(BF16)
    return _dot(a_hi, b_hi) + (_dot(a_lo, b_hi) + _dot(a_hi, b_lo))


def _split_bf16(table):
    hi = table.astype(BF16)
    lo = (table - hi.astype(np.float64)).astype(BF16)
    return jnp.asarray(hi), jnp.asarray(lo)


def _dot_split(a_hi_ref, a_lo_ref, b):
    b_hi = b.astype(BF16)
    b_lo = (b - b_hi.astype(F32)).astype(BF16)
    a_hi = a_hi_ref[...]
    return _dot(a_hi, b_hi) + (_dot(a_lo_ref[...], b_hi) + _dot(a_hi, b_lo))


def _dwconv3_rows(x, w, first, last):
    rows = x.shape[0]
    prev = jnp.where(first, 0.0, pltpu.roll(x, 1, 0))
    nxt = jnp.where(last, 0.0, pltpu.roll(x, rows - 1, 0))
    return prev * w[0:1] + x * w[1:2] + nxt * w[2:3]


def _ada_kernel(cc_ref, w_ref, b_ref, o_ref):
    cc = cc_ref[...]
    s = cc * _sigmoid(cc)
    o_ref[...] = _dot(s.astype(BF16), w_ref[...].astype(BF16)) + b_ref[...]


def _ada_mod(cc, w, b):
    d, n = w.shape
    tn = _tile(n, 1024)
    return pl.pallas_call(
        _ada_kernel,
        grid=(n // tn,),
        in_specs=[pl.BlockSpec((V7X_SUBLANES, d), lambda j: (0, 0)),
                  pl.BlockSpec((d, tn), lambda j: (0, j)),
                  pl.BlockSpec((1, tn), lambda j: (0, j))],
        out_specs=pl.BlockSpec((V7X_SUBLANES, tn), lambda j: (0, j)),
        out_shape=jax.ShapeDtypeStruct((V7X_SUBLANES, n), F32),
        compiler_params=_cparams("parallel"),
        name="ada_mod",
    )(cc, w, b)


def _in_proj_kernel(*refs, seq, n_hy, n_qkv, n_g, n_cast, emit_w):
    x_ref, g_ref, sc_ref, sh_ref, w_ref, cw_ref = refs[:6]
    cast_in = refs[6:6 + n_cast]
    o_ref = refs[6 + n_cast]
    pos = 7 + n_cast
    if emit_w:
        wbf_ref = refs[pos]
        pos += 1
    cast_out = refs[pos:pos + n_cast]
    h_ref = refs[pos + n_cast]
    j = pl.program_id(1)

    @pl.when(j == 0)
    def _():
        n_mod = sc_ref.shape[0]
        rows = x_ref.shape[0] // n_mod
        for b in range(n_mod):
            rs = slice(b * rows, (b + 1) * rows)
            h = _rms_scale(x_ref[rs], g_ref[...] * (1.0 + sc_ref[b])) + sh_ref[b]
            h_ref[rs] = h.astype(BF16)

    def tile():
        w = w_ref[...]
        if emit_w:
            w = w.astype(BF16)
            wbf_ref[...] = w
        for src, dst in zip(cast_in, cast_out):
            dst[...] = src[...].astype(BF16)
        return _dot(h_ref[...], w)

    @pl.when(j < n_hy)
    def _():
        pos = lax.broadcasted_iota(jnp.int32, o_ref.shape, 0) % seq
        o_ref[...] = _dwconv3_rows(tile(), cw_ref[...], pos == 0, pos == seq - 1).astype(BF16)

    @pl.when((j >= n_hy) & (j < n_hy + n_qkv))
    def _():
        o_ref[...] = tile().astype(BF16)

    @pl.when((j >= n_hy + n_qkv) & (j < n_hy + n_qkv + n_g))
    def _():
        p = tile()
        o_ref[...] = (p * _sigmoid_tanh(p)).astype(BF16)

    @pl.when(j >= n_hy + n_qkv + n_g)
    def _():
        o_ref[...] = _sigmoid_tanh(tile()).astype(BF16)


V7X_BF16_ROW_TILE = 2 * V7X_SUBLANES


def _cast_rows(arr, n_steps):
    rows = arr.shape[0]
    for blk in range(V7X_BF16_ROW_TILE, rows + 1, V7X_BF16_ROW_TILE):
        if rows % blk == 0 and rows // blk <= n_steps:
            return blk
    return None


def _in_proj(x, g, scale, shift, w, short_w, seq, d_hy, d_ret, rows_per_mod, tm, cast=()):
    m, d = x.shape
    n = w.shape[1]
    emit_w = w.dtype != BF16
    if emit_w:
        tm = m
    tn = _tile(math.gcd(3 * d_hy, d_ret, 2 * d), 512 if emit_w else 1024)
    n_hy, n_qkv, n_g = 3 * d_hy // tn, 3 * d_ret // tn, d_ret // tn
    n_col = n // tn
    n_mod = max(1, tm // rows_per_mod)
    mod_spec = pl.BlockSpec((n_mod, 1, d), lambda i, j: ((i * tm) // rows_per_mod // n_mod, 0, 0))
    blks = [_cast_rows(a, (m // tm) * n_col) for a in cast]
    assert all(b is not None for b in blks), [a.shape for a in cast]
    cast_specs = [pl.BlockSpec((b, a.shape[1]), lambda i, j, nb=a.shape[0] // b: (jnp.minimum(i * n_col + j, nb - 1), 0))
                  for a, b in zip(cast, blks)]
    out_specs = [pl.BlockSpec((tm, tn), lambda i, j: (i, j))]
    out_shape = [jax.ShapeDtypeStruct((m, n), BF16)]
    if emit_w:
        out_specs.append(pl.BlockSpec((d, tn), lambda i, j: (0, j)))
        out_shape.append(jax.ShapeDtypeStruct(w.shape, BF16))
    out_specs += cast_specs
    out_shape += [jax.ShapeDtypeStruct(a.shape, BF16) for a in cast]
    return pl.pallas_call(
        functools.partial(_in_proj_kernel, seq=seq, n_hy=n_hy, n_qkv=n_qkv, n_g=n_g, n_cast=len(cast),
                          emit_w=emit_w),
        grid=(m // tm, n_col),
        in_specs=[pl.BlockSpec((tm, d), lambda i, j: (i, 0), pipeline_mode=pl.Buffered(1) if emit_w else None),
                  pl.BlockSpec((1, d), lambda i, j: (0, 0)),
                  mod_spec, mod_spec,
                  pl.BlockSpec((d, tn), lambda i, j: (0, j)),
                  pl.BlockSpec((3, tn), lambda i, j: (0, jnp.minimum(j, n_hy - 1)))] + cast_specs,
        out_specs=out_specs,
        out_shape=out_shape,
        scratch_shapes=[pltpu.VMEM((tm, d), BF16)],
        compiler_params=_cparams("arbitrary", "arbitrary"),
        name="in_proj",
    )(x, g, scale, shift, w, short_w, *cast)


def _dft_tables(seq):
    n_fft = 2 * seq
    idx = np.arange(seq)
    ang = 2.0 * np.pi * ((idx[:, None] * idx[None, :]) % n_fft) / n_fft
    cos = np.cos(ang)
    msin = -np.sin(ang)
    sign = np.where(idx % 2 == 0, 1.0, -1.0)
    msin[0, :] = sign
    fwd = np.concatenate([cos, msin], axis=0)
    wgt = np.full((seq,), 2.0 / n_fft)
    wgt[0] = 1.0 / n_fft
    inv_re = cos.T * wgt[None, :]
    inv_im = msin.T * wgt[None, :]
    inv_im[:, 0] = sign / n_fft
    inv = np.concatenate([inv_re, inv_im], axis=1)
    return fwd, inv


def _filter_feats(seq):
    n = np.arange(seq, dtype=np.float64)
    t = n / seq
    f = np.linspace(1e-4, HY_BANDS - 1, HY_BANDS)
    w = 2.0 * math.pi * n / seq
    z = np.concatenate([t[:, None], np.cos(w[:, None] * f), np.sin(w[:, None] * f)], axis=-1)
    out = np.zeros((seq, V7X_LANES), np.float32)
    out[:, :z.shape[1]] = z
    return out


def _filter_kernel(z_ref, w1_ref, b1_ref, w2_ref, b2_ref, fr_ref, w3f_ref, w3b_ref, b3f_ref, b3b_ref,
                   decf_ref, decb_ref, bias_ref, chi_ref, clo_ref, shi_ref, slo_ref, kr_ref, kiz_ref, krn_ref, h2_ref):
    @pl.when(pl.program_id(0) == 0)
    def _():
        h1 = jnp.sin(fr_ref[0:1, :] * (_dot_hilo(z_ref[...], w1_ref[...]) + b1_ref[...]))
        h2_ref[...] = jnp.sin(fr_ref[1:2, :] * (_dot_hilo(h1, w2_ref[...]) + b2_ref[...]))

    h2 = h2_ref[...]
    seq, cw = kr_ref.shape
    t = z_ref[:, 0:1]
    row = lax.broadcasted_iota(jnp.int32, (seq, cw), 0)
    hf = (_dot_hilo(h2, w3f_ref[...]) + b3f_ref[...]) * jnp.exp(-t * jnp.abs(decf_ref[...]))
    hb = (_dot_hilo(h2, w3b_ref[...]) + b3b_ref[...]) * jnp.exp(-t * jnp.abs(decb_ref[...]))
    hb = jnp.where(row == 0, 0.0, hb)
    norm = (jnp.sum(jnp.abs(hf), axis=0, keepdims=True)
            + jnp.sum(jnp.abs(hb), axis=0, keepdims=True) + FILTER_EPS)
    inv = 1.0 / norm
    even = (hf + hb) * inv
    odd = (hf - hb) * inv
    bias = bias_ref[...]
    kr = _dot_split(chi_ref, clo_ref, even) + bias
    ki = _dot_split(shi_ref, slo_ref, odd)
    nyq = jnp.sum(jnp.where((row & 1) == 0, even, -even), axis=0, keepdims=True) + bias
    kr_ref[...] = kr.astype(BF16)
    kiz_ref[...] = jnp.where(row == 0, 0.0, ki).astype(BF16)
    krn_ref[...] = jnp.where(row == 0, nyq, kr).astype(BF16)


def _filter_params(w1, b1, w2, b2, w3, b3, freq, decay, hy_bias):
    fh = w1.shape[1]
    pad = V7X_LANES
    ncol2 = w3.shape[1]
    return (jnp.zeros((pad, pad), F32).at[:w1.shape[0], :fh].set(w1),
            jnp.zeros((1, pad), F32).at[0, :fh].set(b1),
            jnp.zeros((pad, pad), F32).at[:fh, :fh].set(w2),
            jnp.zeros((1, pad), F32).at[0, :fh].set(b2),
            jnp.zeros((2, pad), F32).at[:, :fh].set(freq),
            jnp.zeros((pad, ncol2), F32).at[:fh].set(w3),
            b3.reshape(1, ncol2), decay.reshape(1, ncol2), hy_bias.reshape(1, ncol2 // 2))


def _hyena_filters(seq, params):
    w1p, b1p, w2p, b2p, frp, w3p, b3r, dec, bias = params
    pad = V7X_LANES
    ncol = bias.shape[1]
    fwd, _ = _dft_tables(seq)
    chi, clo = _split_bf16(fwd[:seq])
    shi, slo = _split_bf16(fwd[seq:])
    z = jnp.asarray(_filter_feats(seq))
    cw = _tile(ncol, 512)
    nb = ncol // cw
    full = lambda shape: pl.BlockSpec(shape, lambda j: (0, 0))
    colf = lambda rows: pl.BlockSpec((rows, cw), lambda j: (0, j))
    colb = lambda rows: pl.BlockSpec((rows, cw), lambda j: (0, nb + j))
    out = jax.ShapeDtypeStruct((seq, ncol), BF16)
    return pl.pallas_call(
        _filter_kernel,
        grid=(nb,),
        in_specs=[full((seq, pad)), full((pad, pad)), full((1, pad)), full((pad, pad)), full((1, pad)),
                  full((2, pad)), colf(pad), colb(pad), colf(1), colb(1), colf(1), colb(1), colf(1),
                  _resident((seq, seq)), _resident((seq, seq)), _resident((seq, seq)), _resident((seq, seq))],
        out_specs=[colf(seq), colf(seq), colf(seq)],
        out_shape=[out, out, out],
        scratch_shapes=[pltpu.VMEM((seq, pad), F32)],
        compiler_params=_cparams("arbitrary"),
        name=f"hy_filter_{seq}",
    )(z, w1p, b1p, w2p, b2p, frp, w3p, w3p, b3r, b3r, dec, dec, bias, chi, clo, shi, slo)


def _hyena_kernel(x1_ref, x2_ref, v_ref, kr0_ref, kiz0_ref, krn0_ref, kr1_ref, kiz1_ref, krn1_ref, f_ref, g_ref,
                  o_ref, *, seq):
    def long_conv(u, kr_ref, kiz_ref, krn_ref):
        spec = _dot(f_ref[...], u.astype(BF16))
        ur = spec[:seq].astype(BF16)
        ui = spec[seq:].astype(BF16)
        kiz = kiz_ref[...]
        yr = ur * kr_ref[...] - ui * kiz
        yi = ur * kiz + ui * krn_ref[...]
        return _dot(g_ref[:, :seq], yr) + _dot(g_ref[:, seq:], yi)

    for s in range(o_ref.shape[0] // seq):
        rs = slice(s * seq, (s + 1) * seq)
        z = x1_ref[rs].astype(F32) * long_conv(v_ref[rs], kr0_ref, kiz0_ref, krn0_ref)
        o_ref[rs] = (x2_ref[rs].astype(F32) * long_conv(z, kr1_ref, kiz1_ref, krn1_ref)).astype(BF16)


def _hyena(proj, batch, seq, tables, d_hy):
    kr, kiz, krn = tables
    cw = _tile(d_hy, 1024 if seq <= 256 else 512)
    nb = d_hy // cw
    fwd, inv = _dft_tables(seq)
    fmat = jnp.asarray(fwd, F32).astype(BF16)
    gmat = jnp.asarray(inv, F32).astype(BF16)
    bs = next(b for b in (4, 2, 1) if batch % b == 0) if seq <= 256 else 1
    col = lambda rows, off: pl.BlockSpec((rows, cw), lambda j, b: (0, off * nb + j))
    act = lambda off: pl.BlockSpec((bs * seq, cw), lambda j, b: (b, off * nb + j))
    return pl.pallas_call(
        functools.partial(_hyena_kernel, seq=seq),
        grid=(nb, batch // bs),
        in_specs=[act(0), act(1), act(2),
                  col(seq, 0), col(seq, 0), col(seq, 0), col(seq, 1), col(seq, 1), col(seq, 1),
                  _resident((2 * seq, seq)), _resident((seq, 2 * seq))],
        out_specs=pl.BlockSpec((bs * seq, cw), lambda j, b: (b, j)),
        out_shape=jax.ShapeDtypeStruct((batch * seq, d_hy), BF16),
        compiler_params=_cparams("parallel", "parallel"),
        name=f"hyena_{seq}",
    )(proj, proj, proj, kr, kiz, krn, kr, kiz, krn, fmat, gmat)


def _rope_tables(seq, dk):
    rows = seq // GRID_W
    row = jnp.repeat(jnp.arange(rows), GRID_W).astype(F32)
    col = jnp.tile(jnp.arange(GRID_W), rows).astype(F32)
    nfreq = dk // 4
    inv = ROPE_BASE ** (-jnp.arange(nfreq, dtype=F32) / nfreq)
    ang = jnp.concatenate([row[:, None] * inv, col[:, None] * inv], axis=-1)
    cos, sin = jnp.cos(ang), jnp.sin(ang)
    return jnp.concatenate([cos, cos], axis=-1), jnp.concatenate([-sin, sin], axis=-1)


def _retention_kernel(*refs, seq, bs, hb, dk, use_rope, use_state, want_state):
    refs = list(refs)
    q_ref, k_ref, v_ref, g_ref, lg_ref, gn_ref = refs[:6]
    pos = 6
    if use_rope:
        cos_ref, sin_ref = refs[pos:pos + 2]
        pos += 2
    if use_state:
        s0_ref = refs[pos]
        pos += 1
    o_ref = refs[pos]
    pos += 1
    if want_state:
        st_ref = refs[pos]
        pos += 1
    d_ref = refs[pos]
    if want_state:
        wt_ref = refs[pos + 1]

    def log_gamma(hh, direction):
        return jnp.log(_sigmoid(lg_ref[hh, direction]))[:, 0:1]

    @pl.when(pl.program_id(1) == 0)
    def _():
        i = lax.broadcasted_iota(jnp.int32, (dk, dk), 0)
        j = lax.broadcasted_iota(jnp.int32, (dk, dk), 1)
        diff = (i - j).astype(F32)
        scale = dk ** -0.5
        for hh in range(hb):
            lf = log_gamma(hh, 0)
            lb = log_gamma(hh, 1)
            base_f = scale * jnp.exp(lf * diff)
            base_b = scale * jnp.exp(lb * (-diff))
            diag = jnp.where(diff >= 0, base_f, 0.0) + jnp.where(diff <= 0, base_b, 0.0)
            for bi in range(seq // dk):
                for bj in range(seq // dk):
                    if bi == bj:
                        blk = diag
                    elif bi > bj:
                        blk = base_f * jnp.exp(lf * float(dk * (bi - bj)))
                    else:
                        blk = base_b * jnp.exp(lb * float(dk * (bj - bi)))
                    d_ref[hh, bi * dk:(bi + 1) * dk, bj * dk:(bj + 1) * dk] = blk
            if want_state:
                t = lax.broadcasted_iota(jnp.int32, (seq, dk), 0).astype(F32)
                wt_ref[hh, 0] = scale * jnp.exp(lf * (seq - 1.0 - t))
                wt_ref[hh, 1] = scale * jnp.exp(lb * t)

    pos_f = lax.broadcasted_iota(jnp.int32, (seq, dk), 0).astype(F32)
    for sq, hh in [(sq, hh) for sq in range(bs) for hh in range(hb)]:
        rs = slice(sq * seq, (sq + 1) * seq)
        sl = slice(hh * dk, (hh + 1) * dk)
        qb = q_ref[rs, sl]
        kb16 = k_ref[rs, sl]
        vb = v_ref[rs, sl]
        if use_rope:
            cos = cos_ref[...]
            sin = sin_ref[...]
            q = qb.astype(F32)
            k = kb16.astype(F32)
            qb = (q * cos + pltpu.roll(q, dk // 2, 1) * sin).astype(BF16)
            kb16 = (k * cos + pltpu.roll(k, dk // 2, 1) * sin).astype(BF16)
        s = lax.dot_general(qb, kb16, (((1,), (1,)), ((), ())), preferred_element_type=F32)
        o = _dot((s * d_ref[hh]).astype(BF16), vb)
        lf = log_gamma(hh, 0)
        lb = log_gamma(hh, 1)
        if use_state:
            o = o + _dot(qb, s0_ref[sq, 0, hh].astype(BF16)) * jnp.exp(lf * (pos_f + 1.0))
            o = o + _dot(qb, s0_ref[sq, 1, hh].astype(BF16)) * jnp.exp(lb * (seq - pos_f))
        if want_state:
            k = kb16.astype(F32)
            kf = (k * wt_ref[hh, 0]).astype(BF16)
            kb = (k * wt_ref[hh, 1]).astype(BF16)
            tn = (((0,), (0,)), ((), ()))
            st_ref[sq, 0, hh] = lax.dot_general(kf, vb, tn, preferred_element_type=F32)
            st_ref[sq, 1, hh] = lax.dot_general(kb, vb, tn, preferred_element_type=F32)
        mu = jnp.mean(o, axis=-1, keepdims=True)
        oc = o - mu
        var = jnp.mean(oc * oc, axis=-1, keepdims=True)
        y = (oc * lax.rsqrt(var + GN_EPS)) * gn_ref[:, sl] * g_ref[rs, sl].astype(F32)
        o_ref[rs, sl] = y.astype(BF16)


def _retention(proj, batch, seq, col0, n_heads, dk, decay_logit, ret_gn, rope, state0, want_state):
    d_ret = n_heads * dk
    hb = n_heads if seq <= 256 else min(n_heads, 2)
    bs = 2 if batch % 2 == 0 and seq <= 256 else 1
    bw = hb * dk
    nhb = n_heads // hb
    lg = jnp.broadcast_to(decay_logit.T[:, :, None, None], (n_heads, 2, 1, V7X_LANES))
    act = lambda part: pl.BlockSpec((bs * seq, bw), lambda h, b: (b, (col0 + part * d_ret) // bw + h))
    in_specs = [act(0), act(1), act(2), act(3),
                pl.BlockSpec((hb, 2, 1, V7X_LANES), lambda h, b: (h, 0, 0, 0)),
                pl.BlockSpec((1, bw), lambda h, b: (0, h))]
    args = [proj, proj, proj, proj, lg, ret_gn.reshape(1, d_ret)]
    if rope is not None:
        in_specs += [pl.BlockSpec((seq, dk), lambda h, b: (0, 0))] * 2
        args += list(rope)
    if state0 is not None:
        in_specs.append(pl.BlockSpec((bs, 2, hb, dk, dk), lambda h, b: (b, 0, h, 0, 0)))
        args.append(state0)
    out_specs = [pl.BlockSpec((bs * seq, bw), lambda h, b: (b, h))]
    out_shape = [jax.ShapeDtypeStruct((batch * seq, d_ret), BF16)]
    if want_state:
        out_specs.append(pl.BlockSpec((bs, 2, hb, dk, dk), lambda h, b: (b, 0, h, 0, 0)))
        out_shape.append(jax.ShapeDtypeStruct((batch, 2, n_heads, dk, dk), F32))
    body = functools.partial(_retention_kernel, seq=seq, bs=bs, hb=hb, dk=dk, use_rope=rope is not None,
                             use_state=state0 is not None, want_state=want_state)
    return pl.pallas_call(
        body,
        grid=(nhb, batch // bs),
        in_specs=in_specs,
        out_specs=out_specs,
        out_shape=out_shape,
        scratch_shapes=[pltpu.VMEM((hb, seq, seq), F32)]
        + ([pltpu.VMEM((hb, 2, seq, dk), F32)] if want_state else []),
        compiler_params=_cparams("parallel", "arbitrary"),
        name=f"retention_{seq}",
    )(*args)


def _merge_kernel(*refs, nblk):
    gate_refs = refs[:2 * nblk]
    (yhy_ref, yret_ref, x_ref, wbh_ref, wbr_ref, wo_ref, gpost_ref, gm_ref, gpre_ref, sc_ref, sh_ref,
     o_ref, h_ref) = refs[2 * nblk:]
    a = _dot(yhy_ref[...], wbh_ref[...])
    b = _dot(yret_ref[...], wbr_ref[...])
    wblk = gate_refs[0].shape[1]
    parts = []
    for kk in range(nblk):
        sl = slice(kk * wblk, (kk + 1) * wblk)
        g_hy = gate_refs[kk][...].astype(F32)
        g_ret = gate_refs[nblk + kk][...].astype(F32)
        parts.append((g_hy * a[:, sl] + g_ret * b[:, sl]).astype(BF16))
    merged = parts[0] if nblk == 1 else jnp.concatenate(parts, axis=1)
    half = merged.shape[0] // 2
    for c in range(2):
        rs = slice(c * half, (c + 1) * half)
        out = _dot(merged[rs], wo_ref[...])
        x1 = x_ref[rs] + _rms_scale(out, gm_ref[...] * gpost_ref[...])
        o_ref[rs] = x1
        h_ref[rs] = (_rms_scale(x1, gpre_ref[...] * (1.0 + sc_ref[...])) + sh_ref[...]).astype(BF16)


def _merge(proj, col0, y_hy, y_ret, x, w_br_hy, w_br_ret, w_out, g_post, gate_m, g_pre_f, scale_f, shift_f,
           rows_per_mod, tm):
    m, d = x.shape
    vec = pl.BlockSpec((1, d), lambda i: (0, 0))
    mod_spec = pl.BlockSpec((None, 1, d), lambda i: ((i * tm) // rows_per_mod, 0, 0))
    wblk = math.gcd(col0, d)
    nblk = d // wblk
    gate_spec = lambda kk: pl.BlockSpec((tm, wblk), lambda i: (i, col0 // wblk + kk))
    const = lambda arr: _resident(arr.shape)
    row = lambda width: pl.BlockSpec((tm, width), lambda i: (i, 0))
    return pl.pallas_call(
        functools.partial(_merge_kernel, nblk=nblk),
        grid=(m // tm,),
        in_specs=[gate_spec(kk) for kk in range(2 * nblk)]
        + [row(y_hy.shape[1]), row(y_ret.shape[1]), row(d), const(w_br_hy), const(w_br_ret), const(w_out),
           vec, mod_spec, vec, mod_spec, mod_spec],
        out_specs=[row(d), row(d)],
        out_shape=[jax.ShapeDtypeStruct((m, d), F32), jax.ShapeDtypeStruct((m, d), BF16)],
        compiler_params=_cparams("parallel"),
        name="merge",
    )(*([proj] * (2 * nblk)), y_hy, y_ret, x, w_br_hy, w_br_ret, w_out, g_post, gate_m, g_pre_f, scale_f, shift_f)


def _ffn_kernel(x_hbm, h_ref, gate_ref, gpost_ref, wa_ref, wb_ref, ca_ref, cb_ref, wd_ref, o_ref, act_ref, x_ref,
                x_sem, *, seq):
    j = pl.program_id(1)
    nf = pl.num_programs(1) - 1
    tm = x_ref.shape[0]
    tf = wa_ref.shape[1]

    def x_copy():
        return pltpu.make_async_copy(x_hbm.at[pl.ds(pl.program_id(0) * tm, tm)], x_ref, x_sem)

    def up(slot):
        pos = lax.broadcasted_iota(jnp.int32, (tm, tf), 0) % seq
        first = pos == 0
        last = pos == seq - 1
        h = h_ref[...]
        a = _dwconv3_rows(_dot(h, wa_ref[...]), ca_ref[...], first, last)
        b = _dwconv3_rows(_dot(h, wb_ref[...]), cb_ref[...], first, last)
        c1 = math.sqrt(2.0 / math.pi)
        half = 0.5 * a
        gelu = half + half * jnp.tanh(a * (c1 + (c1 * 0.044715) * (a * a)))
        act_ref[slot] = (gelu * b).astype(BF16)

    def down(slot):
        return _dot(act_ref[slot], wd_ref[...])

    @pl.when(j == 0)
    def _():
        x_copy().start()
        up(0)

    @pl.when(j == 1)
    def _():
        o_ref[...] = down(0)
        up(1)

    @pl.when((j > 1) & (j < nf))
    def _():
        slot = j % 2
        o_ref[...] += down(1 - slot)
        up(slot)

    @pl.when(j == nf)
    def _():
        x_copy().wait()
        slot = (nf - 1) % 2
        half = tm // 2
        for c in range(2):
            rs = slice(c * half, (c + 1) * half)
            f = o_ref[rs] + _dot(act_ref[slot, rs], wd_ref[...])
            o_ref[rs] = x_ref[rs] + _rms_scale(f, gate_ref[...] * gpost_ref[...])


def _ffn(x, h, seq, gate, g_post, w_up, conv_w, w_down, rows_per_mod, tm):
    m, d = x.shape
    d_ff = w_down.shape[0]
    tf = _tile(d_ff, 512)
    nf = d_ff // tf
    assert nf >= 2, (d_ff, tf)
    up_blk = lambda j: jnp.minimum(j, nf - 1)
    down_blk = lambda j: jnp.maximum(j - 1, 0)
    return pl.pallas_call(
        functools.partial(_ffn_kernel, seq=seq),
        grid=(m // tm, nf + 1),
        in_specs=[pl.BlockSpec(memory_space=pl.ANY), pl.BlockSpec((tm, d), lambda i, j: (i, 0)),
                  pl.BlockSpec((None, 1, d), lambda i, j: ((i * tm) // rows_per_mod, 0, 0)),
                  pl.BlockSpec((1, d), lambda i, j: (0, 0)),
                  pl.BlockSpec((d, tf), lambda i, j: (0, up_blk(j))),
                  pl.BlockSpec((d, tf), lambda i, j: (0, nf + up_blk(j))),
                  pl.BlockSpec((3, tf), lambda i, j: (0, up_blk(j))),
                  pl.BlockSpec((3, tf), lambda i, j: (0, nf + up_blk(j))),
                  pl.BlockSpec((tf, d), lambda i, j: (down_blk(j), 0))],
        out_specs=pl.BlockSpec((tm, d), lambda i, j: (i, 0)),
        out_shape=jax.ShapeDtypeStruct((m, d), F32),
        scratch_shapes=[pltpu.VMEM((2, tm, tf), BF16), pltpu.VMEM((tm, d), F32), pltpu.SemaphoreType.DMA(())],
        compiler_params=_cparams("parallel", "arbitrary"),
        name="ffn",
    )(x, h, gate, g_post, w_up, w_up, conv_w, conv_w, w_down)


class _Group:
    def __init__(self, x3, mod):
        self.batch, self.seq, self.d = x3.shape
        self.m = self.batch * self.seq
        self.x = x3.reshape(self.m, self.d)
        per_seq_mod = mod.shape[0] != 1
        self.rows_per_mod = self.seq if per_seq_mod else self.m
        self.tm = _row_tile(self.m, self.seq, per_seq_mod, 1024)
        self.tm_merge = _tile(self.tm, 512, unit=V7X_SUBLANES)
        (self.shift_m, self.scale_m, self.gate_m,
         self.shift_f, self.scale_f, self.gate_f) = (mod[:, i][:, None, :] for i in range(6))

    def in_proj(self, p, w_in, cast=()):
        return _in_proj(self.x, p["g_pre_m"], self.scale_m, self.shift_m, w_in, p["hy_short_w"], self.seq,
                        p["d_hy"], p["d_ret"], self.rows_per_mod, self.tm, cast)


def _mix_and_ffn(grp, proj, rope, state0, want_state, p):
    d_hy, d_ret, n_heads, dk = p["d_hy"], p["d_ret"], p["n_heads"], p["dk"]
    tables = _hyena_filters(grp.seq, p["filter_params"])
    y_hy = _hyena(proj, grp.batch, grp.seq, tables, d_hy)
    ret = _retention(proj, grp.batch, grp.seq, 3 * d_hy, n_heads, dk, p["ret_decay_logit"], p["ret_gn"], rope,
                     state0, want_state)
    x, h_ffn = _merge(proj, 3 * d_hy + 4 * d_ret, y_hy, ret[0], grp.x, p["w_br_hy"], p["w_br_ret"], p["w_out"],
                      p["g_post_m"], grp.gate_m, p["g_pre_f"], grp.scale_f, grp.shift_f, grp.rows_per_mod,
                      grp.tm_merge)
    x = _ffn(x, h_ffn, grp.seq, grp.gate_f, p["g_post_f"], p["ffn_w_up"], p["ffn_conv"], p["ffn_w_down"],
             grp.rows_per_mod, grp.tm)
    return x.reshape(grp.batch, grp.seq, grp.d), (ret[1] if want_state else None)


def kernel(x_prompt, x_sample, state_ret, c, c_ctx, w_ada, b_ada, norm_pre_mix, norm_post_mix, norm_pre_ffn,
           norm_post_ffn, w_in, hy_short_w, hy_w1, hy_b1, hy_w2, hy_b2, hy_w3, hy_b3, hy_freq, hy_decay, hy_bias,
           ret_decay_logit, ret_gn, w_br_hy, w_br_ret, w_out, ffn_w_up, ffn_conv, ffn_w_down):
    depth = w_in.shape[0]
    d = x_prompt.shape[-1]
    n_dec = x_sample.shape[0]
    n_heads, dk = state_ret.shape[3], state_ret.shape[4]
    d_hy = hy_bias.shape[-1]
    assert 1 + n_dec <= V7X_SUBLANES, n_dec
    cc = jnp.concatenate([c_ctx[None], c, jnp.zeros((V7X_SUBLANES - 1 - n_dec, d), F32)], axis=0)
    rope = _rope_tables(x_sample.shape[1], dk)
    x_p, x_s = x_prompt, x_sample
    states = []
    for l in range(depth):
        p = dict(
            d_hy=d_hy, d_ret=n_heads * dk, n_heads=n_heads, dk=dk,
            g_pre_m=norm_pre_mix[l][None], g_post_m=norm_post_mix[l][None],
            g_pre_f=norm_pre_ffn[l][None], g_post_f=norm_post_ffn[l][None],
            hy_short_w=hy_short_w[l],
            filter_params=_filter_params(hy_w1[l], hy_b1[l], hy_w2[l], hy_b2[l], hy_w3[l], hy_b3[l], hy_freq[l],
                                         hy_decay[l], hy_bias[l]),
            ret_decay_logit=ret_decay_logit[l], ret_gn=ret_gn[l], ffn_conv=ffn_conv[l],
        )
        mod = _ada_mod(cc, w_ada[l], b_ada[l][None]).reshape(V7X_SUBLANES, 6, d)
        ctx = _Group(x_p, mod[0:1])
        lat = _Group(x_s, mod[1:1 + n_dec])
        proj_s, w_in_bf = lat.in_proj(p, w_in[l])
        proj_p, *rest = ctx.in_proj(p, w_in_bf, (ffn_w_up[l], ffn_w_down[l], w_out[l], w_br_hy[l], w_br_ret[l]))
        p["ffn_w_up"], p["ffn_w_down"], p["w_out"], p["w_br_hy"], p["w_br_ret"] = rest
        x_p, st = _mix_and_ffn(ctx, proj_p, None, None, True, p)
        x_s, _ = _mix_and_ffn(lat, proj_s, rope, state_ret[:, l], False, p)
        states.append(st)
    return x_p, x_s, jnp.stack(states, axis=1)
```

```python
import functools
import math

import jax
import jax.numpy as jnp
import numpy as np
from jax import lax
from jax.experimental import pallas as pl
from jax.experimental.pallas import tpu as pltpu

F32 = jnp.float32
BF16 = jnp.bfloat16

RMS_EPS = 1e-6
GN_EPS = 1e-5
FILTER_EPS = 1e-6
HY_BANDS = 16
GRID_W = 64
ROPE_BASE = 10000.0

V7X_VMEM_LIMIT_BYTES = 58 * 1024 * 1024
V7X_LANES = 128
V7X_SUBLANES = 8


def _cparams(*sem):
    return pltpu.CompilerParams(dimension_semantics=sem, vmem_limit_bytes=V7X_VMEM_LIMIT_BYTES)


def _tile(n, target, unit=V7X_LANES):
    if n <= target:
        return n
    best = unit
    for t in range(unit, target + 1, unit):
        if n % t == 0:
            best = t
    assert n % best == 0, (n, target, unit)
    return best


def _row_tile(m, seq, per_seq_mod, target):
    if per_seq_mod or seq >= target:
        return seq
    return seq * _tile(m // seq, target // seq, unit=1)


def _resident(shape):
    return pl.BlockSpec(shape, lambda *_: (0,) * len(shape), pipeline_mode=pl.Buffered(1))


def _sigmoid(x):
    return 1.0 / (1.0 + jnp.exp(-x))


def _sigmoid_tanh(x):
    return 0.5 * jnp.tanh(0.5 * x) + 0.5


def _rms_scale(x, g):
    ms = jnp.mean(x * x, axis=-1, keepdims=True)
    return (x * lax.rsqrt(ms + RMS_EPS)) * g


def _dot(a, b):
    return jnp.dot(a, b, preferred_element_type=F32)


def _dot_hilo(a, b):
    a_hi = a.astype(BF16)
    a_lo = (a - a_hi.astype(F32)).astype(BF16)
    b_hi = b.astype(BF16)
    b_lo = (b - b_hi.astype(F32)).astype(BF16)
    return _dot(a_hi, b_hi) + (_dot(a_lo, b_hi) + _dot(a_hi, b_lo))


def _split_bf16(table):
    hi = table.astype(BF16)
    lo = (table - hi.astype(np.float64)).astype(BF16)
    return jnp.asarray(hi), jnp.asarray(lo)


def _dot_split(a_hi_ref, a_lo_ref, b):
    b_hi = b.astype(BF16)
    b_lo = (b - b_hi.astype(F32)).astype(BF16)
    a_hi = a_hi_ref[...]
    return _dot(a_hi, b_hi) + (_dot(a_lo_ref[...], b_hi) + _dot(a_hi, b_lo))


def _dwconv3_rows(x, w, first, last):
    rows = x.shape[0]
    prev = jnp.where(first, 0.0, pltpu.roll(x, 1, 0))
    nxt = jnp.where(last, 0.0, pltpu.roll(x, rows - 1, 0))
    return prev * w[0:1] + x * w[1:2] + nxt * w[2:3]


def _ada_kernel(cc_ref, w_ref, b_ref, o_ref):
    cc = cc_ref[...]
    s = cc * _sigmoid(cc)
    o_ref[...] = _dot(s.astype(BF16), w_ref[...].astype(BF16)) + b_ref[...]


def _ada_mod(cc, w, b):
    d, n = w.shape
    tn = _tile(n, 1024)
    return pl.pallas_call(
        _ada_kernel,
        grid=(n // tn,),
        in_specs=[pl.BlockSpec((V7X_SUBLANES, d), lambda j: (0, 0)),
                  pl.BlockSpec((d, tn), lambda j: (0, j)),
                  pl.BlockSpec((1, tn), lambda j: (0, j))],
        out_specs=pl.BlockSpec((V7X_SUBLANES, tn), lambda j: (0, j)),
        out_shape=jax.ShapeDtypeStruct((V7X_SUBLANES, n), F32),
        compiler_params=_cparams("parallel"),
        name="ada_mod",
    )(cc, w, b)


def _in_proj_kernel(*refs, seq, n_hy, n_qkv, n_g, n_cast, emit_w):
    x_ref, g_ref, sc_ref, sh_ref, w_ref, cw_ref = refs[:6]
    cast_in = refs[6:6 + n_cast]
    o_ref = refs[6 + n_cast]
    pos = 7 + n_cast
    if emit_w:
        wbf_ref = refs[pos]
        pos += 1
    cast_out = refs[pos:pos + n_cast]
    h_ref = refs[pos + n_cast]
    j = pl.program_id(1)
    k = pl.program_id(2)

    @pl.when(j == 0)
    def _():
        n_mod = sc_ref.shape[0]
        rows = x_ref.shape[0] // n_mod
        for b in range(n_mod):
            rs = slice(b * rows, (b + 1) * rows)
            h = _rms_scale(x_ref[rs], g_ref[...] * (1.0 + sc_ref[b])) + sh_ref[b]
            h_ref[k, rs] = h.astype(BF16)

    def tile():
        w = w_ref[...]
        if emit_w:
            w = w.astype(BF16)
            wbf_ref[...] = w
        for src, dst in zip(cast_in, cast_out):
            dst[...] = src[...].astype(BF16)
        return _dot(h_ref[k], w)

    @pl.when(j < n_hy)
    def _():
        pos = lax.broadcasted_iota(jnp.int32, o_ref.shape, 0) % seq
        o_ref[...] = _dwconv3_rows(tile(), cw_ref[...], pos == 0, pos == seq - 1).astype(BF16)

    @pl.when((j >= n_hy) & (j < n_hy + n_qkv))
    def _():
        o_ref[...] = tile().astype(BF16)

    @pl.when((j >= n_hy + n_qkv) & (j < n_hy + n_qkv + n_g))
    def _():
        p = tile()
        o_ref[...] = (p * _sigmoid_tanh(p)).astype(BF16)

    @pl.when(j >= n_hy + n_qkv + n_g)
    def _():
        o_ref[...] = _sigmoid_tanh(tile()).astype(BF16)


V7X_BF16_ROW_TILE = 2 * V7X_SUBLANES


def _cast_rows(arr, n_steps):
    rows = arr.shape[0]
    for blk in range(V7X_BF16_ROW_TILE, rows + 1, V7X_BF16_ROW_TILE):
        if rows % blk == 0 and rows // blk <= n_steps:
            return blk
    return None


def _in_proj(x, g, scale, shift, w, short_w, seq, d_hy, d_ret, rows_per_mod, tm, cast=()):
    m, d = x.shape
    n = w.shape[1]
    emit_w = w.dtype != BF16
    if emit_w:
        tm = m
    tn = _tile(math.gcd(3 * d_hy, d_ret, 2 * d), 512 if emit_w else 1024)
    n_hy, n_qkv, n_g = 3 * d_hy // tn, 3 * d_ret // tn, d_ret // tn
    n_col = n // tn
    n_mod = max(1, tm // rows_per_mod)
    n_row = m // tm
    grp = 2 if n_row % 2 == 0 else 1
    row = lambda i, k: i * grp + k
    step = lambda i, j, k: (i * n_col + j) * grp + k
    x_row = lambda i, j, k: jnp.where(j == 0, row(i, k), row(i, grp - 1))
    mod_spec = pl.BlockSpec((n_mod, 1, d), lambda i, j, k: ((x_row(i, j, k) * tm) // rows_per_mod // n_mod, 0, 0))
    blks = [_cast_rows(a, n_row * n_col) for a in cast]
    assert all(b is not None for b in blks), [a.shape for a in cast]
    cast_specs = [pl.BlockSpec((b, a.shape[1]),
                               lambda i, j, k, nb=a.shape[0] // b: (jnp.minimum(step(i, j, k), nb - 1), 0))
                  for a, b in zip(cast, blks)]
    out_specs = [pl.BlockSpec((tm, tn), lambda i, j, k: (row(i, k), j))]
    out_shape = [jax.ShapeDtypeStruct((m, n), BF16)]
    if emit_w:
        out_specs.append(pl.BlockSpec((d, tn), lambda i, j, k: (0, j)))
        out_shape.append(jax.ShapeDtypeStruct(w.shape, BF16))
    out_specs += cast_specs
    out_shape += [jax.ShapeDtypeStruct(a.shape, BF16) for a in cast]
    return pl.pallas_call(
        functools.partial(_in_proj_kernel, seq=seq, n_hy=n_hy, n_qkv=n_qkv, n_g=n_g, n_cast=len(cast),
                          emit_w=emit_w),
        grid=(n_row // grp, n_col, grp),
        in_specs=[pl.BlockSpec((tm, d), lambda i, j, k: (x_row(i, j, k), 0),
                               pipeline_mode=pl.Buffered(1) if emit_w else None),
                  pl.BlockSpec((1, d), lambda i, j, k: (0, 0)),
                  mod_spec, mod_spec,
                  pl.BlockSpec((d, tn), lambda i, j, k: (0, j)),
                  pl.BlockSpec((3, tn), lambda i, j, k: (0, jnp.minimum(j, n_hy - 1)))] + cast_specs,
        out_specs=out_specs,
        out_shape=out_shape,
        scratch_shapes=[pltpu.VMEM((grp, tm, d), BF16)],
        compiler_params=_cparams("arbitrary", "arbitrary", "arbitrary"),
        name="in_proj",
    )(x, g, scale, shift, w, short_w, *cast)


def _dft_tables(seq):
    n_fft = 2 * seq
    idx = np.arange(seq)
    ang = 2.0 * np.pi * ((idx[:, None] * idx[None, :]) % n_fft) / n_fft
    cos = np.cos(ang)
    msin = -np.sin(ang)
    sign = np.where(idx % 2 == 0, 1.0, -1.0)
    msin[0, :] = sign
    fwd = np.concatenate([cos, msin], axis=0)
    wgt = np.full((seq,), 2.0 / n_fft)
    wgt[0] = 1.0 / n_fft
    inv_re = cos.T * wgt[None, :]
    inv_im = msin.T * wgt[None, :]
    inv_im[:, 0] = sign / n_fft
    inv = np.concatenate([inv_re, inv_im], axis=1)
    return fwd, inv


def _filter_feats(seq):
    n = np.arange(seq, dtype=np.float64)
    t = n / seq
    f = np.linspace(1e-4, HY_BANDS - 1, HY_BANDS)
    w = 2.0 * math.pi * n / seq
    z = np.concatenate([t[:, None], np.cos(w[:, None] * f), np.sin(w[:, None] * f)], axis=-1)
    out = np.zeros((seq, V7X_LANES), np.float32)
    out[:, :z.shape[1]] = z
    return out


def _filter_kernel(z_ref, w1_ref, b1_ref, w2_ref, b2_ref, fr_ref, w3f_ref, w3b_ref, b3f_ref, b3b_ref,
                   decf_ref, decb_ref, bias_ref, chi_ref, clo_ref, shi_ref, slo_ref, kr_ref, kiz_ref, krn_ref, h2_ref):
    @pl.when(pl.program_id(0) == 0)
    def _():
        h1 = jnp.sin(fr_ref[0:1, :] * (_dot_hilo(z_ref[...], w1_ref[...]) + b1_ref[...]))
        h2_ref[...] = jnp.sin(fr_ref[1:2, :] * (_dot_hilo(h1, w2_ref[...]) + b2_ref[...]))

    h2 = h2_ref[...]
    seq, cw = kr_ref.shape
    t = z_ref[:, 0:1]
    row = lax.broadcasted_iota(jnp.int32, (seq, cw), 0)
    hf = (_dot_hilo(h2, w3f_ref[...]) + b3f_ref[...]) * jnp.exp(-t * jnp.abs(decf_ref[...]))
    hb = (_dot_hilo(h2, w3b_ref[...]) + b3b_ref[...]) * jnp.exp(-t * jnp.abs(decb_ref[...]))
    hb = jnp.where(row == 0, 0.0, hb)
    norm = (jnp.sum(jnp.abs(hf), axis=0, keepdims=True)
            + jnp.sum(jnp.abs(hb), axis=0, keepdims=True) + FILTER_EPS)
    inv = 1.0 / norm
    even = (hf + hb) * inv
    odd = (hf - hb) * inv
    bias = bias_ref[...]
    kr = _dot_split(chi_ref, clo_ref, even) + bias
    ki = _dot_split(shi_ref, slo_ref, odd)
    nyq = jnp.sum(jnp.where((row & 1) == 0, even, -even), axis=0, keepdims=True) + bias
    kr_ref[...] = kr.astype(BF16)
    kiz_ref[...] = jnp.where(row == 0, 0.0, ki).astype(BF16)
    krn_ref[...] = jnp.where(row == 0, nyq, kr).astype(BF16)


def _filter_params(w1, b1, w2, b2, w3, b3, freq, decay, hy_bias):
    fh = w1.shape[1]
    pad = V7X_LANES
    ncol2 = w3.shape[1]
    return (jnp.zeros((pad, pad), F32).at[:w1.shape[0], :fh].set(w1),
            jnp.zeros((1, pad), F32).at[0, :fh].set(b1),
            jnp.zeros((pad, pad), F32).at[:fh, :fh].set(w2),
            jnp.zeros((1, pad), F32).at[0, :fh].set(b2),
            jnp.zeros((2, pad), F32).at[:, :fh].set(freq),
            jnp.zeros((pad, ncol2), F32).at[:fh].set(w3),
            b3.reshape(1, ncol2), decay.reshape(1, ncol2), hy_bias.reshape(1, ncol2 // 2))


def _hyena_filters(seq, params):
    w1p, b1p, w2p, b2p, frp, w3p, b3r, dec, bias = params
    pad = V7X_LANES
    ncol = bias.shape[1]
    fwd, _ = _dft_tables(seq)
    chi, clo = _split_bf16(fwd[:seq])
    shi, slo = _split_bf16(fwd[seq:])
    z = jnp.asarray(_filter_feats(seq))
    cw = _tile(ncol, 512)
    nb = ncol // cw
    full = lambda shape: pl.BlockSpec(shape, lambda j: (0, 0))
    colf = lambda rows: pl.BlockSpec((rows, cw), lambda j: (0, j))
    colb = lambda rows: pl.BlockSpec((rows, cw), lambda j: (0, nb + j))
    out = jax.ShapeDtypeStruct((seq, ncol), BF16)
    return pl.pallas_call(
        _filter_kernel,
        grid=(nb,),
        in_specs=[full((seq, pad)), full((pad, pad)), full((1, pad)), full((pad, pad)), full((1, pad)),
                  full((2, pad)), colf(pad), colb(pad), colf(1), colb(1), colf(1), colb(1), colf(1),
                  _resident((seq, seq)), _resident((seq, seq)), _resident((seq, seq)), _resident((seq, seq))],
        out_specs=[colf(seq), colf(seq), colf(seq)],
        out_shape=[out, out, out],
        scratch_shapes=[pltpu.VMEM((seq, pad), F32)],
        compiler_params=_cparams("arbitrary"),
        name=f"hy_filter_{seq}",
    )(z, w1p, b1p, w2p, b2p, frp, w3p, w3p, b3r, b3r, dec, dec, bias, chi, clo, shi, slo)


def _hyena_kernel(x1_ref, x2_ref, v_ref, kr0_ref, kiz0_ref, krn0_ref, kr1_ref, kiz1_ref, krn1_ref, f_ref, g_ref,
                  o_ref, *, seq):
    def long_conv(u, kr_ref, kiz_ref, krn_ref):
        spec = _dot(f_ref[...], u.astype(BF16))
        ur = spec[:seq].astype(BF16)
        ui = spec[seq:].astype(BF16)
        kiz = kiz_ref[...]
        yr = ur * kr_ref[...] - ui * kiz
        yi = ur * kiz + ui * krn_ref[...]
        return _dot(g_ref[:, :seq], yr) + _dot(g_ref[:, seq:], yi)

    for s in range(o_ref.shape[0] // seq):
        rs = slice(s * seq, (s + 1) * seq)
        z = x1_ref[rs].astype(F32) * long_conv(v_ref[rs], kr0_ref, kiz0_ref, krn0_ref)
        o_ref[rs] = (x2_ref[rs].astype(F32) * long_conv(z, kr1_ref, kiz1_ref, krn1_ref)).astype(BF16)


def _hyena(proj, batch, seq, tables, d_hy):
    kr, kiz, krn = tables
    cw = _tile(d_hy, 1024 if seq <= 256 else 512)
    nb = d_hy // cw
    fwd, inv = _dft_tables(seq)
    fmat = jnp.asarray(fwd, F32).astype(BF16)
    gmat = jnp.asarray(inv, F32).astype(BF16)
    bs = next(b for b in (4, 2, 1) if batch % b == 0) if seq <= 256 else 1
    col = lambda rows, off: pl.BlockSpec((rows, cw), lambda j, b: (0, off * nb + j))
    act = lambda off: pl.BlockSpec((bs * seq, cw), lambda j, b: (b, off * nb + j))
    return pl.pallas_call(
        functools.partial(_hyena_kernel, seq=seq),
        grid=(nb, batch // bs),
        in_specs=[act(0), act(1), act(2),
                  col(seq, 0), col(seq, 0), col(seq, 0), col(seq, 1), col(seq, 1), col(seq, 1),
                  _resident((2 * seq, seq)), _resident((seq, 2 * seq))],
        out_specs=pl.BlockSpec((bs * seq, cw), lambda j, b: (b, j)),
        out_shape=jax.ShapeDtypeStruct((batch * seq, d_hy), BF16),
        compiler_params=_cparams("parallel", "parallel"),
        name=f"hyena_{seq}",
    )(proj, proj, proj, kr, kiz, krn, kr, kiz, krn, fmat, gmat)


def _rope_tables(seq, dk):
    rows = seq // GRID_W
    row = jnp.repeat(jnp.arange(rows), GRID_W).astype(F32)
    col = jnp.tile(jnp.arange(GRID_W), rows).astype(F32)
    nfreq = dk // 4
    inv = ROPE_BASE ** (-jnp.arange(nfreq, dtype=F32) / nfreq)
    ang = jnp.concatenate([row[:, None] * inv, col[:, None] * inv], axis=-1)
    cos, sin = jnp.cos(ang), jnp.sin(ang)
    return jnp.concatenate([cos, cos], axis=-1), jnp.concatenate([-sin, sin], axis=-1)


def _retention_kernel(*refs, seq, bs, hb, dk, use_rope, use_state, want_state):
    refs = list(refs)
    q_ref, k_ref, v_ref, g_ref, lg_ref, gn_ref = refs[:6]
    pos = 6
    if use_rope:
        cos_ref, sin_ref = refs[pos:pos + 2]
        pos += 2
    if use_state:
        s0_ref = refs[pos]
        pos += 1
    o_ref = refs[pos]
    pos += 1
    if want_state:
        st_ref = refs[pos]
        pos += 1
    d_ref = refs[pos]
    if want_state:
        wt_ref = refs[pos + 1]

    def log_gamma(hh, direction):
        return jnp.log(_sigmoid(lg_ref[hh, direction]))[:, 0:1]

    @pl.when(pl.program_id(1) == 0)
    def _():
        i = lax.broadcasted_iota(jnp.int32, (dk, dk), 0)
        j = lax.broadcasted_iota(jnp.int32, (dk, dk), 1)
        diff = (i - j).astype(F32)
        scale = dk ** -0.5
        for hh in range(hb):
            lf = log_gamma(hh, 0)
            lb = log_gamma(hh, 1)
            base_f = scale * jnp.exp(lf * diff)
            base_b = scale * jnp.exp(lb * (-diff))
            diag = jnp.where(diff >= 0, base_f, 0.0) + jnp.where(diff <= 0, base_b, 0.0)
            for bi in range(seq // dk):
                for bj in range(seq // dk):
                    if bi == bj:
                        blk = diag
                    elif bi > bj:
                        blk = base_f * jnp.exp(lf * float(dk * (bi - bj)))
                    else:
                        blk = base_b * jnp.exp(lb * float(dk * (bj - bi)))
                    d_ref[hh, bi * dk:(bi + 1) * dk, bj * dk:(bj + 1) * dk] = blk
            if want_state:
                t = lax.broadcasted_iota(jnp.int32, (seq, dk), 0).astype(F32)
                wt_ref[hh, 0] = scale * jnp.exp(lf * (seq - 1.0 - t))
                wt_ref[hh, 1] = scale * jnp.exp(lb * t)

    pos_f = lax.broadcasted_iota(jnp.int32, (seq, dk), 0).astype(F32)
    for sq, hh in [(sq, hh) for sq in range(bs) for hh in range(hb)]:
        rs = slice(sq * seq, (sq + 1) * seq)
        sl = slice(hh * dk, (hh + 1) * dk)
        qb = q_ref[rs, sl]
        kb16 = k_ref[rs, sl]
        vb = v_ref[rs, sl]
        if use_rope:
            cos = cos_ref[...]
            sin = sin_ref[...]
            q = qb.astype(F32)
            k = kb16.astype(F32)
            qb = (q * cos + pltpu.roll(q, dk // 2, 1) * sin).astype(BF16)
            kb16 = (k * cos + pltpu.roll(k, dk // 2, 1) * sin).astype(BF16)
        s = lax.dot_general(qb, kb16, (((1,), (1,)), ((), ())), preferred_element_type=F32)
        o = _dot((s * d_ref[hh]).astype(BF16), vb)
        lf = log_gamma(hh, 0)
        lb = log_gamma(hh, 1)
        if use_state:
            o = o + _dot(qb, s0_ref[sq, 0, hh].astype(BF16)) * jnp.exp(lf * (pos_f + 1.0))
            o = o + _dot(qb, s0_ref[sq, 1, hh].astype(BF16)) * jnp.exp(lb * (seq - pos_f))
        if want_state:
            k = kb16.astype(F32)
            kf = (k * wt_ref[hh, 0]).astype(BF16)
            kb = (k * wt_ref[hh, 1]).astype(BF16)
            tn = (((0,), (0,)), ((), ()))
            st_ref[sq, 0, hh] = lax.dot_general(kf, vb, tn, preferred_element_type=F32)
            st_ref[sq, 1, hh] = lax.dot_general(kb, vb, tn, preferred_element_type=F32)
        mu = jnp.mean(o, axis=-1, keepdims=True)
        oc = o - mu
        var = jnp.mean(oc * oc, axis=-1, keepdims=True)
        y = (oc * lax.rsqrt(var + GN_EPS)) * gn_ref[:, sl] * g_ref[rs, sl].astype(F32)
        o_ref[rs, sl] = y.astype(BF16)


def _retention(proj, batch, seq, col0, n_heads, dk, decay_logit, ret_gn, rope, state0, want_state):
    d_ret = n_heads * dk
    hb = n_heads if seq <= 256 else min(n_heads, 2)
    bs = next(b for b in (4, 2, 1) if batch % b == 0) if seq <= 256 else 1
    bw = hb * dk
    nhb = n_heads // hb
    lg = jnp.broadcast_to(decay_logit.T[:, :, None, None], (n_heads, 2, 1, V7X_LANES))
    act = lambda part: pl.BlockSpec((bs * seq, bw), lambda h, b: (b, (col0 + part * d_ret) // bw + h))
    in_specs = [act(0), act(1), act(2), act(3),
                pl.BlockSpec((hb, 2, 1, V7X_LANES), lambda h, b: (h, 0, 0, 0)),
                pl.BlockSpec((1, bw), lambda h, b: (0, h))]
    args = [proj, proj, proj, proj, lg, ret_gn.reshape(1, d_ret)]
    if rope is not None:
        in_specs += [pl.BlockSpec((seq, dk), lambda h, b: (0, 0))] * 2
        args += list(rope)
    if state0 is not None:
        in_specs.append(pl.BlockSpec((bs, 2, hb, dk, dk), lambda h, b: (b, 0, h, 0, 0)))
        args.append(state0)
    out_specs = [pl.BlockSpec((bs * seq, bw), lambda h, b: (b, h))]
    out_shape = [jax.ShapeDtypeStruct((batch * seq, d_ret), BF16)]
    if want_state:
        out_specs.append(pl.BlockSpec((bs, 2, hb, dk, dk), lambda h, b: (b, 0, h, 0, 0)))
        out_shape.append(jax.ShapeDtypeStruct((batch, 2, n_heads, dk, dk), F32))
    body = functools.partial(_retention_kernel, seq=seq, bs=bs, hb=hb, dk=dk, use_rope=rope is not None,
                             use_state=state0 is not None, want_state=want_state)
    return pl.pallas_call(
        body,
        grid=(nhb, batch // bs),
        in_specs=in_specs,
        out_specs=out_specs,
        out_shape=out_shape,
        scratch_shapes=[pltpu.VMEM((hb, seq, seq), F32)]
        + ([pltpu.VMEM((hb, 2, seq, dk), F32)] if want_state else []),
        compiler_params=_cparams("parallel", "arbitrary"),
        name=f"retention_{seq}",
    )(*args)


def _merge_kernel(*refs, nblk):
    gate_refs = refs[:2 * nblk]
    (yhy_ref, yret_ref, x_ref, wbh_ref, wbr_ref, wo_ref, gpost_ref, gm_ref, gpre_ref, sc_ref, sh_ref,
     o_ref, h_ref) = refs[2 * nblk:]
    a = _dot(yhy_ref[...], wbh_ref[...])
    b = _dot(yret_ref[...], wbr_ref[...])
    wblk = gate_refs[0].shape[1]
    parts = []
    for kk in range(nblk):
        sl = slice(kk * wblk, (kk + 1) * wblk)
        g_hy = gate_refs[kk][...].astype(F32)
        g_ret = gate_refs[nblk + kk][...].astype(F32)
        parts.append((g_hy * a[:, sl] + g_ret * b[:, sl]).astype(BF16))
    merged = parts[0] if nblk == 1 else jnp.concatenate(parts, axis=1)
    half = merged.shape[0] // 2
    for c in range(2):
        rs = slice(c * half, (c + 1) * half)
        out = _dot(merged[rs], wo_ref[...])
        x1 = x_ref[rs] + _rms_scale(out, gm_ref[...] * gpost_ref[...])
        o_ref[rs] = x1
        h_ref[rs] = (_rms_scale(x1, gpre_ref[...] * (1.0 + sc_ref[...])) + sh_ref[...]).astype(BF16)


def _merge(proj, col0, y_hy, y_ret, x, w_br_hy, w_br_ret, w_out, g_post, gate_m, g_pre_f, scale_f, shift_f,
           rows_per_mod, tm):
    m, d = x.shape
    vec = pl.BlockSpec((1, d), lambda i: (0, 0))
    mod_spec = pl.BlockSpec((None, 1, d), lambda i: ((i * tm) // rows_per_mod, 0, 0))
    wblk = math.gcd(col0, d)
    nblk = d // wblk
    gate_spec = lambda kk: pl.BlockSpec((tm, wblk), lambda i: (i, col0 // wblk + kk))
    const = lambda arr: _resident(arr.shape)
    row = lambda width: pl.BlockSpec((tm, width), lambda i: (i, 0))
    return pl.pallas_call(
        functools.partial(_merge_kernel, nblk=nblk),
        grid=(m // tm,),
        in_specs=[gate_spec(kk) for kk in range(2 * nblk)]
        + [row(y_hy.shape[1]), row(y_ret.shape[1]), row(d), const(w_br_hy), const(w_br_ret), const(w_out),
           vec, mod_spec, vec, mod_spec, mod_spec],
        out_specs=[row(d), row(d)],
        out_shape=[jax.ShapeDtypeStruct((m, d), F32), jax.ShapeDtypeStruct((m, d), BF16)],
        compiler_params=_cparams("parallel"),
        name="merge",
    )(*([proj] * (2 * nblk)), y_hy, y_ret, x, w_br_hy, w_br_ret, w_out, g_post, gate_m, g_pre_f, scale_f, shift_f)


def _ffn_kernel(x_hbm, h_ref, gate_ref, gpost_ref, wa_ref, wb_ref, ca_ref, cb_ref, wd_ref, o_ref, act_ref, x_ref,
                x_sem, *, seq):
    j = pl.program_id(1)
    nf = pl.num_programs(1) - 1
    tm = x_ref.shape[0]
    tf = wa_ref.shape[1]

    def x_copy():
        return pltpu.make_async_copy(x_hbm.at[pl.ds(pl.program_id(0) * tm, tm)], x_ref, x_sem)

    def up(slot):
        pos = lax.broadcasted_iota(jnp.int32, (tm, tf), 0) % seq
        first = pos == 0
        last = pos == seq - 1
        h = h_ref[...]
        a = _dwconv3_rows(_dot(h, wa_ref[...]), ca_ref[...], first, last)
        b = _dwconv3_rows(_dot(h, wb_ref[...]), cb_ref[...], first, last)
        c1 = math.sqrt(2.0 / math.pi)
        half = 0.5 * a
        gelu = half + half * jnp.tanh(a * (c1 + (c1 * 0.044715) * (a * a)))
        act_ref[slot] = (gelu * b).astype(BF16)

    def down(slot):
        return _dot(act_ref[slot], wd_ref[...])

    @pl.when(j == 0)
    def _():
        x_copy().start()
        up(0)

    @pl.when(j == 1)
    def _():
        o_ref[...] = down(0)
        up(1)

    @pl.when((j > 1) & (j < nf))
    def _():
        slot = j % 2
        o_ref[...] += down(1 - slot)
        up(slot)

    @pl.when(j == nf)
    def _():
        x_copy().wait()
        slot = (nf - 1) % 2
        half = tm // 2
        for c in range(2):
            rs = slice(c * half, (c + 1) * half)
            f = o_ref[rs] + _dot(act_ref[slot, rs], wd_ref[...])
            o_ref[rs] = x_ref[rs] + _rms_scale(f, gate_ref[...] * gpost_ref[...])


def _ffn(x, h, seq, gate, g_post, w_up, conv_w, w_down, rows_per_mod, tm):
    m, d = x.shape
    d_ff = w_down.shape[0]
    tf = _tile(d_ff, 512)
    nf = d_ff // tf
    assert nf >= 2, (d_ff, tf)
    up_blk = lambda j: jnp.minimum(j, nf - 1)
    down_blk = lambda j: jnp.maximum(j - 1, 0)
    return pl.pallas_call(
        functools.partial(_ffn_kernel, seq=seq),
        grid=(m // tm, nf + 1),
        in_specs=[pl.BlockSpec(memory_space=pl.ANY), pl.BlockSpec((tm, d), lambda i, j: (i, 0)),
                  pl.BlockSpec((None, 1, d), lambda i, j: ((i * tm) // rows_per_mod, 0, 0)),
                  pl.BlockSpec((1, d), lambda i, j: (0, 0)),
                  pl.BlockSpec((d, tf), lambda i, j: (0, up_blk(j))),
                  pl.BlockSpec((d, tf), lambda i, j: (0, nf + up_blk(j))),
                  pl.BlockSpec((3, tf), lambda i, j: (0, up_blk(j))),
                  pl.BlockSpec((3, tf), lambda i, j: (0, nf + up_blk(j))),
                  pl.BlockSpec((tf, d), lambda i, j: (down_blk(j), 0))],
        out_specs=pl.BlockSpec((tm, d), lambda i, j: (i, 0)),
        out_shape=jax.ShapeDtypeStruct((m, d), F32),
        scratch_shapes=[pltpu.VMEM((2, tm, tf), BF16), pltpu.VMEM((tm, d), F32), pltpu.SemaphoreType.DMA(())],
        compiler_params=_cparams("parallel", "arbitrary"),
        name="ffn",
    )(x, h, gate, g_post, w_up, w_up, conv_w, conv_w, w_down)


class _Group:
    def __init__(self, x3, mod):
        self.batch, self.seq, self.d = x3.shape
        self.m = self.batch * self.seq
        self.x = x3.reshape(self.m, self.d)
        per_seq_mod = mod.shape[0] != 1
        self.rows_per_mod = self.seq if per_seq_mod else self.m
        self.tm = _row_tile(self.m, self.seq, per_seq_mod, 1024)
        self.tm_merge = _tile(self.tm, 512, unit=V7X_SUBLANES)
        (self.shift_m, self.scale_m, self.gate_m,
         self.shift_f, self.scale_f, self.gate_f) = (mod[:, i][:, None, :] for i in range(6))

    def in_proj(self, p, w_in, cast=()):
        return _in_proj(self.x, p["g_pre_m"], self.scale_m, self.shift_m, w_in, p["hy_short_w"], self.seq,
                        p["d_hy"], p["d_ret"], self.rows_per_mod, self.tm, cast)


def _mix_and_ffn(grp, proj, rope, state0, want_state, p):
    d_hy, d_ret, n_heads, dk = p["d_hy"], p["d_ret"], p["n_heads"], p["dk"]
    tables = _hyena_filters(grp.seq, p["filter_params"])
    y_hy = _hyena(proj, grp.batch, grp.seq, tables, d_hy)
    ret = _retention(proj, grp.batch, grp.seq, 3 * d_hy, n_heads, dk, p["ret_decay_logit"], p["ret_gn"], rope,
                     state0, want_state)
    x, h_ffn = _merge(proj, 3 * d_hy + 4 * d_ret, y_hy, ret[0], grp.x, p["w_br_hy"], p["w_br_ret"], p["w_out"],
                      p["g_post_m"], grp.gate_m, p["g_pre_f"], grp.scale_f, grp.shift_f, grp.rows_per_mod,
                      grp.tm_merge)
    x = _ffn(x, h_ffn, grp.seq, grp.gate_f, p["g_post_f"], p["ffn_w_up"], p["ffn_conv"], p["ffn_w_down"],
             grp.rows_per_mod, grp.tm)
    return x.reshape(grp.batch, grp.seq, grp.d), (ret[1] if want_state else None)


def kernel(x_prompt, x_sample, state_ret, c, c_ctx, w_ada, b_ada, norm_pre_mix, norm_post_mix, norm_pre_ffn,
           norm_post_ffn, w_in, hy_short_w, hy_w1, hy_b1, hy_w2, hy_b2, hy_w3, hy_b3, hy_freq, hy_decay, hy_bias,
           ret_decay_logit, ret_gn, w_br_hy, w_br_ret, w_out, ffn_w_up, ffn_conv, ffn_w_down):
    depth = w_in.shape[0]
    d = x_prompt.shape[-1]
    n_dec = x_sample.shape[0]
    n_heads, dk = state_ret.shape[3], state_ret.shape[4]
    d_hy = hy_bias.shape[-1]
    assert 1 + n_dec <= V7X_SUBLANES, n_dec
    cc = jnp.concatenate([c_ctx[None], c, jnp.zeros((V7X_SUBLANES - 1 - n_dec, d), F32)], axis=0)
    rope = _rope_tables(x_sample.shape[1], dk)
    x_p, x_s = x_prompt, x_sample
    states = []
    for l in range(depth):
        p = dict(
            d_hy=d_hy, d_ret=n_heads * dk, n_heads=n_heads, dk=dk,
            g_pre_m=norm_pre_mix[l][None], g_post_m=norm_post_mix[l][None],
            g_pre_f=norm_pre_ffn[l][None], g_post_f=norm_post_ffn[l][None],
            hy_short_w=hy_short_w[l],
            filter_params=_filter_params(hy_w1[l], hy_b1[l], hy_w2[l], hy_b2[l], hy_w3[l], hy_b3[l], hy_freq[l],
                                         hy_decay[l], hy_bias[l]),
            ret_decay_logit=ret_decay_logit[l], ret_gn=ret_gn[l], ffn_conv=ffn_conv[l],
        )
        mod = _ada_mod(cc, w_ada[l], b_ada[l][None]).reshape(V7X_SUBLANES, 6, d)
        ctx = _Group(x_p, mod[0:1])
        lat = _Group(x_s, mod[1:1 + n_dec])
        proj_s, w_in_bf = lat.in_proj(p, w_in[l])
        proj_p, *rest = ctx.in_proj(p, w_in_bf, (ffn_w_up[l], ffn_w_down[l], w_out[l], w_br_hy[l], w_br_ret[l]))
        p["ffn_w_up"], p["ffn_w_down"], p["w_out"], p["w_br_hy"], p["w_br_ret"] = rest
        x_p, st = _mix_and_ffn(ctx, proj_p, None, None, True, p)
        x_s, _ = _mix_and_ffn(lat, proj_s, rope, state_ret[:, l], False, p)
        states.append(st)
    return x_p, x_s, jnp.stack(states, axis=1)
```

```python
import functools
import math

import jax
import jax.numpy as jnp
import numpy as np
from jax import lax
from jax.experimental import pallas as pl
from jax.experimental.pallas import tpu as pltpu

F32 = jnp.float32
BF16 = jnp.bfloat16

RMS_EPS = 1e-6
GN_EPS = 1e-5
FILTER_EPS = 1e-6
HY_BANDS = 16
GRID_W = 64
ROPE_BASE = 10000.0

V7X_VMEM_LIMIT_BYTES = 58 * 1024 * 1024
V7X_LANES = 128
V7X_SUBLANES = 8


def _cparams(*sem):
    return pltpu.CompilerParams(dimension_semantics=sem, vmem_limit_bytes=V7X_VMEM_LIMIT_BYTES)


def _tile(n, target, unit=V7X_LANES):
    if n <= target:
        return n
    best = unit
    for t in range(unit, target + 1, unit):
        if n % t == 0:
            best = t
    assert n % best == 0, (n, target, unit)
    return best


def _row_tile(m, seq, per_seq_mod, target):
    if per_seq_mod or seq >= target:
        return seq
    return seq * _tile(m // seq, target // seq, unit=1)


def _resident(shape):
    return pl.BlockSpec(shape, lambda *_: (0,) * len(shape), pipeline_mode=pl.Buffered(1))


def _sigmoid(x):
    return 1.0 / (1.0 + jnp.exp(-x))


def _sigmoid_tanh(x):
    return 0.5 * jnp.tanh(0.5 * x) + 0.5


def _rms_scale(x, g):
    ms = jnp.mean(x * x, axis=-1, keepdims=True)
    return (x * lax.rsqrt(ms + RMS_EPS)) * g


def _dot(a, b):
    return jnp.dot(a, b, preferred_element_type=F32)


def _dot_hilo(a, b):
    a_hi = a.astype(BF16)
    a_lo = (a - a_hi.astype(F32)).astype(BF16)
    b_hi = b.astype(BF16)
    b_lo = (b - b_hi.astype(F32)).astype(BF16)
    return _dot(a_hi, b_hi) + (_dot(a_lo, b_hi) + _dot(a_hi, b_lo))


def _split_bf16(table):
    hi = table.astype(BF16)
    lo = (table - hi.astype(np.float64)).astype(BF16)
    return jnp.asarray(hi), jnp.asarray(lo)


def _dot_split(a_hi_ref, a_lo_ref, b):
    b_hi = b.astype(BF16)
    b_lo = (b - b_hi.astype(F32)).astype(BF16)
    a_hi = a_hi_ref[...]
    return _dot(a_hi, b_hi) + (_dot(a_lo_ref[...], b_hi) + _dot(a_hi, b_lo))


def _dwconv3_rows(x, w, first, last):
    rows = x.shape[0]
    prev = jnp.where(first, 0.0, pltpu.roll(x, 1, 0))
    nxt = jnp.where(last, 0.0, pltpu.roll(x, rows - 1, 0))
    return prev * w[0:1] + x * w[1:2] + nxt * w[2:3]


def _ada_kernel(cc_ref, w_ref, b_ref, o_ref):
    cc = cc_ref[...]
    s = cc * _sigmoid(cc)
    o_ref[...] = _dot(s.astype(BF16), w_ref[...].astype(BF16)) + b_ref[...]


def _ada_mod(cc, w, b):
    d, n = w.shape
    tn = _tile(n, 1024)
    return pl.pallas_call(
        _ada_kernel,
        grid=(n // tn,),
        in_specs=[pl.BlockSpec((V7X_SUBLANES, d), lambda j: (0, 0)),
                  pl.BlockSpec((d, tn), lambda j: (0, j)),
                  pl.BlockSpec((1, tn), lambda j: (0, j))],
        out_specs=pl.BlockSpec((V7X_SUBLANES, tn), lambda j: (0, j)),
        out_shape=jax.ShapeDtypeStruct((V7X_SUBLANES, n), F32),
        compiler_params=_cparams("parallel"),
        name="ada_mod",
    )(cc, w, b)


def _in_proj_kernel(*refs, seq, n_hy, n_qkv, n_g, n_cast, emit_w):
    x_ref, g_ref, sc_ref, sh_ref, w_ref, cw_ref = refs[:6]
    cast_in = refs[6:6 + n_cast]
    o_ref = refs[6 + n_cast]
    pos = 7 + n_cast
    if emit_w:
        wbf_ref = refs[pos]
        pos += 1
    cast_out = refs[pos:pos + n_cast]
    h_ref = refs[pos + n_cast]
    j = pl.program_id(1)
    k = pl.program_id(2)

    @pl.when(j == 0)
    def _():
        n_mod = sc_ref.shape[0]
        rows = x_ref.shape[0] // n_mod
        for b in range(n_mod):
            rs = slice(b * rows, (b + 1) * rows)
            h = _rms_scale(x_ref[rs], g_ref[...] * (1.0 + sc_ref[b])) + sh_ref[b]
            h_ref[k, rs] = h.astype(BF16)

    def tile():
        w = w_ref[...]
        if emit_w:
            w = w.astype(BF16)
            wbf_ref[...] = w
        for src, dst in zip(cast_in, cast_out):
            dst[...] = src[...].astype(BF16)
        return _dot(h_ref[k], w)

    @pl.when(j < n_hy)
    def _():
        pos = lax.broadcasted_iota(jnp.int32, o_ref.shape, 0) % seq
        o_ref[...] = _dwconv3_rows(tile(), cw_ref[...], pos == 0, pos == seq - 1).astype(BF16)

    @pl.when((j >= n_hy) & (j < n_hy + n_qkv))
    def _():
        o_ref[...] = tile().astype(BF16)

    @pl.when((j >= n_hy + n_qkv) & (j < n_hy + n_qkv + n_g))
    def _():
        p = tile()
        o_ref[...] = (p * _sigmoid_tanh(p)).astype(BF16)

    @pl.when(j >= n_hy + n_qkv + n_g)
    def _():
        o_ref[...] = _sigmoid_tanh(tile()).astype(BF16)


V7X_BF16_ROW_TILE = 2 * V7X_SUBLANES


def _cast_rows(arr, n_steps):
    rows = arr.shape[0]
    for blk in range(V7X_BF16_ROW_TILE, rows + 1, V7X_BF16_ROW_TILE):
        if rows % blk == 0 and rows // blk <= n_steps:
            return blk
    return None


def _in_proj(x, g, scale, shift, w, short_w, seq, d_hy, d_ret, rows_per_mod, tm, cast=()):
    m, d = x.shape
    n = w.shape[1]
    emit_w = w.dtype != BF16
    if emit_w:
        tm = m
    tn = _tile(math.gcd(3 * d_hy, d_ret, 2 * d), 512 if emit_w else 1024)
    n_hy, n_qkv, n_g = 3 * d_hy // tn, 3 * d_ret // tn, d_ret // tn
    n_col = n // tn
    n_mod = max(1, tm // rows_per_mod)
    n_row = m // tm
    grp = next(c for c in (4, 2, 1) if n_row % c == 0)
    row = lambda i, k: i * grp + k
    step = lambda i, j, k: (i * n_col + j) * grp + k
    x_row = lambda i, j, k: jnp.where(j == 0, row(i, k), row(i, grp - 1))
    mod_spec = pl.BlockSpec((n_mod, 1, d), lambda i, j, k: ((x_row(i, j, k) * tm) // rows_per_mod // n_mod, 0, 0))
    blks = [_cast_rows(a, n_row * n_col) for a in cast]
    assert all(b is not None for b in blks), [a.shape for a in cast]
    cast_specs = [pl.BlockSpec((b, a.shape[1]),
                               lambda i, j, k, nb=a.shape[0] // b: (jnp.minimum(step(i, j, k), nb - 1), 0))
                  for a, b in zip(cast, blks)]
    out_specs = [pl.BlockSpec((tm, tn), lambda i, j, k: (row(i, k), j))]
    out_shape = [jax.ShapeDtypeStruct((m, n), BF16)]
    if emit_w:
        out_specs.append(pl.BlockSpec((d, tn), lambda i, j, k: (0, j)))
        out_shape.append(jax.ShapeDtypeStruct(w.shape, BF16))
    out_specs += cast_specs
    out_shape += [jax.ShapeDtypeStruct(a.shape, BF16) for a in cast]
    return pl.pallas_call(
        functools.partial(_in_proj_kernel, seq=seq, n_hy=n_hy, n_qkv=n_qkv, n_g=n_g, n_cast=len(cast),
                          emit_w=emit_w),
        grid=(n_row // grp, n_col, grp),
        in_specs=[pl.BlockSpec((tm, d), lambda i, j, k: (x_row(i, j, k), 0),
                               pipeline_mode=pl.Buffered(1) if emit_w else None),
                  pl.BlockSpec((1, d), lambda i, j, k: (0, 0)),
                  mod_spec, mod_spec,
                  pl.BlockSpec((d, tn), lambda i, j, k: (0, j)),
                  pl.BlockSpec((3, tn), lambda i, j, k: (0, jnp.minimum(j, n_hy - 1)))] + cast_specs,
        out_specs=out_specs,
        out_shape=out_shape,
        scratch_shapes=[pltpu.VMEM((grp, tm, d), BF16)],
        compiler_params=_cparams("arbitrary", "arbitrary", "arbitrary"),
        name="in_proj",
    )(x, g, scale, shift, w, short_w, *cast)


def _dft_tables(seq):
    n_fft = 2 * seq
    idx = np.arange(seq)
    ang = 2.0 * np.pi * ((idx[:, None] * idx[None, :]) % n_fft) / n_fft
    cos = np.cos(ang)
    msin = -np.sin(ang)
    sign = np.where(idx % 2 == 0, 1.0, -1.0)
    msin[0, :] = sign
    fwd = np.concatenate([cos, msin], axis=0)
    wgt = np.full((seq,), 2.0 / n_fft)
    wgt[0] = 1.0 / n_fft
    inv_re = cos.T * wgt[None, :]
    inv_im = msin.T * wgt[None, :]
    inv_im[:, 0] = sign / n_fft
    inv = np.concatenate([inv_re, inv_im], axis=1)
    return fwd, inv


def _filter_feats(seq):
    n = np.arange(seq, dtype=np.float64)
    t = n / seq
    f = np.linspace(1e-4, HY_BANDS - 1, HY_BANDS)
    w = 2.0 * math.pi * n / seq
    z = np.concatenate([t[:, None], np.cos(w[:, None] * f), np.sin(w[:, None] * f)], axis=-1)
    out = np.zeros((seq, V7X_LANES), np.float32)
    out[:, :z.shape[1]] = z
    return out


def _filter_kernel(z_ref, w1_ref, b1_ref, w2_ref, b2_ref, fr_ref, w3f_ref, w3b_ref, b3f_ref, b3b_ref,
                   decf_ref, decb_ref, bias_ref, chi_ref, clo_ref, shi_ref, slo_ref, kr_ref, kiz_ref, krn_ref, h2_ref):
    @pl.when(pl.program_id(0) == 0)
    def _():
        h1 = jnp.sin(fr_ref[0:1, :] * (_dot_hilo(z_ref[...], w1_ref[...]) + b1_ref[...]))
        h2_ref[...] = jnp.sin(fr_ref[1:2, :] * (_dot_hilo(h1, w2_ref[...]) + b2_ref[...]))

    h2 = h2_ref[...]
    seq, cw = kr_ref.shape
    t = z_ref[:, 0:1]
    row = lax.broadcasted_iota(jnp.int32, (seq, cw), 0)
    hf = (_dot_hilo(h2, w3f_ref[...]) + b3f_ref[...]) * jnp.exp(-t * jnp.abs(decf_ref[...]))
    hb = (_dot_hilo(h2, w3b_ref[...]) + b3b_ref[...]) * jnp.exp(-t * jnp.abs(decb_ref[...]))
    hb = jnp.where(row == 0, 0.0, hb)
    norm = (jnp.sum(jnp.abs(hf), axis=0, keepdims=True)
            + jnp.sum(jnp.abs(hb), axis=0, keepdims=True) + FILTER_EPS)
    inv = 1.0 / norm
    even = (hf + hb) * inv
    odd = (hf - hb) * inv
    bias = bias_ref[...]
    kr = _dot_split(chi_ref, clo_ref, even) + bias
    ki = _dot_split(shi_ref, slo_ref, odd)
    nyq = jnp.sum(jnp.where((row & 1) == 0, even, -even), axis=0, keepdims=True) + bias
    kr_ref[...] = kr.astype(BF16)
    kiz_ref[...] = jnp.where(row == 0, 0.0, ki).astype(BF16)
    krn_ref[...] = jnp.where(row == 0, nyq, kr).astype(BF16)


def _filter_params(w1, b1, w2, b2, w3, b3, freq, decay, hy_bias):
    fh = w1.shape[1]
    pad = V7X_LANES
    ncol2 = w3.shape[1]
    return (jnp.zeros((pad, pad), F32).at[:w1.shape[0], :fh].set(w1),
            jnp.zeros((1, pad), F32).at[0, :fh].set(b1),
            jnp.zeros((pad, pad), F32).at[:fh, :fh].set(w2),
            jnp.zeros((1, pad), F32).at[0, :fh].set(b2),
            jnp.zeros((2, pad), F32).at[:, :fh].set(freq),
            jnp.zeros((pad, ncol2), F32).at[:fh].set(w3),
            b3.reshape(1, ncol2), decay.reshape(1, ncol2), hy_bias.reshape(1, ncol2 // 2))


def _hyena_filters(seq, params):
    w1p, b1p, w2p, b2p, frp, w3p, b3r, dec, bias = params
    pad = V7X_LANES
    ncol = bias.shape[1]
    fwd, _ = _dft_tables(seq)
    chi, clo = _split_bf16(fwd[:seq])
    shi, slo = _split_bf16(fwd[seq:])
    z = jnp.asarray(_filter_feats(seq))
    cw = _tile(ncol, 512)
    nb = ncol // cw
    full = lambda shape: pl.BlockSpec(shape, lambda j: (0, 0))
    colf = lambda rows: pl.BlockSpec((rows, cw), lambda j: (0, j))
    colb = lambda rows: pl.BlockSpec((rows, cw), lambda j: (0, nb + j))
    out = jax.ShapeDtypeStruct((seq, ncol), BF16)
    return pl.pallas_call(
        _filter_kernel,
        grid=(nb,),
        in_specs=[full((seq, pad)), full((pad, pad)), full((1, pad)), full((pad, pad)), full((1, pad)),
                  full((2, pad)), colf(pad), colb(pad), colf(1), colb(1), colf(1), colb(1), colf(1),
                  _resident((seq, seq)), _resident((seq, seq)), _resident((seq, seq)), _resident((seq, seq))],
        out_specs=[colf(seq), colf(seq), colf(seq)],
        out_shape=[out, out, out],
        scratch_shapes=[pltpu.VMEM((seq, pad), F32)],
        compiler_params=_cparams("arbitrary"),
        name=f"hy_filter_{seq}",
    )(z, w1p, b1p, w2p, b2p, frp, w3p, w3p, b3r, b3r, dec, dec, bias, chi, clo, shi, slo)


def _hyena_kernel(x1_ref, x2_ref, v_ref, kr0_ref, kiz0_ref, krn0_ref, kr1_ref, kiz1_ref, krn1_ref, f_ref, g_ref,
                  o_ref, *, seq):
    def long_conv(u, kr_ref, kiz_ref, krn_ref):
        spec = _dot(f_ref[...], u.astype(BF16))
        ur = spec[:seq].astype(BF16)
        ui = spec[seq:].astype(BF16)
        kiz = kiz_ref[...]
        yr = ur * kr_ref[...] - ui * kiz
        yi = ur * kiz + ui * krn_ref[...]
        return _dot(g_ref[:, :seq], yr) + _dot(g_ref[:, seq:], yi)

    for s in range(o_ref.shape[0] // seq):
        rs = slice(s * seq, (s + 1) * seq)
        z = x1_ref[rs].astype(F32) * long_conv(v_ref[rs], kr0_ref, kiz0_ref, krn0_ref)
        o_ref[rs] = (x2_ref[rs].astype(F32) * long_conv(z, kr1_ref, kiz1_ref, krn1_ref)).astype(BF16)


def _hyena(proj, batch, seq, tables, d_hy):
    kr, kiz, krn = tables
    cw = _tile(d_hy, 1024 if seq <= 256 else 512)
    nb = d_hy // cw
    fwd, inv = _dft_tables(seq)
    fmat = jnp.asarray(fwd, F32).astype(BF16)
    gmat = jnp.asarray(inv, F32).astype(BF16)
    bs = next(b for b in (4, 2, 1) if batch % b == 0) if seq <= 256 else 1
    col = lambda rows, off: pl.BlockSpec((rows, cw), lambda j, b: (0, off * nb + j))
    act = lambda off: pl.BlockSpec((bs * seq, cw), lambda j, b: (b, off * nb + j))
    return pl.pallas_call(
        functools.partial(_hyena_kernel, seq=seq),
        grid=(nb, batch // bs),
        in_specs=[act(0), act(1), act(2),
                  col(seq, 0), col(seq, 0), col(seq, 0), col(seq, 1), col(seq, 1), col(seq, 1),
                  _resident((2 * seq, seq)), _resident((seq, 2 * seq))],
        out_specs=pl.BlockSpec((bs * seq, cw), lambda j, b: (b, j)),
        out_shape=jax.ShapeDtypeStruct((batch * seq, d_hy), BF16),
        compiler_params=_cparams("parallel", "parallel"),
        name=f"hyena_{seq}",
    )(proj, proj, proj, kr, kiz, krn, kr, kiz, krn, fmat, gmat)


def _rope_tables(seq, dk):
    rows = seq // GRID_W
    row = jnp.repeat(jnp.arange(rows), GRID_W).astype(F32)
    col = jnp.tile(jnp.arange(GRID_W), rows).astype(F32)
    nfreq = dk // 4
    inv = ROPE_BASE ** (-jnp.arange(nfreq, dtype=F32) / nfreq)
    ang = jnp.concatenate([row[:, None] * inv, col[:, None] * inv], axis=-1)
    cos, sin = jnp.cos(ang), jnp.sin(ang)
    return jnp.concatenate([cos, cos], axis=-1), jnp.concatenate([-sin, sin], axis=-1)


def _retention_kernel(*refs, seq, bs, hb, dk, use_rope, use_state, want_state):
    refs = list(refs)
    q_ref, k_ref, v_ref, g_ref, lg_ref, gn_ref = refs[:6]
    pos = 6
    if use_rope:
        cos_ref, sin_ref = refs[pos:pos + 2]
        pos += 2
    if use_state:
        s0_ref = refs[pos]
        pos += 1
    o_ref = refs[pos]
    pos += 1
    if want_state:
        st_ref = refs[pos]
        pos += 1
    d_ref = refs[pos]
    if want_state:
        wt_ref = refs[pos + 1]

    def log_gamma(hh, direction):
        return jnp.log(_sigmoid(lg_ref[hh, direction]))[:, 0:1]

    @pl.when(pl.program_id(1) == 0)
    def _():
        i = lax.broadcasted_iota(jnp.int32, (dk, dk), 0)
        j = lax.broadcasted_iota(jnp.int32, (dk, dk), 1)
        diff = (i - j).astype(F32)
        scale = dk ** -0.5
        for hh in range(hb):
            lf = log_gamma(hh, 0)
            lb = log_gamma(hh, 1)
            base_f = scale * jnp.exp(lf * diff)
            base_b = scale * jnp.exp(lb * (-diff))
            diag = jnp.where(diff >= 0, base_f, 0.0) + jnp.where(diff <= 0, base_b, 0.0)
            for bi in range(seq // dk):
                for bj in range(seq // dk):
                    if bi == bj:
                        blk = diag
                    elif bi > bj:
                        blk = base_f * jnp.exp(lf * float(dk * (bi - bj)))
                    else:
                        blk = base_b * jnp.exp(lb * float(dk * (bj - bi)))
                    d_ref[hh, bi * dk:(bi + 1) * dk, bj * dk:(bj + 1) * dk] = blk
            if want_state:
                t = lax.broadcasted_iota(jnp.int32, (seq, dk), 0).astype(F32)
                wt_ref[hh, 0] = scale * jnp.exp(lf * (seq - 1.0 - t))
                wt_ref[hh, 1] = scale * jnp.exp(lb * t)

    pos_f = lax.broadcasted_iota(jnp.int32, (seq, dk), 0).astype(F32)
    for sq, hh in [(sq, hh) for sq in range(bs) for hh in range(hb)]:
        rs = slice(sq * seq, (sq + 1) * seq)
        sl = slice(hh * dk, (hh + 1) * dk)
        qb = q_ref[rs, sl]
        kb16 = k_ref[rs, sl]
        vb = v_ref[rs, sl]
        if use_rope:
            cos = cos_ref[...]
            sin = sin_ref[...]
            q = qb.astype(F32)
            k = kb16.astype(F32)
            qb = (q * cos + pltpu.roll(q, dk // 2, 1) * sin).astype(BF16)
            kb16 = (k * cos + pltpu.roll(k, dk // 2, 1) * sin).astype(BF16)
        s = lax.dot_general(qb, kb16, (((1,), (1,)), ((), ())), preferred_element_type=F32)
        o = _dot((s * d_ref[hh]).astype(BF16), vb)
        lf = log_gamma(hh, 0)
        lb = log_gamma(hh, 1)
        if use_state:
            o = o + _dot(qb, s0_ref[sq, 0, hh].astype(BF16)) * jnp.exp(lf * (pos_f + 1.0))
            o = o + _dot(qb, s0_ref[sq, 1, hh].astype(BF16)) * jnp.exp(lb * (seq - pos_f))
        if want_state:
            k = kb16.astype(F32)
            kf = (k * wt_ref[hh, 0]).astype(BF16)
            kb = (k * wt_ref[hh, 1]).astype(BF16)
            tn = (((0,), (0,)), ((), ()))
            st_ref[sq, 0, hh] = lax.dot_general(kf, vb, tn, preferred_element_type=F32)
            st_ref[sq, 1, hh] = lax.dot_general(kb, vb, tn, preferred_element_type=F32)
        mu = jnp.mean(o, axis=-1, keepdims=True)
        oc = o - mu
        var = jnp.mean(oc * oc, axis=-1, keepdims=True)
        y = (oc * lax.rsqrt(var + GN_EPS)) * gn_ref[:, sl] * g_ref[rs, sl].astype(F32)
        o_ref[rs, sl] = y.astype(BF16)


def _retention(proj, batch, seq, col0, n_heads, dk, decay_logit, ret_gn, rope, state0, want_state):
    d_ret = n_heads * dk
    hb = n_heads if seq <= 256 else min(n_heads, 2)
    bs = next(b for b in (4, 2, 1) if batch % b == 0) if seq <= 256 else 1
    bw = hb * dk
    nhb = n_heads // hb
    lg = jnp.broadcast_to(decay_logit.T[:, :, None, None], (n_heads, 2, 1, V7X_LANES))
    act = lambda part: pl.BlockSpec((bs * seq, bw), lambda h, b: (b, (col0 + part * d_ret) // bw + h))
    in_specs = [act(0), act(1), act(2), act(3),
                pl.BlockSpec((hb, 2, 1, V7X_LANES), lambda h, b: (h, 0, 0, 0)),
                pl.BlockSpec((1, bw), lambda h, b: (0, h))]
    args = [proj, proj, proj, proj, lg, ret_gn.reshape(1, d_ret)]
    if rope is not None:
        in_specs += [pl.BlockSpec((seq, dk), lambda h, b: (0, 0))] * 2
        args += list(rope)
    if state0 is not None:
        in_specs.append(pl.BlockSpec((bs, 2, hb, dk, dk), lambda h, b: (b, 0, h, 0, 0)))
        args.append(state0)
    out_specs = [pl.BlockSpec((bs * seq, bw), lambda h, b: (b, h))]
    out_shape = [jax.ShapeDtypeStruct((batch * seq, d_ret), BF16)]
    if want_state:
        out_specs.append(pl.BlockSpec((bs, 2, hb, dk, dk), lambda h, b: (b, 0, h, 0, 0)))
        out_shape.append(jax.ShapeDtypeStruct((batch, 2, n_heads, dk, dk), F32))
    body = functools.partial(_retention_kernel, seq=seq, bs=bs, hb=hb, dk=dk, use_rope=rope is not None,
                             use_state=state0 is not None, want_state=want_state)
    return pl.pallas_call(
        body,
        grid=(nhb, batch // bs),
        in_specs=in_specs,
        out_specs=out_specs,
        out_shape=out_shape,
        scratch_shapes=[pltpu.VMEM((hb, seq, seq), F32)]
        + ([pltpu.VMEM((hb, 2, seq, dk), F32)] if want_state else []),
        compiler_params=_cparams("parallel", "arbitrary"),
        name=f"retention_{seq}",
    )(*args)


def _merge_kernel(*refs, nblk):
    gate_refs = refs[:2 * nblk]
    (yhy_ref, yret_ref, x_ref, wbh_ref, wbr_ref, wo_ref, gpost_ref, gm_ref, gpre_ref, sc_ref, sh_ref,
     o_ref, h_ref) = refs[2 * nblk:]
    a = _dot(yhy_ref[...], wbh_ref[...])
    b = _dot(yret_ref[...], wbr_ref[...])
    wblk = gate_refs[0].shape[1]
    parts = []
    for kk in range(nblk):
        sl = slice(kk * wblk, (kk + 1) * wblk)
        g_hy = gate_refs[kk][...].astype(F32)
        g_ret = gate_refs[nblk + kk][...].astype(F32)
        parts.append((g_hy * a[:, sl] + g_ret * b[:, sl]).astype(BF16))
    merged = parts[0] if nblk == 1 else jnp.concatenate(parts, axis=1)
    half = merged.shape[0] // 2
    for c in range(2):
        rs = slice(c * half, (c + 1) * half)
        out = _dot(merged[rs], wo_ref[...])
        x1 = x_ref[rs] + _rms_scale(out, gm_ref[...] * gpost_ref[...])
        o_ref[rs] = x1
        h_ref[rs] = (_rms_scale(x1, gpre_ref[...] * (1.0 + sc_ref[...])) + sh_ref[...]).astype(BF16)


def _merge(proj, col0, y_hy, y_ret, x, w_br_hy, w_br_ret, w_out, g_post, gate_m, g_pre_f, scale_f, shift_f,
           rows_per_mod, tm):
    m, d = x.shape
    vec = pl.BlockSpec((1, d), lambda i: (0, 0))
    mod_spec = pl.BlockSpec((None, 1, d), lambda i: ((i * tm) // rows_per_mod, 0, 0))
    wblk = math.gcd(col0, d)
    nblk = d // wblk
    gate_spec = lambda kk: pl.BlockSpec((tm, wblk), lambda i: (i, col0 // wblk + kk))
    const = lambda arr: _resident(arr.shape)
    row = lambda width: pl.BlockSpec((tm, width), lambda i: (i, 0))
    return pl.pallas_call(
        functools.partial(_merge_kernel, nblk=nblk),
        grid=(m // tm,),
        in_specs=[gate_spec(kk) for kk in range(2 * nblk)]
        + [row(y_hy.shape[1]), row(y_ret.shape[1]), row(d), const(w_br_hy), const(w_br_ret), const(w_out),
           vec, mod_spec, vec, mod_spec, mod_spec],
        out_specs=[row(d), row(d)],
        out_shape=[jax.ShapeDtypeStruct((m, d), F32), jax.ShapeDtypeStruct((m, d), BF16)],
        compiler_params=_cparams("parallel"),
        name="merge",
    )(*([proj] * (2 * nblk)), y_hy, y_ret, x, w_br_hy, w_br_ret, w_out, g_post, gate_m, g_pre_f, scale_f, shift_f)


def _ffn_kernel(x_hbm, h_ref, gate_ref, gpost_ref, wa_ref, wb_ref, ca_ref, cb_ref, wd_ref, o_ref, act_ref, x_ref,
                x_sem, *, seq):
    j = pl.program_id(1)
    nf = pl.num_programs(1) - 1
    tm = x_ref.shape[0]
    tf = wa_ref.shape[1]

    def x_copy():
        return pltpu.make_async_copy(x_hbm.at[pl.ds(pl.program_id(0) * tm, tm)], x_ref, x_sem)

    def up(slot):
        pos = lax.broadcasted_iota(jnp.int32, (tm, tf), 0) % seq
        first = pos == 0
        last = pos == seq - 1
        h = h_ref[...]
        a = _dwconv3_rows(_dot(h, wa_ref[...]), ca_ref[...], first, last)
        b = _dwconv3_rows(_dot(h, wb_ref[...]), cb_ref[...], first, last)
        c1 = math.sqrt(2.0 / math.pi)
        half = 0.5 * a
        gelu = half + half * jnp.tanh(a * (c1 + (c1 * 0.044715) * (a * a)))
        act_ref[slot] = (gelu * b).astype(BF16)

    def down(slot):
        return _dot(act_ref[slot], wd_ref[...])

    @pl.when(j == 0)
    def _():
        x_copy().start()
        up(0)

    @pl.when(j == 1)
    def _():
        o_ref[...] = down(0)
        up(1)

    @pl.when((j > 1) & (j < nf))
    def _():
        slot = j % 2
        o_ref[...] += down(1 - slot)
        up(slot)

    @pl.when(j == nf)
    def _():
        x_copy().wait()
        slot = (nf - 1) % 2
        half = tm // 2
        for c in range(2):
            rs = slice(c * half, (c + 1) * half)
            f = o_ref[rs] + _dot(act_ref[slot, rs], wd_ref[...])
            o_ref[rs] = x_ref[rs] + _rms_scale(f, gate_ref[...] * gpost_ref[...])


def _ffn(x, h, seq, gate, g_post, w_up, conv_w, w_down, rows_per_mod, tm):
    m, d = x.shape
    d_ff = w_down.shape[0]
    tf = _tile(d_ff, 512)
    nf = d_ff // tf
    assert nf >= 2, (d_ff, tf)
    up_blk = lambda j: jnp.minimum(j, nf - 1)
    down_blk = lambda j: jnp.maximum(j - 1, 0)
    return pl.pallas_call(
        functools.partial(_ffn_kernel, seq=seq),
        grid=(m // tm, nf + 1),
        in_specs=[pl.BlockSpec(memory_space=pl.ANY), pl.BlockSpec((tm, d), lambda i, j: (i, 0)),
                  pl.BlockSpec((None, 1, d), lambda i, j: ((i * tm) // rows_per_mod, 0, 0)),
                  pl.BlockSpec((1, d), lambda i, j: (0, 0)),
                  pl.BlockSpec((d, tf), lambda i, j: (0, up_blk(j))),
                  pl.BlockSpec((d, tf), lambda i, j: (0, nf + up_blk(j))),
                  pl.BlockSpec((3, tf), lambda i, j: (0, up_blk(j))),
                  pl.BlockSpec((3, tf), lambda i, j: (0, nf + up_blk(j))),
                  pl.BlockSpec((tf, d), lambda i, j: (down_blk(j), 0))],
        out_specs=pl.BlockSpec((tm, d), lambda i, j: (i, 0)),
        out_shape=jax.ShapeDtypeStruct((m, d), F32),
        scratch_shapes=[pltpu.VMEM((2, tm, tf), BF16), pltpu.VMEM((tm, d), F32), pltpu.SemaphoreType.DMA(())],
        compiler_params=_cparams("parallel", "arbitrary"),
        name="ffn",
    )(x, h, gate, g_post, w_up, w_up, conv_w, conv_w, w_down)


class _Group:
    def __init__(self, x3, mod):
        self.batch, self.seq, self.d = x3.shape
        self.m = self.batch * self.seq
        self.x = x3.reshape(self.m, self.d)
        per_seq_mod = mod.shape[0] != 1
        self.rows_per_mod = self.seq if per_seq_mod else self.m
        self.tm = _row_tile(self.m, self.seq, per_seq_mod, 1024)
        self.tm_merge = _tile(self.tm, 512, unit=V7X_SUBLANES)
        (self.shift_m, self.scale_m, self.gate_m,
         self.shift_f, self.scale_f, self.gate_f) = (mod[:, i][:, None, :] for i in range(6))

    def in_proj(self, p, w_in, cast=()):
        return _in_proj(self.x, p["g_pre_m"], self.scale_m, self.shift_m, w_in, p["hy_short_w"], self.seq,
                        p["d_hy"], p["d_ret"], self.rows_per_mod, self.tm, cast)


def _mix_and_ffn(grp, proj, rope, state0, want_state, p):
    d_hy, d_ret, n_heads, dk = p["d_hy"], p["d_ret"], p["n_heads"], p["dk"]
    tables = _hyena_filters(grp.seq, p["filter_params"])
    y_hy = _hyena(proj, grp.batch, grp.seq, tables, d_hy)
    ret = _retention(proj, grp.batch, grp.seq, 3 * d_hy, n_heads, dk, p["ret_decay_logit"], p["ret_gn"], rope,
                     state0, want_state)
    x, h_ffn = _merge(proj, 3 * d_hy + 4 * d_ret, y_hy, ret[0], grp.x, p["w_br_hy"], p["w_br_ret"], p["w_out"],
                      p["g_post_m"], grp.gate_m, p["g_pre_f"], grp.scale_f, grp.shift_f, grp.rows_per_mod,
                      grp.tm_merge)
    x = _ffn(x, h_ffn, grp.seq, grp.gate_f, p["g_post_f"], p["ffn_w_up"], p["ffn_conv"], p["ffn_w_down"],
             grp.rows_per_mod, grp.tm)
    return x.reshape(grp.batch, grp.seq, grp.d), (ret[1] if want_state else None)


def kernel(x_prompt, x_sample, state_ret, c, c_ctx, w_ada, b_ada, norm_pre_mix, norm_post_mix, norm_pre_ffn,
           norm_post_ffn, w_in, hy_short_w, hy_w1, hy_b1, hy_w2, hy_b2, hy_w3, hy_b3, hy_freq, hy_decay, hy_bias,
           ret_decay_logit, ret_gn, w_br_hy, w_br_ret, w_out, ffn_w_up, ffn_conv, ffn_w_down):
    depth = w_in.shape[0]
    d = x_prompt.shape[-1]
    n_dec = x_sample.shape[0]
    n_heads, dk = state_ret.shape[3], state_ret.shape[4]
    d_hy = hy_bias.shape[-1]
    assert 1 + n_dec <= V7X_SUBLANES, n_dec
    cc = jnp.concatenate([c_ctx[None], c, jnp.zeros((V7X_SUBLANES - 1 - n_dec, d), F32)], axis=0)
    rope = _rope_tables(x_sample.shape[1], dk)
    x_p, x_s = x_prompt, x_sample
    states = []
    for l in range(depth):
        p = dict(
            d_hy=d_hy, d_ret=n_heads * dk, n_heads=n_heads, dk=dk,
            g_pre_m=norm_pre_mix[l][None], g_post_m=norm_post_mix[l][None],
            g_pre_f=norm_pre_ffn[l][None], g_post_f=norm_post_ffn[l][None],
            hy_short_w=hy_short_w[l],
            filter_params=_filter_params(hy_w1[l], hy_b1[l], hy_w2[l], hy_b2[l], hy_w3[l], hy_b3[l], hy_freq[l],
                                         hy_decay[l], hy_bias[l]),
            ret_decay_logit=ret_decay_logit[l], ret_gn=ret_gn[l], ffn_conv=ffn_conv[l],
        )
        mod = _ada_mod(cc, w_ada[l], b_ada[l][None]).reshape(V7X_SUBLANES, 6, d)
        ctx = _Group(x_p, mod[0:1])
        lat = _Group(x_s, mod[1:1 + n_dec])
        proj_s, w_in_bf = lat.in_proj(p, w_in[l])
        proj_p, *rest = ctx.in_proj(p, w_in_bf, (ffn_w_up[l], ffn_w_down[l], w_out[l], w_br_hy[l], w_br_ret[l]))
        p["ffn_w_up"], p["ffn_w_down"], p["w_out"], p["w_br_hy"], p["w_br_ret"] = rest
        x_p, st = _mix_and_ffn(ctx, proj_p, None, None, True, p)
        x_s, _ = _mix_and_ffn(lat, proj_s, rope, state_ret[:, l], False, p)
        states.append(st)
    return x_p, x_s, jnp.stack(states, axis=1)
```

```python
import functools
import math

import jax
import jax.numpy as jnp
import numpy as np
from jax import lax
from jax.experimental import pallas as pl
from jax.experimental.pallas import tpu as pltpu

F32 = jnp.float32
BF16 = jnp.bfloat16

RMS_EPS = 1e-6
GN_EPS = 1e-5
FILTER_EPS = 1e-6
HY_BANDS = 16
GRID_W = 64
ROPE_BASE = 10000.0

V7X_VMEM_LIMIT_BYTES = 58 * 1024 * 1024
V7X_LANES = 128
V7X_SUBLANES = 8


def _cparams(*sem):
    return pltpu.CompilerParams(dimension_semantics=sem, vmem_limit_bytes=V7X_VMEM_LIMIT_BYTES)


def _tile(n, target, unit=V7X_LANES):
    if n <= target:
        return n
    best = unit
    for t in range(unit, target + 1, unit):
        if n % t == 0:
            best = t
    assert n % best == 0, (n, target, unit)
    return best


def _row_tile(m, seq, per_seq_mod, target):
    if per_seq_mod or seq >= target:
        return seq
    return seq * _tile(m // seq, target // seq, unit=1)


def _resident(shape):
    return pl.BlockSpec(shape, lambda *_: (0,) * len(shape), pipeline_mode=pl.Buffered(1))


def _sigmoid(x):
    return 1.0 / (1.0 + jnp.exp(-x))


def _sigmoid_tanh(x):
    return 0.5 * jnp.tanh(0.5 * x) + 0.5


def _rms_scale(x, g):
    ms = jnp.mean(x * x, axis=-1, keepdims=True)
    return (x * lax.rsqrt(ms + RMS_EPS)) * g


def _dot(a, b):
    return jnp.dot(a, b, preferred_element_type=F32)


def _dot_hilo(a, b):
    a_hi = a.astype(BF16)
    a_lo = (a - a_hi.astype(F32)).astype(BF16)
    b_hi = b.astype(BF16)
    b_lo = (b - b_hi.astype(F32)).astype(BF16)
    return _dot(a_hi, b_hi) + (_dot(a_lo, b_hi) + _dot(a_hi, b_lo))


def _split_bf16(table):
    hi = table.astype(BF16)
    lo = (table - hi.astype(np.float64)).astype(BF16)
    return jnp.asarray(hi), jnp.asarray(lo)


def _dot_split(a_hi_ref, a_lo_ref, b):
    b_hi = b.astype(BF16)
    b_lo = (b - b_hi.astype(F32)).astype(BF16)
    a_hi = a_hi_ref[...]
    return _dot(a_hi, b_hi) + (_dot(a_lo_ref[...], b_hi) + _dot(a_hi, b_lo))


def _dwconv3_rows(x, w, first, last):
    rows = x.shape[0]
    prev = jnp.where(first, 0.0, pltpu.roll(x, 1, 0))
    nxt = jnp.where(last, 0.0, pltpu.roll(x, rows - 1, 0))
    return prev * w[0:1] + x * w[1:2] + nxt * w[2:3]


def _ada_kernel(cc_ref, w_ref, b_ref, o_ref):
    cc = cc_ref[...]
    s = cc * _sigmoid(cc)
    o_ref[...] = _dot(s.astype(BF16), w_ref[...].astype(BF16)) + b_ref[...]


def _ada_mod(cc, w, b):
    d, n = w.shape
    tn = _tile(n, 1024)
    return pl.pallas_call(
        _ada_kernel,
        grid=(n // tn,),
        in_specs=[pl.BlockSpec((V7X_SUBLANES, d), lambda j: (0, 0)),
                  pl.BlockSpec((d, tn), lambda j: (0, j)),
                  pl.BlockSpec((1, tn), lambda j: (0, j))],
        out_specs=pl.BlockSpec((V7X_SUBLANES, tn), lambda j: (0, j)),
        out_shape=jax.ShapeDtypeStruct((V7X_SUBLANES, n), F32),
        compiler_params=_cparams("parallel"),
        name="ada_mod",
    )(cc, w, b)


def _in_proj_kernel(*refs, seq, n_hy, n_qkv, n_g, n_cast, emit_w):
    x_ref, g_ref, sc_ref, sh_ref, w_ref, cw_ref = refs[:6]
    cast_in = refs[6:6 + n_cast]
    o_ref = refs[6 + n_cast]
    pos = 7 + n_cast
    if emit_w:
        wbf_ref = refs[pos]
        pos += 1
    cast_out = refs[pos:pos + n_cast]
    h_ref = refs[pos + n_cast]
    j = pl.program_id(1)
    k = pl.program_id(2)

    @pl.when(j == 0)
    def _():
        n_mod = sc_ref.shape[0]
        rows = x_ref.shape[0] // n_mod
        for b in range(n_mod):
            rs = slice(b * rows, (b + 1) * rows)
            h = _rms_scale(x_ref[rs], g_ref[...] * (1.0 + sc_ref[b])) + sh_ref[b]
            h_ref[k, rs] = h.astype(BF16)

    def tile():
        w = w_ref[...]
        if emit_w:
            w = w.astype(BF16)
            wbf_ref[...] = w
        for src, dst in zip(cast_in, cast_out):
            dst[...] = src[...].astype(BF16)
        return _dot(h_ref[k], w)

    @pl.when(j < n_hy)
    def _():
        pos = lax.broadcasted_iota(jnp.int32, o_ref.shape, 0) % seq
        o_ref[...] = _dwconv3_rows(tile(), cw_ref[...], pos == 0, pos == seq - 1).astype(BF16)

    @pl.when((j >= n_hy) & (j < n_hy + n_qkv))
    def _():
        o_ref[...] = tile().astype(BF16)

    @pl.when((j >= n_hy + n_qkv) & (j < n_hy + n_qkv + n_g))
    def _():
        p = tile()
        o_ref[...] = (p * _sigmoid_tanh(p)).astype(BF16)

    @pl.when(j >= n_hy + n_qkv + n_g)
    def _():
        o_ref[...] = _sigmoid_tanh(tile()).astype(BF16)


V7X_BF16_ROW_TILE = 2 * V7X_SUBLANES


def _cast_rows(arr, n_steps):
    rows = arr.shape[0]
    for blk in range(V7X_BF16_ROW_TILE, rows + 1, V7X_BF16_ROW_TILE):
        if rows % blk == 0 and rows // blk <= n_steps:
            return blk
    return None


def _in_proj(x, g, scale, shift, w, short_w, seq, d_hy, d_ret, rows_per_mod, tm, cast=()):
    m, d = x.shape
    n = w.shape[1]
    emit_w = w.dtype != BF16
    if emit_w:
        tm = m
    tn = _tile(math.gcd(3 * d_hy, d_ret, 2 * d), 512 if emit_w else 1024)
    n_hy, n_qkv, n_g = 3 * d_hy // tn, 3 * d_ret // tn, d_ret // tn
    n_col = n // tn
    n_mod = max(1, tm // rows_per_mod)
    n_row = m // tm
    grp = next(c for c in (4, 2, 1) if n_row % c == 0)
    row = lambda i, k: i * grp + k
    step = lambda i, j, k: (i * n_col + j) * grp + k
    x_row = lambda i, j, k: jnp.where(j == 0, row(i, k), row(i, grp - 1))
    mod_spec = pl.BlockSpec((n_mod, 1, d), lambda i, j, k: ((x_row(i, j, k) * tm) // rows_per_mod // n_mod, 0, 0))
    blks = [_cast_rows(a, n_row * n_col) for a in cast]
    assert all(b is not None for b in blks), [a.shape for a in cast]
    cast_specs = [pl.BlockSpec((b, a.shape[1]),
                               lambda i, j, k, nb=a.shape[0] // b: (jnp.minimum(step(i, j, k), nb - 1), 0))
                  for a, b in zip(cast, blks)]
    out_specs = [pl.BlockSpec((tm, tn), lambda i, j, k: (row(i, k), j))]
    out_shape = [jax.ShapeDtypeStruct((m, n), BF16)]
    if emit_w:
        out_specs.append(pl.BlockSpec((d, tn), lambda i, j, k: (0, j)))
        out_shape.append(jax.ShapeDtypeStruct(w.shape, BF16))
    out_specs += cast_specs
    out_shape += [jax.ShapeDtypeStruct(a.shape, BF16) for a in cast]
    return pl.pallas_call(
        functools.partial(_in_proj_kernel, seq=seq, n_hy=n_hy, n_qkv=n_qkv, n_g=n_g, n_cast=len(cast),
                          emit_w=emit_w),
        grid=(n_row // grp, n_col, grp),
        in_specs=[pl.BlockSpec((tm, d), lambda i, j, k: (x_row(i, j, k), 0),
                               pipeline_mode=pl.Buffered(1) if emit_w else None),
                  pl.BlockSpec((1, d), lambda i, j, k: (0, 0)),
                  mod_spec, mod_spec,
                  pl.BlockSpec((d, tn), lambda i, j, k: (0, j)),
                  pl.BlockSpec((3, tn), lambda i, j, k: (0, jnp.minimum(j, n_hy - 1)))] + cast_specs,
        out_specs=out_specs,
        out_shape=out_shape,
        scratch_shapes=[pltpu.VMEM((grp, tm, d), BF16)],
        compiler_params=_cparams("arbitrary", "arbitrary", "arbitrary"),
        name="in_proj",
    )(x, g, scale, shift, w, short_w, *cast)


def _dft_tables(seq):
    n_fft = 2 * seq
    idx = np.arange(seq)
    ang = 2.0 * np.pi * ((idx[:, None] * idx[None, :]) % n_fft) / n_fft
    cos = np.cos(ang)
    msin = -np.sin(ang)
    sign = np.where(idx % 2 == 0, 1.0, -1.0)
    msin[0, :] = sign
    fwd = np.concatenate([cos, msin], axis=0)
    wgt = np.full((seq,), 2.0 / n_fft)
    wgt[0] = 1.0 / n_fft
    inv_re = cos.T * wgt[None, :]
    inv_im = msin.T * wgt[None, :]
    inv_im[:, 0] = sign / n_fft
    inv = np.concatenate([inv_re, inv_im], axis=1)
    return fwd, inv


def _filter_feats(seq):
    n = np.arange(seq, dtype=np.float64)
    t = n / seq
    f = np.linspace(1e-4, HY_BANDS - 1, HY_BANDS)
    w = 2.0 * math.pi * n / seq
    z = np.concatenate([t[:, None], np.cos(w[:, None] * f), np.sin(w[:, None] * f)], axis=-1)
    out = np.zeros((seq, V7X_LANES), np.float32)
    out[:, :z.shape[1]] = z
    return out


def _filter_kernel(z_ref, w1_ref, b1_ref, w2_ref, b2_ref, fr_ref, w3f_ref, w3b_ref, b3f_ref, b3b_ref,
                   decf_ref, decb_ref, bias_ref, chi_ref, clo_ref, shi_ref, slo_ref, kr_ref, kiz_ref, krn_ref, h2_ref):
    @pl.when(pl.program_id(0) == 0)
    def _():
        h1 = jnp.sin(fr_ref[0:1, :] * (_dot_hilo(z_ref[...], w1_ref[...]) + b1_ref[...]))
        h2_ref[...] = jnp.sin(fr_ref[1:2, :] * (_dot_hilo(h1, w2_ref[...]) + b2_ref[...]))

    h2 = h2_ref[...]
    seq, cw = kr_ref.shape
    t = z_ref[:, 0:1]
    row = lax.broadcasted_iota(jnp.int32, (seq, cw), 0)
    hf = (_dot_hilo(h2, w3f_ref[...]) + b3f_ref[...]) * jnp.exp(-t * jnp.abs(decf_ref[...]))
    hb = (_dot_hilo(h2, w3b_ref[...]) + b3b_ref[...]) * jnp.exp(-t * jnp.abs(decb_ref[...]))
    hb = jnp.where(row == 0, 0.0, hb)
    norm = (jnp.sum(jnp.abs(hf), axis=0, keepdims=True)
            + jnp.sum(jnp.abs(hb), axis=0, keepdims=True) + FILTER_EPS)
    inv = 1.0 / norm
    even = (hf + hb) * inv
    odd = (hf - hb) * inv
    bias = bias_ref[...]
    kr = _dot_split(chi_ref, clo_ref, even) + bias
    ki = _dot_split(shi_ref, slo_ref, odd)
    nyq = jnp.sum(jnp.where((row & 1) == 0, even, -even), axis=0, keepdims=True) + bias
    kr_ref[...] = kr.astype(BF16)
    kiz_ref[...] = jnp.where(row == 0, 0.0, ki).astype(BF16)
    krn_ref[...] = jnp.where(row == 0, nyq, kr).astype(BF16)


def _filter_params(w1, b1, w2, b2, w3, b3, freq, decay, hy_bias):
    fh = w1.shape[1]
    pad = V7X_LANES
    ncol2 = w3.shape[1]
    return (jnp.zeros((pad, pad), F32).at[:w1.shape[0], :fh].set(w1),
            jnp.zeros((1, pad), F32).at[0, :fh].set(b1),
            jnp.zeros((pad, pad), F32).at[:fh, :fh].set(w2),
            jnp.zeros((1, pad), F32).at[0, :fh].set(b2),
            jnp.zeros((2, pad), F32).at[:, :fh].set(freq),
            jnp.zeros((pad, ncol2), F32).at[:fh].set(w3),
            b3.reshape(1, ncol2), decay.reshape(1, ncol2), hy_bias.reshape(1, ncol2 // 2))


def _hyena_filters(seq, params):
    w1p, b1p, w2p, b2p, frp, w3p, b3r, dec, bias = params
    pad = V7X_LANES
    ncol = bias.shape[1]
    fwd, _ = _dft_tables(seq)
    chi, clo = _split_bf16(fwd[:seq])
    shi, slo = _split_bf16(fwd[seq:])
    z = jnp.asarray(_filter_feats(seq))
    cw = _tile(ncol, 512)
    nb = ncol // cw
    full = lambda shape: pl.BlockSpec(shape, lambda j: (0, 0))
    colf = lambda rows: pl.BlockSpec((rows, cw), lambda j: (0, j))
    colb = lambda rows: pl.BlockSpec((rows, cw), lambda j: (0, nb + j))
    out = jax.ShapeDtypeStruct((seq, ncol), BF16)
    return pl.pallas_call(
        _filter_kernel,
        grid=(nb,),
        in_specs=[full((seq, pad)), full((pad, pad)), full((1, pad)), full((pad, pad)), full((1, pad)),
                  full((2, pad)), colf(pad), colb(pad), colf(1), colb(1), colf(1), colb(1), colf(1),
                  _resident((seq, seq)), _resident((seq, seq)), _resident((seq, seq)), _resident((seq, seq))],
        out_specs=[colf(seq), colf(seq), colf(seq)],
        out_shape=[out, out, out],
        scratch_shapes=[pltpu.VMEM((seq, pad), F32)],
        compiler_params=_cparams("arbitrary"),
        name=f"hy_filter_{seq}",
    )(z, w1p, b1p, w2p, b2p, frp, w3p, w3p, b3r, b3r, dec, dec, bias, chi, clo, shi, slo)


def _hyena_kernel(x1_ref, x2_ref, v_ref, kr0_ref, kiz0_ref, krn0_ref, kr1_ref, kiz1_ref, krn1_ref, f_ref, g_ref,
                  o_ref, *, seq):
    def long_conv(u, kr_ref, kiz_ref, krn_ref):
        spec = _dot(f_ref[...], u.astype(BF16))
        ur = spec[:seq].astype(BF16)
        ui = spec[seq:].astype(BF16)
        kiz = kiz_ref[...]
        yr = ur * kr_ref[...] - ui * kiz
        yi = ur * kiz + ui * krn_ref[...]
        return _dot(g_ref[:, :seq], yr) + _dot(g_ref[:, seq:], yi)

    for s in range(o_ref.shape[0] // seq):
        rs = slice(s * seq, (s + 1) * seq)
        z = x1_ref[rs].astype(F32) * long_conv(v_ref[rs], kr0_ref, kiz0_ref, krn0_ref)
        o_ref[rs] = (x2_ref[rs].astype(F32) * long_conv(z, kr1_ref, kiz1_ref, krn1_ref)).astype(BF16)


def _hyena(proj, batch, seq, tables, d_hy):
    kr, kiz, krn = tables
    cw = _tile(d_hy, 1024 if seq <= 256 else 512)
    nb = d_hy // cw
    fwd, inv = _dft_tables(seq)
    fmat = jnp.asarray(fwd, F32).astype(BF16)
    gmat = jnp.asarray(inv, F32).astype(BF16)
    bs = next(b for b in (4, 2, 1) if batch % b == 0) if seq <= 256 else 1
    col = lambda rows, off: pl.BlockSpec((rows, cw), lambda j, b: (0, off * nb + j))
    act = lambda off: pl.BlockSpec((bs * seq, cw), lambda j, b: (b, off * nb + j))
    return pl.pallas_call(
        functools.partial(_hyena_kernel, seq=seq),
        grid=(nb, batch // bs),
        in_specs=[act(0), act(1), act(2),
                  col(seq, 0), col(seq, 0), col(seq, 0), col(seq, 1), col(seq, 1), col(seq, 1),
                  _resident((2 * seq, seq)), _resident((seq, 2 * seq))],
        out_specs=pl.BlockSpec((bs * seq, cw), lambda j, b: (b, j)),
        out_shape=jax.ShapeDtypeStruct((batch * seq, d_hy), BF16),
        compiler_params=_cparams("parallel", "parallel"),
        name=f"hyena_{seq}",
    )(proj, proj, proj, kr, kiz, krn, kr, kiz, krn, fmat, gmat)


def _rope_tables(seq, dk):
    rows = seq // GRID_W
    row = jnp.repeat(jnp.arange(rows), GRID_W).astype(F32)
    col = jnp.tile(jnp.arange(GRID_W), rows).astype(F32)
    nfreq = dk // 4
    inv = ROPE_BASE ** (-jnp.arange(nfreq, dtype=F32) / nfreq)
    ang = jnp.concatenate([row[:, None] * inv, col[:, None] * inv], axis=-1)
    cos, sin = jnp.cos(ang), jnp.sin(ang)
    return jnp.concatenate([cos, cos], axis=-1), jnp.concatenate([-sin, sin], axis=-1)


def _retention_kernel(*refs, seq, bs, hb, dk, use_rope, use_state, want_state):
    refs = list(refs)
    q_ref, k_ref, v_ref, g_ref, lg_ref, gn_ref = refs[:6]
    pos = 6
    if use_rope:
        cos_ref, sin_ref = refs[pos:pos + 2]
        pos += 2
    if use_state:
        s0_ref = refs[pos]
        pos += 1
    o_ref = refs[pos]
    pos += 1
    if want_state:
        st_ref = refs[pos]
        pos += 1
    d_ref = refs[pos]
    if want_state:
        wt_ref = refs[pos + 1]

    def log_gamma(hh, direction):
        return jnp.log(_sigmoid(lg_ref[hh, direction]))[:, 0:1]

    @pl.when(pl.program_id(1) == 0)
    def _():
        i = lax.broadcasted_iota(jnp.int32, (dk, dk), 0)
        j = lax.broadcasted_iota(jnp.int32, (dk, dk), 1)
        diff = (i - j).astype(F32)
        scale = dk ** -0.5
        for hh in range(hb):
            lf = log_gamma(hh, 0)
            lb = log_gamma(hh, 1)
            base_f = scale * jnp.exp(lf * diff)
            base_b = scale * jnp.exp(lb * (-diff))
            diag = jnp.where(diff >= 0, base_f, 0.0) + jnp.where(diff <= 0, base_b, 0.0)
            for bi in range(seq // dk):
                for bj in range(seq // dk):
                    if bi == bj:
                        blk = diag
                    elif bi > bj:
                        blk = base_f * jnp.exp(lf * float(dk * (bi - bj)))
                    else:
                        blk = base_b * jnp.exp(lb * float(dk * (bj - bi)))
                    d_ref[hh, bi * dk:(bi + 1) * dk, bj * dk:(bj + 1) * dk] = blk
            if want_state:
                t = lax.broadcasted_iota(jnp.int32, (seq, dk), 0).astype(F32)
                wt_ref[hh, 0] = scale * jnp.exp(lf * (seq - 1.0 - t))
                wt_ref[hh, 1] = scale * jnp.exp(lb * t)

    pos_f = lax.broadcasted_iota(jnp.int32, (seq, dk), 0).astype(F32)
    for sq, hh in [(sq, hh) for sq in range(bs) for hh in range(hb)]:
        rs = slice(sq * seq, (sq + 1) * seq)
        sl = slice(hh * dk, (hh + 1) * dk)
        qb = q_ref[rs, sl]
        kb16 = k_ref[rs, sl]
        vb = v_ref[rs, sl]
        if use_rope:
            cos = cos_ref[...]
            sin = sin_ref[...]
            q = qb.astype(F32)
            k = kb16.astype(F32)
            qb = (q * cos + pltpu.roll(q, dk // 2, 1) * sin).astype(BF16)
            kb16 = (k * cos + pltpu.roll(k, dk // 2, 1) * sin).astype(BF16)
        s = lax.dot_general(qb, kb16, (((1,), (1,)), ((), ())), preferred_element_type=F32)
        o = _dot((s * d_ref[hh]).astype(BF16), vb)
        lf = log_gamma(hh, 0)
        lb = log_gamma(hh, 1)
        if use_state:
            o = o + _dot(qb, s0_ref[sq, 0, hh].astype(BF16)) * jnp.exp(lf * (pos_f + 1.0))
            o = o + _dot(qb, s0_ref[sq, 1, hh].astype(BF16)) * jnp.exp(lb * (seq - pos_f))
        if want_state:
            k = kb16.astype(F32)
            kf = (k * wt_ref[hh, 0]).astype(BF16)
            kb = (k * wt_ref[hh, 1]).astype(BF16)
            tn = (((0,), (0,)), ((), ()))
            st_ref[sq, 0, hh] = lax.dot_general(kf, vb, tn, preferred_element_type=F32)
            st_ref[sq, 1, hh] = lax.dot_general(kb, vb, tn, preferred_element_type=F32)
        mu = jnp.mean(o, axis=-1, keepdims=True)
        oc = o - mu
        var = jnp.mean(oc * oc, axis=-1, keepdims=True)
        y = (oc * lax.rsqrt(var + GN_EPS)) * gn_ref[:, sl] * g_ref[rs, sl].astype(F32)
        o_ref[rs, sl] = y.astype(BF16)


def _retention(proj, batch, seq, col0, n_heads, dk, decay_logit, ret_gn, rope, state0, want_state):
    d_ret = n_heads * dk
    hb = n_heads if seq <= 256 else min(n_heads, 4)
    bs = next(b for b in (4, 2, 1) if batch % b == 0) if seq <= 256 else 1
    bw = hb * dk
    nhb = n_heads // hb
    lg = jnp.broadcast_to(decay_logit.T[:, :, None, None], (n_heads, 2, 1, V7X_LANES))
    act = lambda part: pl.BlockSpec((bs * seq, bw), lambda h, b: (b, (col0 + part * d_ret) // bw + h))
    in_specs = [act(0), act(1), act(2), act(3),
                pl.BlockSpec((hb, 2, 1, V7X_LANES), lambda h, b: (h, 0, 0, 0)),
                pl.BlockSpec((1, bw), lambda h, b: (0, h))]
    args = [proj, proj, proj, proj, lg, ret_gn.reshape(1, d_ret)]
    if rope is not None:
        in_specs += [pl.BlockSpec((seq, dk), lambda h, b: (0, 0))] * 2
        args += list(rope)
    if state0 is not None:
        in_specs.append(pl.BlockSpec((bs, 2, hb, dk, dk), lambda h, b: (b, 0, h, 0, 0)))
        args.append(state0)
    out_specs = [pl.BlockSpec((bs * seq, bw), lambda h, b: (b, h))]
    out_shape = [jax.ShapeDtypeStruct((batch * seq, d_ret), BF16)]
    if want_state:
        out_specs.append(pl.BlockSpec((bs, 2, hb, dk, dk), lambda h, b: (b, 0, h, 0, 0)))
        out_shape.append(jax.ShapeDtypeStruct((batch, 2, n_heads, dk, dk), F32))
    body = functools.partial(_retention_kernel, seq=seq, bs=bs, hb=hb, dk=dk, use_rope=rope is not None,
                             use_state=state0 is not None, want_state=want_state)
    return pl.pallas_call(
        body,
        grid=(nhb, batch // bs),
        in_specs=in_specs,
        out_specs=out_specs,
        out_shape=out_shape,
        scratch_shapes=[pltpu.VMEM((hb, seq, seq), F32)]
        + ([pltpu.VMEM((hb, 2, seq, dk), F32)] if want_state else []),
        compiler_params=_cparams("parallel", "arbitrary"),
        name=f"retention_{seq}",
    )(*args)


def _merge_kernel(*refs, nblk):
    gate_refs = refs[:2 * nblk]
    (yhy_ref, yret_ref, x_ref, wbh_ref, wbr_ref, wo_ref, gpost_ref, gm_ref, gpre_ref, sc_ref, sh_ref,
     o_ref, h_ref) = refs[2 * nblk:]
    a = _dot(yhy_ref[...], wbh_ref[...])
    b = _dot(yret_ref[...], wbr_ref[...])
    wblk = gate_refs[0].shape[1]
    parts = []
    for kk in range(nblk):
        sl = slice(kk * wblk, (kk + 1) * wblk)
        g_hy = gate_refs[kk][...].astype(F32)
        g_ret = gate_refs[nblk + kk][...].astype(F32)
        parts.append((g_hy * a[:, sl] + g_ret * b[:, sl]).astype(BF16))
    merged = parts[0] if nblk == 1 else jnp.concatenate(parts, axis=1)
    half = merged.shape[0] // 2
    for c in range(2):
        rs = slice(c * half, (c + 1) * half)
        out = _dot(merged[rs], wo_ref[...])
        x1 = x_ref[rs] + _rms_scale(out, gm_ref[...] * gpost_ref[...])
        o_ref[rs] = x1
        h_ref[rs] = (_rms_scale(x1, gpre_ref[...] * (1.0 + sc_ref[...])) + sh_ref[...]).astype(BF16)


def _merge(proj, col0, y_hy, y_ret, x, w_br_hy, w_br_ret, w_out, g_post, gate_m, g_pre_f, scale_f, shift_f,
           rows_per_mod, tm):
    m, d = x.shape
    vec = pl.BlockSpec((1, d), lambda i: (0, 0))
    mod_spec = pl.BlockSpec((None, 1, d), lambda i: ((i * tm) // rows_per_mod, 0, 0))
    wblk = math.gcd(col0, d)
    nblk = d // wblk
    gate_spec = lambda kk: pl.BlockSpec((tm, wblk), lambda i: (i, col0 // wblk + kk))
    const = lambda arr: _resident(arr.shape)
    row = lambda width: pl.BlockSpec((tm, width), lambda i: (i, 0))
    return pl.pallas_call(
        functools.partial(_merge_kernel, nblk=nblk),
        grid=(m // tm,),
        in_specs=[gate_spec(kk) for kk in range(2 * nblk)]
        + [row(y_hy.shape[1]), row(y_ret.shape[1]), row(d), const(w_br_hy), const(w_br_ret), const(w_out),
           vec, mod_spec, vec, mod_spec, mod_spec],
        out_specs=[row(d), row(d)],
        out_shape=[jax.ShapeDtypeStruct((m, d), F32), jax.ShapeDtypeStruct((m, d), BF16)],
        compiler_params=_cparams("parallel"),
        name="merge",
    )(*([proj] * (2 * nblk)), y_hy, y_ret, x, w_br_hy, w_br_ret, w_out, g_post, gate_m, g_pre_f, scale_f, shift_f)


def _ffn_kernel(x_hbm, h_ref, gate_ref, gpost_ref, wa_ref, wb_ref, ca_ref, cb_ref, wd_ref, o_ref, act_ref, x_ref,
                x_sem, *, seq):
    j = pl.program_id(1)
    nf = pl.num_programs(1) - 1
    tm = x_ref.shape[0]
    tf = wa_ref.shape[1]

    def x_copy():
        return pltpu.make_async_copy(x_hbm.at[pl.ds(pl.program_id(0) * tm, tm)], x_ref, x_sem)

    def up(slot):
        pos = lax.broadcasted_iota(jnp.int32, (tm, tf), 0) % seq
        first = pos == 0
        last = pos == seq - 1
        h = h_ref[...]
        a = _dwconv3_rows(_dot(h, wa_ref[...]), ca_ref[...], first, last)
        b = _dwconv3_rows(_dot(h, wb_ref[...]), cb_ref[...], first, last)
        c1 = math.sqrt(2.0 / math.pi)
        half = 0.5 * a
        gelu = half + half * jnp.tanh(a * (c1 + (c1 * 0.044715) * (a * a)))
        act_ref[slot] = (gelu * b).astype(BF16)

    def down(slot):
        return _dot(act_ref[slot], wd_ref[...])

    @pl.when(j == 0)
    def _():
        x_copy().start()
        up(0)

    @pl.when(j == 1)
    def _():
        o_ref[...] = down(0)
        up(1)

    @pl.when((j > 1) & (j < nf))
    def _():
        slot = j % 2
        o_ref[...] += down(1 - slot)
        up(slot)

    @pl.when(j == nf)
    def _():
        x_copy().wait()
        slot = (nf - 1) % 2
        half = tm // 2
        for c in range(2):
            rs = slice(c * half, (c + 1) * half)
            f = o_ref[rs] + _dot(act_ref[slot, rs], wd_ref[...])
            o_ref[rs] = x_ref[rs] + _rms_scale(f, gate_ref[...] * gpost_ref[...])


def _ffn(x, h, seq, gate, g_post, w_up, conv_w, w_down, rows_per_mod, tm):
    m, d = x.shape
    d_ff = w_down.shape[0]
    tf = _tile(d_ff, 512)
    nf = d_ff // tf
    assert nf >= 2, (d_ff, tf)
    up_blk = lambda j: jnp.minimum(j, nf - 1)
    down_blk = lambda j: jnp.maximum(j - 1, 0)
    return pl.pallas_call(
        functools.partial(_ffn_kernel, seq=seq),
        grid=(m // tm, nf + 1),
        in_specs=[pl.BlockSpec(memory_space=pl.ANY), pl.BlockSpec((tm, d), lambda i, j: (i, 0)),
                  pl.BlockSpec((None, 1, d), lambda i, j: ((i * tm) // rows_per_mod, 0, 0)),
                  pl.BlockSpec((1, d), lambda i, j: (0, 0)),
                  pl.BlockSpec((d, tf), lambda i, j: (0, up_blk(j))),
                  pl.BlockSpec((d, tf), lambda i, j: (0, nf + up_blk(j))),
                  pl.BlockSpec((3, tf), lambda i, j: (0, up_blk(j))),
                  pl.BlockSpec((3, tf), lambda i, j: (0, nf + up_blk(j))),
                  pl.BlockSpec((tf, d), lambda i, j: (down_blk(j), 0))],
        out_specs=pl.BlockSpec((tm, d), lambda i, j: (i, 0)),
        out_shape=jax.ShapeDtypeStruct((m, d), F32),
        scratch_shapes=[pltpu.VMEM((2, tm, tf), BF16), pltpu.VMEM((tm, d), F32), pltpu.SemaphoreType.DMA(())],
        compiler_params=_cparams("parallel", "arbitrary"),
        name="ffn",
    )(x, h, gate, g_post, w_up, w_up, conv_w, conv_w, w_down)


class _Group:
    def __init__(self, x3, mod):
        self.batch, self.seq, self.d = x3.shape
        self.m = self.batch * self.seq
        self.x = x3.reshape(self.m, self.d)
        per_seq_mod = mod.shape[0] != 1
        self.rows_per_mod = self.seq if per_seq_mod else self.m
        self.tm = _row_tile(self.m, self.seq, per_seq_mod, 1024)
        self.tm_merge = _tile(self.tm, 512, unit=V7X_SUBLANES)
        (self.shift_m, self.scale_m, self.gate_m,
         self.shift_f, self.scale_f, self.gate_f) = (mod[:, i][:, None, :] for i in range(6))

    def in_proj(self, p, w_in, cast=()):
        return _in_proj(self.x, p["g_pre_m"], self.scale_m, self.shift_m, w_in, p["hy_short_w"], self.seq,
                        p["d_hy"], p["d_ret"], self.rows_per_mod, self.tm, cast)


def _mix_and_ffn(grp, proj, rope, state0, want_state, p):
    d_hy, d_ret, n_heads, dk = p["d_hy"], p["d_ret"], p["n_heads"], p["dk"]
    tables = _hyena_filters(grp.seq, p["filter_params"])
    y_hy = _hyena(proj, grp.batch, grp.seq, tables, d_hy)
    ret = _retention(proj, grp.batch, grp.seq, 3 * d_hy, n_heads, dk, p["ret_decay_logit"], p["ret_gn"], rope,
                     state0, want_state)
    x, h_ffn = _merge(proj, 3 * d_hy + 4 * d_ret, y_hy, ret[0], grp.x, p["w_br_hy"], p["w_br_ret"], p["w_out"],
                      p["g_post_m"], grp.gate_m, p["g_pre_f"], grp.scale_f, grp.shift_f, grp.rows_per_mod,
                      grp.tm_merge)
    x = _ffn(x, h_ffn, grp.seq, grp.gate_f, p["g_post_f"], p["ffn_w_up"], p["ffn_conv"], p["ffn_w_down"],
             grp.rows_per_mod, grp.tm)
    return x.reshape(grp.batch, grp.seq, grp.d), (ret[1] if want_state else None)


def kernel(x_prompt, x_sample, state_ret, c, c_ctx, w_ada, b_ada, norm_pre_mix, norm_post_mix, norm_pre_ffn,
           norm_post_ffn, w_in, hy_short_w, hy_w1, hy_b1, hy_w2, hy_b2, hy_w3, hy_b3, hy_freq, hy_decay, hy_bias,
           ret_decay_logit, ret_gn, w_br_hy, w_br_ret, w_out, ffn_w_up, ffn_conv, ffn_w_down):
    depth = w_in.shape[0]
    d = x_prompt.shape[-1]
    n_dec = x_sample.shape[0]
    n_heads, dk = state_ret.shape[3], state_ret.shape[4]
    d_hy = hy_bias.shape[-1]
    assert 1 + n_dec <= V7X_SUBLANES, n_dec
    cc = jnp.concatenate([c_ctx[None], c, jnp.zeros((V7X_SUBLANES - 1 - n_dec, d), F32)], axis=0)
    rope = _rope_tables(x_sample.shape[1], dk)
    x_p, x_s = x_prompt, x_sample
    states = []
    for l in range(depth):
        p = dict(
            d_hy=d_hy, d_ret=n_heads * dk, n_heads=n_heads, dk=dk,
            g_pre_m=norm_pre_mix[l][None], g_post_m=norm_post_mix[l][None],
            g_pre_f=norm_pre_ffn[l][None], g_post_f=norm_post_ffn[l][None],
            hy_short_w=hy_short_w[l],
            filter_params=_filter_params(hy_w1[l], hy_b1[l], hy_w2[l], hy_b2[l], hy_w3[l], hy_b3[l], hy_freq[l],
                                         hy_decay[l], hy_bias[l]),
            ret_decay_logit=ret_decay_logit[l], ret_gn=ret_gn[l], ffn_conv=ffn_conv[l],
        )
        mod = _ada_mod(cc, w_ada[l], b_ada[l][None]).reshape(V7X_SUBLANES, 6, d)
        ctx = _Group(x_p, mod[0:1])
        lat = _Group(x_s, mod[1:1 + n_dec])
        proj_s, w_in_bf = lat.in_proj(p, w_in[l])
        proj_p, *rest = ctx.in_proj(p, w_in_bf, (ffn_w_up[l], ffn_w_down[l], w_out[l], w_br_hy[l], w_br_ret[l]))
        p["ffn_w_up"], p["ffn_w_down"], p["w_out"], p["w_br_hy"], p["w_br_ret"] = rest
        x_p, st = _mix_and_ffn(ctx, proj_p, None, None, True, p)
        x_s, _ = _mix_and_ffn(lat, proj_s, rope, state_ret[:, l], False, p)
        states.append(st)
    return x_p, x_s, jnp.stack(states, axis=1)
```
